```python
import math
import jax, jax.numpy as jnp
from jax import lax
import numpy as np

D_MODEL = 1024
BATCH = 4
SEQ = 4096
DEPTH = 2
DEC_BATCH = 128
DEC_SEQ = 1
PAST_LEN = 8192
PAGE_SIZE = 128

N_A_LAYERS = (DEPTH + 1) // 2
N_C_LAYERS = DEPTH // 2
EPS = 1e-6
CONV_W = 4
CHUNK = 64
Q_BLOCK = 128
GDN_HEADS = 4
GDN_DK = 128
GDN_DV = 128
GDN_QK = GDN_HEADS * GDN_DK
GDN_VW = GDN_HEADS * GDN_DV
GDN_CONV_CH = 2 * GDN_QK + GDN_VW
SSM_HEADS = 8
SSM_P = 64
SSM_N = 128
SSM_GROUPS = 2
SSM_INNER = SSM_HEADS * SSM_P
SSM_BC = SSM_GROUPS * SSM_N
SSM_CONV_CH = SSM_INNER + 2 * SSM_BC
IN_A_WIDTHS = (GDN_CONV_CH, GDN_VW, GDN_HEADS, GDN_HEADS, SSM_INNER, SSM_CONV_CH, SSM_HEADS)
IN_A = sum(IN_A_WIDTHS)
MIX_A = GDN_VW + SSM_INNER
MLA_HEADS = 16
Q_LORA = 512
KV_LORA = 256
QK_NOPE = 64
QK_ROPE = 32
QK_HEAD = QK_NOPE + QK_ROPE
V_HEAD = 64
MLA_IN = Q_LORA + KV_LORA + QK_ROPE
MLA_ROW = KV_LORA + QK_ROPE + MLA_HEADS
ROPE_THETA = 10000.0
D_FF = -(-8 * D_MODEL // (3 * 256)) * 256

kernel_name = 'hybrid_gdn_ssd_mla_decoder_step'


def rms_norm(x, g):
    xf = x.astype(jnp.float32)
    y = xf * lax.rsqrt(jnp.mean(xf * xf, axis=-1, keepdims=True) + EPS)
    return (y * g.astype(jnp.float32)).astype(x.dtype)


def l2_normalize(x):
    xf = x.astype(jnp.float32)
    return xf * lax.rsqrt(jnp.sum(xf * xf, axis=-1, keepdims=True) + EPS)


def rope(x, pos):
    half = x.shape[-1] // 2
    inv_freq = ROPE_THETA ** (-jnp.arange(half, dtype=jnp.float32) / half)
    ang = pos[:, None] * inv_freq[None, :]
    shape = (1, pos.shape[0]) + (1,) * (x.ndim - 3) + (half,)
    cos, sin = jnp.cos(ang).reshape(shape), jnp.sin(ang).reshape(shape)
    x1, x2 = x[..., :half], x[..., half:]
    return jnp.concatenate([x1 * cos - x2 * sin, x2 * cos + x1 * sin], axis=-1)


def causal_conv(x, prev, w):
    T = x.shape[1]
    xp = jnp.concatenate([prev.astype(x.dtype), x], axis=1)
    y = sum(w[i] * xp[:, i:i + T] for i in range(CONV_W))
    return y, xp[:, xp.shape[1] - (CONV_W - 1):]


def to_chunks(a, chunk):
    B, T = a.shape[:2]
    return jnp.moveaxis(a.reshape((B, T // chunk, chunk) + a.shape[2:]), 1, 0)


def from_chunks(a):
    n, B, C = a.shape[:3]
    return jnp.moveaxis(a, 0, 1).reshape((B, n * C) + a.shape[3:])


def gated_delta_chunked(q, k, v, g, beta, S0, chunk):
    f32 = jnp.float32
    incl = jnp.tril(jnp.ones((chunk, chunk), bool))
    strict = jnp.tril(jnp.ones((chunk, chunk), bool), -1)
    eye = jnp.eye(chunk, dtype=f32)
    dv = v.shape[-1]

    def step(S, inp):
        qi, ki, vi, gi, bi = (jnp.swapaxes(a.astype(f32), 1, 2) for a in inp)
        cum = jnp.cumsum(gi, axis=-1)
        decay = jnp.exp(jnp.where(incl, cum[..., :, None] - cum[..., None, :], -jnp.inf))
        kb = ki * bi[..., None]
        lower = jnp.where(strict, jnp.einsum('bhid,bhjd->bhij', kb, ki) * decay, 0.0)
        rhs = jnp.concatenate([vi * bi[..., None], kb * jnp.exp(cum)[..., None]], axis=-1)
        sol = lax.linalg.triangular_solve(eye + lower, rhs, left_side=True, lower=True,
                                          unit_diagonal=True)
        u, w = sol[..., :dv], sol[..., dv:]
        v_new = u - w @ S
        attn = jnp.einsum('bhid,bhjd->bhij', qi, ki) * decay
        o = (qi * jnp.exp(cum)[..., None]) @ S + attn @ v_new
        last = cum[..., -1:]
        S = S * jnp.exp(last)[..., None] + jnp.einsum(
            'bhjd,bhjv->bhdv', ki * jnp.exp(last - cum)[..., None], v_new)
        return S, jnp.swapaxes(o, 1, 2)

    S, o = lax.scan(step, S0.astype(f32), tuple(to_chunks(a, chunk) for a in (q, k, v, g, beta)))
    return from_chunks(o), S


def ssd_chunked(x, dt, A, Bh, Ch, h0, chunk):
    f32 = jnp.float32
    incl = jnp.tril(jnp.ones((chunk, chunk), bool))
    la = dt * A
    xdt = x.astype(f32) * dt[..., None]

    def step(h, inp):
        xi, lai, bi, ci = inp
        cum = jnp.cumsum(lai, axis=1)
        cum_h = jnp.swapaxes(cum, 1, 2)
        seg = jnp.exp(jnp.where(incl, cum_h[..., :, None] - cum_h[..., None, :], -jnp.inf))
        scores = jnp.einsum('bihn,bjhn->bhij', ci, bi) * seg
        y = (jnp.einsum('bhij,bjhp->bihp', scores, xi)
             + jnp.einsum('bihn,bhpn->bihp', ci, h) * jnp.exp(cum)[..., None])
        last = cum[:, -1:]
        h = h * jnp.exp(last[:, 0])[..., None, None] + jnp.einsum(
            'bjhn,bjhp->bhpn', bi * jnp.exp(last - cum)[..., None], xi)
        return h, y

    h, y = lax.scan(step, h0.astype(f32),
                    tuple(to_chunks(a, chunk) for a in (xdt, la, Bh.astype(f32), Ch.astype(f32))))
    return from_chunks(y), h


def hybrid_layer(x, S0, gconv0, h0, sconv0, chunk, norm_g, w_in, conv_gdn_w, gdn_A_log,
                 gdn_dt_bias, gdn_norm, conv_ssm_w, conv_ssm_b, ssm_A_log, ssm_dt_bias, ssm_D,
                 ssm_norm, w_out):
    B, T, _ = x.shape
    f32 = jnp.float32
    proj = rms_norm(x, norm_g) @ w_in
    g_qkv, g_z, g_a, g_b, s_z, s_xbc, s_dt = jnp.split(
        proj, np.cumsum(IN_A_WIDTHS)[:-1].tolist(), axis=-1)
    qkv, gdn_conv = causal_conv(g_qkv, gconv0, conv_gdn_w)
    q, k, v = jnp.split(jax.nn.silu(qkv), [GDN_QK, 2 * GDN_QK], axis=-1)
    q = l2_normalize(q.reshape(B, T, GDN_HEADS, GDN_DK)) * GDN_DK ** -0.5
    k = l2_normalize(k.reshape(B, T, GDN_HEADS, GDN_DK))
    v = v.reshape(B, T, GDN_HEADS, GDN_DV)
    beta = jax.nn.sigmoid(g_b.astype(f32))
    g = -jnp.exp(gdn_A_log.astype(f32)) * jax.nn.softplus(g_a.astype(f32) + gdn_dt_bias)
    o, S = gated_delta_chunked(q, k, v, g, beta, S0, chunk)
    o = rms_norm(o, gdn_norm) * jax.nn.silu(g_z.reshape(B, T, GDN_HEADS, GDN_DV).astype(f32))
    xbc, ssm_conv = causal_conv(s_xbc, sconv0, conv_ssm_w)
    xs, bm, cm = jnp.split(jax.nn.silu(xbc + conv_ssm_b), [SSM_INNER, SSM_INNER + SSM_BC], axis=-1)
    xs = xs.reshape(B, T, SSM_HEADS, SSM_P)
    rep = SSM_HEADS // SSM_GROUPS
    bh = jnp.repeat(bm.reshape(B, T, SSM_GROUPS, SSM_N), rep, axis=2)
    ch = jnp.repeat(cm.reshape(B, T, SSM_GROUPS, SSM_N), rep, axis=2)
    dt = jax.nn.softplus(s_dt.astype(f32) + ssm_dt_bias)
    y, h = ssd_chunked(xs, dt, -jnp.exp(ssm_A_log.astype(f32)), bh, ch, h0, chunk)
    y = (y + ssm_D[:, None] * xs).reshape(B, T, SSM_INNER) * jax.nn.silu(s_z.astype(f32))
    y = rms_norm(y.reshape(B, T, SSM_GROUPS, SSM_INNER // SSM_GROUPS),
                 ssm_norm.reshape(SSM_GROUPS, -1)).reshape(B, T, SSM_INNER)
    mix = jnp.concatenate([o.reshape(B, T, GDN_VW), y], axis=-1).astype(x.dtype) @ w_out
    return x + mix, (S, gdn_conv, h, ssm_conv)


def causal_block_attention(q, k, v):
    B, T, H, Dq = q.shape
    nb = T // Q_BLOCK
    q_blocks = jnp.moveaxis(q.reshape(B, nb, Q_BLOCK, H, Dq), 1, 0)
    k_pos = jnp.arange(T)
    scale = Dq ** -0.5

    def attend(args):
        qb, start = args
        s = jnp.einsum('bqhd,bkhd->bhqk', qb, k).astype(jnp.float32) * scale
        q_pos = start + jnp.arange(Q_BLOCK)
        s = jnp.where(k_pos[None, :] <= q_pos[:, None], s, -jnp.inf)
        return jnp.einsum('bhqk,bkhd->bqhd', jax.nn.softmax(s, axis=-1), v)

    o = lax.map(attend, (q_blocks, jnp.arange(nb) * Q_BLOCK))
    return jnp.moveaxis(o, 0, 1).reshape(B, T, H, v.shape[-1])


def latent_scores(qa, q_rope, rows):
    c, kr, inv_r = jnp.split(rows, [KV_LORA, KV_LORA + QK_ROPE], axis=-1)
    s = (jnp.einsum('bthc,bsc->bhts', qa, c).astype(jnp.float32)
         + jnp.einsum('bthr,bsr->bhts', q_rope, kr).astype(jnp.float32))
    return s * jnp.swapaxes(inv_r, 1, 2)[:, :, None, :].astype(jnp.float32) * QK_HEAD ** -0.5


def mla_layer(x, pos, past, norm_g, w_in, q_a_norm, kv_a_norm, w_uq, w_uk, w_uv, q_norm, k_norm,
              w_out):
    B, T, _ = x.shape
    f32 = jnp.float32
    cq, ckv, kr = jnp.split(rms_norm(x, norm_g) @ w_in, [Q_LORA, Q_LORA + KV_LORA], axis=-1)
    q = rms_norm((rms_norm(cq, q_a_norm) @ w_uq).reshape(B, T, MLA_HEADS, QK_HEAD), q_norm)
    q_nope, q_rope = q[..., :QK_NOPE], rope(q[..., QK_NOPE:], pos)
    c = rms_norm(ckv, kv_a_norm)
    k_nope = jnp.einsum('btc,chd->bthd', c, w_uk)
    krf = kr.astype(f32)
    ssq = jnp.sum(jnp.square(k_nope.astype(f32)), axis=-1) + jnp.sum(krf * krf, axis=-1)[..., None]
    inv_r = lax.rsqrt(ssq / QK_HEAD + EPS)
    kr_rot = rope(kr * k_norm[QK_NOPE:], pos)
    rows = jnp.concatenate([c, kr_rot, inv_r.astype(c.dtype)], axis=-1)
    if past is None:
        k = jnp.concatenate([k_nope * k_norm[:QK_NOPE],
                             jnp.broadcast_to(kr_rot[:, :, None, :], (B, T, MLA_HEADS, QK_ROPE))],
                            axis=-1) * inv_r[..., None]
        v = jnp.einsum('btc,chv->bthv', c, w_uv)
        o = causal_block_attention(jnp.concatenate([q_nope, q_rope], axis=-1), k, v)
    else:
        qa = jnp.einsum('bthd,chd->bthc', q_nope * k_norm[:QK_NOPE], w_uk)
        s_past = latent_scores(qa, q_rope, past)
        s_new = jnp.where(jnp.tril(jnp.ones((T, T), bool)), latent_scores(qa, q_rope, rows), -jnp.inf)
        p = jax.nn.softmax(jnp.concatenate([s_past, s_new], axis=-1), axis=-1)
        n_past = past.shape[1]
        ctx = (jnp.einsum('bhts,bsc->bthc', p[..., :n_past], past[..., :KV_LORA].astype(f32))
               + jnp.einsum('bhts,bsc->bthc', p[..., n_past:], c.astype(f32)))
        o = jnp.einsum('bthc,chv->bthv', ctx, w_uv)
    y = o.reshape(B, T, MLA_HEADS * V_HEAD).astype(x.dtype) @ w_out
    return x + y, rows


def swiglu_ffn(x, norm_g, w_gate_up, w_down):
    gate, up = jnp.split(rms_norm(x, norm_g) @ w_gate_up, 2, axis=-1)
    return x + (jax.nn.silu(gate) * up) @ w_down


def trunk(x, pos, gdn_S, gdn_conv, ssm_h, ssm_conv, cache_mla, page_table, wa, wc, wf):
    T = x.shape[1]
    chunk = CHUNK if T % CHUNK == 0 else T
    new_S, new_gc, new_h, new_sc, new_rows = [], [], [], [], []
    norm_ffn, w_gate_up, w_down = wf
    for layer in range(DEPTH):
        j = layer // 2
        if layer % 2 == 0:
            x, (S, gc, h, sc) = hybrid_layer(x, gdn_S[j], gdn_conv[j], ssm_h[j], ssm_conv[j], chunk,
                                             *[w[j] for w in wa])
            new_S.append(S)
            new_gc.append(gc)
            new_h.append(h)
            new_sc.append(sc)
        else:
            past = None if cache_mla is None else cache_mla[j, page_table].reshape(
                x.shape[0], -1, MLA_ROW)
            x, rows = mla_layer(x, pos, past, *[w[j] for w in wc])
            new_rows.append(rows)
        x = swiglu_ffn(x, norm_ffn[layer], w_gate_up[layer], w_down[layer])
    return (x, jnp.stack(new_S), jnp.stack(new_gc), jnp.stack(new_h), jnp.stack(new_sc),
            jnp.stack(new_rows))


def setup_inputs(seed: int = 0) -> dict:
    keys = iter(jax.random.split(jax.random.key(seed), 64))
    f32 = jnp.float32

    def normal(shape, scale=1.0):
        return jax.random.normal(next(keys), shape, f32) * scale

    def uniform(shape, lo, hi):
        return jax.random.uniform(next(keys), shape, f32, lo, hi)

    def gain(shape):
        return 1.0 + normal(shape, 0.02)

    def dt_bias(shape):
        dt = jnp.exp(uniform(shape, math.log(1e-3), math.log(1e-1)))
        return dt + jnp.log(-jnp.expm1(-dt))

    na, nc = N_A_LAYERS, N_C_LAYERS
    n_pages = PAST_LEN // PAGE_SIZE
    n_pool = (5 * DEC_BATCH * n_pages) // 4
    x_prompt = normal((BATCH, SEQ, D_MODEL))
    x_sample = normal((DEC_BATCH, DEC_SEQ, D_MODEL))
    state_gdn = normal((na, DEC_BATCH, GDN_HEADS, GDN_DK, GDN_DV), 0.1)
    state_gdn_conv = normal((na, DEC_BATCH, CONV_W - 1, GDN_CONV_CH))
    state_ssm = normal((na, DEC_BATCH, SSM_HEADS, SSM_P, SSM_N), 0.1)
    state_ssm_conv = normal((na, DEC_BATCH, CONV_W - 1, SSM_CONV_CH))
    cache_mla = jnp.concatenate([normal((nc, n_pool, PAGE_SIZE, KV_LORA)),
                                 normal((nc, n_pool, PAGE_SIZE, QK_ROPE)),
                                 uniform((nc, n_pool, PAGE_SIZE, MLA_HEADS), 0.7, 1.3)], axis=-1)
    page_table = jax.random.permutation(next(keys), n_pool)[:DEC_BATCH * n_pages].reshape(
        DEC_BATCH, n_pages).astype(jnp.int32)
    return {
        'x_prompt': x_prompt,
        'x_sample': x_sample,
        'state_gdn': state_gdn,
        'state_gdn_conv': state_gdn_conv,
        'state_ssm': state_ssm,
        'state_ssm_conv': state_ssm_conv,
        'cache_mla': cache_mla,
        'page_table': page_table,
        'norm_mix_a': gain((na, D_MODEL)),
        'w_in_a': normal((na, D_MODEL, IN_A), D_MODEL ** -0.5),
        'conv_gdn_w': normal((na, CONV_W, GDN_CONV_CH), CONV_W ** -0.5),
        'gdn_A_log': jnp.log(uniform((na, GDN_HEADS), 1.0, 16.0)),
        'gdn_dt_bias': dt_bias((na, GDN_HEADS)),
        'gdn_norm': gain((na, GDN_DV)),
        'conv_ssm_w': normal((na, CONV_W, SSM_CONV_CH), CONV_W ** -0.5),
        'conv_ssm_b': normal((na, SSM_CONV_CH), 0.1),
        'ssm_A_log': jnp.log(uniform((na, SSM_HEADS), 1.0, 16.0)),
        'ssm_dt_bias': dt_bias((na, SSM_HEADS)),
        'ssm_D': gain((na, SSM_HEADS)),
        'ssm_norm': gain((na, SSM_INNER)),
        'w_out_a': normal((na, MIX_A, D_MODEL), MIX_A ** -0.5),
        'norm_mix_c': gain((nc, D_MODEL)),
        'w_in_c': normal((nc, D_MODEL, MLA_IN), D_MODEL ** -0.5),
        'q_a_norm': gain((nc, Q_LORA)),
        'kv_a_norm': gain((nc, KV_LORA)),
        'w_uq': normal((nc, Q_LORA, MLA_HEADS * QK_HEAD), Q_LORA ** -0.5),
        'w_uk': normal((nc, KV_LORA, MLA_HEADS, QK_NOPE), KV_LORA ** -0.5),
        'w_uv': normal((nc, KV_LORA, MLA_HEADS, V_HEAD), KV_LORA ** -0.5),
        'q_norm': gain((nc, QK_HEAD)),
        'k_norm': gain((nc, QK_HEAD)),
        'w_out_c': normal((nc, MLA_HEADS * V_HEAD, D_MODEL), (MLA_HEADS * V_HEAD) ** -0.5),
        'norm_ffn': gain((DEPTH, D_MODEL)),
        'w_gate_up': normal((DEPTH, D_MODEL, 2 * D_FF), D_MODEL ** -0.5),
        'w_down': normal((DEPTH, D_FF, D_MODEL), D_FF ** -0.5),
    }


def reference(x_prompt, x_sample, state_gdn, state_gdn_conv, state_ssm, state_ssm_conv, cache_mla,
              page_table, norm_mix_a, w_in_a, conv_gdn_w, gdn_A_log, gdn_dt_bias, gdn_norm,
              conv_ssm_w, conv_ssm_b, ssm_A_log, ssm_dt_bias, ssm_D, ssm_norm, w_out_a,
              norm_mix_c, w_in_c, q_a_norm, kv_a_norm, w_uq, w_uk, w_uv, q_norm, k_norm, w_out_c,
              norm_ffn, w_gate_up, w_down):
    wa = (norm_mix_a, w_in_a, conv_gdn_w, gdn_A_log, gdn_dt_bias, gdn_norm, conv_ssm_w,
          conv_ssm_b, ssm_A_log, ssm_dt_bias, ssm_D, ssm_norm, w_out_a)
    wc = (norm_mix_c, w_in_c, q_a_norm, kv_a_norm, w_uq, w_uk, w_uv, q_norm, k_norm, w_out_c)
    wf = (norm_ffn, w_gate_up, w_down)
    b, t, _ = x_prompt.shape
    zero_S = jnp.zeros((N_A_LAYERS, b, GDN_HEADS, GDN_DK, GDN_DV), jnp.float32)
    zero_gc = jnp.zeros((N_A_LAYERS, b, CONV_W - 1, GDN_CONV_CH), x_prompt.dtype)
    zero_h = jnp.zeros((N_A_LAYERS, b, SSM_HEADS, SSM_P, SSM_N), jnp.float32)
    zero_sc = jnp.zeros((N_A_LAYERS, b, CONV_W - 1, SSM_CONV_CH), x_prompt.dtype)
    y_prompt, p_gdn, p_gdn_conv, p_ssm, p_ssm_conv, p_mla_rows = trunk(
        x_prompt, jnp.arange(t, dtype=jnp.float32), zero_S, zero_gc, zero_h, zero_sc, None, None,
        wa, wc, wf)
    past_len = page_table.shape[1] * PAGE_SIZE
    pos_s = past_len + jnp.arange(x_sample.shape[1], dtype=jnp.float32)
    y_sample, s_gdn, s_gdn_conv, s_ssm, s_ssm_conv, s_mla_rows = trunk(
        x_sample, pos_s, state_gdn, state_gdn_conv, state_ssm, state_ssm_conv, cache_mla,
        page_table, wa, wc, wf)
    return (y_prompt, y_sample, p_gdn, p_gdn_conv, p_ssm, p_ssm_conv, p_mla_rows,
            s_gdn, s_gdn_conv, s_ssm, s_ssm_conv, s_mla_rows)
```

```python
import functools
import math

import jax
import jax.numpy as jnp
import numpy as np
from jax import lax
from jax.experimental import pallas as pl
from jax.experimental.pallas import tpu as pltpu

F32 = jnp.float32
BF16 = jnp.bfloat16

EPS = 1e-6
CONV_W = 4
CHUNK = 64
PAGE_SIZE = 128
GDN_HEADS = 4
GDN_DK = 128
GDN_DV = 128
GDN_QK = GDN_HEADS * GDN_DK
GDN_VW = GDN_HEADS * GDN_DV
GDN_CONV_CH = 2 * GDN_QK + GDN_VW
SSM_HEADS = 8
SSM_P = 64
SSM_N = 128
SSM_GROUPS = 2
SSM_INNER = SSM_HEADS * SSM_P
SSM_BC = SSM_GROUPS * SSM_N
SSM_CONV_CH = SSM_INNER + 2 * SSM_BC
MLA_HEADS = 16
Q_LORA = 512
KV_LORA = 256
QK_NOPE = 64
QK_ROPE = 32
QK_HEAD = QK_NOPE + QK_ROPE
V_HEAD = 64
MLA_ROW = KV_LORA + QK_ROPE + MLA_HEADS
ROPE_THETA = 10000.0

LANES = 128
HEAD_W = MLA_HEADS * LANES
GATE_W = LANES
GATE_A, GATE_B, GATE_DT = 0, GDN_HEADS, 2 * GDN_HEADS
Q_EXT = 3 * LANES
VMEM_LIMIT = 48 * 1024 * 1024

_HI = lax.Precision.HIGHEST


def _cparams(*sem):
    return pltpu.CompilerParams(dimension_semantics=sem, vmem_limit_bytes=VMEM_LIMIT)


def _const_spec(shape):
    zeros = (0,) * len(shape)
    return pl.BlockSpec(shape, lambda *_: zeros)


def _mm(a, b):
    return jnp.dot(a.astype(BF16), b.astype(BF16), preferred_element_type=F32)


def _mm_nt(a, b):
    return lax.dot_general(a.astype(BF16), b.astype(BF16), (((1,), (1,)), ((), ())),
                           preferred_element_type=F32)


def _mm_tn(a, b):
    return lax.dot_general(a.astype(BF16), b.astype(BF16), (((0,), (0,)), ((), ())),
                           preferred_element_type=F32)


def _mm_split(a, sel):
    hi = a.astype(BF16)
    lo = (a - hi.astype(F32)).astype(BF16)
    return (jnp.dot(hi, sel, preferred_element_type=F32)
            + jnp.dot(lo, sel, preferred_element_type=F32))


def _sigmoid(x):
    return 1.0 / (1.0 + jnp.exp(-x))


def _silu(x):
    return x * _sigmoid(x)


def _softplus(x):
    return jnp.maximum(x, 0.0) + jnp.log(1.0 + jnp.exp(-jnp.abs(x)))


def _rms(x, gain):
    return x * lax.rsqrt(jnp.mean(x * x, axis=-1, keepdims=True) + EPS) * gain


def _l2n(x):
    return x * lax.rsqrt(jnp.sum(x * x, axis=-1, keepdims=True) + EPS)


def _iota(shape, dim):
    return lax.broadcasted_iota(jnp.int32, shape, dim)


def _chunk_cumsum(g, tb):
    row, col = _iota((tb, tb), 0), _iota((tb, tb), 1)
    shift = int(math.log2(CHUNK))
    same_chunk = lax.shift_right_logical(row, shift) == lax.shift_right_logical(col, shift)
    tril = jnp.where((col <= row) & same_chunk, 1.0, 0.0).astype(F32)
    return jnp.dot(tril, g, precision=_HI, preferred_element_type=F32)


def _rows_of(cols, lane0):
    sel = jnp.where(_iota((8, LANES), 1) == _iota((8, LANES), 0) + lane0, 1.0, 0.0).astype(F32)
    return lax.dot_general(sel, cols, (((1,), (1,)), ((), ())), precision=_HI,
                           preferred_element_type=F32)


def _unit_lower_inverse(low):
    eye = jnp.where(_iota((CHUNK, CHUNK), 0) == _iota((CHUNK, CHUNK), 1), 1.0, 0.0).astype(F32)
    inv = eye - low
    power = low
    for _ in range(int(math.log2(CHUNK)) - 1):
        power = _mm(power, power)
        inv = inv + _mm(inv, power)
    return inv


def _causal_conv_block(x_ref, xp_ref, cw, tb, first):
    @pl.when(first)
    def _():
        xp_ref[0:8, :] = jnp.zeros((8, xp_ref.shape[1]), F32)

    xp_ref[8:8 + tb, :] = x_ref[...]
    off = 8 - (CONV_W - 1)
    y = cw[0:1] * xp_ref[off:off + tb, :]
    for i in range(1, CONV_W):
        y = y + cw[i:i + 1] * xp_ref[off + i:off + i + tb, :]
    xp_ref[0:8, :] = xp_ref[tb:tb + 8, :]
    return y


def _norm_proj_body(n_out, x_ref, g_ref, *refs):
    xn = _rms(x_ref[...], g_ref[...]).astype(BF16)
    for w_ref, o_ref in zip(refs[:n_out], refs[n_out:]):
        o_ref[...] = jnp.dot(xn, w_ref[...], preferred_element_type=F32)


def _norm_proj(x, gain, weights, tm):
    rows, d = x.shape
    n = len(weights)
    return pl.pallas_call(
        functools.partial(_norm_proj_body, n),
        grid=(rows // tm,),
        in_specs=[pl.BlockSpec((tm, d), lambda i: (i, 0)), _const_spec((1, d))]
        + [_const_spec(w.shape) for w in weights],
        out_specs=[pl.BlockSpec((tm, w.shape[1]), lambda i: (i, 0)) for w in weights],
        out_shape=[jax.ShapeDtypeStruct((rows, w.shape[1]), F32) for w in weights],
        compiler_params=_cparams("parallel"),
        name="norm_proj",
    )(x, gain.reshape(1, d), *weights)


def _gdn_prompt_body(nc, qkv_ref, gz_ref, sm_ref, cw_ref, gp_ref, gn_ref, o_ref, s_ref, xp_ref):
    tb = nc * CHUNK
    first = pl.program_id(1) == 0

    @pl.when(first)
    def _():
        s_ref[...] = jnp.zeros(s_ref.shape, F32)

    y = _silu(_causal_conv_block(qkv_ref.at[0], xp_ref, cw_ref[...], tb, first))
    sm, gp = sm_ref[0], gp_ref[...]
    g_all = -jnp.exp(gp[0:1]) * _softplus(sm + gp[1:2])
    beta_all = _sigmoid(sm)
    cum_all = _chunk_cumsum(g_all, tb)
    cum_rows = _rows_of(cum_all, GATE_A)
    incl = _iota((CHUNK, CHUNK), 1) <= _iota((CHUNK, CHUNK), 0)
    strict = _iota((CHUNK, CHUNK), 1) < _iota((CHUNK, CHUNK), 0)
    gn = gn_ref[...]
    for h in range(GDN_HEADS):
        lo = h * GDN_DK
        q_h = _l2n(y[:, lo:lo + GDN_DK]) * GDN_DK ** -0.5
        k_h = _l2n(y[:, GDN_QK + lo:GDN_QK + lo + GDN_DK])
        v_h = y[:, 2 * GDN_QK + lo:2 * GDN_QK + lo + GDN_DV]
        for c in range(nc):
            r = slice(c * CHUNK, (c + 1) * CHUNK)
            q, k, v = q_h[r], k_h[r], v_h[r]
            cum_c = cum_all[r, GATE_A + h:GATE_A + h + 1]
            cum_r = cum_rows[h:h + 1, c * CHUNK:(c + 1) * CHUNK]
            beta = beta_all[r, GATE_B + h:GATE_B + h + 1]
            decay = jnp.where(incl, jnp.exp(jnp.minimum(cum_c - cum_r, 0.0)), 0.0)
            e_c = jnp.exp(cum_c)
            kb = k * beta
            low = jnp.where(strict, _mm_nt(kb, k) * decay, 0.0)
            rhs = jnp.concatenate([v * beta, kb * e_c], axis=1)
            sol = _mm(_unit_lower_inverse(low), rhs)
            u, w = sol[:, :GDN_DV], sol[:, GDN_DV:]
            state = s_ref[0, h]
            v_new = u - _mm(w, state)
            attn = _mm_nt(q, k) * decay
            o = _mm(q * e_c, state) + _mm(attn, v_new)
            last = cum_c[CHUNK - 1:CHUNK]
            s_ref[0, h] = state * jnp.exp(last) + _mm_tn(k * jnp.exp(last - cum_c), v_new)
            gate = _silu(gz_ref[0, r, lo:lo + GDN_DV])
            o_ref[0, r, lo:lo + GDN_DV] = (_rms(o, gn) * gate).astype(o_ref.dtype)


def _gdn_prompt(qkv, gz, small, conv_w, gate_par, gdn_norm, nc):
    b, t, _ = qkv.shape
    tb = nc * CHUNK
    blk = lambda w: pl.BlockSpec((1, tb, w), lambda i, j: (i, j, 0))
    return pl.pallas_call(
        functools.partial(_gdn_prompt_body, nc),
        grid=(b, t // tb),
        in_specs=[blk(GDN_CONV_CH), blk(GDN_VW), blk(GATE_W), _const_spec(conv_w.shape),
                  _const_spec(gate_par.shape), _const_spec((1, GDN_DV))],
        out_specs=[blk(GDN_VW),
                   pl.BlockSpec((1, GDN_HEADS, GDN_DK, GDN_DV), lambda i, j: (i, 0, 0, 0))],
        out_shape=[jax.ShapeDtypeStruct((b, t, GDN_VW), BF16),
                   jax.ShapeDtypeStruct((b, GDN_HEADS, GDN_DK, GDN_DV), F32)],
        scratch_shapes=[pltpu.VMEM((tb + 8, GDN_CONV_CH), F32)],
        compiler_params=_cparams("parallel", "arbitrary"),
        name="gdn_prompt",
    )(qkv, gz, small, conv_w, gate_par, gdn_norm.reshape(1, GDN_DV))


def _ssd_prompt_body(nc, xbc_ref, sz_ref, sm_ref, cw_ref, cb_ref, sp_ref, sn_ref, o_ref, h_ref,
                     xp_ref):
    tb = nc * CHUNK
    first = pl.program_id(1) == 0

    @pl.when(first)
    def _():
        h_ref[...] = jnp.zeros(h_ref.shape, F32)

    y = _silu(_causal_conv_block(xbc_ref.at[0], xp_ref, cw_ref[...], tb, first) + cb_ref[...])
    sm, sp = sm_ref[0], sp_ref[...]
    dt_all = _softplus(sm + sp[1:2])
    cum_all = _chunk_cumsum(dt_all * -jnp.exp(sp[0:1]), tb)
    cum_rows = _rows_of(cum_all, GATE_DT)
    incl = _iota((CHUNK, CHUNK), 1) <= _iota((CHUNK, CHUNK), 0)
    lane_lo = _iota((CHUNK, LANES), 1) < SSM_P
    row_lo = _iota((2 * SSM_P, SSM_N), 0) < SSM_P
    par_lo = _iota((1, LANES), 1) < SSM_P
    sn = sn_ref[...]
    pairs_per_group = SSM_HEADS // SSM_GROUPS // 2
    for c in range(nc):
        r = slice(c * CHUNK, (c + 1) * CHUNK)
        for grp in range(SSM_GROUPS):
            b_g = y[r, SSM_INNER + grp * SSM_N:SSM_INNER + (grp + 1) * SSM_N]
            c_g = y[r, SSM_INNER + SSM_BC + grp * SSM_N:SSM_INNER + SSM_BC + (grp + 1) * SSM_N]
            scores = _mm_nt(c_g, b_g)
            outs = []
            for jj in range(pairs_per_group):
                j = grp * pairs_per_group + jj
                cums, segs, dts = [], [], []
                for head in (2 * j, 2 * j + 1):
                    cum_c = cum_all[r, GATE_DT + head:GATE_DT + head + 1]
                    cum_r = cum_rows[head:head + 1, c * CHUNK:(c + 1) * CHUNK]
                    cums.append(cum_c)
                    segs.append(jnp.where(incl, jnp.exp(jnp.minimum(cum_c - cum_r, 0.0)), 0.0))
                    dts.append(dt_all[r, GATE_DT + head:GATE_DT + head + 1])
                x_pair = y[r, j * LANES:(j + 1) * LANES]
                xdt = x_pair * jnp.where(lane_lo, dts[0], dts[1])
                y_pair = (_mm(scores * segs[0], jnp.where(lane_lo, xdt, 0.0))
                          + _mm(scores * segs[1], jnp.where(lane_lo, 0.0, xdt)))
                h_pair = h_ref[0, j]
                y_pair = y_pair + _mm_nt(c_g, h_pair) * jnp.where(
                    lane_lo, jnp.exp(cums[0]), jnp.exp(cums[1]))
                last = [cm[CHUNK - 1:CHUNK] for cm in cums]
                dec = jnp.where(lane_lo, jnp.exp(last[0] - cums[0]), jnp.exp(last[1] - cums[1]))
                h_ref[0, j] = (h_pair * jnp.where(row_lo, jnp.exp(last[0]), jnp.exp(last[1]))
                               + _mm_tn(xdt * dec, b_g))
                d_pair = jnp.where(par_lo, sp[2:3, GATE_DT + 2 * j:GATE_DT + 2 * j + 1],
                                   sp[2:3, GATE_DT + 2 * j + 1:GATE_DT + 2 * j + 2])
                outs.append((y_pair + d_pair * x_pair)
                            * _silu(sz_ref[0, r, j * LANES:(j + 1) * LANES]))
            gw = SSM_INNER // SSM_GROUPS
            ms = sum(jnp.sum(o * o, axis=-1, keepdims=True) for o in outs) / gw
            inv = lax.rsqrt(ms + EPS)
            for jj, o in enumerate(outs):
                lo = grp * gw + jj * LANES
                o_ref[0, r, lo:lo + LANES] = (o * inv * sn[:, lo:lo + LANES]).astype(o_ref.dtype)


def _ssd_prompt(xbc, sz, small, conv_w, conv_b, ssm_par, ssm_norm, nc):
    b, t, _ = xbc.shape
    tb = nc * CHUNK
    blk = lambda w: pl.BlockSpec((1, tb, w), lambda i, j: (i, j, 0))
    n_pair = SSM_HEADS // 2
    return pl.pallas_call(
        functools.partial(_ssd_prompt_body, nc),
        grid=(b, t // tb),
        in_specs=[blk(SSM_CONV_CH), blk(SSM_INNER), blk(GATE_W), _const_spec(conv_w.shape),
                  _const_spec((1, SSM_CONV_CH)), _const_spec(ssm_par.shape),
                  _const_spec((1, SSM_INNER))],
        out_specs=[blk(SSM_INNER),
                   pl.BlockSpec((1, n_pair, 2 * SSM_P, SSM_N), lambda i, j: (i, 0, 0, 0))],
        out_shape=[jax.ShapeDtypeStruct((b, t, SSM_INNER), BF16),
                   jax.ShapeDtypeStruct((b, n_pair, 2 * SSM_P, SSM_N), F32)],
        scratch_shapes=[pltpu.VMEM((tb + 8, SSM_CONV_CH), F32)],
        compiler_params=_cparams("parallel", "arbitrary"),
        name="ssd_prompt",
    )(xbc, sz, small, conv_w, conv_b.reshape(1, -1), ssm_par, ssm_norm.reshape(1, -1))


def _col_bcast(row):
    return jnp.transpose(jnp.broadcast_to(row, (LANES, LANES)))


def _gdn_decode_body(qkv_ref, gz_ref, sm_ref, cv_ref, s0_ref, cw_ref, gp_ref, gn_ref, o_ref,
                     s_ref):
    cw = cw_ref[...]
    y = jnp.sum(cw[0:CONV_W - 1] * cv_ref[0], axis=0, keepdims=True) + cw[CONV_W - 1:] * qkv_ref[0]
    y = _silu(y)
    sm, gp = sm_ref[0], gp_ref[...]
    a_all = jnp.exp(-jnp.exp(gp[0:1]) * _softplus(sm + gp[1:2]))
    beta_all = _sigmoid(sm)
    gn = gn_ref[...]
    for h in range(GDN_HEADS):
        lo = h * GDN_DK
        q = _l2n(y[:, lo:lo + GDN_DK]) * GDN_DK ** -0.5
        k = _l2n(y[:, GDN_QK + lo:GDN_QK + lo + GDN_DK])
        v = y[:, 2 * GDN_QK + lo:2 * GDN_QK + lo + GDN_DV]
        a = a_all[:, GATE_A + h:GATE_A + h + 1]
        beta = beta_all[:, GATE_B + h:GATE_B + h + 1]
        k_col, q_col = _col_bcast(k), _col_bcast(q)
        decayed = s0_ref[0, h] * a
        v_new = beta * (v - jnp.sum(k_col * decayed, axis=0, keepdims=True))
        state = decayed + k_col * v_new
        s_ref[0, h] = state
        o = jnp.sum(q_col * state, axis=0, keepdims=True)
        o_ref[0, :, lo:lo + GDN_DV] = (_rms(o, gn) * _silu(gz_ref[0, :, lo:lo + GDN_DV])
                                       ).astype(o_ref.dtype)


def _gdn_decode(qkv, gz, small, conv0, s0, conv_w, gate_par, gdn_norm):
    b = qkv.shape[0]
    row = lambda w: pl.BlockSpec((1, 1, w), lambda i: (i, 0, 0))
    st = pl.BlockSpec((1, GDN_HEADS, GDN_DK, GDN_DV), lambda i: (i, 0, 0, 0))
    return pl.pallas_call(
        _gdn_decode_body,
        grid=(b,),
        in_specs=[row(GDN_CONV_CH), row(GDN_VW), row(GATE_W),
                  pl.BlockSpec((1, CONV_W - 1, GDN_CONV_CH), lambda i: (i, 0, 0)), st,
                  _const_spec(conv_w.shape), _const_spec(gate_par.shape), _const_spec((1, GDN_DV))],
        out_specs=[row(GDN_VW), st],
        out_shape=[jax.ShapeDtypeStruct((b, 1, GDN_VW), BF16),
                   jax.ShapeDtypeStruct(s0.shape, F32)],
        compiler_params=_cparams("parallel"),
        name="gdn_decode",
    )(qkv, gz, small, conv0, s0, conv_w, gate_par, gdn_norm.reshape(1, GDN_DV))


def _ssd_decode_body(xbc_ref, sz_ref, sm_ref, cv_ref, h0_ref, cw_ref, cb_ref, sp_ref, sn_ref,
                     o_ref, h_ref):
    cw = cw_ref[...]
    y = (jnp.sum(cw[0:CONV_W - 1] * cv_ref[0], axis=0, keepdims=True)
         + cw[CONV_W - 1:] * xbc_ref[0] + cb_ref[...])
    y = _silu(y)
    sm, sp = sm_ref[0], sp_ref[...]
    dt_all = _softplus(sm + sp[1:2])
    da_all = jnp.exp(dt_all * -jnp.exp(sp[0:1]))
    par_lo = _iota((1, LANES), 1) < SSM_P
    row_lo = _iota((2 * SSM_P, SSM_N), 0) < SSM_P
    sn = sn_ref[...]
    pairs_per_group = SSM_HEADS // SSM_GROUPS // 2
    gw = SSM_INNER // SSM_GROUPS
    for grp in range(SSM_GROUPS):
        b_g = y[:, SSM_INNER + grp * SSM_N:SSM_INNER + (grp + 1) * SSM_N]
        c_g = y[:, SSM_INNER + SSM_BC + grp * SSM_N:SSM_INNER + SSM_BC + (grp + 1) * SSM_N]
        outs = []
        for jj in range(pairs_per_group):
            j = grp * pairs_per_group + jj
            la, lb = GATE_DT + 2 * j, GATE_DT + 2 * j + 1
            x_pair = y[:, j * LANES:(j + 1) * LANES]
            xdt = x_pair * jnp.where(par_lo, dt_all[:, la:la + 1], dt_all[:, lb:lb + 1])
            state = (h0_ref[0, j] * jnp.where(row_lo, da_all[:, la:la + 1], da_all[:, lb:lb + 1])
                     + _col_bcast(xdt) * b_g)
            h_ref[0, j] = state
            y_col = jnp.sum(state * c_g, axis=1, keepdims=True)
            y_row = jnp.transpose(jnp.broadcast_to(y_col, (LANES, LANES)))[0:1]
            d_pair = jnp.where(par_lo, sp[2:3, la:la + 1], sp[2:3, lb:lb + 1])
            outs.append((y_row + d_pair * x_pair) * _silu(sz_ref[0, :, j * LANES:(j + 1) * LANES]))
        ms = sum(jnp.sum(o * o, axis=-1, keepdims=True) for o in outs) / gw
        inv = lax.rsqrt(ms + EPS)
        for jj, o in enumerate(outs):
            lo = grp * gw + jj * LANES
            o_ref[0, :, lo:lo + LANES] = (o * inv * sn[:, lo:lo + LANES]).astype(o_ref.dtype)


def _ssd_decode(xbc, sz, small, conv0, h0, conv_w, conv_b, ssm_par, ssm_norm):
    b = xbc.shape[0]
    n_pair = SSM_HEADS // 2
    row = lambda w: pl.BlockSpec((1, 1, w), lambda i: (i, 0, 0))
    st = pl.BlockSpec((1, n_pair, 2 * SSM_P, SSM_N), lambda i: (i, 0, 0, 0))
    return pl.pallas_call(
        _ssd_decode_body,
        grid=(b,),
        in_specs=[row(SSM_CONV_CH), row(SSM_INNER), row(GATE_W),
                  pl.BlockSpec((1, CONV_W - 1, SSM_CONV_CH), lambda i: (i, 0, 0)), st,
                  _const_spec(conv_w.shape), _const_spec((1, SSM_CONV_CH)),
                  _const_spec(ssm_par.shape), _const_spec((1, SSM_INNER))],
        out_specs=[row(SSM_INNER), st],
        out_shape=[jax.ShapeDtypeStruct((b, 1, SSM_INNER), BF16),
                   jax.ShapeDtypeStruct(h0.shape, F32)],
        compiler_params=_cparams("parallel"),
        name="ssd_decode",
    )(xbc, sz, small, conv0, h0, conv_w, conv_b.reshape(1, -1), ssm_par, ssm_norm.reshape(1, -1))


def _res_ffn_body(n_a, n_ff, x_ref, *refs):
    a_refs, w_refs = refs[:n_a], refs[n_a:2 * n_a]
    g_ref, wg_ref, wu_ref, wd_ref, o_ref, xn_ref = refs[2 * n_a:]
    x1 = x_ref[...]
    for a_ref, w_ref in zip(a_refs, w_refs):
        x1 = x1 + jnp.dot(a_ref[...], w_ref[...], preferred_element_type=F32)
    xn_ref[...] = _rms(x1, g_ref[...]).astype(BF16)
    o_ref[...] = x1

    def step(c, carry):
        xn = xn_ref[...]
        gate = jnp.dot(xn, wg_ref[c], preferred_element_type=F32)
        up = jnp.dot(xn, wu_ref[c], preferred_element_type=F32)
        hid = (_silu(gate) * up).astype(BF16)
        o_ref[...] += jnp.dot(hid, wd_ref[c], preferred_element_type=F32)
        return carry

    lax.fori_loop(0, n_ff, step, 0)


def _res_ffn(x, acts, projs, gain, wg, wu, wd, tm):
    rows, d = x.shape
    n_a, n_ff = len(acts), wg.shape[0]
    return pl.pallas_call(
        functools.partial(_res_ffn_body, n_a, n_ff),
        grid=(rows // tm,),
        in_specs=[pl.BlockSpec((tm, d), lambda i: (i, 0))]
        + [pl.BlockSpec((tm, a.shape[1]), lambda i: (i, 0)) for a in acts]
        + [_const_spec(p.shape) for p in projs]
        + [_const_spec((1, d)), _const_spec(wg.shape), _const_spec(wu.shape), _const_spec(wd.shape)],
        out_specs=pl.BlockSpec((tm, d), lambda i: (i, 0)),
        out_shape=jax.ShapeDtypeStruct((rows, d), F32),
        scratch_shapes=[pltpu.VMEM((tm, d), BF16)],
        compiler_params=_cparams("parallel"),
        name="res_ffn",
    )(x, *acts, *projs, gain.reshape(1, d), wg, wu, wd)


def _mla_proj_body(tm, seq, pos0, x_ref, g_ref, wcq_ref, wckv_ref, wkr_ref, qan_ref, kvn_ref,
                   wuq_ref, wuk_ref, wuv_ref, gq_ref, gk_ref, gkr_ref, segq_ref, segqt_ref,
                   segk_ref, segkt_ref, q_ref, k_ref, v_ref, rows_ref):
    xn = _rms(x_ref[...], g_ref[...]).astype(BF16)
    lane = _iota((tm, LANES), 1)
    if seq == 1:
        pos = jnp.full((tm, LANES), pos0, F32)
    else:
        base = (pl.program_id(0) * tm) % seq + pos0
        pos = (base + _iota((tm, LANES), 0)).astype(F32)
    half = QK_ROPE // 2
    inv_freq = jnp.exp((lane & (half - 1)).astype(F32) * (-math.log(ROPE_THETA) / half))
    cos_t, sin_t = jnp.cos(pos * inv_freq), jnp.sin(pos * inv_freq)

    kr = jnp.dot(xn, wkr_ref[...], preferred_element_type=F32)
    ssq_kr = jnp.sum(jnp.where(lane < QK_ROPE, kr * kr, 0.0), axis=-1, keepdims=True)
    krg = kr * gkr_ref[...]
    kr_rot = krg * cos_t + pltpu.roll(krg, LANES - QK_ROPE, 1) * sin_t

    c = _rms(jnp.dot(xn, wckv_ref[...], preferred_element_type=F32), kvn_ref[...])
    cb = c.astype(BF16)
    kx = jnp.dot(cb, wuk_ref[...], preferred_element_type=F32)
    ssq_k = _mm_split(kx * kx, segk_ref[...])
    inv_r = lax.rsqrt((ssq_k + ssq_kr) / QK_HEAD + EPS)
    tail = jnp.where(lane < QK_ROPE, kr_rot, inv_r)
    rows_ref[:, 0:KV_LORA] = c
    rows_ref[:, KV_LORA:MLA_ROW] = tail[:, 0:MLA_ROW - KV_LORA]
    inv_b = _mm_split(inv_r, segkt_ref[...])
    kr_put = jnp.where((lane >= QK_NOPE) & (lane < QK_HEAD), pltpu.roll(kr_rot, QK_NOPE, 1), 0.0)
    gk = gk_ref[...]
    for h in range(MLA_HEADS):
        hs = slice(h * LANES, (h + 1) * LANES)
        k_ref[:, hs] = ((kx[:, hs] * gk + kr_put) * inv_b[:, hs]).astype(k_ref.dtype)
    v_ref[...] = jnp.dot(cb, wuv_ref[...], preferred_element_type=F32).astype(v_ref.dtype)

    cq = _rms(jnp.dot(xn, wcq_ref[...], preferred_element_type=F32), qan_ref[...]).astype(BF16)
    qx = jnp.dot(cq, wuq_ref[...], preferred_element_type=F32)
    inv_q = lax.rsqrt(_mm_split(qx * qx, segq_ref[...]) / QK_HEAD + EPS)
    qn = qx * _mm_split(inv_q, segqt_ref[...]) * gq_ref[...]
    scale = QK_HEAD ** -0.5
    in_rope = (lane >= QK_NOPE) & (lane < QK_HEAD)
    keep = jnp.where(lane < QK_NOPE, scale, jnp.where(in_rope, cos_t * scale, 0.0))
    swap = jnp.where(in_rope, sin_t * scale, 0.0)
    for h in range(MLA_HEADS):
        hs = slice(h * LANES, (h + 1) * LANES)
        t = qn[:, hs]
        q_ref[:, hs] = (t * keep + pltpu.roll(t, LANES - QK_ROPE, 1) * swap).astype(q_ref.dtype)


def _mla_proj(x, seq, pos0, w, tm, q_dtype):
    rows, d = x.shape
    names = ("norm", "wcq", "wckv", "wkr", "qan", "kvn", "wuq", "wuk", "wuv", "gq", "gk", "gkr",
             "segq", "segqt", "segk", "segkt")
    consts = [w[n] for n in names]
    out_w = (HEAD_W, HEAD_W, MLA_HEADS * V_HEAD, MLA_ROW)
    out_dt = (q_dtype, BF16, BF16, F32)
    return pl.pallas_call(
        functools.partial(_mla_proj_body, tm, seq, pos0),
        grid=(rows // tm,),
        in_specs=[pl.BlockSpec((tm, d), lambda i: (i, 0))] + [_const_spec(c.shape) for c in consts],
        out_specs=[pl.BlockSpec((tm, ow), lambda i: (i, 0)) for ow in out_w],
        out_shape=[jax.ShapeDtypeStruct((rows, ow), dt) for ow, dt in zip(out_w, out_dt)],
        compiler_params=_cparams("parallel"),
        name="mla_proj",
    )(x, *consts)


def _flash_body(tq, q_ref, k_ref, v_ref, o_ref):
    qi = pl.program_id(2)
    causal = _iota((tq, tq), 1) <= _iota((tq, tq), 0)
    outs = []
    for j in range(2):
        hs = slice(j * LANES, (j + 1) * LANES)
        q = q_ref[0, :, hs]

        def block(kb, carry, diag, hs=hs, q=q):
            m, l, acc = carry
            k = k_ref[0, pl.ds(pl.multiple_of(kb * tq, tq), tq), hs]
            v = v_ref[0, pl.ds(pl.multiple_of(kb * tq, tq), tq), :]
            s = lax.dot_general(q, k, (((1,), (1,)), ((), ())), preferred_element_type=F32)
            if diag:
                s = jnp.where(causal, s, -jnp.inf)
            m_new = jnp.maximum(m, jnp.max(s, axis=-1, keepdims=True))
            alpha = jnp.exp(m - m_new)
            p = jnp.exp(s - m_new)
            l = alpha * l + jnp.sum(p, axis=-1, keepdims=True)
            acc = alpha * acc + jnp.dot(p.astype(BF16), v, preferred_element_type=F32)
            return m_new, l, acc

        init = (jnp.full((tq, 1), -jnp.inf, F32), jnp.zeros((tq, 1), F32),
                jnp.zeros((tq, LANES), F32))
        carry = lax.fori_loop(0, qi, lambda kb, cr: block(kb, cr, False), init)
        _, l, acc = block(qi, carry, True)
        outs.append(acc / l)
    o_ref[0] = jnp.where(_iota((tq, LANES), 1) < V_HEAD, outs[0], outs[1]).astype(o_ref.dtype)


def _flash_attention(q, k, v, tq):
    b, t, _ = q.shape
    return pl.pallas_call(
        functools.partial(_flash_body, tq),
        grid=(b, MLA_HEADS // 2, t // tq),
        in_specs=[pl.BlockSpec((1, tq, 2 * LANES), lambda i, j, n: (i, n, j)),
                  pl.BlockSpec((1, t, 2 * LANES), lambda i, j, n: (i, 0, j)),
                  pl.BlockSpec((1, t, LANES), lambda i, j, n: (i, 0, j))],
        out_specs=pl.BlockSpec((1, tq, LANES), lambda i, j, n: (i, n, j)),
        out_shape=jax.ShapeDtypeStruct((b, t, MLA_HEADS * V_HEAD), BF16),
        compiler_params=_cparams("parallel", "parallel", "arbitrary"),
        name="flash_attention",
    )(q, k, v)


def _absorb_body(q_ref, g_ref, w_ref, o_ref):
    t = (q_ref[...] * g_ref[...]).astype(BF16)
    o_ref[...] = jnp.dot(t, w_ref[...], preferred_element_type=F32).astype(o_ref.dtype)


def _absorb(q, gain, w_abs):
    b = q.shape[0]
    return pl.pallas_call(
        _absorb_body,
        grid=(MLA_HEADS,),
        in_specs=[pl.BlockSpec((b, LANES), lambda h: (0, h)), _const_spec((1, LANES)),
                  pl.BlockSpec((None, LANES, Q_EXT), lambda h: (h, 0, 0))],
        out_specs=pl.BlockSpec((None, b, Q_EXT), lambda h: (h, 0, 0)),
        out_shape=jax.ShapeDtypeStruct((MLA_HEADS, b, Q_EXT), BF16),
        compiler_params=_cparams("parallel"),
        name="mla_absorb",
    )(q, gain, w_abs)


def _paged_attn_body(n_pages, pt_ref, q_ref, rn_ref, *refs):
    del pt_ref
    page_refs, o_ref, s_ref = refs[:n_pages], refs[n_pages], refs[n_pages + 1]
    q = q_ref[0]
    inv_lo = MLA_ROW - LANES
    for p in range(n_pages):
        page = page_refs[p]
        s = lax.dot_general(q[:, :MLA_ROW], page[...].astype(BF16), (((1,), (1,)), ((), ())),
                            preferred_element_type=F32)
        inv_t = jnp.transpose(page[:, inv_lo:MLA_ROW])[LANES - MLA_HEADS:LANES]
        s_ref[:, p * PAGE_SIZE:(p + 1) * PAGE_SIZE] = s * inv_t
    rn = rn_ref[0]
    qf = q[:, :MLA_ROW].astype(F32)
    pick = _iota((MLA_HEADS, MLA_ROW), 1) == _iota((MLA_HEADS, MLA_ROW), 0) + (MLA_ROW - MLA_HEADS)
    inv_new = jnp.sum(jnp.where(pick, rn, 0.0), axis=-1, keepdims=True)
    s_new = jnp.sum(qf * rn, axis=-1, keepdims=True) * inv_new
    s_all = s_ref[...]
    m = jnp.maximum(jnp.max(s_all, axis=-1, keepdims=True), s_new)
    e_new = jnp.exp(s_new - m)
    den = e_new
    ctx = e_new * rn[:, 0:KV_LORA]
    for p in range(n_pages):
        e = jnp.exp(s_ref[:, p * PAGE_SIZE:(p + 1) * PAGE_SIZE] - m)
        den = den + jnp.sum(e, axis=-1, keepdims=True)
        ctx = ctx + jnp.dot(e.astype(BF16), page_refs[p][:, 0:KV_LORA].astype(BF16),
                            preferred_element_type=F32)
    o_ref[0] = ctx / den


def _paged_attention(q_ext, rows_new, cache, page_table):
    b, n_pages = page_table.shape
    page_specs = [pl.BlockSpec((None, PAGE_SIZE, MLA_ROW),
                               functools.partial(lambda p, i, pt: (pt[i, p], 0, 0), p))
                  for p in range(n_pages)]
    grid_spec = pltpu.PrefetchScalarGridSpec(
        num_scalar_prefetch=1,
        grid=(b,),
        in_specs=[pl.BlockSpec((1, MLA_HEADS, Q_EXT), lambda i, pt: (i, 0, 0)),
                  pl.BlockSpec((1, 1, MLA_ROW), lambda i, pt: (i, 0, 0))] + page_specs,
        out_specs=pl.BlockSpec((1, MLA_HEADS, KV_LORA), lambda i, pt: (i, 0, 0)),
        scratch_shapes=[pltpu.VMEM((MLA_HEADS, n_pages * PAGE_SIZE), F32)],
    )
    return pl.pallas_call(
        functools.partial(_paged_attn_body, n_pages),
        grid_spec=grid_spec,
        out_shape=jax.ShapeDtypeStruct((b, MLA_HEADS, KV_LORA), F32),
        compiler_params=_cparams("arbitrary"),
        name="paged_attention",
    )(page_table, q_ext, rows_new, *([cache] * n_pages))


def _ctx_out_body(c_ref, w_ref, o_ref):
    o_ref[...] = (jnp.dot(c_ref[0].astype(BF16), w_ref[0], preferred_element_type=F32)
                  + jnp.dot(c_ref[1].astype(BF16), w_ref[1], preferred_element_type=F32)
                  ).astype(o_ref.dtype)


def _ctx_out(ctx_t, w_pairs):
    b = ctx_t.shape[1]
    return pl.pallas_call(
        _ctx_out_body,
        grid=(MLA_HEADS // 2,),
        in_specs=[pl.BlockSpec((2, b, KV_LORA), lambda j: (j, 0, 0)),
                  pl.BlockSpec((None, 2, KV_LORA, LANES), lambda j: (j, 0, 0, 0))],
        out_specs=pl.BlockSpec((b, LANES), lambda j: (0, j)),
        out_shape=jax.ShapeDtypeStruct((b, MLA_HEADS * V_HEAD), BF16),
        compiler_params=_cparams("parallel"),
        name="mla_ctx_out",
    )(ctx_t, w_pairs)


def _gate_tile(rows):
    tile = jnp.zeros((8, GATE_W), F32)
    for i, (off, vec) in enumerate(rows):
        tile = tile.at[i, off:off + vec.shape[0]].set(vec.astype(F32))
    return tile


def _prep_hybrid(j, w_in_a, conv_gdn_w, gdn_A_log, gdn_dt_bias, conv_ssm_w, ssm_A_log,
                 ssm_dt_bias, ssm_D):
    widths = (GDN_CONV_CH, GDN_VW, GDN_HEADS, GDN_HEADS, SSM_INNER, SSM_CONV_CH, SSM_HEADS)
    offs = np.concatenate([[0], np.cumsum(widths)])
    w = w_in_a[j]
    d = w.shape[0]
    part = lambda i: w[:, offs[i]:offs[i + 1]]
    small = jnp.zeros((d, GATE_W), F32)
    small = small.at[:, GATE_A:GATE_A + GDN_HEADS].set(part(2))
    small = small.at[:, GATE_B:GATE_B + GDN_HEADS].set(part(3))
    small = small.at[:, GATE_DT:GATE_DT + SSM_HEADS].set(part(6))
    return dict(
        w_in=[part(0).astype(BF16), part(1).astype(BF16), part(4).astype(BF16),
              part(5).astype(BF16), small.astype(BF16)],
        conv_gdn=conv_gdn_w[j], conv_ssm=conv_ssm_w[j],
        gate_gdn=_gate_tile([(GATE_A, gdn_A_log[j]), (GATE_A, gdn_dt_bias[j])]),
        gate_ssm=_gate_tile([(GATE_DT, ssm_A_log[j]), (GATE_DT, ssm_dt_bias[j]),
                             (GATE_DT, ssm_D[j])]),
    )


def _prep_mla(j, norm_mix_c, w_in_c, q_a_norm, kv_a_norm, w_uq, w_uk, w_uv, q_norm, k_norm):
    half = QK_ROPE // 2
    w_in = w_in_c[j]
    d = w_in.shape[0]

    def swapped(cols):
        return jnp.concatenate([-cols[..., half:], cols[..., :half]], axis=-1)

    def swapped_gain(g):
        return jnp.concatenate([g[half:], g[:half]])

    w_kr = w_in[:, Q_LORA + KV_LORA:]
    wkr = jnp.zeros((d, LANES), F32).at[:, :QK_ROPE].set(w_kr).at[:, QK_ROPE:2 * QK_ROPE].set(
        swapped(w_kr))
    qn, kn = q_norm[j], k_norm[j]
    gkr = jnp.zeros((1, LANES), F32).at[0, :QK_ROPE].set(kn[QK_NOPE:]).at[
        0, QK_ROPE:2 * QK_ROPE].set(swapped_gain(kn[QK_NOPE:]))
    gk = jnp.zeros((1, LANES), F32).at[0, :QK_NOPE].set(kn[:QK_NOPE])
    gq_head = jnp.concatenate([qn, swapped_gain(qn[QK_NOPE:])])
    gq = jnp.tile(gq_head, MLA_HEADS).reshape(1, HEAD_W)

    uq = w_uq[j].reshape(Q_LORA, MLA_HEADS, QK_HEAD)
    wuq = jnp.concatenate([uq, swapped(uq[..., QK_NOPE:])], axis=-1).reshape(Q_LORA, HEAD_W)
    wuk = jnp.zeros((KV_LORA, MLA_HEADS, LANES), F32).at[..., :QK_NOPE].set(w_uk[j]).reshape(
        KV_LORA, HEAD_W)
    wuv = w_uv[j].reshape(KV_LORA, MLA_HEADS * V_HEAD)

    col = np.arange(HEAD_W)
    head, within = col // LANES, col % LANES
    lane = np.arange(LANES)
    inv_lane = QK_ROPE + head
    segq = ((within < QK_HEAD)[:, None] & (lane[None, :] == head[:, None]))
    segqt = (lane[:, None] == head[None, :])
    segk = ((within < QK_NOPE)[:, None] & (lane[None, :] == inv_lane[:, None]))
    segkt = ((lane[:, None] == inv_lane[None, :]) & (within < QK_HEAD)[None, :])
    sel = lambda m: jnp.asarray(m, BF16)

    w_abs = jnp.zeros((MLA_HEADS, LANES, Q_EXT), F32)
    w_abs = w_abs.at[:, :QK_NOPE, :KV_LORA].set(jnp.transpose(w_uk[j], (1, 2, 0)))
    w_abs = w_abs.at[:, QK_NOPE:QK_HEAD, KV_LORA:KV_LORA + QK_ROPE].set(jnp.eye(QK_ROPE, dtype=F32))
    g_abs = jnp.zeros((1, LANES), F32).at[0, :QK_NOPE].set(kn[:QK_NOPE]).at[
        0, QK_NOPE:QK_HEAD].set(1.0)
    uv = jnp.transpose(w_uv[j], (1, 0, 2)).reshape(MLA_HEADS // 2, 2, KV_LORA, V_HEAD)
    w_pairs = jnp.zeros((MLA_HEADS // 2, 2, KV_LORA, LANES), F32)
    w_pairs = w_pairs.at[:, 0, :, :V_HEAD].set(uv[:, 0]).at[:, 1, :, V_HEAD:].set(uv[:, 1])
    return dict(
        norm=norm_mix_c[j].reshape(1, d), wcq=w_in[:, :Q_LORA].astype(BF16),
        wckv=w_in[:, Q_LORA:Q_LORA + KV_LORA].astype(BF16), wkr=wkr.astype(BF16),
        qan=q_a_norm[j].reshape(1, Q_LORA), kvn=kv_a_norm[j].reshape(1, KV_LORA),
        wuq=wuq.astype(BF16), wuk=wuk.astype(BF16), wuv=wuv.astype(BF16), gq=gq, gk=gk, gkr=gkr,
        segq=sel(segq), segqt=sel(segqt), segk=sel(segk), segkt=sel(segkt),
        w_abs=w_abs.astype(BF16), g_abs=g_abs, w_pairs=w_pairs.astype(BF16),
    )


def _prep_ffn(layer, w_gate_up, w_down, ff_tile):
    d, two_ff = w_gate_up.shape[1:]
    d_ff = two_ff // 2
    n = d_ff // ff_tile
    split = lambda m: jnp.transpose(m.reshape(d, n, ff_tile), (1, 0, 2)).astype(BF16)
    return (split(w_gate_up[layer][:, :d_ff]), split(w_gate_up[layer][:, d_ff:]),
            w_down[layer].reshape(n, ff_tile, d).astype(BF16))


FF_TILE = 256
ROW_TILE = 512
MLA_ROW_TILE = 256
ATTN_TILE = 512
GDN_CHUNKS = 2
SSD_CHUNKS = 2


def _row_tile(rows, want):
    return want if rows % want == 0 else rows


def _trunk(x, pos0, states, cache_mla, page_table, hyb, mla, ffn, norm_mix_a, gdn_norm, conv_ssm_b,
           ssm_norm, w_out_a, w_out_c, norm_ffn):
    b, t, d = x.shape
    rows = b * t
    tm = _row_tile(rows, ROW_TILE)
    xf = x.reshape(rows, d)
    decode = states is not None

    qkv, gz, sz, xbc, small = _norm_proj(xf, norm_mix_a[0], hyb["w_in"], tm)
    if decode:
        s0, gconv0, h0, sconv0 = states
        r3 = lambda a: a.reshape(b, 1, a.shape[-1])
        n_pair = SSM_HEADS // 2
        o_gdn, s_new = _gdn_decode(r3(qkv), r3(gz), r3(small), gconv0[0], s0[0], hyb["conv_gdn"],
                                   hyb["gate_gdn"], gdn_norm[0])
        o_ssd, h_new = _ssd_decode(r3(xbc), r3(sz), r3(small), sconv0[0],
                                   h0[0].reshape(b, n_pair, 2 * SSM_P, SSM_N), hyb["conv_ssm"],
                                   conv_ssm_b[0], hyb["gate_ssm"], ssm_norm[0])
        gconv = jnp.concatenate([gconv0[0][:, 1:], r3(qkv)], axis=1)
        sconv = jnp.concatenate([sconv0[0][:, 1:], r3(xbc)], axis=1)
    else:
        r3 = lambda a: a.reshape(b, t, a.shape[-1])
        o_gdn, s_new = _gdn_prompt(r3(qkv), r3(gz), r3(small), hyb["conv_gdn"], hyb["gate_gdn"],
                                   gdn_norm[0], GDN_CHUNKS)
        o_ssd, h_new = _ssd_prompt(r3(xbc), r3(sz), r3(small), hyb["conv_ssm"], conv_ssm_b[0],
                                   hyb["gate_ssm"], ssm_norm[0], SSD_CHUNKS)
        gconv = r3(qkv)[:, t - (CONV_W - 1):]
        sconv = r3(xbc)[:, t - (CONV_W - 1):]
    h_new = h_new.reshape(b, SSM_HEADS, SSM_P, SSM_N)
    wo = w_out_a[0].astype(BF16)
    x1 = _res_ffn(xf, [o_gdn.reshape(rows, GDN_VW), o_ssd.reshape(rows, SSM_INNER)],
                  [wo[:GDN_VW], wo[GDN_VW:]], norm_ffn[0], *ffn[0], tm)

    tm_c = _row_tile(rows, MLA_ROW_TILE)
    q, k, v, mla_rows = _mla_proj(x1, t, pos0, mla, tm_c, F32 if decode else BF16)
    if decode:
        q_ext = jnp.transpose(_absorb(q, mla["g_abs"], mla["w_abs"]), (1, 0, 2))
        ctx = _paged_attention(q_ext, mla_rows.reshape(b, 1, MLA_ROW), cache_mla[0], page_table)
        attn = _ctx_out(jnp.transpose(ctx, (1, 0, 2)), mla["w_pairs"])
    else:
        r3 = lambda a: a.reshape(b, t, a.shape[-1])
        attn = _flash_attention(r3(q), r3(k), r3(v), _row_tile(t, ATTN_TILE)).reshape(rows, -1)
    x2 = _res_ffn(x1, [attn], [w_out_c[0].astype(BF16)], norm_ffn[1], *ffn[1], tm)
    return (x2.reshape(b, t, d), s_new[None], gconv[None], h_new[None], sconv[None],
            mla_rows.reshape(1, b, t, MLA_ROW))


def kernel(x_prompt, x_sample, state_gdn, state_gdn_conv, state_ssm, state_ssm_conv, cache_mla, page_table, norm_mix_a, w_in_a, conv_gdn_w, gdn_A_log, gdn_dt_bias, gdn_norm, conv_ssm_w, conv_ssm_b, ssm_A_log, ssm_dt_bias, ssm_D, ssm_norm, w_out_a, norm_mix_c, w_in_c, q_a_norm, kv_a_norm, w_uq, w_uk, w_uv, q_norm, k_norm, w_out_c, norm_ffn, w_gate_up, w_down):
    assert w_in_a.shape[0] == 1 and w_in_c.shape[0] == 1 and norm_ffn.shape[0] == 2
    hyb = _prep_hybrid(0, w_in_a, conv_gdn_w, gdn_A_log, gdn_dt_bias, conv_ssm_w, ssm_A_log,
                       ssm_dt_bias, ssm_D)
    mla = _prep_mla(0, norm_mix_c, w_in_c, q_a_norm, kv_a_norm, w_uq, w_uk, w_uv, q_norm, k_norm)
    ffn = [_prep_ffn(layer, w_gate_up, w_down, FF_TILE) for layer in range(2)]
    shared = (hyb, mla, ffn, norm_mix_a, gdn_norm, conv_ssm_b, ssm_norm, w_out_a, w_out_c, norm_ffn)
    prompt = _trunk(x_prompt, 0, None, None, None, *shared)
    past_len = page_table.shape[1] * PAGE_SIZE
    sample = _trunk(x_sample, past_len, (state_gdn, state_gdn_conv, state_ssm, state_ssm_conv),
                    cache_mla, page_table, *shared)
    return (prompt[0], sample[0]) + prompt[1:] + sample[1:]
```

```python
import functools
import math

import jax
import jax.numpy as jnp
import numpy as np
from jax import lax
from jax.experimental import pallas as pl
from jax.experimental.pallas import tpu as pltpu

F32 = jnp.float32
BF16 = jnp.bfloat16

EPS = 1e-6
CONV_W = 4
CHUNK = 64
PAGE_SIZE = 128
GDN_HEADS = 4
GDN_DK = 128
GDN_DV = 128
GDN_QK = GDN_HEADS * GDN_DK
GDN_VW = GDN_HEADS * GDN_DV
GDN_CONV_CH = 2 * GDN_QK + GDN_VW
SSM_HEADS = 8
SSM_P = 64
SSM_N = 128
SSM_GROUPS = 2
SSM_INNER = SSM_HEADS * SSM_P
SSM_BC = SSM_GROUPS * SSM_N
SSM_CONV_CH = SSM_INNER + 2 * SSM_BC
MLA_HEADS = 16
Q_LORA = 512
KV_LORA = 256
QK_NOPE = 64
QK_ROPE = 32
QK_HEAD = QK_NOPE + QK_ROPE
V_HEAD = 64
MLA_ROW = KV_LORA + QK_ROPE + MLA_HEADS
ROPE_THETA = 10000.0

LANES = 128
HEAD_W = MLA_HEADS * LANES
GATE_W = LANES
GATE_A, GATE_B, GATE_DT = 0, GDN_HEADS, 2 * GDN_HEADS
Q_EXT = 3 * LANES
VMEM_LIMIT = 48 * 1024 * 1024

_HI = lax.Precision.HIGHEST


def _cparams(*sem):
    return pltpu.CompilerParams(dimension_semantics=sem, vmem_limit_bytes=VMEM_LIMIT)


def _const_spec(shape):
    zeros = (0,) * len(shape)
    return pl.BlockSpec(shape, lambda *_: zeros)


def _mm(a, b):
    return jnp.dot(a.astype(BF16), b.astype(BF16), preferred_element_type=F32)


def _mm_nt(a, b):
    return lax.dot_general(a.astype(BF16), b.astype(BF16), (((1,), (1,)), ((), ())),
                           preferred_element_type=F32)


def _mm_tn(a, b):
    return lax.dot_general(a.astype(BF16), b.astype(BF16), (((0,), (0,)), ((), ())),
                           preferred_element_type=F32)


def _mm_split(a, sel):
    hi = a.astype(BF16)
    lo = (a - hi.astype(F32)).astype(BF16)
    return (jnp.dot(hi, sel, preferred_element_type=F32)
            + jnp.dot(lo, sel, preferred_element_type=F32))


def _sigmoid(x):
    return 1.0 / (1.0 + jnp.exp(-x))


def _silu(x):
    return x * _sigmoid(x)


def _softplus(x):
    return jnp.maximum(x, 0.0) + jnp.log(1.0 + jnp.exp(-jnp.abs(x)))


def _rms(x, gain):
    return x * lax.rsqrt(jnp.mean(x * x, axis=-1, keepdims=True) + EPS) * gain


def _l2n(x):
    return x * lax.rsqrt(jnp.sum(x * x, axis=-1, keepdims=True) + EPS)


def _iota(shape, dim):
    return lax.broadcasted_iota(jnp.int32, shape, dim)


def _chunk_cumsum(g, tb):
    row, col = _iota((tb, tb), 0), _iota((tb, tb), 1)
    shift = int(math.log2(CHUNK))
    same_chunk = lax.shift_right_logical(row, shift) == lax.shift_right_logical(col, shift)
    tril = jnp.where((col <= row) & same_chunk, 1.0, 0.0).astype(F32)
    return jnp.dot(tril, g, precision=_HI, preferred_element_type=F32)


def _rows_of(cols, lane0):
    sel = jnp.where(_iota((8, LANES), 1) == _iota((8, LANES), 0) + lane0, 1.0, 0.0).astype(F32)
    return lax.dot_general(sel, cols, (((1,), (1,)), ((), ())), precision=_HI,
                           preferred_element_type=F32)


def _unit_lower_inverse(low):
    eye = jnp.where(_iota((CHUNK, CHUNK), 0) == _iota((CHUNK, CHUNK), 1), 1.0, 0.0).astype(F32)
    inv = eye - low
    power = low
    for _ in range(int(math.log2(CHUNK)) - 1):
        power = _mm(power, power)
        inv = inv + _mm(inv, power)
    return inv


def _causal_conv_block(x_ref, xp_ref, cw, tb, first):
    @pl.when(first)
    def _():
        xp_ref[0:8, :] = jnp.zeros((8, xp_ref.shape[1]), F32)

    xp_ref[8:8 + tb, :] = x_ref[...]
    off = 8 - (CONV_W - 1)
    y = cw[0:1] * xp_ref[off:off + tb, :]
    for i in range(1, CONV_W):
        y = y + cw[i:i + 1] * xp_ref[off + i:off + i + tb, :]
    xp_ref[0:8, :] = xp_ref[tb:tb + 8, :]
    return y


def _norm_proj_body(n_out, x_ref, g_ref, *refs):
    xn = _rms(x_ref[...], g_ref[...]).astype(BF16)
    for w_ref, o_ref in zip(refs[:n_out], refs[n_out:]):
        o_ref[...] = jnp.dot(xn, w_ref[...], preferred_element_type=F32)


def _norm_proj(x, gain, weights, tm):
    rows, d = x.shape
    n = len(weights)
    return pl.pallas_call(
        functools.partial(_norm_proj_body, n),
        grid=(rows // tm,),
        in_specs=[pl.BlockSpec((tm, d), lambda i: (i, 0)), _const_spec((1, d))]
        + [_const_spec(w.shape) for w in weights],
        out_specs=[pl.BlockSpec((tm, w.shape[1]), lambda i: (i, 0)) for w in weights],
        out_shape=[jax.ShapeDtypeStruct((rows, w.shape[1]), F32) for w in weights],
        compiler_params=_cparams("parallel"),
        name="norm_proj",
    )(x, gain.reshape(1, d), *weights)


def _gdn_prompt_body(nc, qkv_ref, gz_ref, sm_ref, cw_ref, gp_ref, gn_ref, o_ref, s_ref, xp_ref):
    tb = nc * CHUNK
    first = pl.program_id(1) == 0

    @pl.when(first)
    def _():
        s_ref[...] = jnp.zeros(s_ref.shape, F32)

    y = _silu(_causal_conv_block(qkv_ref.at[0], xp_ref, cw_ref[...], tb, first))
    sm, gp = sm_ref[0], gp_ref[...]
    g_all = -jnp.exp(gp[0:1]) * _softplus(sm + gp[1:2])
    beta_all = _sigmoid(sm)
    cum_all = _chunk_cumsum(g_all, tb)
    cum_rows = _rows_of(cum_all, GATE_A)
    incl = _iota((CHUNK, CHUNK), 1) <= _iota((CHUNK, CHUNK), 0)
    strict = _iota((CHUNK, CHUNK), 1) < _iota((CHUNK, CHUNK), 0)
    gn = gn_ref[...]
    pairs = [(c, h) for c in range(nc) for h in range(GDN_HEADS)]
    qs, ks, vs = [], [], []
    for h in range(GDN_HEADS):
        lo = h * GDN_DK
        qs.append(_l2n(y[:, lo:lo + GDN_DK]) * GDN_DK ** -0.5)
        ks.append(_l2n(y[:, GDN_QK + lo:GDN_QK + lo + GDN_DK]))
        vs.append(y[:, 2 * GDN_QK + lo:2 * GDN_QK + lo + GDN_DV])
    pre = []
    for c, h in pairs:
        r = slice(c * CHUNK, (c + 1) * CHUNK)
        q, k, v = qs[h][r], ks[h][r], vs[h][r]
        cum_c = cum_all[r, GATE_A + h:GATE_A + h + 1]
        cum_r = cum_rows[h:h + 1, c * CHUNK:(c + 1) * CHUNK]
        beta = beta_all[r, GATE_B + h:GATE_B + h + 1]
        decay = jnp.where(incl, jnp.exp(jnp.minimum(cum_c - cum_r, 0.0)), 0.0)
        e_c = jnp.exp(cum_c)
        kb = k * beta
        last = cum_c[CHUNK - 1:CHUNK]
        pre.append(dict(q=q.astype(BF16), k=k.astype(BF16), kb=kb, decay=decay,
                        rhs=jnp.concatenate([v * beta, kb * e_c], axis=1).astype(BF16),
                        qe=(q * e_c).astype(BF16), kd=(k * jnp.exp(last - cum_c)).astype(BF16),
                        e_last=jnp.exp(last)))
    lows = [jnp.where(strict, _mm_nt(p["kb"], p["k"]) * p["decay"], 0.0) for p in pre]
    attns = [(_mm_nt(p["q"], p["k"]) * p["decay"]).astype(BF16) for p in pre]
    eye = jnp.where(_iota((CHUNK, CHUNK), 0) == _iota((CHUNK, CHUNK), 1), 1.0, 0.0).astype(F32)
    invs = [eye - low for low in lows]
    powers = lows
    for _ in range(int(math.log2(CHUNK)) - 1):
        powers = [_mm(pw, pw) for pw in powers]
        invs = [inv + _mm(inv, pw) for inv, pw in zip(invs, powers)]
    sols = [_mm(inv, p["rhs"]) for inv, p in zip(invs, pre)]
    states = [s_ref[0, h] for h in range(GDN_HEADS)]
    for c in range(nc):
        r = slice(c * CHUNK, (c + 1) * CHUNK)
        idx = [c * GDN_HEADS + h for h in range(GDN_HEADS)]
        v_new = [sols[i][:, :GDN_DV] - _mm(sols[i][:, GDN_DV:], states[h])
                 for h, i in enumerate(idx)]
        outs = [_mm(pre[i]["qe"], states[h]) + _mm(attns[i], v_new[h]) for h, i in enumerate(idx)]
        states = [states[h] * pre[i]["e_last"] + _mm_tn(pre[i]["kd"], v_new[h])
                  for h, i in enumerate(idx)]
        for h, o in enumerate(outs):
            lo = h * GDN_DV
            gate = _silu(gz_ref[0, r, lo:lo + GDN_DV])
            o_ref[0, r, lo:lo + GDN_DV] = (_rms(o, gn) * gate).astype(o_ref.dtype)
    for h in range(GDN_HEADS):
        s_ref[0, h] = states[h]


def _gdn_prompt(qkv, gz, small, conv_w, gate_par, gdn_norm, nc):
    b, t, _ = qkv.shape
    tb = nc * CHUNK
    blk = lambda w: pl.BlockSpec((1, tb, w), lambda i, j: (i, j, 0))
    return pl.pallas_call(
        functools.partial(_gdn_prompt_body, nc),
        grid=(b, t // tb),
        in_specs=[blk(GDN_CONV_CH), blk(GDN_VW), blk(GATE_W), _const_spec(conv_w.shape),
                  _const_spec(gate_par.shape), _const_spec((1, GDN_DV))],
        out_specs=[blk(GDN_VW),
                   pl.BlockSpec((1, GDN_HEADS, GDN_DK, GDN_DV), lambda i, j: (i, 0, 0, 0))],
        out_shape=[jax.ShapeDtypeStruct((b, t, GDN_VW), BF16),
                   jax.ShapeDtypeStruct((b, GDN_HEADS, GDN_DK, GDN_DV), F32)],
        scratch_shapes=[pltpu.VMEM((tb + 8, GDN_CONV_CH), F32)],
        compiler_params=_cparams("parallel", "arbitrary"),
        name="gdn_prompt",
    )(qkv, gz, small, conv_w, gate_par, gdn_norm.reshape(1, GDN_DV))


def _ssd_prompt_body(nc, xbc_ref, sz_ref, sm_ref, cw_ref, cb_ref, sp_ref, sn_ref, o_ref, h_ref,
                     xp_ref):
    tb = nc * CHUNK
    first = pl.program_id(1) == 0

    @pl.when(first)
    def _():
        h_ref[...] = jnp.zeros(h_ref.shape, F32)

    y = _silu(_causal_conv_block(xbc_ref.at[0], xp_ref, cw_ref[...], tb, first) + cb_ref[...])
    sm, sp = sm_ref[0], sp_ref[...]
    dt_all = _softplus(sm + sp[1:2])
    cum_all = _chunk_cumsum(dt_all * -jnp.exp(sp[0:1]), tb)
    cum_rows = _rows_of(cum_all, GATE_DT)
    incl = _iota((CHUNK, CHUNK), 1) <= _iota((CHUNK, CHUNK), 0)
    lane_lo = _iota((CHUNK, LANES), 1) < SSM_P
    row_lo = _iota((2 * SSM_P, SSM_N), 0) < SSM_P
    par_lo = _iota((1, LANES), 1) < SSM_P
    sn = sn_ref[...]
    pairs_per_group = SSM_HEADS // SSM_GROUPS // 2
    for c in range(nc):
        r = slice(c * CHUNK, (c + 1) * CHUNK)
        for grp in range(SSM_GROUPS):
            b_g = y[r, SSM_INNER + grp * SSM_N:SSM_INNER + (grp + 1) * SSM_N]
            c_g = y[r, SSM_INNER + SSM_BC + grp * SSM_N:SSM_INNER + SSM_BC + (grp + 1) * SSM_N]
            scores = _mm_nt(c_g, b_g)
            outs = []
            for jj in range(pairs_per_group):
                j = grp * pairs_per_group + jj
                cums, segs, dts = [], [], []
                for head in (2 * j, 2 * j + 1):
                    cum_c = cum_all[r, GATE_DT + head:GATE_DT + head + 1]
                    cum_r = cum_rows[head:head + 1, c * CHUNK:(c + 1) * CHUNK]
                    cums.append(cum_c)
                    segs.append(jnp.where(incl, jnp.exp(jnp.minimum(cum_c - cum_r, 0.0)), 0.0))
                    dts.append(dt_all[r, GATE_DT + head:GATE_DT + head + 1])
                x_pair = y[r, j * LANES:(j + 1) * LANES]
                xdt = x_pair * jnp.where(lane_lo, dts[0], dts[1])
                y_pair = (_mm(scores * segs[0], jnp.where(lane_lo, xdt, 0.0))
                          + _mm(scores * segs[1], jnp.where(lane_lo, 0.0, xdt)))
                h_pair = h_ref[0, j]
                y_pair = y_pair + _mm_nt(c_g, h_pair) * jnp.where(
                    lane_lo, jnp.exp(cums[0]), jnp.exp(cums[1]))
                last = [cm[CHUNK - 1:CHUNK] for cm in cums]
                dec = jnp.where(lane_lo, jnp.exp(last[0] - cums[0]), jnp.exp(last[1] - cums[1]))
                h_ref[0, j] = (h_pair * jnp.where(row_lo, jnp.exp(last[0]), jnp.exp(last[1]))
                               + _mm_tn(xdt * dec, b_g))
                d_pair = jnp.where(par_lo, sp[2:3, GATE_DT + 2 * j:GATE_DT + 2 * j + 1],
                                   sp[2:3, GATE_DT + 2 * j + 1:GATE_DT + 2 * j + 2])
                outs.append((y_pair + d_pair * x_pair)
                            * _silu(sz_ref[0, r, j * LANES:(j + 1) * LANES]))
            gw = SSM_INNER // SSM_GROUPS
            ms = sum(jnp.sum(o * o, axis=-1, keepdims=True) for o in outs) / gw
            inv = lax.rsqrt(ms + EPS)
            for jj, o in enumerate(outs):
                lo = grp * gw + jj * LANES
                o_ref[0, r, lo:lo + LANES] = (o * inv * sn[:, lo:lo + LANES]).astype(o_ref.dtype)


def _ssd_prompt(xbc, sz, small, conv_w, conv_b, ssm_par, ssm_norm, nc):
    b, t, _ = xbc.shape
    tb = nc * CHUNK
    blk = lambda w: pl.BlockSpec((1, tb, w), lambda i, j: (i, j, 0))
    n_pair = SSM_HEADS // 2
    return pl.pallas_call(
        functools.partial(_ssd_prompt_body, nc),
        grid=(b, t // tb),
        in_specs=[blk(SSM_CONV_CH), blk(SSM_INNER), blk(GATE_W), _const_spec(conv_w.shape),
                  _const_spec((1, SSM_CONV_CH)), _const_spec(ssm_par.shape),
                  _const_spec((1, SSM_INNER))],
        out_specs=[blk(SSM_INNER),
                   pl.BlockSpec((1, n_pair, 2 * SSM_P, SSM_N), lambda i, j: (i, 0, 0, 0))],
        out_shape=[jax.ShapeDtypeStruct((b, t, SSM_INNER), BF16),
                   jax.ShapeDtypeStruct((b, n_pair, 2 * SSM_P, SSM_N), F32)],
        scratch_shapes=[pltpu.VMEM((tb + 8, SSM_CONV_CH), F32)],
        compiler_params=_cparams("parallel", "arbitrary"),
        name="ssd_prompt",
    )(xbc, sz, small, conv_w, conv_b.reshape(1, -1), ssm_par, ssm_norm.reshape(1, -1))


def _col_bcast(row):
    return jnp.transpose(jnp.broadcast_to(row, (LANES, LANES)))


def _gdn_decode_body(qkv_ref, gz_ref, sm_ref, cv_ref, s0_ref, cw_ref, gp_ref, gn_ref, o_ref,
                     s_ref):
    cw = cw_ref[...]
    y = jnp.sum(cw[0:CONV_W - 1] * cv_ref[0], axis=0, keepdims=True) + cw[CONV_W - 1:] * qkv_ref[0]
    y = _silu(y)
    sm, gp = sm_ref[0], gp_ref[...]
    a_all = jnp.exp(-jnp.exp(gp[0:1]) * _softplus(sm + gp[1:2]))
    beta_all = _sigmoid(sm)
    gn = gn_ref[...]
    for h in range(GDN_HEADS):
        lo = h * GDN_DK
        q = _l2n(y[:, lo:lo + GDN_DK]) * GDN_DK ** -0.5
        k = _l2n(y[:, GDN_QK + lo:GDN_QK + lo + GDN_DK])
        v = y[:, 2 * GDN_QK + lo:2 * GDN_QK + lo + GDN_DV]
        a = a_all[:, GATE_A + h:GATE_A + h + 1]
        beta = beta_all[:, GATE_B + h:GATE_B + h + 1]
        k_col, q_col = _col_bcast(k), _col_bcast(q)
        decayed = s0_ref[0, h] * a
        v_new = beta * (v - jnp.sum(k_col * decayed, axis=0, keepdims=True))
        state = decayed + k_col * v_new
        s_ref[0, h] = state
        o = jnp.sum(q_col * state, axis=0, keepdims=True)
        o_ref[0, :, lo:lo + GDN_DV] = (_rms(o, gn) * _silu(gz_ref[0, :, lo:lo + GDN_DV])
                                       ).astype(o_ref.dtype)


def _gdn_decode(qkv, gz, small, conv0, s0, conv_w, gate_par, gdn_norm):
    b = qkv.shape[0]
    row = lambda w: pl.BlockSpec((1, 1, w), lambda i: (i, 0, 0))
    st = pl.BlockSpec((1, GDN_HEADS, GDN_DK, GDN_DV), lambda i: (i, 0, 0, 0))
    return pl.pallas_call(
        _gdn_decode_body,
        grid=(b,),
        in_specs=[row(GDN_CONV_CH), row(GDN_VW), row(GATE_W),
                  pl.BlockSpec((1, CONV_W - 1, GDN_CONV_CH), lambda i: (i, 0, 0)), st,
                  _const_spec(conv_w.shape), _const_spec(gate_par.shape), _const_spec((1, GDN_DV))],
        out_specs=[row(GDN_VW), st],
        out_shape=[jax.ShapeDtypeStruct((b, 1, GDN_VW), BF16),
                   jax.ShapeDtypeStruct(s0.shape, F32)],
        compiler_params=_cparams("parallel"),
        name="gdn_decode",
    )(qkv, gz, small, conv0, s0, conv_w, gate_par, gdn_norm.reshape(1, GDN_DV))


def _ssd_decode_body(xbc_ref, sz_ref, sm_ref, cv_ref, h0_ref, cw_ref, cb_ref, sp_ref, sn_ref,
                     o_ref, h_ref):
    cw = cw_ref[...]
    y = (jnp.sum(cw[0:CONV_W - 1] * cv_ref[0], axis=0, keepdims=True)
         + cw[CONV_W - 1:] * xbc_ref[0] + cb_ref[...])
    y = _silu(y)
    sm, sp = sm_ref[0], sp_ref[...]
    dt_all = _softplus(sm + sp[1:2])
    da_all = jnp.exp(dt_all * -jnp.exp(sp[0:1]))
    par_lo = _iota((1, LANES), 1) < SSM_P
    row_lo = _iota((2 * SSM_P, SSM_N), 0) < SSM_P
    sn = sn_ref[...]
    pairs_per_group = SSM_HEADS // SSM_GROUPS // 2
    gw = SSM_INNER // SSM_GROUPS
    for grp in range(SSM_GROUPS):
        b_g = y[:, SSM_INNER + grp * SSM_N:SSM_INNER + (grp + 1) * SSM_N]
        c_g = y[:, SSM_INNER + SSM_BC + grp * SSM_N:SSM_INNER + SSM_BC + (grp + 1) * SSM_N]
        outs = []
        for jj in range(pairs_per_group):
            j = grp * pairs_per_group + jj
            la, lb = GATE_DT + 2 * j, GATE_DT + 2 * j + 1
            x_pair = y[:, j * LANES:(j + 1) * LANES]
            xdt = x_pair * jnp.where(par_lo, dt_all[:, la:la + 1], dt_all[:, lb:lb + 1])
            state = (h0_ref[0, j] * jnp.where(row_lo, da_all[:, la:la + 1], da_all[:, lb:lb + 1])
                     + _col_bcast(xdt) * b_g)
            h_ref[0, j] = state
            y_col = jnp.sum(state * c_g, axis=1, keepdims=True)
            y_row = jnp.transpose(jnp.broadcast_to(y_col, (LANES, LANES)))[0:1]
            d_pair = jnp.where(par_lo, sp[2:3, la:la + 1], sp[2:3, lb:lb + 1])
            outs.append((y_row + d_pair * x_pair) * _silu(sz_ref[0, :, j * LANES:(j + 1) * LANES]))
        ms = sum(jnp.sum(o * o, axis=-1, keepdims=True) for o in outs) / gw
        inv = lax.rsqrt(ms + EPS)
        for jj, o in enumerate(outs):
            lo = grp * gw + jj * LANES
            o_ref[0, :, lo:lo + LANES] = (o * inv * sn[:, lo:lo + LANES]).astype(o_ref.dtype)


def _ssd_decode(xbc, sz, small, conv0, h0, conv_w, conv_b, ssm_par, ssm_norm):
    b = xbc.shape[0]
    n_pair = SSM_HEADS // 2
    row = lambda w: pl.BlockSpec((1, 1, w), lambda i: (i, 0, 0))
    st = pl.BlockSpec((1, n_pair, 2 * SSM_P, SSM_N), lambda i: (i, 0, 0, 0))
    return pl.pallas_call(
        _ssd_decode_body,
        grid=(b,),
        in_specs=[row(SSM_CONV_CH), row(SSM_INNER), row(GATE_W),
                  pl.BlockSpec((1, CONV_W - 1, SSM_CONV_CH), lambda i: (i, 0, 0)), st,
                  _const_spec(conv_w.shape), _const_spec((1, SSM_CONV_CH)),
                  _const_spec(ssm_par.shape), _const_spec((1, SSM_INNER))],
        out_specs=[row(SSM_INNER), st],
        out_shape=[jax.ShapeDtypeStruct((b, 1, SSM_INNER), BF16),
                   jax.ShapeDtypeStruct(h0.shape, F32)],
        compiler_params=_cparams("parallel"),
        name="ssd_decode",
    )(xbc, sz, small, conv0, h0, conv_w, conv_b.reshape(1, -1), ssm_par, ssm_norm.reshape(1, -1))


def _res_ffn_body(n_a, n_ff, x_ref, *refs):
    a_refs, w_refs = refs[:n_a], refs[n_a:2 * n_a]
    g_ref, wg_ref, wu_ref, wd_ref, o_ref, xn_ref = refs[2 * n_a:]
    x1 = x_ref[...]
    for a_ref, w_ref in zip(a_refs, w_refs):
        x1 = x1 + jnp.dot(a_ref[...], w_ref[...], preferred_element_type=F32)
    xn_ref[...] = _rms(x1, g_ref[...]).astype(BF16)
    o_ref[...] = x1

    def step(c, carry):
        xn = xn_ref[...]
        gate = jnp.dot(xn, wg_ref[c], preferred_element_type=F32)
        up = jnp.dot(xn, wu_ref[c], preferred_element_type=F32)
        hid = (_silu(gate) * up).astype(BF16)
        o_ref[...] += jnp.dot(hid, wd_ref[c], preferred_element_type=F32)
        return carry

    lax.fori_loop(0, n_ff, step, 0)


def _res_ffn(x, acts, projs, gain, wg, wu, wd, tm):
    rows, d = x.shape
    n_a, n_ff = len(acts), wg.shape[0]
    return pl.pallas_call(
        functools.partial(_res_ffn_body, n_a, n_ff),
        grid=(rows // tm,),
        in_specs=[pl.BlockSpec((tm, d), lambda i: (i, 0))]
        + [pl.BlockSpec((tm, a.shape[1]), lambda i: (i, 0)) for a in acts]
        + [_const_spec(p.shape) for p in projs]
        + [_const_spec((1, d)), _const_spec(wg.shape), _const_spec(wu.shape), _const_spec(wd.shape)],
        out_specs=pl.BlockSpec((tm, d), lambda i: (i, 0)),
        out_shape=jax.ShapeDtypeStruct((rows, d), F32),
        scratch_shapes=[pltpu.VMEM((tm, d), BF16)],
        compiler_params=_cparams("parallel"),
        name="res_ffn",
    )(x, *acts, *projs, gain.reshape(1, d), wg, wu, wd)


def _mla_proj_body(tm, seq, pos0, q_scale, x_ref, g_ref, wcq_ref, wckv_ref, wkr_ref, qan_ref,
                   kvn_ref, wuq_ref, wuk_ref, wuv_ref, vone_ref, gq_ref, gk_ref, gkr_ref, segq_ref,
                   segqt_ref, segk_ref, segkt_ref, q_ref, k_ref, v_ref, rows_ref):
    xn = _rms(x_ref[...], g_ref[...]).astype(BF16)
    lane = _iota((tm, LANES), 1)
    if seq == 1:
        pos = jnp.full((tm, LANES), pos0, F32)
    else:
        base = (pl.program_id(0) * tm) % seq + pos0
        pos = (base + _iota((tm, LANES), 0)).astype(F32)
    half = QK_ROPE // 2
    inv_freq = jnp.exp((lane & (half - 1)).astype(F32) * (-math.log(ROPE_THETA) / half))
    cos_t, sin_t = jnp.cos(pos * inv_freq), jnp.sin(pos * inv_freq)

    kr = jnp.dot(xn, wkr_ref[...], preferred_element_type=F32)
    ssq_kr = jnp.sum(jnp.where(lane < QK_ROPE, kr * kr, 0.0), axis=-1, keepdims=True)
    krg = kr * gkr_ref[...]
    kr_rot = krg * cos_t + pltpu.roll(krg, LANES - QK_ROPE, 1) * sin_t

    c = _rms(jnp.dot(xn, wckv_ref[...], preferred_element_type=F32), kvn_ref[...])
    cb = c.astype(BF16)
    kx = jnp.dot(cb, wuk_ref[...], preferred_element_type=F32)
    ssq_k = _mm_split(kx * kx, segk_ref[...])
    inv_r = lax.rsqrt((ssq_k + ssq_kr) / QK_HEAD + EPS)
    tail = jnp.where(lane < QK_ROPE, kr_rot, inv_r)
    rows_ref[:, 0:KV_LORA] = c
    rows_ref[:, KV_LORA:MLA_ROW] = tail[:, 0:MLA_ROW - KV_LORA]
    inv_b = _mm_split(inv_r, segkt_ref[...])
    kr_put = jnp.where((lane >= QK_NOPE) & (lane < QK_HEAD), pltpu.roll(kr_rot, QK_NOPE, 1), 0.0)
    gk = gk_ref[...]
    for h in range(MLA_HEADS):
        hs = slice(h * LANES, (h + 1) * LANES)
        k_ref[:, hs] = ((kx[:, hs] * gk + kr_put) * inv_b[:, hs]).astype(k_ref.dtype)
    v_ref[...] = (jnp.dot(cb, wuv_ref[...], preferred_element_type=F32)
                  + vone_ref[...]).astype(v_ref.dtype)

    cq = _rms(jnp.dot(xn, wcq_ref[...], preferred_element_type=F32), qan_ref[...]).astype(BF16)
    qx = jnp.dot(cq, wuq_ref[...], preferred_element_type=F32)
    inv_q = lax.rsqrt(_mm_split(qx * qx, segq_ref[...]) / QK_HEAD + EPS)
    qn = qx * _mm_split(inv_q, segqt_ref[...]) * gq_ref[...]
    scale = q_scale
    in_rope = (lane >= QK_NOPE) & (lane < QK_HEAD)
    keep = jnp.where(lane < QK_NOPE, scale, jnp.where(in_rope, cos_t * scale, 0.0))
    swap = jnp.where(in_rope, sin_t * scale, 0.0)
    for h in range(MLA_HEADS):
        hs = slice(h * LANES, (h + 1) * LANES)
        t = qn[:, hs]
        q_ref[:, hs] = (t * keep + pltpu.roll(t, LANES - QK_ROPE, 1) * swap).astype(q_ref.dtype)


def _mla_proj(x, seq, pos0, w, tm, q_dtype, q_scale):
    rows, d = x.shape
    names = ("norm", "wcq", "wckv", "wkr", "qan", "kvn", "wuq", "wuk", "wuv", "vone", "gq", "gk",
             "gkr", "segq", "segqt", "segk", "segkt")
    consts = [w[n] for n in names]
    out_w = (HEAD_W, HEAD_W, HEAD_W, MLA_ROW)
    out_dt = (q_dtype, BF16, BF16, F32)
    return pl.pallas_call(
        functools.partial(_mla_proj_body, tm, seq, pos0, q_scale),
        grid=(rows // tm,),
        in_specs=[pl.BlockSpec((tm, d), lambda i: (i, 0))] + [_const_spec(c.shape) for c in consts],
        out_specs=[pl.BlockSpec((tm, ow), lambda i: (i, 0)) for ow in out_w],
        out_shape=[jax.ShapeDtypeStruct((rows, ow), dt) for ow, dt in zip(out_w, out_dt)],
        compiler_params=_cparams("parallel"),
        name="mla_proj",
    )(x, *consts)


def _flash_body(tq, q_ref, k_ref, v_ref, o_ref):
    qi = pl.program_id(2)
    causal = _iota((tq, tq), 1) <= _iota((tq, tq), 0)
    heads = [slice(j * LANES, (j + 1) * LANES) for j in range(2)]
    q = [q_ref[0, :, hs] for hs in heads]

    def block(kb, carry, diag):
        rows = pl.ds(pl.multiple_of(kb * tq, tq), tq)
        s = [lax.dot_general(q[j], k_ref[0, rows, hs], (((1,), (1,)), ((), ())),
                             preferred_element_type=F32) for j, hs in enumerate(heads)]
        if diag:
            s = [jnp.where(causal, sj, -jnp.inf) for sj in s]
        m_new = [jnp.maximum(carry[j][0], jnp.max(s[j], axis=-1, keepdims=True)) for j in range(2)]
        p = [jnp.exp2(s[j] - m_new[j]).astype(BF16) for j in range(2)]
        pv = [jnp.dot(p[j], v_ref[0, rows, hs], preferred_element_type=F32)
              for j, hs in enumerate(heads)]
        return tuple((m_new[j], jnp.exp2(carry[j][0] - m_new[j]) * carry[j][1] + pv[j])
                     for j in range(2))

    init = tuple((jnp.full((tq, 1), -jnp.inf, F32), jnp.zeros((tq, LANES), F32)) for _ in range(2))
    carry = lax.fori_loop(0, qi, lambda kb, cr: block(kb, cr, False), init)
    (_, acc0), (_, acc1) = block(qi, carry, True)
    out0 = acc0 / acc0[:, V_HEAD:V_HEAD + 1]
    out1 = acc1 / acc1[:, V_HEAD:V_HEAD + 1]
    o_ref[0] = jnp.where(_iota((tq, LANES), 1) < V_HEAD, out0,
                         pltpu.roll(out1, V_HEAD, 1)).astype(o_ref.dtype)


def _flash_attention(q, k, v, tq):
    b, t, _ = q.shape
    kv_spec = pl.BlockSpec((1, t, 2 * LANES), lambda i, j, n: (i, 0, j))
    return pl.pallas_call(
        functools.partial(_flash_body, tq),
        grid=(b, MLA_HEADS // 2, t // tq),
        in_specs=[pl.BlockSpec((1, tq, 2 * LANES), lambda i, j, n: (i, n, j)), kv_spec, kv_spec],
        out_specs=pl.BlockSpec((1, tq, LANES), lambda i, j, n: (i, n, j)),
        out_shape=jax.ShapeDtypeStruct((b, t, MLA_HEADS * V_HEAD), BF16),
        compiler_params=_cparams("parallel", "parallel", "arbitrary"),
        name="flash_attention",
    )(q, k, v)


def _absorb_body(q_ref, g_ref, w_ref, o_ref):
    t = (q_ref[...] * g_ref[...]).astype(BF16)
    o_ref[...] = jnp.dot(t, w_ref[...], preferred_element_type=F32).astype(o_ref.dtype)


def _absorb(q, gain, w_abs):
    b = q.shape[0]
    return pl.pallas_call(
        _absorb_body,
        grid=(MLA_HEADS,),
        in_specs=[pl.BlockSpec((b, LANES), lambda h: (0, h)), _const_spec((1, LANES)),
                  pl.BlockSpec((None, LANES, Q_EXT), lambda h: (h, 0, 0))],
        out_specs=pl.BlockSpec((None, b, Q_EXT), lambda h: (h, 0, 0)),
        out_shape=jax.ShapeDtypeStruct((MLA_HEADS, b, Q_EXT), BF16),
        compiler_params=_cparams("parallel"),
        name="mla_absorb",
    )(q, gain, w_abs)


def _paged_attn_body(n_pages, pt_ref, q_ref, rn_ref, *refs):
    del pt_ref
    page_refs, o_ref, s_ref, pb_ref = refs[:n_pages], *refs[n_pages:]
    q = q_ref[0]
    inv_lo = MLA_ROW - MLA_HEADS
    for p in range(n_pages):
        page = page_refs[p][...]
        pb_ref[p] = page.astype(BF16)
        s = jnp.dot(q[:, :MLA_ROW], pb_ref[p], preferred_element_type=F32)
        s_ref[:, p * PAGE_SIZE:(p + 1) * PAGE_SIZE] = s * page[inv_lo:MLA_ROW]
    rn = rn_ref[0]
    qf = q[:, :MLA_ROW].astype(F32)
    pick = _iota((MLA_HEADS, MLA_ROW), 1) == _iota((MLA_HEADS, MLA_ROW), 0) + inv_lo
    inv_new = jnp.sum(jnp.where(pick, rn, 0.0), axis=-1, keepdims=True)
    s_new = jnp.sum(qf * rn, axis=-1, keepdims=True) * inv_new
    m = jnp.maximum(jnp.max(s_ref[...], axis=-1, keepdims=True), s_new)
    e_new = jnp.exp(s_new - m)
    den = e_new
    ctx = e_new * rn[:, 0:KV_LORA]
    for p in range(n_pages):
        e = jnp.exp(s_ref[:, p * PAGE_SIZE:(p + 1) * PAGE_SIZE] - m)
        den = den + jnp.sum(e, axis=-1, keepdims=True)
        ctx = ctx + lax.dot_general(e.astype(BF16), pb_ref[p, 0:KV_LORA, :],
                                    (((1,), (1,)), ((), ())), preferred_element_type=F32)
    o_ref[0] = ctx / den


def _paged_attention(q_ext, rows_new, cache_t, page_table):
    b, n_pages = page_table.shape
    page_specs = [pl.BlockSpec((None, MLA_ROW, PAGE_SIZE),
                               functools.partial(lambda p, i, pt: (pt[i, p], 0, 0), p))
                  for p in range(n_pages)]
    grid_spec = pltpu.PrefetchScalarGridSpec(
        num_scalar_prefetch=1,
        grid=(b,),
        in_specs=[pl.BlockSpec((1, MLA_HEADS, Q_EXT), lambda i, pt: (i, 0, 0)),
                  pl.BlockSpec((1, 1, MLA_ROW), lambda i, pt: (i, 0, 0))] + page_specs,
        out_specs=pl.BlockSpec((1, MLA_HEADS, KV_LORA), lambda i, pt: (i, 0, 0)),
        scratch_shapes=[pltpu.VMEM((MLA_HEADS, n_pages * PAGE_SIZE), F32),
                        pltpu.VMEM((n_pages, MLA_ROW, PAGE_SIZE), BF16)],
    )
    return pl.pallas_call(
        functools.partial(_paged_attn_body, n_pages),
        grid_spec=grid_spec,
        out_shape=jax.ShapeDtypeStruct((b, MLA_HEADS, KV_LORA), F32),
        compiler_params=_cparams("arbitrary"),
        name="paged_attention",
    )(page_table, q_ext, rows_new, *([cache_t] * n_pages))


def _ctx_out_body(c_ref, w_ref, o_ref):
    o_ref[...] = (jnp.dot(c_ref[0].astype(BF16), w_ref[0], preferred_element_type=F32)
                  + jnp.dot(c_ref[1].astype(BF16), w_ref[1], preferred_element_type=F32)
                  ).astype(o_ref.dtype)


def _ctx_out(ctx_t, w_pairs):
    b = ctx_t.shape[1]
    return pl.pallas_call(
        _ctx_out_body,
        grid=(MLA_HEADS // 2,),
        in_specs=[pl.BlockSpec((2, b, KV_LORA), lambda j: (j, 0, 0)),
                  pl.BlockSpec((None, 2, KV_LORA, LANES), lambda j: (j, 0, 0, 0))],
        out_specs=pl.BlockSpec((b, LANES), lambda j: (0, j)),
        out_shape=jax.ShapeDtypeStruct((b, MLA_HEADS * V_HEAD), BF16),
        compiler_params=_cparams("parallel"),
        name="mla_ctx_out",
    )(ctx_t, w_pairs)


def _gate_tile(rows):
    tile = jnp.zeros((8, GATE_W), F32)
    for i, (off, vec) in enumerate(rows):
        tile = tile.at[i, off:off + vec.shape[0]].set(vec.astype(F32))
    return tile


def _prep_hybrid(j, w_in_a, conv_gdn_w, gdn_A_log, gdn_dt_bias, conv_ssm_w, ssm_A_log,
                 ssm_dt_bias, ssm_D):
    widths = (GDN_CONV_CH, GDN_VW, GDN_HEADS, GDN_HEADS, SSM_INNER, SSM_CONV_CH, SSM_HEADS)
    offs = np.concatenate([[0], np.cumsum(widths)])
    w = w_in_a[j]
    d = w.shape[0]
    part = lambda i: w[:, offs[i]:offs[i + 1]]
    small = jnp.zeros((d, GATE_W), F32)
    small = small.at[:, GATE_A:GATE_A + GDN_HEADS].set(part(2))
    small = small.at[:, GATE_B:GATE_B + GDN_HEADS].set(part(3))
    small = small.at[:, GATE_DT:GATE_DT + SSM_HEADS].set(part(6))
    return dict(
        w_in=[part(0).astype(BF16), part(1).astype(BF16), part(4).astype(BF16),
              part(5).astype(BF16), small.astype(BF16)],
        conv_gdn=conv_gdn_w[j], conv_ssm=conv_ssm_w[j],
        gate_gdn=_gate_tile([(GATE_A, gdn_A_log[j]), (GATE_A, gdn_dt_bias[j])]),
        gate_ssm=_gate_tile([(GATE_DT, ssm_A_log[j]), (GATE_DT, ssm_dt_bias[j]),
                             (GATE_DT, ssm_D[j])]),
    )


def _prep_mla(j, norm_mix_c, w_in_c, q_a_norm, kv_a_norm, w_uq, w_uk, w_uv, q_norm, k_norm):
    half = QK_ROPE // 2
    w_in = w_in_c[j]
    d = w_in.shape[0]

    def swapped(cols):
        return jnp.concatenate([-cols[..., half:], cols[..., :half]], axis=-1)

    def swapped_gain(g):
        return jnp.concatenate([g[half:], g[:half]])

    w_kr = w_in[:, Q_LORA + KV_LORA:]
    wkr = jnp.zeros((d, LANES), F32).at[:, :QK_ROPE].set(w_kr).at[:, QK_ROPE:2 * QK_ROPE].set(
        swapped(w_kr))
    qn, kn = q_norm[j], k_norm[j]
    gkr = jnp.zeros((1, LANES), F32).at[0, :QK_ROPE].set(kn[QK_NOPE:]).at[
        0, QK_ROPE:2 * QK_ROPE].set(swapped_gain(kn[QK_NOPE:]))
    gk = jnp.zeros((1, LANES), F32).at[0, :QK_NOPE].set(kn[:QK_NOPE])
    gq_head = jnp.concatenate([qn, swapped_gain(qn[QK_NOPE:])])
    gq = jnp.tile(gq_head, MLA_HEADS).reshape(1, HEAD_W)

    uq = w_uq[j].reshape(Q_LORA, MLA_HEADS, QK_HEAD)
    wuq = jnp.concatenate([uq, swapped(uq[..., QK_NOPE:])], axis=-1).reshape(Q_LORA, HEAD_W)
    wuk = jnp.zeros((KV_LORA, MLA_HEADS, LANES), F32).at[..., :QK_NOPE].set(w_uk[j]).reshape(
        KV_LORA, HEAD_W)
    wuv = jnp.zeros((KV_LORA, MLA_HEADS, LANES), F32).at[..., :V_HEAD].set(w_uv[j]).reshape(
        KV_LORA, HEAD_W)
    vone = jnp.zeros((MLA_HEADS, LANES), F32).at[:, V_HEAD].set(1.0).reshape(1, HEAD_W)

    col = np.arange(HEAD_W)
    head, within = col // LANES, col % LANES
    lane = np.arange(LANES)
    inv_lane = QK_ROPE + head
    segq = ((within < QK_HEAD)[:, None] & (lane[None, :] == head[:, None]))
    segqt = (lane[:, None] == head[None, :])
    segk = ((within < QK_NOPE)[:, None] & (lane[None, :] == inv_lane[:, None]))
    segkt = ((lane[:, None] == inv_lane[None, :]) & (within < QK_HEAD)[None, :])
    sel = lambda m: jnp.asarray(m, BF16)

    w_abs = jnp.zeros((MLA_HEADS, LANES, Q_EXT), F32)
    w_abs = w_abs.at[:, :QK_NOPE, :KV_LORA].set(jnp.transpose(w_uk[j], (1, 2, 0)))
    w_abs = w_abs.at[:, QK_NOPE:QK_HEAD, KV_LORA:KV_LORA + QK_ROPE].set(jnp.eye(QK_ROPE, dtype=F32))
    g_abs = jnp.zeros((1, LANES), F32).at[0, :QK_NOPE].set(kn[:QK_NOPE]).at[
        0, QK_NOPE:QK_HEAD].set(1.0)
    uv = jnp.transpose(w_uv[j], (1, 0, 2)).reshape(MLA_HEADS // 2, 2, KV_LORA, V_HEAD)
    w_pairs = jnp.zeros((MLA_HEADS // 2, 2, KV_LORA, LANES), F32)
    w_pairs = w_pairs.at[:, 0, :, :V_HEAD].set(uv[:, 0]).at[:, 1, :, V_HEAD:].set(uv[:, 1])
    return dict(
        norm=norm_mix_c[j].reshape(1, d), wcq=w_in[:, :Q_LORA].astype(BF16),
        wckv=w_in[:, Q_LORA:Q_LORA + KV_LORA].astype(BF16), wkr=wkr.astype(BF16),
        qan=q_a_norm[j].reshape(1, Q_LORA), kvn=kv_a_norm[j].reshape(1, KV_LORA),
        wuq=wuq.astype(BF16), wuk=wuk.astype(BF16), wuv=wuv.astype(BF16), vone=vone, gq=gq, gk=gk, gkr=gkr,
        segq=sel(segq), segqt=sel(segqt), segk=sel(segk), segkt=sel(segkt),
        w_abs=w_abs.astype(BF16), g_abs=g_abs, w_pairs=w_pairs.astype(BF16),
    )


def _prep_ffn(layer, w_gate_up, w_down, ff_tile):
    d, two_ff = w_gate_up.shape[1:]
    d_ff = two_ff // 2
    n = d_ff // ff_tile
    split = lambda m: jnp.transpose(m.reshape(d, n, ff_tile), (1, 0, 2)).astype(BF16)
    return (split(w_gate_up[layer][:, :d_ff]), split(w_gate_up[layer][:, d_ff:]),
            w_down[layer].reshape(n, ff_tile, d).astype(BF16))


FF_TILE = 256
ROW_TILE = 512
MLA_ROW_TILE = 256
ATTN_TILE = 512
GDN_CHUNKS = 4
SSD_CHUNKS = 2


def _row_tile(rows, want):
    return want if rows % want == 0 else rows


def _trunk(x, pos0, states, cache_mla, page_table, hyb, mla, ffn, norm_mix_a, gdn_norm, conv_ssm_b,
           ssm_norm, w_out_a, w_out_c, norm_ffn):
    b, t, d = x.shape
    rows = b * t
    tm = _row_tile(rows, ROW_TILE)
    xf = x.reshape(rows, d)
    decode = states is not None

    qkv, gz, sz, xbc, small = _norm_proj(xf, norm_mix_a[0], hyb["w_in"], tm)
    if decode:
        s0, gconv0, h0, sconv0 = states
        r3 = lambda a: a.reshape(b, 1, a.shape[-1])
        n_pair = SSM_HEADS // 2
        o_gdn, s_new = _gdn_decode(r3(qkv), r3(gz), r3(small), gconv0[0], s0[0], hyb["conv_gdn"],
                                   hyb["gate_gdn"], gdn_norm[0])
        o_ssd, h_new = _ssd_decode(r3(xbc), r3(sz), r3(small), sconv0[0],
                                   h0[0].reshape(b, n_pair, 2 * SSM_P, SSM_N), hyb["conv_ssm"],
                                   conv_ssm_b[0], hyb["gate_ssm"], ssm_norm[0])
        gconv = jnp.concatenate([gconv0[0][:, 1:], r3(qkv)], axis=1)
        sconv = jnp.concatenate([sconv0[0][:, 1:], r3(xbc)], axis=1)
    else:
        r3 = lambda a: a.reshape(b, t, a.shape[-1])
        o_gdn, s_new = _gdn_prompt(r3(qkv), r3(gz), r3(small), hyb["conv_gdn"], hyb["gate_gdn"],
                                   gdn_norm[0], GDN_CHUNKS)
        o_ssd, h_new = _ssd_prompt(r3(xbc), r3(sz), r3(small), hyb["conv_ssm"], conv_ssm_b[0],
                                   hyb["gate_ssm"], ssm_norm[0], SSD_CHUNKS)
        gconv = r3(qkv)[:, t - (CONV_W - 1):]
        sconv = r3(xbc)[:, t - (CONV_W - 1):]
    h_new = h_new.reshape(b, SSM_HEADS, SSM_P, SSM_N)
    wo = w_out_a[0].astype(BF16)
    x1 = _res_ffn(xf, [o_gdn.reshape(rows, GDN_VW), o_ssd.reshape(rows, SSM_INNER)],
                  [wo[:GDN_VW], wo[GDN_VW:]], norm_ffn[0], *ffn[0], tm)

    tm_c = _row_tile(rows, MLA_ROW_TILE)
    q_scale = QK_HEAD ** -0.5 * (1.0 if decode else math.log2(math.e))
    q, k, v, mla_rows = _mla_proj(x1, t, pos0, mla, tm_c, F32 if decode else BF16, q_scale)
    if decode:
        q_ext = jnp.transpose(_absorb(q, mla["g_abs"], mla["w_abs"]), (1, 0, 2))
        ctx = _paged_attention(q_ext, mla_rows.reshape(b, 1, MLA_ROW),
                               jnp.swapaxes(cache_mla[0], 1, 2), page_table)
        attn = _ctx_out(jnp.transpose(ctx, (1, 0, 2)), mla["w_pairs"])
    else:
        r3 = lambda a: a.reshape(b, t, a.shape[-1])
        attn = _flash_attention(r3(q), r3(k), r3(v), _row_tile(t, ATTN_TILE)).reshape(rows, -1)
    x2 = _res_ffn(x1, [attn], [w_out_c[0].astype(BF16)], norm_ffn[1], *ffn[1], tm)
    return (x2.reshape(b, t, d), s_new[None], gconv[None], h_new[None], sconv[None],
            mla_rows.reshape(1, b, t, MLA_ROW))


def kernel(x_prompt, x_sample, state_gdn, state_gdn_conv, state_ssm, state_ssm_conv, cache_mla, page_table, norm_mix_a, w_in_a, conv_gdn_w, gdn_A_log, gdn_dt_bias, gdn_norm, conv_ssm_w, conv_ssm_b, ssm_A_log, ssm_dt_bias, ssm_D, ssm_norm, w_out_a, norm_mix_c, w_in_c, q_a_norm, kv_a_norm, w_uq, w_uk, w_uv, q_norm, k_norm, w_out_c, norm_ffn, w_gate_up, w_down):
    assert w_in_a.shape[0] == 1 and w_in_c.shape[0] == 1 and norm_ffn.shape[0] == 2
    hyb = _prep_hybrid(0, w_in_a, conv_gdn_w, gdn_A_log, gdn_dt_bias, conv_ssm_w, ssm_A_log,
                       ssm_dt_bias, ssm_D)
    mla = _prep_mla(0, norm_mix_c, w_in_c, q_a_norm, kv_a_norm, w_uq, w_uk, w_uv, q_norm, k_norm)
    ffn = [_prep_ffn(layer, w_gate_up, w_down, FF_TILE) for layer in range(2)]
    shared = (hyb, mla, ffn, norm_mix_a, gdn_norm, conv_ssm_b, ssm_norm, w_out_a, w_out_c, norm_ffn)
    prompt = _trunk(x_prompt, 0, None, None, None, *shared)
    past_len = page_table.shape[1] * PAGE_SIZE
    sample = _trunk(x_sample, past_len, (state_gdn, state_gdn_conv, state_ssm, state_ssm_conv),
                    cache_mla, page_table, *shared)
    return (prompt[0], sample[0]) + prompt[1:] + sample[1:]
```

```python
import functools
import math

import jax
import jax.numpy as jnp
import numpy as np
from jax import lax
from jax.experimental import pallas as pl
from jax.experimental.pallas import tpu as pltpu

F32 = jnp.float32
BF16 = jnp.bfloat16

EPS = 1e-6
CONV_W = 4
CHUNK = 64
PAGE_SIZE = 128
GDN_HEADS = 4
GDN_DK = 128
GDN_DV = 128
GDN_QK = GDN_HEADS * GDN_DK
GDN_VW = GDN_HEADS * GDN_DV
GDN_CONV_CH = 2 * GDN_QK + GDN_VW
SSM_HEADS = 8
SSM_P = 64
SSM_N = 128
SSM_GROUPS = 2
SSM_INNER = SSM_HEADS * SSM_P
SSM_BC = SSM_GROUPS * SSM_N
SSM_CONV_CH = SSM_INNER + 2 * SSM_BC
MLA_HEADS = 16
Q_LORA = 512
KV_LORA = 256
QK_NOPE = 64
QK_ROPE = 32
QK_HEAD = QK_NOPE + QK_ROPE
V_HEAD = 64
MLA_ROW = KV_LORA + QK_ROPE + MLA_HEADS
ROPE_THETA = 10000.0

LANES = 128
HEAD_W = MLA_HEADS * LANES
GATE_W = LANES
GATE_A, GATE_B, GATE_DT = 0, GDN_HEADS, 2 * GDN_HEADS
Q_EXT = 3 * LANES
VMEM_LIMIT = 48 * 1024 * 1024

_HI = lax.Precision.HIGHEST


def _cparams(*sem):
    return pltpu.CompilerParams(dimension_semantics=sem, vmem_limit_bytes=VMEM_LIMIT)


def _const_spec(shape):
    zeros = (0,) * len(shape)
    return pl.BlockSpec(shape, lambda *_: zeros, pipeline_mode=pl.Buffered(1))


def _mm(a, b):
    return jnp.dot(a.astype(BF16), b.astype(BF16), preferred_element_type=F32)


def _mm_nt(a, b):
    return lax.dot_general(a.astype(BF16), b.astype(BF16), (((1,), (1,)), ((), ())),
                           preferred_element_type=F32)


def _mm_tn(a, b):
    return lax.dot_general(a.astype(BF16), b.astype(BF16), (((0,), (0,)), ((), ())),
                           preferred_element_type=F32)


def _mm_split(a, sel):
    hi = a.astype(BF16)
    lo = (a - hi.astype(F32)).astype(BF16)
    return (jnp.dot(hi, sel, preferred_element_type=F32)
            + jnp.dot(lo, sel, preferred_element_type=F32))


def _sigmoid(x):
    return 1.0 / (1.0 + jnp.exp(-x))


def _silu(x):
    return x * _sigmoid(x)


def _softplus(x):
    return jnp.maximum(x, 0.0) + jnp.log(1.0 + jnp.exp(-jnp.abs(x)))


def _rms(x, gain):
    return x * lax.rsqrt(jnp.mean(x * x, axis=-1, keepdims=True) + EPS) * gain


def _l2n(x):
    return x * lax.rsqrt(jnp.sum(x * x, axis=-1, keepdims=True) + EPS)


def _iota(shape, dim):
    return lax.broadcasted_iota(jnp.int32, shape, dim)


def _chunk_cumsum(g, tb):
    row, col = _iota((tb, tb), 0), _iota((tb, tb), 1)
    shift = int(math.log2(CHUNK))
    same_chunk = lax.shift_right_logical(row, shift) == lax.shift_right_logical(col, shift)
    tril = jnp.where((col <= row) & same_chunk, 1.0, 0.0).astype(F32)
    return jnp.dot(tril, g, precision=_HI, preferred_element_type=F32)


def _rows_of(cols, lane0):
    sel = jnp.where(_iota((8, LANES), 1) == _iota((8, LANES), 0) + lane0, 1.0, 0.0).astype(F32)
    return lax.dot_general(sel, cols, (((1,), (1,)), ((), ())), precision=_HI,
                           preferred_element_type=F32)


def _causal_conv_block(x_ref, xp_ref, cw, tb, first):
    @pl.when(first)
    def _():
        xp_ref[0:8, :] = jnp.zeros((8, xp_ref.shape[1]), F32)

    xp_ref[8:8 + tb, :] = x_ref[...]
    off = 8 - (CONV_W - 1)
    y = cw[0:1] * xp_ref[off:off + tb, :]
    for i in range(1, CONV_W):
        y = y + cw[i:i + 1] * xp_ref[off + i:off + i + tb, :]
    xp_ref[0:8, :] = xp_ref[tb:tb + 8, :]
    return y


def _norm_proj_body(n_out, x_ref, g_ref, *refs):
    xn = _rms(x_ref[...], g_ref[...]).astype(BF16)
    for w_ref, o_ref in zip(refs[:n_out], refs[n_out:]):
        o_ref[...] = jnp.dot(xn, w_ref[...], preferred_element_type=F32)


def _norm_proj(x, gain, weights, tm):
    rows, d = x.shape
    n = len(weights)
    return pl.pallas_call(
        functools.partial(_norm_proj_body, n),
        grid=(rows // tm,),
        in_specs=[pl.BlockSpec((tm, d), lambda i: (i, 0)), _const_spec((1, d))]
        + [_const_spec(w.shape) for w in weights],
        out_specs=[pl.BlockSpec((tm, w.shape[1]), lambda i: (i, 0)) for w in weights],
        out_shape=[jax.ShapeDtypeStruct((rows, w.shape[1]), F32) for w in weights],
        compiler_params=_cparams("parallel"),
        name="norm_proj",
    )(x, gain.reshape(1, d), *weights)


def _gdn_prompt_body(nc, qkv_ref, gz_ref, sm_ref, cw_ref, gp_ref, gn_ref, o_ref, s_ref, xp_ref):
    tb = nc * CHUNK
    first = pl.program_id(1) == 0

    @pl.when(first)
    def _():
        s_ref[...] = jnp.zeros(s_ref.shape, F32)

    y = _silu(_causal_conv_block(qkv_ref.at[0], xp_ref, cw_ref[...], tb, first))
    sm, gp = sm_ref[0], gp_ref[...]
    g_all = -jnp.exp(gp[0:1]) * _softplus(sm + gp[1:2])
    beta_all = _sigmoid(sm)
    cum_all = _chunk_cumsum(g_all, tb)
    cum_rows = _rows_of(cum_all, GATE_A)
    incl = _iota((CHUNK, CHUNK), 1) <= _iota((CHUNK, CHUNK), 0)
    strict = _iota((CHUNK, CHUNK), 1) < _iota((CHUNK, CHUNK), 0)
    gn = gn_ref[...]
    pairs = [(c, h) for c in range(nc) for h in range(GDN_HEADS)]
    qs, ks, vs = [], [], []
    for h in range(GDN_HEADS):
        lo = h * GDN_DK
        qs.append(_l2n(y[:, lo:lo + GDN_DK]) * GDN_DK ** -0.5)
        ks.append(_l2n(y[:, GDN_QK + lo:GDN_QK + lo + GDN_DK]))
        vs.append(y[:, 2 * GDN_QK + lo:2 * GDN_QK + lo + GDN_DV])
    pre = []
    for c, h in pairs:
        r = slice(c * CHUNK, (c + 1) * CHUNK)
        q, k, v = qs[h][r], ks[h][r], vs[h][r]
        cum_c = cum_all[r, GATE_A + h:GATE_A + h + 1]
        cum_r = cum_rows[h:h + 1, c * CHUNK:(c + 1) * CHUNK]
        beta = beta_all[r, GATE_B + h:GATE_B + h + 1]
        decay = jnp.where(incl, jnp.exp(jnp.minimum(cum_c - cum_r, 0.0)), 0.0)
        e_c = jnp.exp(cum_c)
        kb = k * beta
        last = cum_c[CHUNK - 1:CHUNK]
        pre.append(dict(q=q.astype(BF16), k=k.astype(BF16), kb=kb, decay=decay,
                        rhs=jnp.concatenate([v * beta, kb * e_c], axis=1).astype(BF16),
                        qe=(q * e_c).astype(BF16), kd=(k * jnp.exp(last - cum_c)).astype(BF16),
                        e_last=jnp.exp(last)))
    lows = [jnp.where(strict, _mm_nt(p["kb"], p["k"]) * p["decay"], 0.0) for p in pre]
    attns = [(_mm_nt(p["q"], p["k"]) * p["decay"]).astype(BF16) for p in pre]
    eye = jnp.where(_iota((CHUNK, CHUNK), 0) == _iota((CHUNK, CHUNK), 1), 1.0, 0.0).astype(F32)
    invs = [eye - low for low in lows]
    powers = lows
    for _ in range(int(math.log2(CHUNK)) - 1):
        powers = [_mm(pw, pw) for pw in powers]
        invs = [inv + _mm(inv, pw) for inv, pw in zip(invs, powers)]
    sols = [_mm(inv, p["rhs"]) for inv, p in zip(invs, pre)]
    states = [s_ref[0, h] for h in range(GDN_HEADS)]
    for c in range(nc):
        r = slice(c * CHUNK, (c + 1) * CHUNK)
        idx = [c * GDN_HEADS + h for h in range(GDN_HEADS)]
        v_new = [sols[i][:, :GDN_DV] - _mm(sols[i][:, GDN_DV:], states[h])
                 for h, i in enumerate(idx)]
        outs = [_mm(pre[i]["qe"], states[h]) + _mm(attns[i], v_new[h]) for h, i in enumerate(idx)]
        states = [states[h] * pre[i]["e_last"] + _mm_tn(pre[i]["kd"], v_new[h])
                  for h, i in enumerate(idx)]
        for h, o in enumerate(outs):
            lo = h * GDN_DV
            gate = _silu(gz_ref[0, r, lo:lo + GDN_DV])
            o_ref[0, r, lo:lo + GDN_DV] = (_rms(o, gn) * gate).astype(o_ref.dtype)
    for h in range(GDN_HEADS):
        s_ref[0, h] = states[h]


def _gdn_prompt(qkv, gz, small, conv_w, gate_par, gdn_norm, nc):
    b, t, _ = qkv.shape
    tb = nc * CHUNK
    blk = lambda w: pl.BlockSpec((1, tb, w), lambda i, j: (i, j, 0))
    return pl.pallas_call(
        functools.partial(_gdn_prompt_body, nc),
        grid=(b, t // tb),
        in_specs=[blk(GDN_CONV_CH), blk(GDN_VW), blk(GATE_W), _const_spec(conv_w.shape),
                  _const_spec(gate_par.shape), _const_spec((1, GDN_DV))],
        out_specs=[blk(GDN_VW),
                   pl.BlockSpec((1, GDN_HEADS, GDN_DK, GDN_DV), lambda i, j: (i, 0, 0, 0))],
        out_shape=[jax.ShapeDtypeStruct((b, t, GDN_VW), BF16),
                   jax.ShapeDtypeStruct((b, GDN_HEADS, GDN_DK, GDN_DV), F32)],
        scratch_shapes=[pltpu.VMEM((tb + 8, GDN_CONV_CH), F32)],
        compiler_params=_cparams("parallel", "arbitrary"),
        name="gdn_prompt",
    )(qkv, gz, small, conv_w, gate_par, gdn_norm.reshape(1, GDN_DV))


def _ssd_prompt_body(nc, xbc_ref, sz_ref, sm_ref, cw_ref, cb_ref, sp_ref, sn_ref, o_ref, h_ref,
                     xp_ref):
    tb = nc * CHUNK
    first = pl.program_id(1) == 0

    @pl.when(first)
    def _():
        h_ref[...] = jnp.zeros(h_ref.shape, F32)

    y = _silu(_causal_conv_block(xbc_ref.at[0], xp_ref, cw_ref[...], tb, first) + cb_ref[...])
    sm, sp = sm_ref[0], sp_ref[...]
    dt_all = _softplus(sm + sp[1:2])
    cum_all = _chunk_cumsum(dt_all * -jnp.exp(sp[0:1]), tb)
    cum_rows = _rows_of(cum_all, GATE_DT)
    incl = _iota((CHUNK, CHUNK), 1) <= _iota((CHUNK, CHUNK), 0)
    lane_lo = _iota((CHUNK, LANES), 1) < SSM_P
    row_lo = _iota((2 * SSM_P, SSM_N), 0) < SSM_P
    par_lo = _iota((1, LANES), 1) < SSM_P
    sn = sn_ref[...]
    n_pair = SSM_HEADS // 2
    pairs_per_group = n_pair // SSM_GROUPS
    gw = SSM_INNER // SSM_GROUPS
    b_bf, c_bf, scores = {}, {}, {}
    for c in range(nc):
        r = slice(c * CHUNK, (c + 1) * CHUNK)
        for grp in range(SSM_GROUPS):
            lo = SSM_INNER + grp * SSM_N
            b_bf[c, grp] = y[r, lo:lo + SSM_N].astype(BF16)
            c_bf[c, grp] = y[r, lo + SSM_BC:lo + SSM_BC + SSM_N].astype(BF16)
            scores[c, grp] = _mm_nt(c_bf[c, grp], b_bf[c, grp])
    pre = {}
    for c in range(nc):
        r = slice(c * CHUNK, (c + 1) * CHUNK)
        for j in range(n_pair):
            grp = j // pairs_per_group
            cums, segs, dts = [], [], []
            for head in (2 * j, 2 * j + 1):
                cum_c = cum_all[r, GATE_DT + head:GATE_DT + head + 1]
                cum_r = cum_rows[head:head + 1, c * CHUNK:(c + 1) * CHUNK]
                cums.append(cum_c)
                segs.append(jnp.where(incl, jnp.exp(jnp.minimum(cum_c - cum_r, 0.0)), 0.0))
                dts.append(dt_all[r, GATE_DT + head:GATE_DT + head + 1])
            x_pair = y[r, j * LANES:(j + 1) * LANES]
            xdt = x_pair * jnp.where(lane_lo, dts[0], dts[1])
            last = [cm[CHUNK - 1:CHUNK] for cm in cums]
            dec = jnp.where(lane_lo, jnp.exp(last[0] - cums[0]), jnp.exp(last[1] - cums[1]))
            d_pair = jnp.where(par_lo, sp[2:3, GATE_DT + 2 * j:GATE_DT + 2 * j + 1],
                               sp[2:3, GATE_DT + 2 * j + 1:GATE_DT + 2 * j + 2])
            pre[c, j] = dict(
                y=(_mm(scores[c, grp] * segs[0], jnp.where(lane_lo, xdt, 0.0))
                   + _mm(scores[c, grp] * segs[1], jnp.where(lane_lo, 0.0, xdt))
                   + d_pair * x_pair),
                e=jnp.where(lane_lo, jnp.exp(cums[0]), jnp.exp(cums[1])),
                xdec=(xdt * dec).astype(BF16),
                e_last=jnp.where(row_lo, jnp.exp(last[0]), jnp.exp(last[1])))
    states = [h_ref[0, j] for j in range(n_pair)]
    for c in range(nc):
        r = slice(c * CHUNK, (c + 1) * CHUNK)
        ys = [pre[c, j]["y"] + _mm_nt(c_bf[c, j // pairs_per_group], states[j]) * pre[c, j]["e"]
              for j in range(n_pair)]
        states = [states[j] * pre[c, j]["e_last"]
                  + _mm_tn(pre[c, j]["xdec"], b_bf[c, j // pairs_per_group]) for j in range(n_pair)]
        for grp in range(SSM_GROUPS):
            outs = [ys[j] * _silu(sz_ref[0, r, j * LANES:(j + 1) * LANES])
                    for j in range(grp * pairs_per_group, (grp + 1) * pairs_per_group)]
            inv = lax.rsqrt(sum(jnp.sum(o * o, axis=-1, keepdims=True) for o in outs) / gw + EPS)
            for jj, o in enumerate(outs):
                lo = grp * gw + jj * LANES
                o_ref[0, r, lo:lo + LANES] = (o * inv * sn[:, lo:lo + LANES]).astype(o_ref.dtype)
    for j in range(n_pair):
        h_ref[0, j] = states[j]


def _ssd_prompt(xbc, sz, small, conv_w, conv_b, ssm_par, ssm_norm, nc):
    b, t, _ = xbc.shape
    tb = nc * CHUNK
    blk = lambda w: pl.BlockSpec((1, tb, w), lambda i, j: (i, j, 0))
    n_pair = SSM_HEADS // 2
    return pl.pallas_call(
        functools.partial(_ssd_prompt_body, nc),
        grid=(b, t // tb),
        in_specs=[blk(SSM_CONV_CH), blk(SSM_INNER), blk(GATE_W), _const_spec(conv_w.shape),
                  _const_spec((1, SSM_CONV_CH)), _const_spec(ssm_par.shape),
                  _const_spec((1, SSM_INNER))],
        out_specs=[blk(SSM_INNER),
                   pl.BlockSpec((1, n_pair, 2 * SSM_P, SSM_N), lambda i, j: (i, 0, 0, 0))],
        out_shape=[jax.ShapeDtypeStruct((b, t, SSM_INNER), BF16),
                   jax.ShapeDtypeStruct((b, n_pair, 2 * SSM_P, SSM_N), F32)],
        scratch_shapes=[pltpu.VMEM((tb + 8, SSM_CONV_CH), F32)],
        compiler_params=_cparams("parallel", "arbitrary"),
        name="ssd_prompt",
    )(xbc, sz, small, conv_w, conv_b.reshape(1, -1), ssm_par, ssm_norm.reshape(1, -1))


def _col_bcast(row):
    return jnp.transpose(jnp.broadcast_to(row, (LANES, LANES)))


def _decode_conv(new_ref, cv_ref, cw):
    y = cw[CONV_W - 1:CONV_W] * new_ref[...]
    for i in range(CONV_W - 1):
        y = y + cw[i:i + 1] * cv_ref[:, i, :]
    return y


def _gdn_decode_body(bb, qkv_ref, gz_ref, sm_ref, cv_ref, s0_ref, cw_ref, gp_ref, gn_ref, o_ref,
                     s_ref, raw_ref):
    y = _silu(_decode_conv(qkv_ref, cv_ref, cw_ref[...]))
    sm, gp = sm_ref[...], gp_ref[...]
    a_all = jnp.exp(-jnp.exp(gp[0:1]) * _softplus(sm + gp[1:2]))
    beta_all = _sigmoid(sm)
    qs, ks, vs = [], [], []
    for h in range(GDN_HEADS):
        lo = h * GDN_DK
        qs.append(_l2n(y[:, lo:lo + GDN_DK]) * GDN_DK ** -0.5)
        ks.append(_l2n(y[:, GDN_QK + lo:GDN_QK + lo + GDN_DK]))
        vs.append(y[:, 2 * GDN_QK + lo:2 * GDN_QK + lo + GDN_DV])
    pairs = [(b, h) for b in range(bb) for h in range(GDN_HEADS)]
    k_col = [_col_bcast(ks[h][b:b + 1]) for b, h in pairs]
    q_col = [_col_bcast(qs[h][b:b + 1]) for b, h in pairs]
    decayed = [s0_ref[b, h] * a_all[b:b + 1, GATE_A + h:GATE_A + h + 1] for b, h in pairs]
    v_new = [beta_all[b:b + 1, GATE_B + h:GATE_B + h + 1]
             * (vs[h][b:b + 1] - jnp.sum(k_col[i] * decayed[i], axis=0, keepdims=True))
             for i, (b, h) in enumerate(pairs)]
    for i, (b, h) in enumerate(pairs):
        state = decayed[i] + k_col[i] * v_new[i]
        s_ref[b, h] = state
        raw_ref[b:b + 1, h * GDN_DV:(h + 1) * GDN_DV] = jnp.sum(q_col[i] * state, axis=0,
                                                                keepdims=True)
    gn = gn_ref[...]
    for h in range(GDN_HEADS):
        hs = slice(h * GDN_DV, (h + 1) * GDN_DV)
        o_ref[:, hs] = (_rms(raw_ref[:, hs], gn) * _silu(gz_ref[:, hs])).astype(o_ref.dtype)


DECODE_BLOCK = 8


def _gdn_decode(qkv, gz, small, conv0, s0, conv_w, gate_par, gdn_norm):
    b = qkv.shape[0]
    bb = DECODE_BLOCK
    row = lambda w: pl.BlockSpec((bb, w), lambda i: (i, 0))
    st = pl.BlockSpec((bb, GDN_HEADS, GDN_DK, GDN_DV), lambda i: (i, 0, 0, 0))
    return pl.pallas_call(
        functools.partial(_gdn_decode_body, bb),
        grid=(b // bb,),
        in_specs=[row(GDN_CONV_CH), row(GDN_VW), row(GATE_W),
                  pl.BlockSpec((bb, CONV_W - 1, GDN_CONV_CH), lambda i: (i, 0, 0)), st,
                  _const_spec(conv_w.shape), _const_spec(gate_par.shape), _const_spec((1, GDN_DV))],
        out_specs=[row(GDN_VW), st],
        out_shape=[jax.ShapeDtypeStruct((b, GDN_VW), F32), jax.ShapeDtypeStruct(s0.shape, F32)],
        scratch_shapes=[pltpu.VMEM((bb, GDN_VW), F32)],
        compiler_params=_cparams("parallel"),
        name="gdn_decode",
    )(qkv, gz, small, conv0, s0, conv_w, gate_par, gdn_norm.reshape(1, GDN_DV))


def _ssd_decode_body(bb, xbc_ref, sz_ref, sm_ref, cv_ref, h0_ref, cw_ref, cb_ref, sp_ref, sn_ref,
                     o_ref, h_ref, raw_ref):
    y = _silu(_decode_conv(xbc_ref, cv_ref, cw_ref[...]) + cb_ref[...])
    sm, sp = sm_ref[...], sp_ref[...]
    dt_all = _softplus(sm + sp[1:2])
    da_all = jnp.exp(dt_all * -jnp.exp(sp[0:1]))
    par_lo = _iota((1, LANES), 1) < SSM_P
    row_lo = _iota((2 * SSM_P, SSM_N), 0) < SSM_P
    n_pair = SSM_HEADS // 2
    pairs_per_group = n_pair // SSM_GROUPS
    pairs = [(b, j) for b in range(bb) for j in range(n_pair)]
    x_col, states = [], []
    for b, j in pairs:
        la, lb = GATE_DT + 2 * j, GATE_DT + 2 * j + 1
        row = slice(b, b + 1)
        xdt = y[row, j * LANES:(j + 1) * LANES] * jnp.where(par_lo, dt_all[row, la:la + 1],
                                                           dt_all[row, lb:lb + 1])
        x_col.append(_col_bcast(xdt))
    for i, (b, j) in enumerate(pairs):
        la, lb = GATE_DT + 2 * j, GATE_DT + 2 * j + 1
        row = slice(b, b + 1)
        grp = j // pairs_per_group
        b_g = y[row, SSM_INNER + grp * SSM_N:SSM_INNER + (grp + 1) * SSM_N]
        state = (h0_ref[b, j] * jnp.where(row_lo, da_all[row, la:la + 1], da_all[row, lb:lb + 1])
                 + x_col[i] * b_g)
        h_ref[b, j] = state
        states.append(state)
    for i, (b, j) in enumerate(pairs):
        grp = j // pairs_per_group
        lo = SSM_INNER + SSM_BC + grp * SSM_N
        y_col = jnp.sum(states[i] * y[b:b + 1, lo:lo + SSM_N], axis=1, keepdims=True)
        raw_ref[b:b + 1, j * LANES:(j + 1) * LANES] = jnp.transpose(
            jnp.broadcast_to(y_col, (LANES, LANES)))[0:1]
    sn = sn_ref[...]
    gw = SSM_INNER // SSM_GROUPS
    for grp in range(SSM_GROUPS):
        outs = []
        for jj in range(pairs_per_group):
            j = grp * pairs_per_group + jj
            la, lb = GATE_DT + 2 * j, GATE_DT + 2 * j + 1
            js = slice(j * LANES, (j + 1) * LANES)
            d_pair = jnp.where(par_lo, sp[2:3, la:la + 1], sp[2:3, lb:lb + 1])
            outs.append((raw_ref[:, js] + d_pair * y[:, js]) * _silu(sz_ref[:, js]))
        inv = lax.rsqrt(sum(jnp.sum(o * o, axis=-1, keepdims=True) for o in outs) / gw + EPS)
        for jj, o in enumerate(outs):
            lo = grp * gw + jj * LANES
            o_ref[:, lo:lo + LANES] = (o * inv * sn[:, lo:lo + LANES]).astype(o_ref.dtype)


def _ssd_decode(xbc, sz, small, conv0, h0, conv_w, conv_b, ssm_par, ssm_norm):
    b = xbc.shape[0]
    bb = DECODE_BLOCK
    n_pair = SSM_HEADS // 2
    row = lambda w: pl.BlockSpec((bb, w), lambda i: (i, 0))
    st = pl.BlockSpec((bb, n_pair, 2 * SSM_P, SSM_N), lambda i: (i, 0, 0, 0))
    return pl.pallas_call(
        functools.partial(_ssd_decode_body, bb),
        grid=(b // bb,),
        in_specs=[row(SSM_CONV_CH), row(SSM_INNER), row(GATE_W),
                  pl.BlockSpec((bb, CONV_W - 1, SSM_CONV_CH), lambda i: (i, 0, 0)), st,
                  _const_spec(conv_w.shape), _const_spec((1, SSM_CONV_CH)),
                  _const_spec(ssm_par.shape), _const_spec((1, SSM_INNER))],
        out_specs=[row(SSM_INNER), st],
        out_shape=[jax.ShapeDtypeStruct((b, SSM_INNER), F32), jax.ShapeDtypeStruct(h0.shape, F32)],
        scratch_shapes=[pltpu.VMEM((bb, SSM_INNER), F32)],
        compiler_params=_cparams("parallel"),
        name="ssd_decode",
    )(xbc, sz, small, conv0, h0, conv_w, conv_b.reshape(1, -1), ssm_par, ssm_norm.reshape(1, -1))


def _res_ffn_body(n_a, n_ff, x_ref, *refs):
    a_refs, w_refs = refs[:n_a], refs[n_a:2 * n_a]
    g_ref, wg_ref, wu_ref, wd_ref, o_ref, xn_ref = refs[2 * n_a:]
    x1 = x_ref[...]
    for a_ref, w_ref in zip(a_refs, w_refs):
        x1 = x1 + jnp.dot(a_ref[...].astype(BF16), w_ref[...], preferred_element_type=F32)
    xn_ref[...] = _rms(x1, g_ref[...]).astype(BF16)
    o_ref[...] = x1

    def step(c, carry):
        xn = xn_ref[...]
        gate = jnp.dot(xn, wg_ref[c], preferred_element_type=F32)
        up = jnp.dot(xn, wu_ref[c], preferred_element_type=F32)
        hid = (_silu(gate) * up).astype(BF16)
        o_ref[...] += jnp.dot(hid, wd_ref[c], preferred_element_type=F32)
        return carry

    lax.fori_loop(0, n_ff, step, 0)


def _res_ffn(x, acts, projs, gain, wg, wu, wd, tm):
    rows, d = x.shape
    n_a, n_ff = len(acts), wg.shape[0]
    return pl.pallas_call(
        functools.partial(_res_ffn_body, n_a, n_ff),
        grid=(rows // tm,),
        in_specs=[pl.BlockSpec((tm, d), lambda i: (i, 0))]
        + [pl.BlockSpec((tm, a.shape[1]), lambda i: (i, 0)) for a in acts]
        + [_const_spec(p.shape) for p in projs]
        + [_const_spec((1, d)), _const_spec(wg.shape), _const_spec(wu.shape), _const_spec(wd.shape)],
        out_specs=pl.BlockSpec((tm, d), lambda i: (i, 0)),
        out_shape=jax.ShapeDtypeStruct((rows, d), F32),
        scratch_shapes=[pltpu.VMEM((tm, d), BF16)],
        compiler_params=_cparams("parallel"),
        name="res_ffn",
    )(x, *acts, *projs, gain.reshape(1, d), wg, wu, wd)


def _mla_proj_body(tm, q_scale, x_ref, cos_ref, sin_ref, g_ref, wcq_ref, wckv_ref, wkr_ref,
                   qan_ref, kvn_ref, wuq_ref, wuk_ref, wuv_ref, vone_ref, gq_ref, gk_ref, gkr_ref,
                   segq_ref, segk_ref, q_ref, k_ref, v_ref, rows_ref):
    xn = _rms(x_ref[...], g_ref[...]).astype(BF16)
    lane = _iota((tm, LANES), 1)
    cos_t, sin_t = cos_ref[...], sin_ref[...]

    kr = jnp.dot(xn, wkr_ref[...], preferred_element_type=F32)
    ssq_kr = jnp.sum(jnp.where(lane < QK_ROPE, kr * kr, 0.0), axis=-1, keepdims=True)
    krg = kr * gkr_ref[...]
    kr_rot = krg * cos_t + pltpu.roll(krg, LANES - QK_ROPE, 1) * sin_t

    c = _rms(jnp.dot(xn, wckv_ref[...], preferred_element_type=F32), kvn_ref[...])
    cb = c.astype(BF16)
    kx = jnp.dot(cb, wuk_ref[...], preferred_element_type=F32)
    ssq_k = jnp.dot((kx * kx).astype(BF16), segk_ref[...],
                    preferred_element_type=F32)
    inv_r = lax.rsqrt((ssq_k + ssq_kr) / QK_HEAD + EPS)
    tail = jnp.where(lane < QK_ROPE, kr_rot, inv_r)
    rows_ref[:, 0:KV_LORA] = c
    rows_ref[:, KV_LORA:MLA_ROW] = tail[:, 0:MLA_ROW - KV_LORA]
    kr_put = jnp.where((lane >= QK_NOPE) & (lane < QK_HEAD), pltpu.roll(kr_rot, QK_NOPE, 1), 0.0)
    gk = gk_ref[...]
    for h in range(MLA_HEADS):
        hs = slice(h * LANES, (h + 1) * LANES)
        inv_h = inv_r[:, QK_ROPE + h:QK_ROPE + h + 1]
        k_ref[:, hs] = ((kx[:, hs] * gk + kr_put) * inv_h).astype(k_ref.dtype)
    v_ref[...] = (jnp.dot(cb, wuv_ref[...], preferred_element_type=F32)
                  + vone_ref[...]).astype(v_ref.dtype)

    cq = _rms(jnp.dot(xn, wcq_ref[...], preferred_element_type=F32), qan_ref[...]).astype(BF16)
    qx = jnp.dot(cq, wuq_ref[...], preferred_element_type=F32)
    ssq_q = jnp.dot((qx * qx).astype(BF16), segq_ref[...], preferred_element_type=F32)
    inv_q = lax.rsqrt(ssq_q / QK_HEAD + EPS)
    gq = gq_ref[...]
    scale = q_scale
    in_rope = (lane >= QK_NOPE) & (lane < QK_HEAD)
    keep = jnp.where(lane < QK_NOPE, scale, jnp.where(in_rope, cos_t * scale, 0.0))
    swap = jnp.where(in_rope, sin_t * scale, 0.0)
    for h in range(MLA_HEADS):
        hs = slice(h * LANES, (h + 1) * LANES)
        t = qx[:, hs] * inv_q[:, h:h + 1] * gq
        q_ref[:, hs] = (t * keep + pltpu.roll(t, LANES - QK_ROPE, 1) * swap).astype(q_ref.dtype)


def _mla_proj(x, seq, pos0, w, tm, q_dtype, q_scale):
    rows, d = x.shape
    half = QK_ROPE // 2
    inv_freq = ROPE_THETA ** (-(jnp.arange(LANES) % half).astype(F32) / half)
    ang = (pos0 + jnp.arange(seq, dtype=F32))[:, None] * inv_freq[None, :]
    tr = tm if seq > 1 else 1
    nblk = seq // tr
    rope_spec = pl.BlockSpec((tr, LANES), lambda i: (i % nblk, 0))
    names = ("norm", "wcq", "wckv", "wkr", "qan", "kvn", "wuq", "wuk", "wuv", "vone", "gq", "gk",
             "gkr", "segq", "segk")
    consts = [w[n] for n in names]
    out_w = (HEAD_W, HEAD_W, HEAD_W, MLA_ROW)
    out_dt = (q_dtype, BF16, BF16, F32)
    return pl.pallas_call(
        functools.partial(_mla_proj_body, tm, q_scale),
        grid=(rows // tm,),
        in_specs=[pl.BlockSpec((tm, d), lambda i: (i, 0)), rope_spec, rope_spec]
        + [_const_spec(c.shape) for c in consts],
        out_specs=[pl.BlockSpec((tm, ow), lambda i: (i, 0)) for ow in out_w],
        out_shape=[jax.ShapeDtypeStruct((rows, ow), dt) for ow, dt in zip(out_w, out_dt)],
        compiler_params=_cparams("parallel"),
        name="mla_proj",
    )(x, jnp.cos(ang), jnp.sin(ang), *consts)


def _flash_body(tq, nh, q_ref, k_ref, v_ref, o_ref):
    qi = pl.program_id(2)
    causal = _iota((tq, tq), 1) <= _iota((tq, tq), 0)
    heads = [slice(j * LANES, (j + 1) * LANES) for j in range(nh)]
    q = [q_ref[0, :, hs] for hs in heads]

    def block(kb, carry, diag):
        rows = pl.ds(pl.multiple_of(kb * tq, tq), tq)
        s = [lax.dot_general(q[j], k_ref[0, rows, hs], (((1,), (1,)), ((), ())),
                             preferred_element_type=F32) for j, hs in enumerate(heads)]
        if diag:
            s = [jnp.where(causal, sj, -jnp.inf) for sj in s]
        m_new = [jnp.maximum(carry[j][0], jnp.max(s[j], axis=-1, keepdims=True)) for j in range(nh)]
        p = [jnp.exp2(s[j] - m_new[j]).astype(BF16) for j in range(nh)]
        pv = [jnp.dot(p[j], v_ref[0, rows, hs], preferred_element_type=F32)
              for j, hs in enumerate(heads)]
        return tuple((m_new[j], jnp.exp2(carry[j][0] - m_new[j]) * carry[j][1] + pv[j])
                     for j in range(nh))

    init = tuple((jnp.full((tq, 1), -jnp.inf, F32), jnp.zeros((tq, LANES), F32)) for _ in range(nh))
    carry = lax.fori_loop(0, qi, lambda kb, cr: block(kb, cr, False), init)
    outs = [acc / acc[:, V_HEAD:V_HEAD + 1] for _, acc in block(qi, carry, True)]
    low = _iota((tq, LANES), 1) < V_HEAD
    for j in range(nh // 2):
        o_ref[0, :, j * LANES:(j + 1) * LANES] = jnp.where(
            low, outs[2 * j], pltpu.roll(outs[2 * j + 1], V_HEAD, 1)).astype(o_ref.dtype)


def _flash_attention(q, k, v, tq, nh):
    b, t, _ = q.shape
    kv_spec = pl.BlockSpec((1, t, nh * LANES), lambda i, j, n: (i, 0, j))
    return pl.pallas_call(
        functools.partial(_flash_body, tq, nh),
        grid=(b, MLA_HEADS // nh, t // tq),
        in_specs=[pl.BlockSpec((1, tq, nh * LANES), lambda i, j, n: (i, n, j)), kv_spec, kv_spec],
        out_specs=pl.BlockSpec((1, tq, nh * V_HEAD), lambda i, j, n: (i, n, j)),
        out_shape=jax.ShapeDtypeStruct((b, t, MLA_HEADS * V_HEAD), BF16),
        compiler_params=_cparams("parallel", "parallel", "arbitrary"),
        name="flash_attention",
    )(q, k, v)


def _absorb_body(q_ref, g_ref, w_ref, o_ref):
    t = (q_ref[...] * g_ref[...]).astype(BF16)
    o_ref[...] = jnp.dot(t, w_ref[...], preferred_element_type=F32).astype(o_ref.dtype)


def _absorb(q, gain, w_abs):
    b = q.shape[0]
    return pl.pallas_call(
        _absorb_body,
        grid=(MLA_HEADS,),
        in_specs=[pl.BlockSpec((b, LANES), lambda h: (0, h)), _const_spec((1, LANES)),
                  pl.BlockSpec((None, LANES, Q_EXT), lambda h: (h, 0, 0))],
        out_specs=pl.BlockSpec((None, b, Q_EXT), lambda h: (h, 0, 0)),
        out_shape=jax.ShapeDtypeStruct((MLA_HEADS, b, Q_EXT), BF16),
        compiler_params=_cparams("parallel"),
        name="mla_absorb",
    )(q, gain, w_abs)


def _paged_attn_body(n_pages, pt_ref, q_ref, rn_ref, *refs):
    del pt_ref
    page_refs, o_ref, s_ref, pb_ref = refs[:n_pages], *refs[n_pages:]
    q = q_ref[0]
    inv_lo = MLA_ROW - MLA_HEADS
    for p in range(n_pages):
        page = page_refs[p][...]
        pb_ref[p] = page.astype(BF16)
        s = jnp.dot(q[:, :MLA_ROW], pb_ref[p], preferred_element_type=F32)
        s_ref[:, p * PAGE_SIZE:(p + 1) * PAGE_SIZE] = s * page[inv_lo:MLA_ROW]
    rn = rn_ref[0]
    qf = q[:, :MLA_ROW].astype(F32)
    pick = _iota((MLA_HEADS, MLA_ROW), 1) == _iota((MLA_HEADS, MLA_ROW), 0) + inv_lo
    inv_new = jnp.sum(jnp.where(pick, rn, 0.0), axis=-1, keepdims=True)
    s_new = jnp.sum(qf * rn, axis=-1, keepdims=True) * inv_new
    m = jnp.maximum(jnp.max(s_ref[...], axis=-1, keepdims=True), s_new)
    e_new = jnp.exp(s_new - m)
    den = e_new
    ctx = e_new * rn[:, 0:KV_LORA]
    for p in range(n_pages):
        e = jnp.exp(s_ref[:, p * PAGE_SIZE:(p + 1) * PAGE_SIZE] - m)
        den = den + jnp.sum(e, axis=-1, keepdims=True)
        ctx = ctx + lax.dot_general(e.astype(BF16), pb_ref[p, 0:KV_LORA, :],
                                    (((1,), (1,)), ((), ())), preferred_element_type=F32)
    o_ref[0] = ctx / den


def _paged_attention(q_ext, rows_new, cache_t, page_table):
    b, n_pages = page_table.shape
    page_specs = [pl.BlockSpec((None, MLA_ROW, PAGE_SIZE),
                               functools.partial(lambda p, i, pt: (pt[i, p], 0, 0), p))
                  for p in range(n_pages)]
    grid_spec = pltpu.PrefetchScalarGridSpec(
        num_scalar_prefetch=1,
        grid=(b,),
        in_specs=[pl.BlockSpec((1, MLA_HEADS, Q_EXT), lambda i, pt: (i, 0, 0)),
                  pl.BlockSpec((1, 1, MLA_ROW), lambda i, pt: (i, 0, 0))] + page_specs,
        out_specs=pl.BlockSpec((1, MLA_HEADS, KV_LORA), lambda i, pt: (i, 0, 0)),
        scratch_shapes=[pltpu.VMEM((MLA_HEADS, n_pages * PAGE_SIZE), F32),
                        pltpu.VMEM((n_pages, MLA_ROW, PAGE_SIZE), BF16)],
    )
    return pl.pallas_call(
        functools.partial(_paged_attn_body, n_pages),
        grid_spec=grid_spec,
        out_shape=jax.ShapeDtypeStruct((b, MLA_HEADS, KV_LORA), F32),
        compiler_params=_cparams("arbitrary"),
        name="paged_attention",
    )(page_table, q_ext, rows_new, *([cache_t] * n_pages))


def _ctx_out_body(c_ref, w_ref, o_ref):
    o_ref[...] = (jnp.dot(c_ref[0].astype(BF16), w_ref[0], preferred_element_type=F32)
                  + jnp.dot(c_ref[1].astype(BF16), w_ref[1], preferred_element_type=F32)
                  ).astype(o_ref.dtype)


def _ctx_out(ctx_t, w_pairs):
    b = ctx_t.shape[1]
    return pl.pallas_call(
        _ctx_out_body,
        grid=(MLA_HEADS // 2,),
        in_specs=[pl.BlockSpec((2, b, KV_LORA), lambda j: (j, 0, 0)),
                  pl.BlockSpec((None, 2, KV_LORA, LANES), lambda j: (j, 0, 0, 0))],
        out_specs=pl.BlockSpec((b, LANES), lambda j: (0, j)),
        out_shape=jax.ShapeDtypeStruct((b, MLA_HEADS * V_HEAD), BF16),
        compiler_params=_cparams("parallel"),
        name="mla_ctx_out",
    )(ctx_t, w_pairs)


def _gate_tile(rows):
    tile = jnp.zeros((8, GATE_W), F32)
    for i, (off, vec) in enumerate(rows):
        tile = tile.at[i, off:off + vec.shape[0]].set(vec.astype(F32))
    return tile


def _prep_hybrid(j, w_in_a, conv_gdn_w, gdn_A_log, gdn_dt_bias, conv_ssm_w, ssm_A_log,
                 ssm_dt_bias, ssm_D):
    widths = (GDN_CONV_CH, GDN_VW, GDN_HEADS, GDN_HEADS, SSM_INNER, SSM_CONV_CH, SSM_HEADS)
    offs = np.concatenate([[0], np.cumsum(widths)])
    w = w_in_a[j]
    d = w.shape[0]
    part = lambda i: w[:, offs[i]:offs[i + 1]]
    small = jnp.zeros((d, GATE_W), F32)
    small = small.at[:, GATE_A:GATE_A + GDN_HEADS].set(part(2))
    small = small.at[:, GATE_B:GATE_B + GDN_HEADS].set(part(3))
    small = small.at[:, GATE_DT:GATE_DT + SSM_HEADS].set(part(6))
    return dict(
        w_in=[part(0).astype(BF16), part(1).astype(BF16), part(4).astype(BF16),
              part(5).astype(BF16), small.astype(BF16)],
        conv_gdn=conv_gdn_w[j], conv_ssm=conv_ssm_w[j],
        gate_gdn=_gate_tile([(GATE_A, gdn_A_log[j]), (GATE_A, gdn_dt_bias[j])]),
        gate_ssm=_gate_tile([(GATE_DT, ssm_A_log[j]), (GATE_DT, ssm_dt_bias[j]),
                             (GATE_DT, ssm_D[j])]),
    )


def _prep_mla(j, norm_mix_c, w_in_c, q_a_norm, kv_a_norm, w_uq, w_uk, w_uv, q_norm, k_norm):
    half = QK_ROPE // 2
    w_in = w_in_c[j]
    d = w_in.shape[0]

    def swapped(cols):
        return jnp.concatenate([-cols[..., half:], cols[..., :half]], axis=-1)

    def swapped_gain(g):
        return jnp.concatenate([g[half:], g[:half]])

    w_kr = w_in[:, Q_LORA + KV_LORA:]
    wkr = jnp.zeros((d, LANES), F32).at[:, :QK_ROPE].set(w_kr).at[:, QK_ROPE:2 * QK_ROPE].set(
        swapped(w_kr))
    qn, kn = q_norm[j], k_norm[j]
    gkr = jnp.zeros((1, LANES), F32).at[0, :QK_ROPE].set(kn[QK_NOPE:]).at[
        0, QK_ROPE:2 * QK_ROPE].set(swapped_gain(kn[QK_NOPE:]))
    gk = jnp.zeros((1, LANES), F32).at[0, :QK_NOPE].set(kn[:QK_NOPE])
    gq_head = jnp.concatenate([qn, swapped_gain(qn[QK_NOPE:])])
    gq = gq_head.reshape(1, LANES)

    uq = w_uq[j].reshape(Q_LORA, MLA_HEADS, QK_HEAD)
    wuq = jnp.concatenate([uq, swapped(uq[..., QK_NOPE:])], axis=-1).reshape(Q_LORA, HEAD_W)
    wuk = jnp.zeros((KV_LORA, MLA_HEADS, LANES), F32).at[..., :QK_NOPE].set(w_uk[j]).reshape(
        KV_LORA, HEAD_W)
    wuv = jnp.zeros((KV_LORA, MLA_HEADS, LANES), F32).at[..., :V_HEAD].set(w_uv[j]).reshape(
        KV_LORA, HEAD_W)
    vone = jnp.zeros((MLA_HEADS, LANES), F32).at[:, V_HEAD].set(1.0).reshape(1, HEAD_W)

    col = np.arange(HEAD_W)
    head, within = col // LANES, col % LANES
    lane = np.arange(LANES)
    inv_lane = QK_ROPE + head
    segq = ((within < QK_HEAD)[:, None] & (lane[None, :] == head[:, None]))
    segk = ((within < QK_NOPE)[:, None] & (lane[None, :] == inv_lane[:, None]))
    sel = lambda m: jnp.asarray(m, BF16)

    w_abs = jnp.zeros((MLA_HEADS, LANES, Q_EXT), F32)
    w_abs = w_abs.at[:, :QK_NOPE, :KV_LORA].set(jnp.transpose(w_uk[j], (1, 2, 0)))
    w_abs = w_abs.at[:, QK_NOPE:QK_HEAD, KV_LORA:KV_LORA + QK_ROPE].set(jnp.eye(QK_ROPE, dtype=F32))
    g_abs = jnp.zeros((1, LANES), F32).at[0, :QK_NOPE].set(kn[:QK_NOPE]).at[
        0, QK_NOPE:QK_HEAD].set(1.0)
    uv = jnp.transpose(w_uv[j], (1, 0, 2)).reshape(MLA_HEADS // 2, 2, KV_LORA, V_HEAD)
    w_pairs = jnp.zeros((MLA_HEADS // 2, 2, KV_LORA, LANES), F32)
    w_pairs = w_pairs.at[:, 0, :, :V_HEAD].set(uv[:, 0]).at[:, 1, :, V_HEAD:].set(uv[:, 1])
    return dict(
        norm=norm_mix_c[j].reshape(1, d), wcq=w_in[:, :Q_LORA].astype(BF16),
        wckv=w_in[:, Q_LORA:Q_LORA + KV_LORA].astype(BF16), wkr=wkr.astype(BF16),
        qan=q_a_norm[j].reshape(1, Q_LORA), kvn=kv_a_norm[j].reshape(1, KV_LORA),
        wuq=wuq.astype(BF16), wuk=wuk.astype(BF16), wuv=wuv.astype(BF16), vone=vone, gq=gq, gk=gk, gkr=gkr,
        segq=sel(segq), segk=sel(segk),
        w_abs=w_abs.astype(BF16), g_abs=g_abs, w_pairs=w_pairs.astype(BF16),
    )


def _prep_ffn(layer, w_gate_up, w_down, ff_tile):
    d, two_ff = w_gate_up.shape[1:]
    d_ff = two_ff // 2
    n = d_ff // ff_tile
    split = lambda m: jnp.transpose(m.reshape(d, n, ff_tile), (1, 0, 2)).astype(BF16)
    return (split(w_gate_up[layer][:, :d_ff]), split(w_gate_up[layer][:, d_ff:]),
            w_down[layer].reshape(n, ff_tile, d).astype(BF16))


FF_TILE = 256
ROW_TILE = 512
FFN_ROW_TILE = 1024
MLA_ROW_TILE = 256
ATTN_TILE = 512
ATTN_HEADS = 4
GDN_CHUNKS = 4
SSD_CHUNKS = 4


def _row_tile(rows, want):
    return want if rows % want == 0 else rows


def _trunk(x, pos0, states, cache_mla, page_table, hyb, mla, ffn, norm_mix_a, gdn_norm, conv_ssm_b,
           ssm_norm, w_out_a, w_out_c, norm_ffn):
    b, t, d = x.shape
    rows = b * t
    tm = _row_tile(rows, ROW_TILE)
    tm_f = _row_tile(rows, FFN_ROW_TILE)
    xf = x.reshape(rows, d)
    decode = states is not None

    qkv, gz, sz, xbc, small = _norm_proj(xf, norm_mix_a[0], hyb["w_in"], tm)
    if decode:
        s0, gconv0, h0, sconv0 = states
        r3 = lambda a: a.reshape(b, 1, a.shape[-1])
        n_pair = SSM_HEADS // 2
        o_gdn, s_new = _gdn_decode(qkv, gz, small, gconv0[0], s0[0], hyb["conv_gdn"],
                                   hyb["gate_gdn"], gdn_norm[0])
        o_ssd, h_new = _ssd_decode(xbc, sz, small, sconv0[0],
                                   h0[0].reshape(b, n_pair, 2 * SSM_P, SSM_N), hyb["conv_ssm"],
                                   conv_ssm_b[0], hyb["gate_ssm"], ssm_norm[0])
        gconv = jnp.concatenate([gconv0[0][:, 1:], r3(qkv)], axis=1)
        sconv = jnp.concatenate([sconv0[0][:, 1:], r3(xbc)], axis=1)
    else:
        r3 = lambda a: a.reshape(b, t, a.shape[-1])
        o_gdn, s_new = _gdn_prompt(r3(qkv), r3(gz), r3(small), hyb["conv_gdn"], hyb["gate_gdn"],
                                   gdn_norm[0], GDN_CHUNKS)
        o_ssd, h_new = _ssd_prompt(r3(xbc), r3(sz), r3(small), hyb["conv_ssm"], conv_ssm_b[0],
                                   hyb["gate_ssm"], ssm_norm[0], SSD_CHUNKS)
        gconv = r3(qkv)[:, t - (CONV_W - 1):]
        sconv = r3(xbc)[:, t - (CONV_W - 1):]
    h_new = h_new.reshape(b, SSM_HEADS, SSM_P, SSM_N)
    wo = w_out_a[0].astype(BF16)
    x1 = _res_ffn(xf, [o_gdn.reshape(rows, GDN_VW), o_ssd.reshape(rows, SSM_INNER)],
                  [wo[:GDN_VW], wo[GDN_VW:]], norm_ffn[0], *ffn[0], tm_f)

    tm_c = _row_tile(rows, MLA_ROW_TILE)
    q_scale = QK_HEAD ** -0.5 * (1.0 if decode else math.log2(math.e))
    q, k, v, mla_rows = _mla_proj(x1, t, pos0, mla, tm_c, F32 if decode else BF16, q_scale)
    if decode:
        q_ext = jnp.transpose(_absorb(q, mla["g_abs"], mla["w_abs"]), (1, 0, 2))
        ctx = _paged_attention(q_ext, mla_rows.reshape(b, 1, MLA_ROW),
                               jnp.swapaxes(cache_mla[0], 1, 2), page_table)
        attn = _ctx_out(jnp.transpose(ctx, (1, 0, 2)), mla["w_pairs"])
    else:
        r3 = lambda a: a.reshape(b, t, a.shape[-1])
        attn = _flash_attention(r3(q), r3(k), r3(v), _row_tile(t, ATTN_TILE),
                                ATTN_HEADS).reshape(rows, -1)
    x2 = _res_ffn(x1, [attn], [w_out_c[0].astype(BF16)], norm_ffn[1], *ffn[1], tm_f)
    return (x2.reshape(b, t, d), s_new[None], gconv[None], h_new[None], sconv[None],
            mla_rows.reshape(1, b, t, MLA_ROW))


def kernel(x_prompt, x_sample, state_gdn, state_gdn_conv, state_ssm, state_ssm_conv, cache_mla, page_table, norm_mix_a, w_in_a, conv_gdn_w, gdn_A_log, gdn_dt_bias, gdn_norm, conv_ssm_w, conv_ssm_b, ssm_A_log, ssm_dt_bias, ssm_D, ssm_norm, w_out_a, norm_mix_c, w_in_c, q_a_norm, kv_a_norm, w_uq, w_uk, w_uv, q_norm, k_norm, w_out_c, norm_ffn, w_gate_up, w_down):
    assert w_in_a.shape[0] == 1 and w_in_c.shape[0] == 1 and norm_ffn.shape[0] == 2
    hyb = _prep_hybrid(0, w_in_a, conv_gdn_w, gdn_A_log, gdn_dt_bias, conv_ssm_w, ssm_A_log,
                       ssm_dt_bias, ssm_D)
    mla = _prep_mla(0, norm_mix_c, w_in_c, q_a_norm, kv_a_norm, w_uq, w_uk, w_uv, q_norm, k_norm)
    ffn = [_prep_ffn(layer, w_gate_up, w_down, FF_TILE) for layer in range(2)]
    shared = (hyb, mla, ffn, norm_mix_a, gdn_norm, conv_ssm_b, ssm_norm, w_out_a, w_out_c, norm_ffn)
    prompt = _trunk(x_prompt, 0, None, None, None, *shared)
    past_len = page_table.shape[1] * PAGE_SIZE
    sample = _trunk(x_sample, past_len, (state_gdn, state_gdn_conv, state_ssm, state_ssm_conv),
                    cache_mla, page_table, *shared)
    return (prompt[0], sample[0]) + prompt[1:] + sample[1:]
```

```python
import functools
import math

import jax
import jax.numpy as jnp
import numpy as np
from jax import lax
from jax.experimental import pallas as pl
from jax.experimental.pallas import tpu as pltpu

F32 = jnp.float32
BF16 = jnp.bfloat16

EPS = 1e-6
CONV_W = 4
CHUNK = 64
PAGE_SIZE = 128
GDN_HEADS = 4
GDN_DK = 128
GDN_DV = 128
GDN_QK = GDN_HEADS * GDN_DK
GDN_VW = GDN_HEADS * GDN_DV
GDN_CONV_CH = 2 * GDN_QK + GDN_VW
SSM_HEADS = 8
SSM_P = 64
SSM_N = 128
SSM_GROUPS = 2
SSM_INNER = SSM_HEADS * SSM_P
SSM_BC = SSM_GROUPS * SSM_N
SSM_CONV_CH = SSM_INNER + 2 * SSM_BC
MLA_HEADS = 16
Q_LORA = 512
KV_LORA = 256
QK_NOPE = 64
QK_ROPE = 32
QK_HEAD = QK_NOPE + QK_ROPE
V_HEAD = 64
MLA_ROW = KV_LORA + QK_ROPE + MLA_HEADS
ROPE_THETA = 10000.0

LANES = 128
HEAD_W = MLA_HEADS * LANES
GATE_W = LANES
GATE_A, GATE_B, GATE_DT = 0, GDN_HEADS, 2 * GDN_HEADS
Q_EXT = 3 * LANES
VMEM_LIMIT = 48 * 1024 * 1024

_HI = lax.Precision.HIGHEST


def _cparams(*sem):
    return pltpu.CompilerParams(dimension_semantics=sem, vmem_limit_bytes=VMEM_LIMIT)


def _const_spec(shape):
    zeros = (0,) * len(shape)
    return pl.BlockSpec(shape, lambda *_: zeros, pipeline_mode=pl.Buffered(1))


def _mm(a, b):
    return jnp.dot(a.astype(BF16), b.astype(BF16), preferred_element_type=F32)


def _mm_nt(a, b):
    return lax.dot_general(a.astype(BF16), b.astype(BF16), (((1,), (1,)), ((), ())),
                           preferred_element_type=F32)


def _mm_tn(a, b):
    return lax.dot_general(a.astype(BF16), b.astype(BF16), (((0,), (0,)), ((), ())),
                           preferred_element_type=F32)


def _mm_split(a, sel):
    hi = a.astype(BF16)
    lo = (a - hi.astype(F32)).astype(BF16)
    return (jnp.dot(hi, sel, preferred_element_type=F32)
            + jnp.dot(lo, sel, preferred_element_type=F32))


def _sigmoid(x):
    return 1.0 / (1.0 + jnp.exp(-x))


def _silu(x):
    return x * _sigmoid(x)


def _softplus(x):
    return jnp.maximum(x, 0.0) + jnp.log(1.0 + jnp.exp(-jnp.abs(x)))


def _rms(x, gain):
    return x * lax.rsqrt(jnp.mean(x * x, axis=-1, keepdims=True) + EPS) * gain


def _l2n(x):
    return x * lax.rsqrt(jnp.sum(x * x, axis=-1, keepdims=True) + EPS)


def _iota(shape, dim):
    return lax.broadcasted_iota(jnp.int32, shape, dim)


def _chunk_cumsum(g, tb):
    row, col = _iota((tb, tb), 0), _iota((tb, tb), 1)
    shift = int(math.log2(CHUNK))
    same_chunk = lax.shift_right_logical(row, shift) == lax.shift_right_logical(col, shift)
    tril = jnp.where((col <= row) & same_chunk, 1.0, 0.0).astype(F32)
    return jnp.dot(tril, g, precision=_HI, preferred_element_type=F32)


def _rows_of(cols, lane0):
    sel = jnp.where(_iota((8, LANES), 1) == _iota((8, LANES), 0) + lane0, 1.0, 0.0).astype(F32)
    return lax.dot_general(sel, cols, (((1,), (1,)), ((), ())), precision=_HI,
                           preferred_element_type=F32)


def _causal_conv_block(x_ref, xp_ref, cw, tb, first):
    @pl.when(first)
    def _():
        xp_ref[0:8, :] = jnp.zeros((8, xp_ref.shape[1]), F32)

    xp_ref[8:8 + tb, :] = x_ref[...]
    off = 8 - (CONV_W - 1)
    y = cw[0:1] * xp_ref[off:off + tb, :]
    for i in range(1, CONV_W):
        y = y + cw[i:i + 1] * xp_ref[off + i:off + i + tb, :]
    xp_ref[0:8, :] = xp_ref[tb:tb + 8, :]
    return y


def _norm_proj_body(n_out, x_ref, g_ref, *refs):
    xn = _rms(x_ref[...], g_ref[...]).astype(BF16)
    for w_ref, o_ref in zip(refs[:n_out], refs[n_out:]):
        o_ref[...] = jnp.dot(xn, w_ref[...], preferred_element_type=F32)


def _norm_proj(x, gain, weights, tm):
    rows, d = x.shape
    n = len(weights)
    return pl.pallas_call(
        functools.partial(_norm_proj_body, n),
        grid=(rows // tm,),
        in_specs=[pl.BlockSpec((tm, d), lambda i: (i, 0)), _const_spec((1, d))]
        + [_const_spec(w.shape) for w in weights],
        out_specs=[pl.BlockSpec((tm, w.shape[1]), lambda i: (i, 0)) for w in weights],
        out_shape=[jax.ShapeDtypeStruct((rows, w.shape[1]), F32) for w in weights],
        compiler_params=_cparams("parallel"),
        name="norm_proj",
    )(x, gain.reshape(1, d), *weights)


def _gdn_prompt_body(nc, qkv_ref, gz_ref, sm_ref, cw_ref, gp_ref, gn_ref, o_ref, s_ref, xp_ref):
    tb = nc * CHUNK
    first = pl.program_id(1) == 0

    @pl.when(first)
    def _():
        s_ref[...] = jnp.zeros(s_ref.shape, F32)

    y = _silu(_causal_conv_block(qkv_ref.at[0], xp_ref, cw_ref[...], tb, first))
    sm, gp = sm_ref[0], gp_ref[...]
    g_all = -jnp.exp(gp[0:1]) * _softplus(sm + gp[1:2])
    beta_all = _sigmoid(sm)
    cum_all = _chunk_cumsum(g_all, tb)
    cum_rows = _rows_of(cum_all, GATE_A)
    incl = _iota((CHUNK, CHUNK), 1) <= _iota((CHUNK, CHUNK), 0)
    strict = _iota((CHUNK, CHUNK), 1) < _iota((CHUNK, CHUNK), 0)
    gn = gn_ref[...]
    pairs = [(c, h) for c in range(nc) for h in range(GDN_HEADS)]
    qs, ks, vs = [], [], []
    for h in range(GDN_HEADS):
        lo = h * GDN_DK
        qs.append(_l2n(y[:, lo:lo + GDN_DK]) * GDN_DK ** -0.5)
        ks.append(_l2n(y[:, GDN_QK + lo:GDN_QK + lo + GDN_DK]))
        vs.append(y[:, 2 * GDN_QK + lo:2 * GDN_QK + lo + GDN_DV])
    pre = []
    for c, h in pairs:
        r = slice(c * CHUNK, (c + 1) * CHUNK)
        q, k, v = qs[h][r], ks[h][r], vs[h][r]
        cum_c = cum_all[r, GATE_A + h:GATE_A + h + 1]
        cum_r = cum_rows[h:h + 1, c * CHUNK:(c + 1) * CHUNK]
        beta = beta_all[r, GATE_B + h:GATE_B + h + 1]
        decay = jnp.where(incl, jnp.exp(jnp.minimum(cum_c - cum_r, 0.0)), 0.0)
        e_c = jnp.exp(cum_c)
        kb = k * beta
        last = cum_c[CHUNK - 1:CHUNK]
        pre.append(dict(q=q.astype(BF16), k=k.astype(BF16), kb=kb, decay=decay,
                        rhs=jnp.concatenate([v * beta, kb * e_c], axis=1).astype(BF16),
                        qe=(q * e_c).astype(BF16), kd=(k * jnp.exp(last - cum_c)).astype(BF16),
                        e_last=jnp.exp(last)))
    lows = [jnp.where(strict, _mm_nt(p["kb"], p["k"]) * p["decay"], 0.0) for p in pre]
    attns = [(_mm_nt(p["q"], p["k"]) * p["decay"]).astype(BF16) for p in pre]
    eye = jnp.where(_iota((CHUNK, CHUNK), 0) == _iota((CHUNK, CHUNK), 1), 1.0, 0.0).astype(F32)
    invs = [eye - low for low in lows]
    powers = lows
    for _ in range(int(math.log2(CHUNK)) - 1):
        powers = [_mm(pw, pw) for pw in powers]
        invs = [inv + _mm(inv, pw) for inv, pw in zip(invs, powers)]
    sols = [_mm(inv, p["rhs"]) for inv, p in zip(invs, pre)]
    states = [s_ref[0, h] for h in range(GDN_HEADS)]
    for c in range(nc):
        r = slice(c * CHUNK, (c + 1) * CHUNK)
        idx = [c * GDN_HEADS + h for h in range(GDN_HEADS)]
        v_new = [sols[i][:, :GDN_DV] - _mm(sols[i][:, GDN_DV:], states[h])
                 for h, i in enumerate(idx)]
        outs = [_mm(pre[i]["qe"], states[h]) + _mm(attns[i], v_new[h]) for h, i in enumerate(idx)]
        states = [states[h] * pre[i]["e_last"] + _mm_tn(pre[i]["kd"], v_new[h])
                  for h, i in enumerate(idx)]
        for h, o in enumerate(outs):
            lo = h * GDN_DV
            gate = _silu(gz_ref[0, r, lo:lo + GDN_DV])
            o_ref[0, r, lo:lo + GDN_DV] = (_rms(o, gn) * gate).astype(o_ref.dtype)
    for h in range(GDN_HEADS):
        s_ref[0, h] = states[h]


def _gdn_prompt(qkv, gz, small, conv_w, gate_par, gdn_norm, nc):
    b, t, _ = qkv.shape
    tb = nc * CHUNK
    blk = lambda w: pl.BlockSpec((1, tb, w), lambda i, j: (i, j, 0))
    return pl.pallas_call(
        functools.partial(_gdn_prompt_body, nc),
        grid=(b, t // tb),
        in_specs=[blk(GDN_CONV_CH), blk(GDN_VW), blk(GATE_W), _const_spec(conv_w.shape),
                  _const_spec(gate_par.shape), _const_spec((1, GDN_DV))],
        out_specs=[blk(GDN_VW),
                   pl.BlockSpec((1, GDN_HEADS, GDN_DK, GDN_DV), lambda i, j: (i, 0, 0, 0))],
        out_shape=[jax.ShapeDtypeStruct((b, t, GDN_VW), BF16),
                   jax.ShapeDtypeStruct((b, GDN_HEADS, GDN_DK, GDN_DV), F32)],
        scratch_shapes=[pltpu.VMEM((tb + 8, GDN_CONV_CH), F32)],
        compiler_params=_cparams("parallel", "arbitrary"),
        name="gdn_prompt",
    )(qkv, gz, small, conv_w, gate_par, gdn_norm.reshape(1, GDN_DV))


def _ssd_prompt_body(nc, xbc_ref, sz_ref, sm_ref, cw_ref, cb_ref, sp_ref, sn_ref, o_ref, h_ref,
                     xp_ref):
    tb = nc * CHUNK
    first = pl.program_id(1) == 0

    @pl.when(first)
    def _():
        h_ref[...] = jnp.zeros(h_ref.shape, F32)

    y = _silu(_causal_conv_block(xbc_ref.at[0], xp_ref, cw_ref[...], tb, first) + cb_ref[...])
    sm, sp = sm_ref[0], sp_ref[...]
    dt_all = _softplus(sm + sp[1:2])
    cum_all = _chunk_cumsum(dt_all * -jnp.exp(sp[0:1]), tb)
    cum_rows = _rows_of(cum_all, GATE_DT)
    incl = _iota((CHUNK, CHUNK), 1) <= _iota((CHUNK, CHUNK), 0)
    lane_lo = _iota((CHUNK, LANES), 1) < SSM_P
    row_lo = _iota((2 * SSM_P, SSM_N), 0) < SSM_P
    par_lo = _iota((1, LANES), 1) < SSM_P
    sn = sn_ref[...]
    n_pair = SSM_HEADS // 2
    pairs_per_group = n_pair // SSM_GROUPS
    gw = SSM_INNER // SSM_GROUPS
    b_bf, c_bf, scores = {}, {}, {}
    for c in range(nc):
        r = slice(c * CHUNK, (c + 1) * CHUNK)
        for grp in range(SSM_GROUPS):
            lo = SSM_INNER + grp * SSM_N
            b_bf[c, grp] = y[r, lo:lo + SSM_N].astype(BF16)
            c_bf[c, grp] = y[r, lo + SSM_BC:lo + SSM_BC + SSM_N].astype(BF16)
            scores[c, grp] = _mm_nt(c_bf[c, grp], b_bf[c, grp])
    pre = {}
    for c in range(nc):
        r = slice(c * CHUNK, (c + 1) * CHUNK)
        for j in range(n_pair):
            grp = j // pairs_per_group
            cums, segs, dts = [], [], []
            for head in (2 * j, 2 * j + 1):
                cum_c = cum_all[r, GATE_DT + head:GATE_DT + head + 1]
                cum_r = cum_rows[head:head + 1, c * CHUNK:(c + 1) * CHUNK]
                cums.append(cum_c)
                segs.append(jnp.where(incl, jnp.exp(jnp.minimum(cum_c - cum_r, 0.0)), 0.0))
                dts.append(dt_all[r, GATE_DT + head:GATE_DT + head + 1])
            x_pair = y[r, j * LANES:(j + 1) * LANES]
            xdt = x_pair * jnp.where(lane_lo, dts[0], dts[1])
            last = [cm[CHUNK - 1:CHUNK] for cm in cums]
            dec = jnp.where(lane_lo, jnp.exp(last[0] - cums[0]), jnp.exp(last[1] - cums[1]))
            d_pair = jnp.where(par_lo, sp[2:3, GATE_DT + 2 * j:GATE_DT + 2 * j + 1],
                               sp[2:3, GATE_DT + 2 * j + 1:GATE_DT + 2 * j + 2])
            pre[c, j] = dict(
                y=(_mm(scores[c, grp] * segs[0], jnp.where(lane_lo, xdt, 0.0))
                   + _mm(scores[c, grp] * segs[1], jnp.where(lane_lo, 0.0, xdt))
                   + d_pair * x_pair),
                e=jnp.where(lane_lo, jnp.exp(cums[0]), jnp.exp(cums[1])),
                xdec=(xdt * dec).astype(BF16),
                e_last=jnp.where(row_lo, jnp.exp(last[0]), jnp.exp(last[1])))
    states = [h_ref[0, j] for j in range(n_pair)]
    for c in range(nc):
        r = slice(c * CHUNK, (c + 1) * CHUNK)
        ys = [pre[c, j]["y"] + _mm_nt(c_bf[c, j // pairs_per_group], states[j]) * pre[c, j]["e"]
              for j in range(n_pair)]
        states = [states[j] * pre[c, j]["e_last"]
                  + _mm_tn(pre[c, j]["xdec"], b_bf[c, j // pairs_per_group]) for j in range(n_pair)]
        for grp in range(SSM_GROUPS):
            outs = [ys[j] * _silu(sz_ref[0, r, j * LANES:(j + 1) * LANES])
                    for j in range(grp * pairs_per_group, (grp + 1) * pairs_per_group)]
            inv = lax.rsqrt(sum(jnp.sum(o * o, axis=-1, keepdims=True) for o in outs) / gw + EPS)
            for jj, o in enumerate(outs):
                lo = grp * gw + jj * LANES
                o_ref[0, r, lo:lo + LANES] = (o * inv * sn[:, lo:lo + LANES]).astype(o_ref.dtype)
    for j in range(n_pair):
        h_ref[0, j] = states[j]


def _ssd_prompt(xbc, sz, small, conv_w, conv_b, ssm_par, ssm_norm, nc):
    b, t, _ = xbc.shape
    tb = nc * CHUNK
    blk = lambda w: pl.BlockSpec((1, tb, w), lambda i, j: (i, j, 0))
    n_pair = SSM_HEADS // 2
    return pl.pallas_call(
        functools.partial(_ssd_prompt_body, nc),
        grid=(b, t // tb),
        in_specs=[blk(SSM_CONV_CH), blk(SSM_INNER), blk(GATE_W), _const_spec(conv_w.shape),
                  _const_spec((1, SSM_CONV_CH)), _const_spec(ssm_par.shape),
                  _const_spec((1, SSM_INNER))],
        out_specs=[blk(SSM_INNER),
                   pl.BlockSpec((1, n_pair, 2 * SSM_P, SSM_N), lambda i, j: (i, 0, 0, 0))],
        out_shape=[jax.ShapeDtypeStruct((b, t, SSM_INNER), BF16),
                   jax.ShapeDtypeStruct((b, n_pair, 2 * SSM_P, SSM_N), F32)],
        scratch_shapes=[pltpu.VMEM((tb + 8, SSM_CONV_CH), F32)],
        compiler_params=_cparams("parallel", "arbitrary"),
        name="ssd_prompt",
    )(xbc, sz, small, conv_w, conv_b.reshape(1, -1), ssm_par, ssm_norm.reshape(1, -1))


def _col_bcast(row):
    return jnp.transpose(jnp.broadcast_to(row, (LANES, LANES)))


def _decode_conv(new_ref, cv_ref, cw):
    y = cw[CONV_W - 1:CONV_W] * new_ref[...]
    for i in range(CONV_W - 1):
        y = y + cw[i:i + 1] * cv_ref[:, i, :]
    return y


def _gdn_decode_body(bb, qkv_ref, gz_ref, sm_ref, cv_ref, s0_ref, cw_ref, gp_ref, gn_ref, o_ref,
                     s_ref, raw_ref):
    y = _silu(_decode_conv(qkv_ref, cv_ref, cw_ref[...]))
    sm, gp = sm_ref[...], gp_ref[...]
    a_all = jnp.exp(-jnp.exp(gp[0:1]) * _softplus(sm + gp[1:2]))
    beta_all = _sigmoid(sm)
    qs, ks, vs = [], [], []
    for h in range(GDN_HEADS):
        lo = h * GDN_DK
        qs.append(_l2n(y[:, lo:lo + GDN_DK]) * GDN_DK ** -0.5)
        ks.append(_l2n(y[:, GDN_QK + lo:GDN_QK + lo + GDN_DK]))
        vs.append(y[:, 2 * GDN_QK + lo:2 * GDN_QK + lo + GDN_DV])
    pairs = [(b, h) for b in range(bb) for h in range(GDN_HEADS)]
    k_col = [_col_bcast(ks[h][b:b + 1]) for b, h in pairs]
    q_col = [_col_bcast(qs[h][b:b + 1]) for b, h in pairs]
    decayed = [s0_ref[b, h] * a_all[b:b + 1, GATE_A + h:GATE_A + h + 1] for b, h in pairs]
    v_new = [beta_all[b:b + 1, GATE_B + h:GATE_B + h + 1]
             * (vs[h][b:b + 1] - jnp.sum(k_col[i] * decayed[i], axis=0, keepdims=True))
             for i, (b, h) in enumerate(pairs)]
    for i, (b, h) in enumerate(pairs):
        state = decayed[i] + k_col[i] * v_new[i]
        s_ref[b, h] = state
        raw_ref[b:b + 1, h * GDN_DV:(h + 1) * GDN_DV] = jnp.sum(q_col[i] * state, axis=0,
                                                                keepdims=True)
    gn = gn_ref[...]
    for h in range(GDN_HEADS):
        hs = slice(h * GDN_DV, (h + 1) * GDN_DV)
        o_ref[:, hs] = (_rms(raw_ref[:, hs], gn) * _silu(gz_ref[:, hs])).astype(o_ref.dtype)


DECODE_BLOCK = 8


def _gdn_decode(qkv, gz, small, conv0, s0, conv_w, gate_par, gdn_norm):
    b = qkv.shape[0]
    bb = DECODE_BLOCK
    row = lambda w: pl.BlockSpec((bb, w), lambda i: (i, 0))
    st = pl.BlockSpec((bb, GDN_HEADS, GDN_DK, GDN_DV), lambda i: (i, 0, 0, 0))
    return pl.pallas_call(
        functools.partial(_gdn_decode_body, bb),
        grid=(b // bb,),
        in_specs=[row(GDN_CONV_CH), row(GDN_VW), row(GATE_W),
                  pl.BlockSpec((bb, CONV_W - 1, GDN_CONV_CH), lambda i: (i, 0, 0)), st,
                  _const_spec(conv_w.shape), _const_spec(gate_par.shape), _const_spec((1, GDN_DV))],
        out_specs=[row(GDN_VW), st],
        out_shape=[jax.ShapeDtypeStruct((b, GDN_VW), F32), jax.ShapeDtypeStruct(s0.shape, F32)],
        scratch_shapes=[pltpu.VMEM((bb, GDN_VW), F32)],
        compiler_params=_cparams("parallel"),
        name="gdn_decode",
    )(qkv, gz, small, conv0, s0, conv_w, gate_par, gdn_norm.reshape(1, GDN_DV))


def _ssd_decode_body(bb, xbc_ref, sz_ref, sm_ref, cv_ref, h0_ref, cw_ref, cb_ref, sp_ref, sn_ref,
                     o_ref, h_ref, raw_ref):
    y = _silu(_decode_conv(xbc_ref, cv_ref, cw_ref[...]) + cb_ref[...])
    sm, sp = sm_ref[...], sp_ref[...]
    dt_all = _softplus(sm + sp[1:2])
    da_all = jnp.exp(dt_all * -jnp.exp(sp[0:1]))
    par_lo = _iota((1, LANES), 1) < SSM_P
    row_lo = _iota((2 * SSM_P, SSM_N), 0) < SSM_P
    n_pair = SSM_HEADS // 2
    pairs_per_group = n_pair // SSM_GROUPS
    pairs = [(b, j) for b in range(bb) for j in range(n_pair)]
    x_col, states = [], []
    for b, j in pairs:
        la, lb = GATE_DT + 2 * j, GATE_DT + 2 * j + 1
        row = slice(b, b + 1)
        xdt = y[row, j * LANES:(j + 1) * LANES] * jnp.where(par_lo, dt_all[row, la:la + 1],
                                                           dt_all[row, lb:lb + 1])
        x_col.append(_col_bcast(xdt))
    for i, (b, j) in enumerate(pairs):
        la, lb = GATE_DT + 2 * j, GATE_DT + 2 * j + 1
        row = slice(b, b + 1)
        grp = j // pairs_per_group
        b_g = y[row, SSM_INNER + grp * SSM_N:SSM_INNER + (grp + 1) * SSM_N]
        state = (h0_ref[b, j] * jnp.where(row_lo, da_all[row, la:la + 1], da_all[row, lb:lb + 1])
                 + x_col[i] * b_g)
        h_ref[b, j] = state
        states.append(state)
    for i, (b, j) in enumerate(pairs):
        grp = j // pairs_per_group
        lo = SSM_INNER + SSM_BC + grp * SSM_N
        y_col = jnp.sum(states[i] * y[b:b + 1, lo:lo + SSM_N], axis=1, keepdims=True)
        raw_ref[b:b + 1, j * LANES:(j + 1) * LANES] = jnp.transpose(
            jnp.broadcast_to(y_col, (LANES, LANES)))[0:1]
    sn = sn_ref[...]
    gw = SSM_INNER // SSM_GROUPS
    for grp in range(SSM_GROUPS):
        outs = []
        for jj in range(pairs_per_group):
            j = grp * pairs_per_group + jj
            la, lb = GATE_DT + 2 * j, GATE_DT + 2 * j + 1
            js = slice(j * LANES, (j + 1) * LANES)
            d_pair = jnp.where(par_lo, sp[2:3, la:la + 1], sp[2:3, lb:lb + 1])
            outs.append((raw_ref[:, js] + d_pair * y[:, js]) * _silu(sz_ref[:, js]))
        inv = lax.rsqrt(sum(jnp.sum(o * o, axis=-1, keepdims=True) for o in outs) / gw + EPS)
        for jj, o in enumerate(outs):
            lo = grp * gw + jj * LANES
            o_ref[:, lo:lo + LANES] = (o * inv * sn[:, lo:lo + LANES]).astype(o_ref.dtype)


def _ssd_decode(xbc, sz, small, conv0, h0, conv_w, conv_b, ssm_par, ssm_norm):
    b = xbc.shape[0]
    bb = DECODE_BLOCK
    n_pair = SSM_HEADS // 2
    row = lambda w: pl.BlockSpec((bb, w), lambda i: (i, 0))
    st = pl.BlockSpec((bb, n_pair, 2 * SSM_P, SSM_N), lambda i: (i, 0, 0, 0))
    return pl.pallas_call(
        functools.partial(_ssd_decode_body, bb),
        grid=(b // bb,),
        in_specs=[row(SSM_CONV_CH), row(SSM_INNER), row(GATE_W),
                  pl.BlockSpec((bb, CONV_W - 1, SSM_CONV_CH), lambda i: (i, 0, 0)), st,
                  _const_spec(conv_w.shape), _const_spec((1, SSM_CONV_CH)),
                  _const_spec(ssm_par.shape), _const_spec((1, SSM_INNER))],
        out_specs=[row(SSM_INNER), st],
        out_shape=[jax.ShapeDtypeStruct((b, SSM_INNER), F32), jax.ShapeDtypeStruct(h0.shape, F32)],
        scratch_shapes=[pltpu.VMEM((bb, SSM_INNER), F32)],
        compiler_params=_cparams("parallel"),
        name="ssd_decode",
    )(xbc, sz, small, conv0, h0, conv_w, conv_b.reshape(1, -1), ssm_par, ssm_norm.reshape(1, -1))


def _res_ffn_body(n_a, n_ff, x_ref, *refs):
    a_refs, w_refs = refs[:n_a], refs[n_a:2 * n_a]
    g_ref, wgu_ref, wd_ref, o_ref, xn_ref = refs[2 * n_a:]
    ff_tile = wd_ref.shape[1]
    d_ff = n_ff * ff_tile
    x1 = x_ref[...]
    for a_ref, w_ref in zip(a_refs, w_refs):
        x1 = x1 + jnp.dot(a_ref[...].astype(BF16), w_ref[...], preferred_element_type=F32)
    xn_ref[...] = _rms(x1, g_ref[...]).astype(BF16)
    o_ref[...] = x1

    def step(c, carry):
        xn = xn_ref[...]
        lo = pl.multiple_of(c * ff_tile, ff_tile)
        gate = jnp.dot(xn, wgu_ref[:, pl.ds(lo, ff_tile)], preferred_element_type=F32)
        up = jnp.dot(xn, wgu_ref[:, pl.ds(d_ff + lo, ff_tile)], preferred_element_type=F32)
        hid = (_silu(gate) * up).astype(BF16)
        o_ref[...] += jnp.dot(hid, wd_ref[c], preferred_element_type=F32)
        return carry

    lax.fori_loop(0, n_ff, step, 0)


def _res_ffn(x, acts, projs, gain, wgu, wd, tm):
    rows, d = x.shape
    n_a, n_ff = len(acts), wd.shape[0]
    return pl.pallas_call(
        functools.partial(_res_ffn_body, n_a, n_ff),
        grid=(rows // tm,),
        in_specs=[pl.BlockSpec((tm, d), lambda i: (i, 0))]
        + [pl.BlockSpec((tm, a.shape[1]), lambda i: (i, 0)) for a in acts]
        + [_const_spec(p.shape) for p in projs]
        + [_const_spec((1, d)), _const_spec(wgu.shape), _const_spec(wd.shape)],
        out_specs=pl.BlockSpec((tm, d), lambda i: (i, 0)),
        out_shape=jax.ShapeDtypeStruct((rows, d), F32),
        scratch_shapes=[pltpu.VMEM((tm, d), BF16)],
        compiler_params=_cparams("parallel"),
        name="res_ffn",
    )(x, *acts, *projs, gain.reshape(1, d), wgu, wd)


def _mla_proj_body(tm, q_scale, x_ref, cos_ref, sin_ref, g_ref, wcq_ref, wckv_ref, wkr_ref,
                   qan_ref, kvn_ref, wuq_ref, wuk_ref, wuv_ref, vone_ref, gq_ref, gk_ref, gkr_ref,
                   segq_ref, segk_ref, q_ref, k_ref, v_ref, rows_ref):
    xn = _rms(x_ref[...], g_ref[...]).astype(BF16)
    lane = _iota((tm, LANES), 1)
    cos_t, sin_t = cos_ref[...], sin_ref[...]

    kr = jnp.dot(xn, wkr_ref[...], preferred_element_type=F32)
    ssq_kr = jnp.sum(jnp.where(lane < QK_ROPE, kr * kr, 0.0), axis=-1, keepdims=True)
    krg = kr * gkr_ref[...]
    kr_rot = krg * cos_t + pltpu.roll(krg, LANES - QK_ROPE, 1) * sin_t

    c = _rms(jnp.dot(xn, wckv_ref[...], preferred_element_type=F32), kvn_ref[...])
    cb = c.astype(BF16)
    kx = jnp.dot(cb, wuk_ref[...], preferred_element_type=F32)
    ssq_k = jnp.dot((kx * kx).astype(BF16), segk_ref[...],
                    preferred_element_type=F32)
    inv_r = lax.rsqrt((ssq_k + ssq_kr) / QK_HEAD + EPS)
    tail = jnp.where(lane < QK_ROPE, kr_rot, inv_r)
    rows_ref[:, 0:KV_LORA] = c
    rows_ref[:, KV_LORA:MLA_ROW] = tail[:, 0:MLA_ROW - KV_LORA]
    kr_put = jnp.where((lane >= QK_NOPE) & (lane < QK_HEAD), pltpu.roll(kr_rot, QK_NOPE, 1), 0.0)
    gk = gk_ref[...]
    for h in range(MLA_HEADS):
        hs = slice(h * LANES, (h + 1) * LANES)
        inv_h = inv_r[:, QK_ROPE + h:QK_ROPE + h + 1]
        k_ref[:, hs] = ((kx[:, hs] * gk + kr_put) * inv_h).astype(k_ref.dtype)
    v_ref[...] = (jnp.dot(cb, wuv_ref[...], preferred_element_type=F32)
                  + vone_ref[...]).astype(v_ref.dtype)

    cq = _rms(jnp.dot(xn, wcq_ref[...], preferred_element_type=F32), qan_ref[...]).astype(BF16)
    qx = jnp.dot(cq, wuq_ref[...], preferred_element_type=F32)
    ssq_q = jnp.dot((qx * qx).astype(BF16), segq_ref[...], preferred_element_type=F32)
    inv_q = lax.rsqrt(ssq_q / QK_HEAD + EPS)
    gq = gq_ref[...]
    scale = q_scale
    in_rope = (lane >= QK_NOPE) & (lane < QK_HEAD)
    keep = jnp.where(lane < QK_NOPE, scale, jnp.where(in_rope, cos_t * scale, 0.0))
    swap = jnp.where(in_rope, sin_t * scale, 0.0)
    for h in range(MLA_HEADS):
        hs = slice(h * LANES, (h + 1) * LANES)
        t = qx[:, hs] * inv_q[:, h:h + 1] * gq
        q_ref[:, hs] = (t * keep + pltpu.roll(t, LANES - QK_ROPE, 1) * swap).astype(q_ref.dtype)


def _mla_proj(x, seq, pos0, w, tm, q_dtype, q_scale):
    rows, d = x.shape
    half = QK_ROPE // 2
    inv_freq = ROPE_THETA ** (-(jnp.arange(LANES) % half).astype(F32) / half)
    ang = (pos0 + jnp.arange(seq, dtype=F32))[:, None] * inv_freq[None, :]
    tr = tm if seq > 1 else 1
    nblk = seq // tr
    rope_spec = pl.BlockSpec((tr, LANES), lambda i: (i % nblk, 0))
    names = ("norm", "wcq", "wckv", "wkr", "qan", "kvn", "wuq", "wuk", "wuv", "vone", "gq", "gk",
             "gkr", "segq", "segk")
    consts = [w[n] for n in names]
    out_w = (HEAD_W, HEAD_W, HEAD_W, MLA_ROW)
    out_dt = (q_dtype, BF16, BF16, F32)
    return pl.pallas_call(
        functools.partial(_mla_proj_body, tm, q_scale),
        grid=(rows // tm,),
        in_specs=[pl.BlockSpec((tm, d), lambda i: (i, 0)), rope_spec, rope_spec]
        + [_const_spec(c.shape) for c in consts],
        out_specs=[pl.BlockSpec((tm, ow), lambda i: (i, 0)) for ow in out_w],
        out_shape=[jax.ShapeDtypeStruct((rows, ow), dt) for ow, dt in zip(out_w, out_dt)],
        compiler_params=_cparams("parallel"),
        name="mla_proj",
    )(x, jnp.cos(ang), jnp.sin(ang), *consts)


def _flash_body(tq, nh, q_ref, k_ref, v_ref, o_ref):
    qi = pl.program_id(2)
    half = tq // 2
    heads = [slice(j * LANES, (j + 1) * LANES) for j in range(nh)]
    q = [q_ref[0, :, hs] for hs in heads]

    def attend(carry, qs, kv_rows, mask):
        s = [lax.dot_general(qs[j], k_ref[0, kv_rows, hs], (((1,), (1,)), ((), ())),
                             preferred_element_type=F32) for j, hs in enumerate(heads)]
        if mask is not None:
            s = [jnp.where(mask, sj, -jnp.inf) for sj in s]
        m_new = [jnp.maximum(carry[j][0], jnp.max(s[j], axis=-1, keepdims=True)) for j in range(nh)]
        p = [jnp.exp2((s[j] - m_new[j]).astype(BF16)) for j in range(nh)]
        pv = [jnp.dot(p[j], v_ref[0, kv_rows, hs], preferred_element_type=F32)
              for j, hs in enumerate(heads)]
        return tuple((m_new[j], jnp.exp2(carry[j][0] - m_new[j]) * carry[j][1] + pv[j])
                     for j in range(nh))

    init = tuple((jnp.full((tq, 1), -jnp.inf, F32), jnp.zeros((tq, LANES), F32)) for _ in range(nh))
    carry = lax.fori_loop(
        0, qi, lambda kb, cr: attend(cr, q, pl.ds(pl.multiple_of(kb * tq, tq), tq), None), init)
    base = pl.multiple_of(qi * tq, tq)
    carry = attend(carry, q, pl.ds(base, half),
                   _iota((tq, half), 1) <= _iota((tq, half), 0))
    lower = attend(tuple((m[half:], acc[half:]) for m, acc in carry), [qj[half:] for qj in q],
                   pl.ds(base + half, half), _iota((half, half), 1) <= _iota((half, half), 0))
    low = _iota((half, LANES), 1) < V_HEAD
    for part, rows in ((tuple((m[:half], acc[:half]) for m, acc in carry), slice(0, half)),
                       (lower, slice(half, tq))):
        outs = [acc / acc[:, V_HEAD:V_HEAD + 1] for _, acc in part]
        for j in range(nh // 2):
            o_ref[0, rows, j * LANES:(j + 1) * LANES] = jnp.where(
                low, outs[2 * j], pltpu.roll(outs[2 * j + 1], V_HEAD, 1)).astype(o_ref.dtype)


def _flash_attention(q, k, v, tq, nh):
    b, t, _ = q.shape
    kv_spec = pl.BlockSpec((1, t, nh * LANES), lambda i, j, n: (i, 0, j))
    return pl.pallas_call(
        functools.partial(_flash_body, tq, nh),
        grid=(b, MLA_HEADS // nh, t // tq),
        in_specs=[pl.BlockSpec((1, tq, nh * LANES), lambda i, j, n: (i, n, j)), kv_spec, kv_spec],
        out_specs=pl.BlockSpec((1, tq, nh * V_HEAD), lambda i, j, n: (i, n, j)),
        out_shape=jax.ShapeDtypeStruct((b, t, MLA_HEADS * V_HEAD), BF16),
        compiler_params=_cparams("parallel", "parallel", "arbitrary"),
        name="flash_attention",
    )(q, k, v)


def _absorb_body(q_ref, g_ref, w_ref, o_ref):
    t = (q_ref[...] * g_ref[...]).astype(BF16)
    o_ref[...] = jnp.dot(t, w_ref[...], preferred_element_type=F32).astype(o_ref.dtype)


def _absorb(q, gain, w_abs):
    b = q.shape[0]
    return pl.pallas_call(
        _absorb_body,
        grid=(MLA_HEADS,),
        in_specs=[pl.BlockSpec((b, LANES), lambda h: (0, h)), _const_spec((1, LANES)),
                  pl.BlockSpec((None, LANES, Q_EXT), lambda h: (h, 0, 0))],
        out_specs=pl.BlockSpec((None, b, Q_EXT), lambda h: (h, 0, 0)),
        out_shape=jax.ShapeDtypeStruct((MLA_HEADS, b, Q_EXT), BF16),
        compiler_params=_cparams("parallel"),
        name="mla_absorb",
    )(q, gain, w_abs)


def _paged_attn_body(n_pages, pt_ref, q_ref, rn_ref, cache_ref, o_ref, buf_ref, sem_ref, s_ref,
                     pb_ref):
    i = pl.program_id(0)
    slot = lax.rem(i, 2)

    def page_copy(seq, p, sl):
        return pltpu.make_async_copy(cache_ref.at[pt_ref[seq, p]], buf_ref.at[sl, p],
                                     sem_ref.at[sl])

    @pl.when(i == 0)
    def _():
        for p in range(n_pages):
            page_copy(0, p, 0).start()

    @pl.when(i + 1 < pl.num_programs(0))
    def _():
        for p in range(n_pages):
            page_copy(i + 1, p, 1 - slot).start()

    for p in range(n_pages):
        page_copy(i, p, slot).wait()

    q = q_ref[0]
    inv_lo = MLA_ROW - MLA_HEADS
    for p in range(n_pages):
        page = buf_ref[slot, p]
        cols = slice(p * PAGE_SIZE, (p + 1) * PAGE_SIZE)
        pb_ref[:, cols] = page.astype(BF16)
        s_ref[:, cols] = page[inv_lo:MLA_ROW]
    s_all = jnp.dot(q[:, :MLA_ROW], pb_ref[...], preferred_element_type=F32) * s_ref[...]
    rn = rn_ref[0]
    qf = q[:, :MLA_ROW].astype(F32)
    pick = _iota((MLA_HEADS, MLA_ROW), 1) == _iota((MLA_HEADS, MLA_ROW), 0) + inv_lo
    inv_new = jnp.sum(jnp.where(pick, rn, 0.0), axis=-1, keepdims=True)
    s_new = jnp.sum(qf * rn, axis=-1, keepdims=True) * inv_new
    m = jnp.maximum(jnp.max(s_all, axis=-1, keepdims=True), s_new)
    e_new = jnp.exp(s_new - m)
    e = jnp.exp(s_all - m)
    den = e_new + jnp.sum(e, axis=-1, keepdims=True)
    ctx = e_new * rn[:, 0:KV_LORA] + lax.dot_general(
        e.astype(BF16), pb_ref[0:KV_LORA, :], (((1,), (1,)), ((), ())), preferred_element_type=F32)
    o_ref[0] = ctx / den


def _paged_attention(q_ext, rows_new, cache_t, page_table):
    b, n_pages = page_table.shape
    grid_spec = pltpu.PrefetchScalarGridSpec(
        num_scalar_prefetch=1,
        grid=(b,),
        in_specs=[pl.BlockSpec((1, MLA_HEADS, Q_EXT), lambda i, pt: (i, 0, 0)),
                  pl.BlockSpec((1, 1, MLA_ROW), lambda i, pt: (i, 0, 0)),
                  pl.BlockSpec(memory_space=pl.ANY)],
        out_specs=pl.BlockSpec((1, MLA_HEADS, KV_LORA), lambda i, pt: (i, 0, 0)),
        scratch_shapes=[pltpu.VMEM((2, n_pages, MLA_ROW, PAGE_SIZE), F32),
                        pltpu.SemaphoreType.DMA((2,)),
                        pltpu.VMEM((MLA_HEADS, n_pages * PAGE_SIZE), F32),
                        pltpu.VMEM((MLA_ROW, n_pages * PAGE_SIZE), BF16)],
    )
    return pl.pallas_call(
        functools.partial(_paged_attn_body, n_pages),
        grid_spec=grid_spec,
        out_shape=jax.ShapeDtypeStruct((b, MLA_HEADS, KV_LORA), F32),
        compiler_params=_cparams("arbitrary"),
        name="paged_attention",
    )(page_table, q_ext, rows_new, cache_t)


def _ctx_out_body(c_ref, w_ref, o_ref):
    o_ref[...] = (jnp.dot(c_ref[0].astype(BF16), w_ref[0], preferred_element_type=F32)
                  + jnp.dot(c_ref[1].astype(BF16), w_ref[1], preferred_element_type=F32)
                  ).astype(o_ref.dtype)


def _ctx_out(ctx_t, w_pairs):
    b = ctx_t.shape[1]
    return pl.pallas_call(
        _ctx_out_body,
        grid=(MLA_HEADS // 2,),
        in_specs=[pl.BlockSpec((2, b, KV_LORA), lambda j: (j, 0, 0)),
                  pl.BlockSpec((None, 2, KV_LORA, LANES), lambda j: (j, 0, 0, 0))],
        out_specs=pl.BlockSpec((b, LANES), lambda j: (0, j)),
        out_shape=jax.ShapeDtypeStruct((b, MLA_HEADS * V_HEAD), BF16),
        compiler_params=_cparams("parallel"),
        name="mla_ctx_out",
    )(ctx_t, w_pairs)


def _gate_tile(rows):
    tile = jnp.zeros((8, GATE_W), F32)
    for i, (off, vec) in enumerate(rows):
        tile = tile.at[i, off:off + vec.shape[0]].set(vec.astype(F32))
    return tile


def _prep_hybrid(j, w_in_a, conv_gdn_w, gdn_A_log, gdn_dt_bias, conv_ssm_w, ssm_A_log,
                 ssm_dt_bias, ssm_D):
    widths = (GDN_CONV_CH, GDN_VW, GDN_HEADS, GDN_HEADS, SSM_INNER, SSM_CONV_CH, SSM_HEADS)
    offs = np.concatenate([[0], np.cumsum(widths)])
    w = w_in_a[j]
    d = w.shape[0]
    part = lambda i: w[:, offs[i]:offs[i + 1]]
    small = jnp.zeros((d, GATE_W), F32)
    small = small.at[:, GATE_A:GATE_A + GDN_HEADS].set(part(2))
    small = small.at[:, GATE_B:GATE_B + GDN_HEADS].set(part(3))
    small = small.at[:, GATE_DT:GATE_DT + SSM_HEADS].set(part(6))
    return dict(
        w_in=[part(0).astype(BF16), part(1).astype(BF16), part(4).astype(BF16),
              part(5).astype(BF16), small.astype(BF16)],
        conv_gdn=conv_gdn_w[j], conv_ssm=conv_ssm_w[j],
        gate_gdn=_gate_tile([(GATE_A, gdn_A_log[j]), (GATE_A, gdn_dt_bias[j])]),
        gate_ssm=_gate_tile([(GATE_DT, ssm_A_log[j]), (GATE_DT, ssm_dt_bias[j]),
                             (GATE_DT, ssm_D[j])]),
    )


def _prep_mla(j, norm_mix_c, w_in_c, q_a_norm, kv_a_norm, w_uq, w_uk, w_uv, q_norm, k_norm):
    half = QK_ROPE // 2
    w_in = w_in_c[j]
    d = w_in.shape[0]

    def swapped(cols):
        return jnp.concatenate([-cols[..., half:], cols[..., :half]], axis=-1)

    def swapped_gain(g):
        return jnp.concatenate([g[half:], g[:half]])

    w_kr = w_in[:, Q_LORA + KV_LORA:]
    wkr = jnp.zeros((d, LANES), F32).at[:, :QK_ROPE].set(w_kr).at[:, QK_ROPE:2 * QK_ROPE].set(
        swapped(w_kr))
    qn, kn = q_norm[j], k_norm[j]
    gkr = jnp.zeros((1, LANES), F32).at[0, :QK_ROPE].set(kn[QK_NOPE:]).at[
        0, QK_ROPE:2 * QK_ROPE].set(swapped_gain(kn[QK_NOPE:]))
    gk = jnp.zeros((1, LANES), F32).at[0, :QK_NOPE].set(kn[:QK_NOPE])
    gq_head = jnp.concatenate([qn, swapped_gain(qn[QK_NOPE:])])
    gq = gq_head.reshape(1, LANES)

    uq = w_uq[j].reshape(Q_LORA, MLA_HEADS, QK_HEAD)
    wuq = jnp.concatenate([uq, swapped(uq[..., QK_NOPE:])], axis=-1).reshape(Q_LORA, HEAD_W)
    wuk = jnp.zeros((KV_LORA, MLA_HEADS, LANES), F32).at[..., :QK_NOPE].set(w_uk[j]).reshape(
        KV_LORA, HEAD_W)
    wuv = jnp.zeros((KV_LORA, MLA_HEADS, LANES), F32).at[..., :V_HEAD].set(w_uv[j]).reshape(
        KV_LORA, HEAD_W)
    vone = jnp.zeros((MLA_HEADS, LANES), F32).at[:, V_HEAD].set(1.0).reshape(1, HEAD_W)

    col = np.arange(HEAD_W)
    head, within = col // LANES, col % LANES
    lane = np.arange(LANES)
    inv_lane = QK_ROPE + head
    segq = ((within < QK_HEAD)[:, None] & (lane[None, :] == head[:, None]))
    segk = ((within < QK_NOPE)[:, None] & (lane[None, :] == inv_lane[:, None]))
    sel = lambda m: jnp.asarray(m, BF16)

    w_abs = jnp.zeros((MLA_HEADS, LANES, Q_EXT), F32)
    w_abs = w_abs.at[:, :QK_NOPE, :KV_LORA].set(jnp.transpose(w_uk[j], (1, 2, 0)))
    w_abs = w_abs.at[:, QK_NOPE:QK_HEAD, KV_LORA:KV_LORA + QK_ROPE].set(jnp.eye(QK_ROPE, dtype=F32))
    g_abs = jnp.zeros((1, LANES), F32).at[0, :QK_NOPE].set(kn[:QK_NOPE]).at[
        0, QK_NOPE:QK_HEAD].set(1.0)
    uv = jnp.transpose(w_uv[j], (1, 0, 2)).reshape(MLA_HEADS // 2, 2, KV_LORA, V_HEAD)
    w_pairs = jnp.zeros((MLA_HEADS // 2, 2, KV_LORA, LANES), F32)
    w_pairs = w_pairs.at[:, 0, :, :V_HEAD].set(uv[:, 0]).at[:, 1, :, V_HEAD:].set(uv[:, 1])
    return dict(
        norm=norm_mix_c[j].reshape(1, d), wcq=w_in[:, :Q_LORA].astype(BF16),
        wckv=w_in[:, Q_LORA:Q_LORA + KV_LORA].astype(BF16), wkr=wkr.astype(BF16),
        qan=q_a_norm[j].reshape(1, Q_LORA), kvn=kv_a_norm[j].reshape(1, KV_LORA),
        wuq=wuq.astype(BF16), wuk=wuk.astype(BF16), wuv=wuv.astype(BF16), vone=vone, gq=gq, gk=gk, gkr=gkr,
        segq=sel(segq), segk=sel(segk),
        w_abs=w_abs.astype(BF16), g_abs=g_abs, w_pairs=w_pairs.astype(BF16),
    )


def _prep_ffn(layer, w_gate_up, w_down, ff_tile):
    d_ff, d = w_down.shape[1:]
    return (w_gate_up[layer].astype(BF16),
            w_down[layer].reshape(d_ff // ff_tile, ff_tile, d).astype(BF16))


FF_TILE = 256
ROW_TILE = 512
FFN_ROW_TILE = 1024
MLA_ROW_TILE = 256
ATTN_TILE = 512
ATTN_HEADS = 4
GDN_CHUNKS = 4
SSD_CHUNKS = 4


def _row_tile(rows, want):
    return want if rows % want == 0 else rows


def _trunk(x, pos0, states, cache_mla, page_table, hyb, mla, ffn, norm_mix_a, gdn_norm, conv_ssm_b,
           ssm_norm, w_out_a, w_out_c, norm_ffn):
    b, t, d = x.shape
    rows = b * t
    tm = _row_tile(rows, ROW_TILE)
    tm_f = _row_tile(rows, FFN_ROW_TILE)
    xf = x.reshape(rows, d)
    decode = states is not None

    qkv, gz, sz, xbc, small = _norm_proj(xf, norm_mix_a[0], hyb["w_in"], tm)
    if decode:
        s0, gconv0, h0, sconv0 = states
        r3 = lambda a: a.reshape(b, 1, a.shape[-1])
        n_pair = SSM_HEADS // 2
        o_gdn, s_new = _gdn_decode(qkv, gz, small, gconv0[0], s0[0], hyb["conv_gdn"],
                                   hyb["gate_gdn"], gdn_norm[0])
        o_ssd, h_new = _ssd_decode(xbc, sz, small, sconv0[0],
                                   h0[0].reshape(b, n_pair, 2 * SSM_P, SSM_N), hyb["conv_ssm"],
                                   conv_ssm_b[0], hyb["gate_ssm"], ssm_norm[0])
        gconv = jnp.concatenate([gconv0[0][:, 1:], r3(qkv)], axis=1)
        sconv = jnp.concatenate([sconv0[0][:, 1:], r3(xbc)], axis=1)
    else:
        r3 = lambda a: a.reshape(b, t, a.shape[-1])
        o_gdn, s_new = _gdn_prompt(r3(qkv), r3(gz), r3(small), hyb["conv_gdn"], hyb["gate_gdn"],
                                   gdn_norm[0], GDN_CHUNKS)
        o_ssd, h_new = _ssd_prompt(r3(xbc), r3(sz), r3(small), hyb["conv_ssm"], conv_ssm_b[0],
                                   hyb["gate_ssm"], ssm_norm[0], SSD_CHUNKS)
        gconv = r3(qkv)[:, t - (CONV_W - 1):]
        sconv = r3(xbc)[:, t - (CONV_W - 1):]
    h_new = h_new.reshape(b, SSM_HEADS, SSM_P, SSM_N)
    wo = w_out_a[0].astype(BF16)
    x1 = _res_ffn(xf, [o_gdn.reshape(rows, GDN_VW), o_ssd.reshape(rows, SSM_INNER)],
                  [wo[:GDN_VW], wo[GDN_VW:]], norm_ffn[0], *ffn[0], tm_f)

    tm_c = _row_tile(rows, MLA_ROW_TILE)
    q_scale = QK_HEAD ** -0.5 * (1.0 if decode else math.log2(math.e))
    q, k, v, mla_rows = _mla_proj(x1, t, pos0, mla, tm_c, F32 if decode else BF16, q_scale)
    if decode:
        q_ext = jnp.transpose(_absorb(q, mla["g_abs"], mla["w_abs"]), (1, 0, 2))
        ctx = _paged_attention(q_ext, mla_rows.reshape(b, 1, MLA_ROW),
                               jnp.swapaxes(cache_mla[0], 1, 2), page_table)
        attn = _ctx_out(jnp.transpose(ctx, (1, 0, 2)), mla["w_pairs"])
    else:
        r3 = lambda a: a.reshape(b, t, a.shape[-1])
        attn = _flash_attention(r3(q), r3(k), r3(v), _row_tile(t, ATTN_TILE),
                                ATTN_HEADS).reshape(rows, -1)
    x2 = _res_ffn(x1, [attn], [w_out_c[0].astype(BF16)], norm_ffn[1], *ffn[1], tm_f)
    return (x2.reshape(b, t, d), s_new[None], gconv[None], h_new[None], sconv[None],
            mla_rows.reshape(1, b, t, MLA_ROW))


def kernel(x_prompt, x_sample, state_gdn, state_gdn_conv, state_ssm, state_ssm_conv, cache_mla, page_table, norm_mix_a, w_in_a, conv_gdn_w, gdn_A_log, gdn_dt_bias, gdn_norm, conv_ssm_w, conv_ssm_b, ssm_A_log, ssm_dt_bias, ssm_D, ssm_norm, w_out_a, norm_mix_c, w_in_c, q_a_norm, kv_a_norm, w_uq, w_uk, w_uv, q_norm, k_norm, w_out_c, norm_ffn, w_gate_up, w_down):
    assert w_in_a.shape[0] == 1 and w_in_c.shape[0] == 1 and norm_ffn.shape[0] == 2
    hyb = _prep_hybrid(0, w_in_a, conv_gdn_w, gdn_A_log, gdn_dt_bias, conv_ssm_w, ssm_A_log,
                       ssm_dt_bias, ssm_D)
    mla = _prep_mla(0, norm_mix_c, w_in_c, q_a_norm, kv_a_norm, w_uq, w_uk, w_uv, q_norm, k_norm)
    ffn = [_prep_ffn(layer, w_gate_up, w_down, FF_TILE) for layer in range(2)]
    shared = (hyb, mla, ffn, norm_mix_a, gdn_norm, conv_ssm_b, ssm_norm, w_out_a, w_out_c, norm_ffn)
    prompt = _trunk(x_prompt, 0, None, None, None, *shared)
    past_len = page_table.shape[1] * PAGE_SIZE
    sample = _trunk(x_sample, past_len, (state_gdn, state_gdn_conv, state_ssm, state_ssm_conv),
                    cache_mla, page_table, *shared)
    return (prompt[0], sample[0]) + prompt[1:] + sample[1:]
```

```python
import functools
import math

import jax
import jax.numpy as jnp
import numpy as np
from jax import lax
from jax.experimental import pallas as pl
from jax.experimental.pallas import tpu as pltpu

F32 = jnp.float32
BF16 = jnp.bfloat16

EPS = 1e-6
CONV_W = 4
CHUNK = 64
PAGE_SIZE = 128
GDN_HEADS = 4
GDN_DK = 128
GDN_DV = 128
GDN_QK = GDN_HEADS * GDN_DK
GDN_VW = GDN_HEADS * GDN_DV
GDN_CONV_CH = 2 * GDN_QK + GDN_VW
SSM_HEADS = 8
SSM_P = 64
SSM_N = 128
SSM_GROUPS = 2
SSM_INNER = SSM_HEADS * SSM_P
SSM_BC = SSM_GROUPS * SSM_N
SSM_CONV_CH = SSM_INNER + 2 * SSM_BC
MLA_HEADS = 16
Q_LORA = 512
KV_LORA = 256
QK_NOPE = 64
QK_ROPE = 32
QK_HEAD = QK_NOPE + QK_ROPE
V_HEAD = 64
MLA_ROW = KV_LORA + QK_ROPE + MLA_HEADS
ROPE_THETA = 10000.0

LANES = 128
HEAD_W = MLA_HEADS * LANES
GATE_W = LANES
GATE_A, GATE_B, GATE_DT = 0, GDN_HEADS, 2 * GDN_HEADS
Q_EXT = 3 * LANES
VMEM_LIMIT = 48 * 1024 * 1024

_HI = lax.Precision.HIGHEST


def _cparams(*sem):
    return pltpu.CompilerParams(dimension_semantics=sem, vmem_limit_bytes=VMEM_LIMIT)


def _const_spec(shape):
    zeros = (0,) * len(shape)
    return pl.BlockSpec(shape, lambda *_: zeros, pipeline_mode=pl.Buffered(1))


def _mm(a, b):
    return jnp.dot(a.astype(BF16), b.astype(BF16), preferred_element_type=F32)


def _mm_nt(a, b):
    return lax.dot_general(a.astype(BF16), b.astype(BF16), (((1,), (1,)), ((), ())),
                           preferred_element_type=F32)


def _mm_tn(a, b):
    return lax.dot_general(a.astype(BF16), b.astype(BF16), (((0,), (0,)), ((), ())),
                           preferred_element_type=F32)


def _mm_split(a, sel):
    hi = a.astype(BF16)
    lo = (a - hi.astype(F32)).astype(BF16)
    return (jnp.dot(hi, sel, preferred_element_type=F32)
            + jnp.dot(lo, sel, preferred_element_type=F32))


def _sigmoid(x):
    return 1.0 / (1.0 + jnp.exp(-x))


def _silu(x):
    return x * _sigmoid(x)


def _softplus(x):
    return jnp.maximum(x, 0.0) + jnp.log(1.0 + jnp.exp(-jnp.abs(x)))


def _rms(x, gain):
    return x * lax.rsqrt(jnp.mean(x * x, axis=-1, keepdims=True) + EPS) * gain


def _l2n(x):
    return x * lax.rsqrt(jnp.sum(x * x, axis=-1, keepdims=True) + EPS)


def _iota(shape, dim):
    return lax.broadcasted_iota(jnp.int32, shape, dim)


def _chunk_cumsum(g, tb):
    row, col = _iota((tb, tb), 0), _iota((tb, tb), 1)
    shift = int(math.log2(CHUNK))
    same_chunk = lax.shift_right_logical(row, shift) == lax.shift_right_logical(col, shift)
    tril = jnp.where((col <= row) & same_chunk, 1.0, 0.0).astype(F32)
    return jnp.dot(tril, g, precision=_HI, preferred_element_type=F32)


def _rows_of(cols, lane0):
    sel = jnp.where(_iota((8, LANES), 1) == _iota((8, LANES), 0) + lane0, 1.0, 0.0).astype(F32)
    return lax.dot_general(sel, cols, (((1,), (1,)), ((), ())), precision=_HI,
                           preferred_element_type=F32)


def _causal_conv_block(x_ref, xp_ref, cw, tb, first):
    @pl.when(first)
    def _():
        xp_ref[0:8, :] = jnp.zeros((8, xp_ref.shape[1]), F32)

    xp_ref[8:8 + tb, :] = x_ref[...]
    off = 8 - (CONV_W - 1)
    y = cw[0:1] * xp_ref[off:off + tb, :]
    for i in range(1, CONV_W):
        y = y + cw[i:i + 1] * xp_ref[off + i:off + i + tb, :]
    xp_ref[0:8, :] = xp_ref[tb:tb + 8, :]
    return y


def _norm_proj_body(n_out, x_ref, g_ref, *refs):
    xn = _rms(x_ref[...], g_ref[...]).astype(BF16)
    for w_ref, o_ref in zip(refs[:n_out], refs[n_out:]):
        o_ref[...] = jnp.dot(xn, w_ref[...], preferred_element_type=F32)


def _norm_proj(x, gain, weights, tm):
    rows, d = x.shape
    n = len(weights)
    return pl.pallas_call(
        functools.partial(_norm_proj_body, n),
        grid=(rows // tm,),
        in_specs=[pl.BlockSpec((tm, d), lambda i: (i, 0)), _const_spec((1, d))]
        + [_const_spec(w.shape) for w in weights],
        out_specs=[pl.BlockSpec((tm, w.shape[1]), lambda i: (i, 0)) for w in weights],
        out_shape=[jax.ShapeDtypeStruct((rows, w.shape[1]), F32) for w in weights],
        compiler_params=_cparams("parallel"),
        name="norm_proj",
    )(x, gain.reshape(1, d), *weights)


def _gdn_prompt_body(nc, qkv_ref, gz_ref, sm_ref, cw_ref, gp_ref, gn_ref, o_ref, s_ref, xp_ref):
    tb = nc * CHUNK
    first = pl.program_id(1) == 0

    @pl.when(first)
    def _():
        s_ref[...] = jnp.zeros(s_ref.shape, F32)

    y = _silu(_causal_conv_block(qkv_ref.at[0], xp_ref, cw_ref[...], tb, first))
    sm, gp = sm_ref[0], gp_ref[...]
    g_all = -jnp.exp(gp[0:1]) * _softplus(sm + gp[1:2])
    beta_all = _sigmoid(sm)
    cum_all = _chunk_cumsum(g_all, tb)
    cum_rows = _rows_of(cum_all, GATE_A)
    incl = _iota((CHUNK, CHUNK), 1) <= _iota((CHUNK, CHUNK), 0)
    strict = _iota((CHUNK, CHUNK), 1) < _iota((CHUNK, CHUNK), 0)
    gn = gn_ref[...]
    pairs = [(c, h) for c in range(nc) for h in range(GDN_HEADS)]
    qs, ks, vs = [], [], []
    for h in range(GDN_HEADS):
        lo = h * GDN_DK
        qs.append(_l2n(y[:, lo:lo + GDN_DK]) * GDN_DK ** -0.5)
        ks.append(_l2n(y[:, GDN_QK + lo:GDN_QK + lo + GDN_DK]))
        vs.append(y[:, 2 * GDN_QK + lo:2 * GDN_QK + lo + GDN_DV])
    pre = []
    for c, h in pairs:
        r = slice(c * CHUNK, (c + 1) * CHUNK)
        q, k, v = qs[h][r], ks[h][r], vs[h][r]
        cum_c = cum_all[r, GATE_A + h:GATE_A + h + 1]
        cum_r = cum_rows[h:h + 1, c * CHUNK:(c + 1) * CHUNK]
        beta = beta_all[r, GATE_B + h:GATE_B + h + 1]
        decay = jnp.where(incl, jnp.exp(jnp.minimum(cum_c - cum_r, 0.0)), 0.0)
        e_c = jnp.exp(cum_c)
        kb = k * beta
        last = cum_c[CHUNK - 1:CHUNK]
        pre.append(dict(q=q.astype(BF16), k=k.astype(BF16), kb=kb, decay=decay,
                        rhs=jnp.concatenate([v * beta, kb * e_c], axis=1).astype(BF16),
                        qe=q * e_c, kd=(k * jnp.exp(last - cum_c)).astype(BF16),
                        e_last=jnp.exp(last)))
    lows = [jnp.where(strict, _mm_nt(p["kb"], p["k"]) * p["decay"], 0.0) for p in pre]
    attns = [(_mm_nt(p["q"], p["k"]) * p["decay"]).astype(BF16) for p in pre]
    eye = jnp.where(_iota((CHUNK, CHUNK), 0) == _iota((CHUNK, CHUNK), 1), 1.0, 0.0).astype(F32)
    invs = [eye - low for low in lows]
    powers = lows
    for _ in range(int(math.log2(CHUNK)) - 1):
        powers = [_mm(pw, pw) for pw in powers]
        invs = [inv + _mm(inv, pw) for inv, pw in zip(invs, powers)]
    sols = [_mm(inv, p["rhs"]).astype(BF16) for inv, p in zip(invs, pre)]
    k_sol = [_mm_tn(p["kd"], sol) for p, sol in zip(pre, sols)]
    a_sol = [_mm(attn, sol) for attn, sol in zip(attns, sols)]
    q_eff = [(p["qe"] - a[:, GDN_DV:]).astype(BF16) for p, a in zip(pre, a_sol)]
    m_eff = [ks[:, GDN_DV:].astype(BF16) for ks in k_sol]
    states = [s_ref[0, h] for h in range(GDN_HEADS)]
    for c in range(nc):
        r = slice(c * CHUNK, (c + 1) * CHUNK)
        idx = [c * GDN_HEADS + h for h in range(GDN_HEADS)]
        s_bf = [st.astype(BF16) for st in states]
        outs = [_mm(q_eff[i], s_bf[h]) + a_sol[i][:, :GDN_DV] for h, i in enumerate(idx)]
        states = [states[h] * pre[i]["e_last"] - _mm(m_eff[i], s_bf[h]) + k_sol[i][:, :GDN_DV]
                  for h, i in enumerate(idx)]
        for h, o in enumerate(outs):
            lo = h * GDN_DV
            gate = _silu(gz_ref[0, r, lo:lo + GDN_DV])
            o_ref[0, r, lo:lo + GDN_DV] = (_rms(o, gn) * gate).astype(o_ref.dtype)
    for h in range(GDN_HEADS):
        s_ref[0, h] = states[h]


def _gdn_prompt(qkv, gz, small, conv_w, gate_par, gdn_norm, nc):
    b, t, _ = qkv.shape
    tb = nc * CHUNK
    blk = lambda w: pl.BlockSpec((1, tb, w), lambda i, j: (i, j, 0))
    return pl.pallas_call(
        functools.partial(_gdn_prompt_body, nc),
        grid=(b, t // tb),
        in_specs=[blk(GDN_CONV_CH), blk(GDN_VW), blk(GATE_W), _const_spec(conv_w.shape),
                  _const_spec(gate_par.shape), _const_spec((1, GDN_DV))],
        out_specs=[blk(GDN_VW),
                   pl.BlockSpec((1, GDN_HEADS, GDN_DK, GDN_DV), lambda i, j: (i, 0, 0, 0))],
        out_shape=[jax.ShapeDtypeStruct((b, t, GDN_VW), BF16),
                   jax.ShapeDtypeStruct((b, GDN_HEADS, GDN_DK, GDN_DV), F32)],
        scratch_shapes=[pltpu.VMEM((tb + 8, GDN_CONV_CH), F32)],
        compiler_params=_cparams("parallel", "arbitrary"),
        name="gdn_prompt",
    )(qkv, gz, small, conv_w, gate_par, gdn_norm.reshape(1, GDN_DV))


def _ssd_prompt_body(nc, xbc_ref, sz_ref, sm_ref, cw_ref, cb_ref, sp_ref, sn_ref, o_ref, h_ref,
                     xp_ref):
    tb = nc * CHUNK
    first = pl.program_id(1) == 0

    @pl.when(first)
    def _():
        h_ref[...] = jnp.zeros(h_ref.shape, F32)

    y = _silu(_causal_conv_block(xbc_ref.at[0], xp_ref, cw_ref[...], tb, first) + cb_ref[...])
    sm, sp = sm_ref[0], sp_ref[...]
    dt_all = _softplus(sm + sp[1:2])
    cum_all = _chunk_cumsum(dt_all * -jnp.exp(sp[0:1]), tb)
    cum_rows = _rows_of(cum_all, GATE_DT)
    incl = _iota((CHUNK, CHUNK), 1) <= _iota((CHUNK, CHUNK), 0)
    lane_lo = _iota((CHUNK, LANES), 1) < SSM_P
    row_lo = _iota((2 * SSM_P, SSM_N), 0) < SSM_P
    par_lo = _iota((1, LANES), 1) < SSM_P
    sn = sn_ref[...]
    n_pair = SSM_HEADS // 2
    pairs_per_group = n_pair // SSM_GROUPS
    gw = SSM_INNER // SSM_GROUPS
    b_bf, c_bf, scores = {}, {}, {}
    for c in range(nc):
        r = slice(c * CHUNK, (c + 1) * CHUNK)
        for grp in range(SSM_GROUPS):
            lo = SSM_INNER + grp * SSM_N
            b_bf[c, grp] = y[r, lo:lo + SSM_N].astype(BF16)
            c_bf[c, grp] = y[r, lo + SSM_BC:lo + SSM_BC + SSM_N].astype(BF16)
            scores[c, grp] = _mm_nt(c_bf[c, grp], b_bf[c, grp])
    pre = {}
    for c in range(nc):
        r = slice(c * CHUNK, (c + 1) * CHUNK)
        for j in range(n_pair):
            grp = j // pairs_per_group
            cums, segs, dts = [], [], []
            for head in (2 * j, 2 * j + 1):
                cum_c = cum_all[r, GATE_DT + head:GATE_DT + head + 1]
                cum_r = cum_rows[head:head + 1, c * CHUNK:(c + 1) * CHUNK]
                cums.append(cum_c)
                segs.append(jnp.where(incl, jnp.exp(jnp.minimum(cum_c - cum_r, 0.0)), 0.0))
                dts.append(dt_all[r, GATE_DT + head:GATE_DT + head + 1])
            x_pair = y[r, j * LANES:(j + 1) * LANES]
            xdt = x_pair * jnp.where(lane_lo, dts[0], dts[1])
            last = [cm[CHUNK - 1:CHUNK] for cm in cums]
            dec = jnp.where(lane_lo, jnp.exp(last[0] - cums[0]), jnp.exp(last[1] - cums[1]))
            d_pair = jnp.where(par_lo, sp[2:3, GATE_DT + 2 * j:GATE_DT + 2 * j + 1],
                               sp[2:3, GATE_DT + 2 * j + 1:GATE_DT + 2 * j + 2])
            pre[c, j] = dict(
                y=(_mm(scores[c, grp] * segs[0], jnp.where(lane_lo, xdt, 0.0))
                   + _mm(scores[c, grp] * segs[1], jnp.where(lane_lo, 0.0, xdt))
                   + d_pair * x_pair),
                e=jnp.where(lane_lo, jnp.exp(cums[0]), jnp.exp(cums[1])),
                xdec=(xdt * dec).astype(BF16),
                e_last=jnp.where(row_lo, jnp.exp(last[0]), jnp.exp(last[1])))
    states = [h_ref[0, j] for j in range(n_pair)]
    for c in range(nc):
        r = slice(c * CHUNK, (c + 1) * CHUNK)
        ys = [pre[c, j]["y"] + _mm_nt(c_bf[c, j // pairs_per_group], states[j]) * pre[c, j]["e"]
              for j in range(n_pair)]
        states = [states[j] * pre[c, j]["e_last"]
                  + _mm_tn(pre[c, j]["xdec"], b_bf[c, j // pairs_per_group]) for j in range(n_pair)]
        for grp in range(SSM_GROUPS):
            outs = [ys[j] * _silu(sz_ref[0, r, j * LANES:(j + 1) * LANES])
                    for j in range(grp * pairs_per_group, (grp + 1) * pairs_per_group)]
            inv = lax.rsqrt(sum(jnp.sum(o * o, axis=-1, keepdims=True) for o in outs) / gw + EPS)
            for jj, o in enumerate(outs):
                lo = grp * gw + jj * LANES
                o_ref[0, r, lo:lo + LANES] = (o * inv * sn[:, lo:lo + LANES]).astype(o_ref.dtype)
    for j in range(n_pair):
        h_ref[0, j] = states[j]


def _ssd_prompt(xbc, sz, small, conv_w, conv_b, ssm_par, ssm_norm, nc):
    b, t, _ = xbc.shape
    tb = nc * CHUNK
    blk = lambda w: pl.BlockSpec((1, tb, w), lambda i, j: (i, j, 0))
    n_pair = SSM_HEADS // 2
    return pl.pallas_call(
        functools.partial(_ssd_prompt_body, nc),
        grid=(b, t // tb),
        in_specs=[blk(SSM_CONV_CH), blk(SSM_INNER), blk(GATE_W), _const_spec(conv_w.shape),
                  _const_spec((1, SSM_CONV_CH)), _const_spec(ssm_par.shape),
                  _const_spec((1, SSM_INNER))],
        out_specs=[blk(SSM_INNER),
                   pl.BlockSpec((1, n_pair, 2 * SSM_P, SSM_N), lambda i, j: (i, 0, 0, 0))],
        out_shape=[jax.ShapeDtypeStruct((b, t, SSM_INNER), BF16),
                   jax.ShapeDtypeStruct((b, n_pair, 2 * SSM_P, SSM_N), F32)],
        scratch_shapes=[pltpu.VMEM((tb + 8, SSM_CONV_CH), F32)],
        compiler_params=_cparams("parallel", "arbitrary"),
        name="ssd_prompt",
    )(xbc, sz, small, conv_w, conv_b.reshape(1, -1), ssm_par, ssm_norm.reshape(1, -1))


def _col_bcast(row):
    return jnp.transpose(jnp.broadcast_to(row, (LANES, LANES)))


def _decode_conv(new_ref, cv_ref, cw):
    y = cw[CONV_W - 1:CONV_W] * new_ref[...]
    for i in range(CONV_W - 1):
        y = y + cw[i:i + 1] * cv_ref[:, i, :]
    return y


def _gdn_decode_body(bb, qkv_ref, gz_ref, sm_ref, cv_ref, s0_ref, cw_ref, gp_ref, gn_ref, o_ref,
                     s_ref, raw_ref):
    y = _silu(_decode_conv(qkv_ref, cv_ref, cw_ref[...]))
    sm, gp = sm_ref[...], gp_ref[...]
    a_all = jnp.exp(-jnp.exp(gp[0:1]) * _softplus(sm + gp[1:2]))
    beta_all = _sigmoid(sm)
    qs, ks, vs = [], [], []
    for h in range(GDN_HEADS):
        lo = h * GDN_DK
        qs.append(_l2n(y[:, lo:lo + GDN_DK]) * GDN_DK ** -0.5)
        ks.append(_l2n(y[:, GDN_QK + lo:GDN_QK + lo + GDN_DK]))
        vs.append(y[:, 2 * GDN_QK + lo:2 * GDN_QK + lo + GDN_DV])
    pairs = [(b, h) for b in range(bb) for h in range(GDN_HEADS)]
    k_col = [_col_bcast(ks[h][b:b + 1]) for b, h in pairs]
    q_col = [_col_bcast(qs[h][b:b + 1]) for b, h in pairs]
    decayed = [s0_ref[b, h] * a_all[b:b + 1, GATE_A + h:GATE_A + h + 1] for b, h in pairs]
    v_new = [beta_all[b:b + 1, GATE_B + h:GATE_B + h + 1]
             * (vs[h][b:b + 1] - jnp.sum(k_col[i] * decayed[i], axis=0, keepdims=True))
             for i, (b, h) in enumerate(pairs)]
    for i, (b, h) in enumerate(pairs):
        state = decayed[i] + k_col[i] * v_new[i]
        s_ref[b, h] = state
        raw_ref[b:b + 1, h * GDN_DV:(h + 1) * GDN_DV] = jnp.sum(q_col[i] * state, axis=0,
                                                                keepdims=True)
    gn = gn_ref[...]
    for h in range(GDN_HEADS):
        hs = slice(h * GDN_DV, (h + 1) * GDN_DV)
        o_ref[:, hs] = (_rms(raw_ref[:, hs], gn) * _silu(gz_ref[:, hs])).astype(o_ref.dtype)


DECODE_BLOCK = 8


def _gdn_decode(qkv, gz, small, conv0, s0, conv_w, gate_par, gdn_norm):
    b = qkv.shape[0]
    bb = DECODE_BLOCK
    row = lambda w: pl.BlockSpec((bb, w), lambda i: (i, 0))
    st = pl.BlockSpec((bb, GDN_HEADS, GDN_DK, GDN_DV), lambda i: (i, 0, 0, 0))
    return pl.pallas_call(
        functools.partial(_gdn_decode_body, bb),
        grid=(b // bb,),
        in_specs=[row(GDN_CONV_CH), row(GDN_VW), row(GATE_W),
                  pl.BlockSpec((bb, CONV_W - 1, GDN_CONV_CH), lambda i: (i, 0, 0)), st,
                  _const_spec(conv_w.shape), _const_spec(gate_par.shape), _const_spec((1, GDN_DV))],
        out_specs=[row(GDN_VW), st],
        out_shape=[jax.ShapeDtypeStruct((b, GDN_VW), F32), jax.ShapeDtypeStruct(s0.shape, F32)],
        scratch_shapes=[pltpu.VMEM((bb, GDN_VW), F32)],
        compiler_params=_cparams("parallel"),
        name="gdn_decode",
    )(qkv, gz, small, conv0, s0, conv_w, gate_par, gdn_norm.reshape(1, GDN_DV))


def _ssd_decode_body(bb, xbc_ref, sz_ref, sm_ref, cv_ref, h0_ref, cw_ref, cb_ref, sp_ref, sn_ref,
                     o_ref, h_ref, raw_ref):
    y = _silu(_decode_conv(xbc_ref, cv_ref, cw_ref[...]) + cb_ref[...])
    sm, sp = sm_ref[...], sp_ref[...]
    dt_all = _softplus(sm + sp[1:2])
    da_all = jnp.exp(dt_all * -jnp.exp(sp[0:1]))
    par_lo = _iota((1, LANES), 1) < SSM_P
    row_lo = _iota((2 * SSM_P, SSM_N), 0) < SSM_P
    n_pair = SSM_HEADS // 2
    pairs_per_group = n_pair // SSM_GROUPS
    pairs = [(b, j) for b in range(bb) for j in range(n_pair)]
    x_col, states = [], []
    for b, j in pairs:
        la, lb = GATE_DT + 2 * j, GATE_DT + 2 * j + 1
        row = slice(b, b + 1)
        xdt = y[row, j * LANES:(j + 1) * LANES] * jnp.where(par_lo, dt_all[row, la:la + 1],
                                                           dt_all[row, lb:lb + 1])
        x_col.append(_col_bcast(xdt))
    for i, (b, j) in enumerate(pairs):
        la, lb = GATE_DT + 2 * j, GATE_DT + 2 * j + 1
        row = slice(b, b + 1)
        grp = j // pairs_per_group
        b_g = y[row, SSM_INNER + grp * SSM_N:SSM_INNER + (grp + 1) * SSM_N]
        state = (h0_ref[b, j] * jnp.where(row_lo, da_all[row, la:la + 1], da_all[row, lb:lb + 1])
                 + x_col[i] * b_g)
        h_ref[b, j] = state
        states.append(state)
    for i, (b, j) in enumerate(pairs):
        grp = j // pairs_per_group
        lo = SSM_INNER + SSM_BC + grp * SSM_N
        y_col = jnp.sum(states[i] * y[b:b + 1, lo:lo + SSM_N], axis=1, keepdims=True)
        raw_ref[b:b + 1, j * LANES:(j + 1) * LANES] = jnp.transpose(
            jnp.broadcast_to(y_col, (LANES, LANES)))[0:1]
    sn = sn_ref[...]
    gw = SSM_INNER // SSM_GROUPS
    for grp in range(SSM_GROUPS):
        outs = []
        for jj in range(pairs_per_group):
            j = grp * pairs_per_group + jj
            la, lb = GATE_DT + 2 * j, GATE_DT + 2 * j + 1
            js = slice(j * LANES, (j + 1) * LANES)
            d_pair = jnp.where(par_lo, sp[2:3, la:la + 1], sp[2:3, lb:lb + 1])
            outs.append((raw_ref[:, js] + d_pair * y[:, js]) * _silu(sz_ref[:, js]))
        inv = lax.rsqrt(sum(jnp.sum(o * o, axis=-1, keepdims=True) for o in outs) / gw + EPS)
        for jj, o in enumerate(outs):
            lo = grp * gw + jj * LANES
            o_ref[:, lo:lo + LANES] = (o * inv * sn[:, lo:lo + LANES]).astype(o_ref.dtype)


def _ssd_decode(xbc, sz, small, conv0, h0, conv_w, conv_b, ssm_par, ssm_norm):
    b = xbc.shape[0]
    bb = DECODE_BLOCK
    n_pair = SSM_HEADS // 2
    row = lambda w: pl.BlockSpec((bb, w), lambda i: (i, 0))
    st = pl.BlockSpec((bb, n_pair, 2 * SSM_P, SSM_N), lambda i: (i, 0, 0, 0))
    return pl.pallas_call(
        functools.partial(_ssd_decode_body, bb),
        grid=(b // bb,),
        in_specs=[row(SSM_CONV_CH), row(SSM_INNER), row(GATE_W),
                  pl.BlockSpec((bb, CONV_W - 1, SSM_CONV_CH), lambda i: (i, 0, 0)), st,
                  _const_spec(conv_w.shape), _const_spec((1, SSM_CONV_CH)),
                  _const_spec(ssm_par.shape), _const_spec((1, SSM_INNER))],
        out_specs=[row(SSM_INNER), st],
        out_shape=[jax.ShapeDtypeStruct((b, SSM_INNER), F32), jax.ShapeDtypeStruct(h0.shape, F32)],
        scratch_shapes=[pltpu.VMEM((bb, SSM_INNER), F32)],
        compiler_params=_cparams("parallel"),
        name="ssd_decode",
    )(xbc, sz, small, conv0, h0, conv_w, conv_b.reshape(1, -1), ssm_par, ssm_norm.reshape(1, -1))


def _res_ffn_body(n_a, n_ff, x_ref, *refs):
    a_refs, w_refs = refs[:n_a], refs[n_a:2 * n_a]
    g_ref, wgu_ref, wd_ref, o_ref, xn_ref = refs[2 * n_a:]
    ff_tile = wd_ref.shape[1]
    d_ff = n_ff * ff_tile
    x1 = x_ref[...]
    for a_ref, w_ref in zip(a_refs, w_refs):
        x1 = x1 + jnp.dot(a_ref[...].astype(BF16), w_ref[...], preferred_element_type=F32)
    xn_ref[...] = _rms(x1, g_ref[...]).astype(BF16)
    o_ref[...] = x1

    def step(c, carry):
        xn = xn_ref[...]
        lo = pl.multiple_of(c * ff_tile, ff_tile)
        gate = jnp.dot(xn, wgu_ref[:, pl.ds(lo, ff_tile)], preferred_element_type=F32)
        up = jnp.dot(xn, wgu_ref[:, pl.ds(d_ff + lo, ff_tile)], preferred_element_type=F32)
        hid = (_silu(gate) * up).astype(BF16)
        o_ref[...] += jnp.dot(hid, wd_ref[c], preferred_element_type=F32)
        return carry

    lax.fori_loop(0, n_ff, step, 0)


def _res_ffn(x, acts, projs, gain, wgu, wd, tm):
    rows, d = x.shape
    n_a, n_ff = len(acts), wd.shape[0]
    return pl.pallas_call(
        functools.partial(_res_ffn_body, n_a, n_ff),
        grid=(rows // tm,),
        in_specs=[pl.BlockSpec((tm, d), lambda i: (i, 0))]
        + [pl.BlockSpec((tm, a.shape[1]), lambda i: (i, 0)) for a in acts]
        + [_const_spec(p.shape) for p in projs]
        + [_const_spec((1, d)), _const_spec(wgu.shape), _const_spec(wd.shape)],
        out_specs=pl.BlockSpec((tm, d), lambda i: (i, 0)),
        out_shape=jax.ShapeDtypeStruct((rows, d), F32),
        scratch_shapes=[pltpu.VMEM((tm, d), BF16)],
        compiler_params=_cparams("parallel"),
        name="res_ffn",
    )(x, *acts, *projs, gain.reshape(1, d), wgu, wd)


def _mla_proj_body(tm, q_scale, x_ref, cos_ref, sin_ref, g_ref, wcq_ref, wckv_ref, wkr_ref,
                   qan_ref, kvn_ref, wuq_ref, wuk_ref, wuv_ref, vone_ref, gq_ref, gk_ref, gkr_ref,
                   segq_ref, segk_ref, q_ref, k_ref, v_ref, rows_ref):
    xn = _rms(x_ref[...], g_ref[...]).astype(BF16)
    lane = _iota((tm, LANES), 1)
    cos_t, sin_t = cos_ref[...], sin_ref[...]

    kr = jnp.dot(xn, wkr_ref[...], preferred_element_type=F32)
    ssq_kr = jnp.sum(jnp.where(lane < QK_ROPE, kr * kr, 0.0), axis=-1, keepdims=True)
    krg = kr * gkr_ref[...]
    kr_rot = krg * cos_t + pltpu.roll(krg, LANES - QK_ROPE, 1) * sin_t

    c = _rms(jnp.dot(xn, wckv_ref[...], preferred_element_type=F32), kvn_ref[...])
    cb = c.astype(BF16)
    kx = jnp.dot(cb, wuk_ref[...], preferred_element_type=F32)
    ssq_k = jnp.dot((kx * kx).astype(BF16), segk_ref[...],
                    preferred_element_type=F32)
    inv_r = lax.rsqrt((ssq_k + ssq_kr) / QK_HEAD + EPS)
    tail = jnp.where(lane < QK_ROPE, kr_rot, inv_r)
    rows_ref[:, 0:KV_LORA] = c
    rows_ref[:, KV_LORA:MLA_ROW] = tail[:, 0:MLA_ROW - KV_LORA]
    kr_put = jnp.where((lane >= QK_NOPE) & (lane < QK_HEAD), pltpu.roll(kr_rot, QK_NOPE, 1), 0.0)
    gk = gk_ref[...]
    for h in range(MLA_HEADS):
        hs = slice(h * LANES, (h + 1) * LANES)
        inv_h = inv_r[:, QK_ROPE + h:QK_ROPE + h + 1]
        k_ref[:, hs] = ((kx[:, hs] * gk + kr_put) * inv_h).astype(k_ref.dtype)
    v_ref[...] = (jnp.dot(cb, wuv_ref[...], preferred_element_type=F32)
                  + vone_ref[...]).astype(v_ref.dtype)

    cq = _rms(jnp.dot(xn, wcq_ref[...], preferred_element_type=F32), qan_ref[...]).astype(BF16)
    qx = jnp.dot(cq, wuq_ref[...], preferred_element_type=F32)
    ssq_q = jnp.dot((qx * qx).astype(BF16), segq_ref[...], preferred_element_type=F32)
    inv_q = lax.rsqrt(ssq_q / QK_HEAD + EPS)
    gq = gq_ref[...]
    scale = q_scale
    in_rope = (lane >= QK_NOPE) & (lane < QK_HEAD)
    keep = jnp.where(lane < QK_NOPE, scale, jnp.where(in_rope, cos_t * scale, 0.0))
    swap = jnp.where(in_rope, sin_t * scale, 0.0)
    for h in range(MLA_HEADS):
        hs = slice(h * LANES, (h + 1) * LANES)
        t = qx[:, hs] * inv_q[:, h:h + 1] * gq
        q_ref[:, hs] = (t * keep + pltpu.roll(t, LANES - QK_ROPE, 1) * swap).astype(q_ref.dtype)


def _mla_proj(x, seq, pos0, w, tm, q_dtype, q_scale):
    rows, d = x.shape
    half = QK_ROPE // 2
    inv_freq = ROPE_THETA ** (-(jnp.arange(LANES) % half).astype(F32) / half)
    ang = (pos0 + jnp.arange(seq, dtype=F32))[:, None] * inv_freq[None, :]
    tr = tm if seq > 1 else 1
    nblk = seq // tr
    rope_spec = pl.BlockSpec((tr, LANES), lambda i: (i % nblk, 0))
    names = ("norm", "wcq", "wckv", "wkr", "qan", "kvn", "wuq", "wuk", "wuv", "vone", "gq", "gk",
             "gkr", "segq", "segk")
    consts = [w[n] for n in names]
    out_w = (HEAD_W, HEAD_W, HEAD_W, MLA_ROW)
    out_dt = (q_dtype, BF16, BF16, F32)
    return pl.pallas_call(
        functools.partial(_mla_proj_body, tm, q_scale),
        grid=(rows // tm,),
        in_specs=[pl.BlockSpec((tm, d), lambda i: (i, 0)), rope_spec, rope_spec]
        + [_const_spec(c.shape) for c in consts],
        out_specs=[pl.BlockSpec((tm, ow), lambda i: (i, 0)) for ow in out_w],
        out_shape=[jax.ShapeDtypeStruct((rows, ow), dt) for ow, dt in zip(out_w, out_dt)],
        compiler_params=_cparams("parallel"),
        name="mla_proj",
    )(x, jnp.cos(ang), jnp.sin(ang), *consts)


def _flash_body(tq, nh, q_ref, k_ref, v_ref, o_ref):
    qi = pl.program_id(2)
    half = tq // 2
    heads = [slice(j * LANES, (j + 1) * LANES) for j in range(nh)]
    q = [q_ref[0, :, hs] for hs in heads]

    def attend(carry, qs, kv_rows, mask):
        s = [lax.dot_general(qs[j], k_ref[0, kv_rows, hs], (((1,), (1,)), ((), ())),
                             preferred_element_type=F32) for j, hs in enumerate(heads)]
        if mask is not None:
            s = [jnp.where(mask, sj, -jnp.inf) for sj in s]
        m_new = [jnp.maximum(carry[j][0], jnp.max(s[j], axis=-1, keepdims=True)) for j in range(nh)]
        p = [jnp.exp2((s[j] - m_new[j]).astype(BF16)) for j in range(nh)]
        pv = [jnp.dot(p[j], v_ref[0, kv_rows, hs], preferred_element_type=F32)
              for j, hs in enumerate(heads)]
        return tuple((m_new[j], jnp.exp2(carry[j][0] - m_new[j]) * carry[j][1] + pv[j])
                     for j in range(nh))

    init = tuple((jnp.full((tq, 1), -jnp.inf, F32), jnp.zeros((tq, LANES), F32)) for _ in range(nh))
    carry = lax.fori_loop(
        0, 2 * qi, lambda kb, cr: attend(cr, q, pl.ds(pl.multiple_of(kb * half, half), half), None),
        init)
    base = pl.multiple_of(qi * tq, tq)
    carry = attend(carry, q, pl.ds(base, half),
                   _iota((tq, half), 1) <= _iota((tq, half), 0))
    lower = attend(tuple((m[half:], acc[half:]) for m, acc in carry), [qj[half:] for qj in q],
                   pl.ds(base + half, half), _iota((half, half), 1) <= _iota((half, half), 0))
    low = _iota((half, LANES), 1) < V_HEAD
    for part, rows in ((tuple((m[:half], acc[:half]) for m, acc in carry), slice(0, half)),
                       (lower, slice(half, tq))):
        outs = [acc / acc[:, V_HEAD:V_HEAD + 1] for _, acc in part]
        for j in range(nh // 2):
            o_ref[0, rows, j * LANES:(j + 1) * LANES] = jnp.where(
                low, outs[2 * j], pltpu.roll(outs[2 * j + 1], V_HEAD, 1)).astype(o_ref.dtype)


def _flash_attention(q, k, v, tq, nh):
    b, t, _ = q.shape
    kv_spec = pl.BlockSpec((1, t, nh * LANES), lambda i, j, n: (i, 0, j))
    return pl.pallas_call(
        functools.partial(_flash_body, tq, nh),
        grid=(b, MLA_HEADS // nh, t // tq),
        in_specs=[pl.BlockSpec((1, tq, nh * LANES), lambda i, j, n: (i, n, j)), kv_spec, kv_spec],
        out_specs=pl.BlockSpec((1, tq, nh * V_HEAD), lambda i, j, n: (i, n, j)),
        out_shape=jax.ShapeDtypeStruct((b, t, MLA_HEADS * V_HEAD), BF16),
        compiler_params=_cparams("parallel", "parallel", "arbitrary"),
        name="flash_attention",
    )(q, k, v)


def _absorb_body(q_ref, g_ref, w_ref, o_ref):
    t = (q_ref[...] * g_ref[...]).astype(BF16)
    o_ref[...] = jnp.dot(t, w_ref[...], preferred_element_type=F32).astype(o_ref.dtype)


def _absorb(q, gain, w_abs):
    b = q.shape[0]
    return pl.pallas_call(
        _absorb_body,
        grid=(MLA_HEADS,),
        in_specs=[pl.BlockSpec((b, LANES), lambda h: (0, h)), _const_spec((1, LANES)),
                  pl.BlockSpec((None, LANES, Q_EXT), lambda h: (h, 0, 0))],
        out_specs=pl.BlockSpec((None, b, Q_EXT), lambda h: (h, 0, 0)),
        out_shape=jax.ShapeDtypeStruct((MLA_HEADS, b, Q_EXT), BF16),
        compiler_params=_cparams("parallel"),
        name="mla_absorb",
    )(q, gain, w_abs)


def _paged_attn_body(n_pages, pt_ref, q_ref, rn_ref, cache_ref, o_ref, buf_ref, sem_ref, s_ref,
                     pb_ref):
    i = pl.program_id(0)
    slot = lax.rem(i, 2)

    def page_copy(seq, p, sl):
        return pltpu.make_async_copy(cache_ref.at[pt_ref[seq, p]], buf_ref.at[sl, p],
                                     sem_ref.at[sl])

    @pl.when(i == 0)
    def _():
        for p in range(n_pages):
            page_copy(0, p, 0).start()

    @pl.when(i + 1 < pl.num_programs(0))
    def _():
        for p in range(n_pages):
            page_copy(i + 1, p, 1 - slot).start()

    for p in range(n_pages):
        page_copy(i, p, slot).wait()

    q = q_ref[0]
    inv_lo = MLA_ROW - MLA_HEADS
    for p in range(n_pages):
        page = buf_ref[slot, p]
        cols = slice(p * PAGE_SIZE, (p + 1) * PAGE_SIZE)
        pb_ref[:, cols] = page.astype(BF16)
        s_ref[:, cols] = page[inv_lo:MLA_ROW]
    s_all = jnp.dot(q[:, :MLA_ROW], pb_ref[...], preferred_element_type=F32) * s_ref[...]
    rn = rn_ref[0]
    qf = q[:, :MLA_ROW].astype(F32)
    pick = _iota((MLA_HEADS, MLA_ROW), 1) == _iota((MLA_HEADS, MLA_ROW), 0) + inv_lo
    inv_new = jnp.sum(jnp.where(pick, rn, 0.0), axis=-1, keepdims=True)
    s_new = jnp.sum(qf * rn, axis=-1, keepdims=True) * inv_new
    m = jnp.maximum(jnp.max(s_all, axis=-1, keepdims=True), s_new)
    e_new = jnp.exp(s_new - m)
    e = jnp.exp(s_all - m)
    den = e_new + jnp.sum(e, axis=-1, keepdims=True)
    ctx = e_new * rn[:, 0:KV_LORA] + lax.dot_general(
        e.astype(BF16), pb_ref[0:KV_LORA, :], (((1,), (1,)), ((), ())), preferred_element_type=F32)
    o_ref[0] = ctx / den


def _paged_attention(q_ext, rows_new, cache_t, page_table):
    b, n_pages = page_table.shape
    grid_spec = pltpu.PrefetchScalarGridSpec(
        num_scalar_prefetch=1,
        grid=(b,),
        in_specs=[pl.BlockSpec((1, MLA_HEADS, Q_EXT), lambda i, pt: (i, 0, 0)),
                  pl.BlockSpec((1, 1, MLA_ROW), lambda i, pt: (i, 0, 0)),
                  pl.BlockSpec(memory_space=pl.ANY)],
        out_specs=pl.BlockSpec((1, MLA_HEADS, KV_LORA), lambda i, pt: (i, 0, 0)),
        scratch_shapes=[pltpu.VMEM((2, n_pages, MLA_ROW, PAGE_SIZE), F32),
                        pltpu.SemaphoreType.DMA((2,)),
                        pltpu.VMEM((MLA_HEADS, n_pages * PAGE_SIZE), F32),
                        pltpu.VMEM((MLA_ROW, n_pages * PAGE_SIZE), BF16)],
    )
    return pl.pallas_call(
        functools.partial(_paged_attn_body, n_pages),
        grid_spec=grid_spec,
        out_shape=jax.ShapeDtypeStruct((b, MLA_HEADS, KV_LORA), F32),
        compiler_params=_cparams("arbitrary"),
        name="paged_attention",
    )(page_table, q_ext, rows_new, cache_t)


def _ctx_out_body(c_ref, w_ref, o_ref):
    o_ref[...] = (jnp.dot(c_ref[0].astype(BF16), w_ref[0], preferred_element_type=F32)
                  + jnp.dot(c_ref[1].astype(BF16), w_ref[1], preferred_element_type=F32)
                  ).astype(o_ref.dtype)


def _ctx_out(ctx_t, w_pairs):
    b = ctx_t.shape[1]
    return pl.pallas_call(
        _ctx_out_body,
        grid=(MLA_HEADS // 2,),
        in_specs=[pl.BlockSpec((2, b, KV_LORA), lambda j: (j, 0, 0)),
                  pl.BlockSpec((None, 2, KV_LORA, LANES), lambda j: (j, 0, 0, 0))],
        out_specs=pl.BlockSpec((b, LANES), lambda j: (0, j)),
        out_shape=jax.ShapeDtypeStruct((b, MLA_HEADS * V_HEAD), BF16),
        compiler_params=_cparams("parallel"),
        name="mla_ctx_out",
    )(ctx_t, w_pairs)


def _lane_pad(x, lo, width):
    pad = [(0, 0)] * (x.ndim - 1) + [(lo, width - lo - x.shape[-1])]
    return jnp.pad(x.astype(F32), pad)


def _gate_tile(rows):
    tile = jnp.stack([_lane_pad(vec, off, GATE_W) for off, vec in rows])
    return jnp.pad(tile, ((0, 8 - len(rows)), (0, 0)))


def _prep_hybrid(j, w_in_a, conv_gdn_w, gdn_A_log, gdn_dt_bias, conv_ssm_w, ssm_A_log,
                 ssm_dt_bias, ssm_D):
    widths = (GDN_CONV_CH, GDN_VW, GDN_HEADS, GDN_HEADS, SSM_INNER, SSM_CONV_CH, SSM_HEADS)
    offs = np.concatenate([[0], np.cumsum(widths)])
    w = w_in_a[j].astype(BF16)
    part = lambda i: w[:, offs[i]:offs[i + 1]]
    assert (GATE_A, GATE_B, GATE_DT) == (0, GDN_HEADS, 2 * GDN_HEADS)
    small = _lane_pad(jnp.concatenate([part(2), part(3), part(6)], axis=1), 0, GATE_W)
    return dict(
        w_in=[part(0), part(1), part(4), part(5), small.astype(BF16)],
        conv_gdn=conv_gdn_w[j], conv_ssm=conv_ssm_w[j],
        gate_gdn=_gate_tile([(GATE_A, gdn_A_log[j]), (GATE_A, gdn_dt_bias[j])]),
        gate_ssm=_gate_tile([(GATE_DT, ssm_A_log[j]), (GATE_DT, ssm_dt_bias[j]),
                             (GATE_DT, ssm_D[j])]),
    )


def _prep_mla(j, norm_mix_c, w_in_c, q_a_norm, kv_a_norm, w_uq, w_uk, w_uv, q_norm, k_norm):
    half = QK_ROPE // 2
    w_in = w_in_c[j]

    def swapped(cols):
        return jnp.concatenate([-cols[..., half:], cols[..., :half]], axis=-1)

    def swapped_gain(g):
        return jnp.concatenate([g[half:], g[:half]])

    w_kr = w_in[:, Q_LORA + KV_LORA:]
    wkr = _lane_pad(jnp.concatenate([w_kr, swapped(w_kr)], axis=1), 0, LANES)
    qn, kn = q_norm[j], k_norm[j]
    gkr = _lane_pad(jnp.concatenate([kn[QK_NOPE:], swapped_gain(kn[QK_NOPE:])]), 0, LANES)
    gk = _lane_pad(kn[:QK_NOPE], 0, LANES)
    gq = jnp.concatenate([qn, swapped_gain(qn[QK_NOPE:])])

    uq = w_uq[j].reshape(Q_LORA, MLA_HEADS, QK_HEAD)
    wuq = jnp.concatenate([uq, swapped(uq[..., QK_NOPE:])], axis=-1).reshape(Q_LORA, HEAD_W)
    wuk = _lane_pad(w_uk[j], 0, LANES).reshape(KV_LORA, HEAD_W)
    wuv = _lane_pad(w_uv[j], 0, LANES).reshape(KV_LORA, HEAD_W)

    col = np.arange(HEAD_W)
    head, within = col // LANES, col % LANES
    lane = np.arange(LANES)
    inv_lane = QK_ROPE + head
    segq = ((within < QK_HEAD)[:, None] & (lane[None, :] == head[:, None]))
    segk = ((within < QK_NOPE)[:, None] & (lane[None, :] == inv_lane[:, None]))
    vone = (within == V_HEAD).astype(np.float32).reshape(1, HEAD_W)
    sel = lambda m: jnp.asarray(m, BF16)

    rope_pass = np.zeros((LANES, Q_EXT), np.float32)
    rope_pass[QK_NOPE:QK_HEAD, KV_LORA:KV_LORA + QK_ROPE] = np.eye(QK_ROPE)
    w_abs = jnp.pad(jnp.transpose(w_uk[j], (1, 2, 0)),
                    ((0, 0), (0, LANES - QK_NOPE), (0, Q_EXT - KV_LORA))) + rope_pass
    g_abs = jnp.concatenate([kn[:QK_NOPE], jnp.ones((QK_ROPE,), F32),
                             jnp.zeros((LANES - QK_HEAD,), F32)])
    uv = jnp.transpose(w_uv[j], (1, 0, 2)).reshape(MLA_HEADS // 2, 2, KV_LORA, V_HEAD)
    w_pairs = jnp.stack([_lane_pad(uv[:, 0], 0, LANES), _lane_pad(uv[:, 1], V_HEAD, LANES)], axis=1)
    row = lambda v: v.reshape(1, -1)
    return dict(
        norm=row(norm_mix_c[j]), wcq=w_in[:, :Q_LORA].astype(BF16),
        wckv=w_in[:, Q_LORA:Q_LORA + KV_LORA].astype(BF16), wkr=wkr.astype(BF16),
        qan=row(q_a_norm[j]), kvn=row(kv_a_norm[j]), wuq=wuq.astype(BF16), wuk=wuk.astype(BF16),
        wuv=wuv.astype(BF16), vone=jnp.asarray(vone), gq=row(gq), gk=row(gk), gkr=row(gkr),
        segq=sel(segq), segk=sel(segk),
        w_abs=w_abs.astype(BF16), g_abs=row(g_abs), w_pairs=w_pairs.astype(BF16),
    )


def _prep_ffn(layer, w_gate_up, w_down, ff_tile):
    d_ff, d = w_down.shape[1:]
    return (w_gate_up[layer].astype(BF16),
            w_down[layer].reshape(d_ff // ff_tile, ff_tile, d).astype(BF16))


FF_TILE = 256
ROW_TILE = 512
FFN_ROW_TILE = 1024
MLA_ROW_TILE = 256
ATTN_TILE = 1024
ATTN_HEADS = 4
GDN_CHUNKS = 4
SSD_CHUNKS = 4


def _row_tile(rows, want):
    return want if rows % want == 0 else rows


def _trunk(x, pos0, states, cache_mla, page_table, hyb, mla, ffn, norm_mix_a, gdn_norm, conv_ssm_b,
           ssm_norm, w_out_a, w_out_c, norm_ffn):
    b, t, d = x.shape
    rows = b * t
    tm = _row_tile(rows, ROW_TILE)
    tm_f = _row_tile(rows, FFN_ROW_TILE)
    xf = x.reshape(rows, d)
    decode = states is not None

    qkv, gz, sz, xbc, small = _norm_proj(xf, norm_mix_a[0], hyb["w_in"], tm)
    if decode:
        s0, gconv0, h0, sconv0 = states
        r3 = lambda a: a.reshape(b, 1, a.shape[-1])
        n_pair = SSM_HEADS // 2
        o_gdn, s_new = _gdn_decode(qkv, gz, small, gconv0[0], s0[0], hyb["conv_gdn"],
                                   hyb["gate_gdn"], gdn_norm[0])
        o_ssd, h_new = _ssd_decode(xbc, sz, small, sconv0[0],
                                   h0[0].reshape(b, n_pair, 2 * SSM_P, SSM_N), hyb["conv_ssm"],
                                   conv_ssm_b[0], hyb["gate_ssm"], ssm_norm[0])
        gconv = jnp.concatenate([gconv0[0][:, 1:], r3(qkv)], axis=1)
        sconv = jnp.concatenate([sconv0[0][:, 1:], r3(xbc)], axis=1)
    else:
        r3 = lambda a: a.reshape(b, t, a.shape[-1])
        o_gdn, s_new = _gdn_prompt(r3(qkv), r3(gz), r3(small), hyb["conv_gdn"], hyb["gate_gdn"],
                                   gdn_norm[0], GDN_CHUNKS)
        o_ssd, h_new = _ssd_prompt(r3(xbc), r3(sz), r3(small), hyb["conv_ssm"], conv_ssm_b[0],
                                   hyb["gate_ssm"], ssm_norm[0], SSD_CHUNKS)
        gconv = r3(qkv)[:, t - (CONV_W - 1):]
        sconv = r3(xbc)[:, t - (CONV_W - 1):]
    h_new = h_new.reshape(b, SSM_HEADS, SSM_P, SSM_N)
    wo = w_out_a[0].astype(BF16)
    x1 = _res_ffn(xf, [o_gdn.reshape(rows, GDN_VW), o_ssd.reshape(rows, SSM_INNER)],
                  [wo[:GDN_VW], wo[GDN_VW:]], norm_ffn[0], *ffn[0], tm_f)

    tm_c = _row_tile(rows, MLA_ROW_TILE)
    q_scale = QK_HEAD ** -0.5 * (1.0 if decode else math.log2(math.e))
    q, k, v, mla_rows = _mla_proj(x1, t, pos0, mla, tm_c, F32 if decode else BF16, q_scale)
    if decode:
        q_ext = jnp.transpose(_absorb(q, mla["g_abs"], mla["w_abs"]), (1, 0, 2))
        ctx = _paged_attention(q_ext, mla_rows.reshape(b, 1, MLA_ROW),
                               jnp.swapaxes(cache_mla[0], 1, 2), page_table)
        attn = _ctx_out(jnp.transpose(ctx, (1, 0, 2)), mla["w_pairs"])
    else:
        r3 = lambda a: a.reshape(b, t, a.shape[-1])
        attn = _flash_attention(r3(q), r3(k), r3(v), _row_tile(t, ATTN_TILE),
                                ATTN_HEADS).reshape(rows, -1)
    x2 = _res_ffn(x1, [attn], [w_out_c[0].astype(BF16)], norm_ffn[1], *ffn[1], tm_f)
    return (x2.reshape(b, t, d), s_new[None], gconv[None], h_new[None], sconv[None],
            mla_rows.reshape(1, b, t, MLA_ROW))


def kernel(x_prompt, x_sample, state_gdn, state_gdn_conv, state_ssm, state_ssm_conv, cache_mla, page_table, norm_mix_a, w_in_a, conv_gdn_w, gdn_A_log, gdn_dt_bias, gdn_norm, conv_ssm_w, conv_ssm_b, ssm_A_log, ssm_dt_bias, ssm_D, ssm_norm, w_out_a, norm_mix_c, w_in_c, q_a_norm, kv_a_norm, w_uq, w_uk, w_uv, q_norm, k_norm, w_out_c, norm_ffn, w_gate_up, w_down):
    assert w_in_a.shape[0] == 1 and w_in_c.shape[0] == 1 and norm_ffn.shape[0] == 2
    hyb = _prep_hybrid(0, w_in_a, conv_gdn_w, gdn_A_log, gdn_dt_bias, conv_ssm_w, ssm_A_log,
                       ssm_dt_bias, ssm_D)
    mla = _prep_mla(0, norm_mix_c, w_in_c, q_a_norm, kv_a_norm, w_uq, w_uk, w_uv, q_norm, k_norm)
    ffn = [_prep_ffn(layer, w_gate_up, w_down, FF_TILE) for layer in range(2)]
    shared = (hyb, mla, ffn, norm_mix_a, gdn_norm, conv_ssm_b, ssm_norm, w_out_a, w_out_c, norm_ffn)
    prompt = _trunk(x_prompt, 0, None, None, None, *shared)
    past_len = page_table.shape[1] * PAGE_SIZE
    sample = _trunk(x_sample, past_len, (state_gdn, state_gdn_conv, state_ssm, state_ssm_conv),
                    cache_mla, page_table, *shared)
    return (prompt[0], sample[0]) + prompt[1:] + sample[1:]
```

```python
import functools
import math

import jax
import jax.numpy as jnp
import numpy as np
from jax import lax
from jax.experimental import pallas as pl
from jax.experimental.pallas import tpu as pltpu

F32 = jnp.float32
BF16 = jnp.bfloat16

EPS = 1e-6
CONV_W = 4
CHUNK = 64
PAGE_SIZE = 128
GDN_HEADS = 4
GDN_DK = 128
GDN_DV = 128
GDN_QK = GDN_HEADS * GDN_DK
GDN_VW = GDN_HEADS * GDN_DV
GDN_CONV_CH = 2 * GDN_QK + GDN_VW
SSM_HEADS = 8
SSM_P = 64
SSM_N = 128
SSM_GROUPS = 2
SSM_INNER = SSM_HEADS * SSM_P
SSM_BC = SSM_GROUPS * SSM_N
SSM_CONV_CH = SSM_INNER + 2 * SSM_BC
MLA_HEADS = 16
Q_LORA = 512
KV_LORA = 256
QK_NOPE = 64
QK_ROPE = 32
QK_HEAD = QK_NOPE + QK_ROPE
V_HEAD = 64
MLA_ROW = KV_LORA + QK_ROPE + MLA_HEADS
ROPE_THETA = 10000.0

LANES = 128
HEAD_W = MLA_HEADS * LANES
GATE_W = LANES
GATE_A, GATE_B, GATE_DT = 0, GDN_HEADS, 2 * GDN_HEADS
Q_EXT = 3 * LANES
VMEM_LIMIT = 48 * 1024 * 1024

_HI = lax.Precision.HIGHEST


def _cparams(*sem):
    return pltpu.CompilerParams(dimension_semantics=sem, vmem_limit_bytes=VMEM_LIMIT)


def _const_spec(shape):
    zeros = (0,) * len(shape)
    return pl.BlockSpec(shape, lambda *_: zeros, pipeline_mode=pl.Buffered(1))


def _mm(a, b):
    return jnp.dot(a.astype(BF16), b.astype(BF16), preferred_element_type=F32)


def _mm_nt(a, b):
    return lax.dot_general(a.astype(BF16), b.astype(BF16), (((1,), (1,)), ((), ())),
                           preferred_element_type=F32)


def _mm_tn(a, b):
    return lax.dot_general(a.astype(BF16), b.astype(BF16), (((0,), (0,)), ((), ())),
                           preferred_element_type=F32)


def _mm_split(a, sel):
    hi = a.astype(BF16)
    lo = (a - hi.astype(F32)).astype(BF16)
    return (jnp.dot(hi, sel, preferred_element_type=F32)
            + jnp.dot(lo, sel, preferred_element_type=F32))


def _sigmoid(x):
    return 1.0 / (1.0 + jnp.exp(-x))


def _silu(x):
    return x * _sigmoid(x)


def _softplus(x):
    return jnp.maximum(x, 0.0) + jnp.log(1.0 + jnp.exp(-jnp.abs(x)))


def _rms(x, gain):
    return x * lax.rsqrt(jnp.mean(x * x, axis=-1, keepdims=True) + EPS) * gain


def _l2n(x):
    return x * lax.rsqrt(jnp.sum(x * x, axis=-1, keepdims=True) + EPS)


def _iota(shape, dim):
    return lax.broadcasted_iota(jnp.int32, shape, dim)


def _chunk_cumsum(g, tb):
    row, col = _iota((tb, tb), 0), _iota((tb, tb), 1)
    shift = int(math.log2(CHUNK))
    same_chunk = lax.shift_right_logical(row, shift) == lax.shift_right_logical(col, shift)
    tril = jnp.where((col <= row) & same_chunk, 1.0, 0.0).astype(F32)
    return jnp.dot(tril, g, precision=_HI, preferred_element_type=F32)


def _rows_of(cols, lane0):
    sel = jnp.where(_iota((8, LANES), 1) == _iota((8, LANES), 0) + lane0, 1.0, 0.0).astype(F32)
    return lax.dot_general(sel, cols, (((1,), (1,)), ((), ())), precision=_HI,
                           preferred_element_type=F32)


def _causal_conv_block(x_ref, xp_ref, cw, tb, first):
    @pl.when(first)
    def _():
        xp_ref[0:8, :] = jnp.zeros((8, xp_ref.shape[1]), F32)

    xp_ref[8:8 + tb, :] = x_ref[...]
    off = 8 - (CONV_W - 1)
    y = cw[0:1] * xp_ref[off:off + tb, :]
    for i in range(1, CONV_W):
        y = y + cw[i:i + 1] * xp_ref[off + i:off + i + tb, :]
    xp_ref[0:8, :] = xp_ref[tb:tb + 8, :]
    return y


def _norm_proj_body(n_out, x_ref, g_ref, *refs):
    xn = _rms(x_ref[...], g_ref[...]).astype(BF16)
    for w_ref, o_ref in zip(refs[:n_out], refs[n_out:]):
        o_ref[...] = jnp.dot(xn, w_ref[...], preferred_element_type=F32)


def _norm_proj(x, gain, weights, tm):
    rows, d = x.shape
    n = len(weights)
    return pl.pallas_call(
        functools.partial(_norm_proj_body, n),
        grid=(rows // tm,),
        in_specs=[pl.BlockSpec((tm, d), lambda i: (i, 0)), _const_spec((1, d))]
        + [_const_spec(w.shape) for w in weights],
        out_specs=[pl.BlockSpec((tm, w.shape[1]), lambda i: (i, 0)) for w in weights],
        out_shape=[jax.ShapeDtypeStruct((rows, w.shape[1]), F32) for w in weights],
        compiler_params=_cparams("parallel"),
        name="norm_proj",
    )(x, gain.reshape(1, d), *weights)


def _gdn_prompt_body(nc, qkv_ref, gz_ref, sm_ref, cw_ref, gp_ref, gn_ref, o_ref, s_ref, xp_ref):
    tb = nc * CHUNK
    first = pl.program_id(1) == 0

    @pl.when(first)
    def _():
        s_ref[...] = jnp.zeros(s_ref.shape, F32)

    y = _silu(_causal_conv_block(qkv_ref.at[0], xp_ref, cw_ref[...], tb, first))
    sm, gp = sm_ref[0], gp_ref[...]
    g_all = -jnp.exp(gp[0:1]) * _softplus(sm + gp[1:2])
    beta_all = _sigmoid(sm)
    cum_all = _chunk_cumsum(g_all, tb)
    cum_rows = _rows_of(cum_all, GATE_A)
    incl = _iota((CHUNK, CHUNK), 1) <= _iota((CHUNK, CHUNK), 0)
    strict = _iota((CHUNK, CHUNK), 1) < _iota((CHUNK, CHUNK), 0)
    gn = gn_ref[...]
    pairs = [(c, h) for c in range(nc) for h in range(GDN_HEADS)]
    qs, ks, vs = [], [], []
    for h in range(GDN_HEADS):
        lo = h * GDN_DK
        qs.append(_l2n(y[:, lo:lo + GDN_DK]) * GDN_DK ** -0.5)
        ks.append(_l2n(y[:, GDN_QK + lo:GDN_QK + lo + GDN_DK]))
        vs.append(y[:, 2 * GDN_QK + lo:2 * GDN_QK + lo + GDN_DV])
    pre = []
    for c, h in pairs:
        r = slice(c * CHUNK, (c + 1) * CHUNK)
        q, k, v = qs[h][r], ks[h][r], vs[h][r]
        cum_c = cum_all[r, GATE_A + h:GATE_A + h + 1]
        cum_r = cum_rows[h:h + 1, c * CHUNK:(c + 1) * CHUNK]
        beta = beta_all[r, GATE_B + h:GATE_B + h + 1]
        decay = jnp.where(incl, jnp.exp(jnp.minimum(cum_c - cum_r, 0.0)), 0.0)
        e_c = jnp.exp(cum_c)
        kb = k * beta
        last = cum_c[CHUNK - 1:CHUNK]
        pre.append(dict(q=q.astype(BF16), k=k.astype(BF16), kb=kb, decay=decay,
                        rhs=jnp.concatenate([v * beta, kb * e_c], axis=1).astype(BF16),
                        qe=q * e_c, kd=(k * jnp.exp(last - cum_c)).astype(BF16),
                        e_last=jnp.exp(last)))
    lows = [jnp.where(strict, _mm_nt(p["kb"], p["k"]) * p["decay"], 0.0) for p in pre]
    attns = [(_mm_nt(p["q"], p["k"]) * p["decay"]).astype(BF16) for p in pre]
    eye = jnp.where(_iota((CHUNK, CHUNK), 0) == _iota((CHUNK, CHUNK), 1), 1.0, 0.0).astype(F32)
    invs = [eye - low for low in lows]
    powers = lows
    for _ in range(int(math.log2(CHUNK)) - 1):
        powers = [_mm(pw, pw) for pw in powers]
        invs = [inv + _mm(inv, pw) for inv, pw in zip(invs, powers)]
    sols = [_mm(inv, p["rhs"]).astype(BF16) for inv, p in zip(invs, pre)]
    k_sol = [_mm_tn(p["kd"], sol) for p, sol in zip(pre, sols)]
    a_sol = [_mm(attn, sol) for attn, sol in zip(attns, sols)]
    q_eff = [(p["qe"] - a[:, GDN_DV:]).astype(BF16) for p, a in zip(pre, a_sol)]
    m_eff = [ks[:, GDN_DV:].astype(BF16) for ks in k_sol]
    states = [s_ref[0, h] for h in range(GDN_HEADS)]
    for c in range(nc):
        r = slice(c * CHUNK, (c + 1) * CHUNK)
        idx = [c * GDN_HEADS + h for h in range(GDN_HEADS)]
        s_bf = [st.astype(BF16) for st in states]
        outs = [_mm(q_eff[i], s_bf[h]) + a_sol[i][:, :GDN_DV] for h, i in enumerate(idx)]
        states = [states[h] * pre[i]["e_last"] - _mm(m_eff[i], s_bf[h]) + k_sol[i][:, :GDN_DV]
                  for h, i in enumerate(idx)]
        for h, o in enumerate(outs):
            lo = h * GDN_DV
            gate = _silu(gz_ref[0, r, lo:lo + GDN_DV])
            o_ref[0, r, lo:lo + GDN_DV] = (_rms(o, gn) * gate).astype(o_ref.dtype)
    for h in range(GDN_HEADS):
        s_ref[0, h] = states[h]


def _gdn_prompt(qkv, gz, small, conv_w, gate_par, gdn_norm, nc):
    b, t, _ = qkv.shape
    tb = nc * CHUNK
    blk = lambda w: pl.BlockSpec((1, tb, w), lambda i, j: (i, j, 0))
    return pl.pallas_call(
        functools.partial(_gdn_prompt_body, nc),
        grid=(b, t // tb),
        in_specs=[blk(GDN_CONV_CH), blk(GDN_VW), blk(GATE_W), _const_spec(conv_w.shape),
                  _const_spec(gate_par.shape), _const_spec((1, GDN_DV))],
        out_specs=[blk(GDN_VW),
                   pl.BlockSpec((1, GDN_HEADS, GDN_DK, GDN_DV), lambda i, j: (i, 0, 0, 0))],
        out_shape=[jax.ShapeDtypeStruct((b, t, GDN_VW), BF16),
                   jax.ShapeDtypeStruct((b, GDN_HEADS, GDN_DK, GDN_DV), F32)],
        scratch_shapes=[pltpu.VMEM((tb + 8, GDN_CONV_CH), F32)],
        compiler_params=_cparams("parallel", "arbitrary"),
        name="gdn_prompt",
    )(qkv, gz, small, conv_w, gate_par, gdn_norm.reshape(1, GDN_DV))


def _ssd_prompt_body(nc, xbc_ref, sz_ref, sm_ref, cw_ref, cb_ref, sp_ref, sn_ref, o_ref, h_ref,
                     xp_ref):
    tb = nc * CHUNK
    first = pl.program_id(1) == 0

    @pl.when(first)
    def _():
        h_ref[...] = jnp.zeros(h_ref.shape, F32)

    y = _silu(_causal_conv_block(xbc_ref.at[0], xp_ref, cw_ref[...], tb, first) + cb_ref[...])
    sm, sp = sm_ref[0], sp_ref[...]
    dt_all = _softplus(sm + sp[1:2])
    cum_all = _chunk_cumsum(dt_all * -jnp.exp(sp[0:1]), tb)
    cum_rows = _rows_of(cum_all, GATE_DT)
    incl = _iota((CHUNK, CHUNK), 1) <= _iota((CHUNK, CHUNK), 0)
    lane_lo = _iota((CHUNK, LANES), 1) < SSM_P
    row_lo = _iota((2 * SSM_P, SSM_N), 0) < SSM_P
    par_lo = _iota((1, LANES), 1) < SSM_P
    sn = sn_ref[...]
    n_pair = SSM_HEADS // 2
    pairs_per_group = n_pair // SSM_GROUPS
    gw = SSM_INNER // SSM_GROUPS
    b_bf, c_bf, scores = {}, {}, {}
    for c in range(nc):
        r = slice(c * CHUNK, (c + 1) * CHUNK)
        for grp in range(SSM_GROUPS):
            lo = SSM_INNER + grp * SSM_N
            b_bf[c, grp] = y[r, lo:lo + SSM_N].astype(BF16)
            c_bf[c, grp] = y[r, lo + SSM_BC:lo + SSM_BC + SSM_N].astype(BF16)
            scores[c, grp] = _mm_nt(c_bf[c, grp], b_bf[c, grp])
    pre = {}
    for c in range(nc):
        r = slice(c * CHUNK, (c + 1) * CHUNK)
        for j in range(n_pair):
            grp = j // pairs_per_group
            cums, segs, dts = [], [], []
            for head in (2 * j, 2 * j + 1):
                cum_c = cum_all[r, GATE_DT + head:GATE_DT + head + 1]
                cum_r = cum_rows[head:head + 1, c * CHUNK:(c + 1) * CHUNK]
                cums.append(cum_c)
                segs.append(jnp.where(incl, jnp.exp(jnp.minimum(cum_c - cum_r, 0.0)), 0.0))
                dts.append(dt_all[r, GATE_DT + head:GATE_DT + head + 1])
            x_pair = y[r, j * LANES:(j + 1) * LANES]
            xdt = x_pair * jnp.where(lane_lo, dts[0], dts[1])
            last = [cm[CHUNK - 1:CHUNK] for cm in cums]
            dec = jnp.where(lane_lo, jnp.exp(last[0] - cums[0]), jnp.exp(last[1] - cums[1]))
            d_pair = jnp.where(par_lo, sp[2:3, GATE_DT + 2 * j:GATE_DT + 2 * j + 1],
                               sp[2:3, GATE_DT + 2 * j + 1:GATE_DT + 2 * j + 2])
            pre[c, j] = dict(
                y=(_mm(scores[c, grp] * segs[0], jnp.where(lane_lo, xdt, 0.0))
                   + _mm(scores[c, grp] * segs[1], jnp.where(lane_lo, 0.0, xdt))
                   + d_pair * x_pair),
                e=jnp.where(lane_lo, jnp.exp(cums[0]), jnp.exp(cums[1])),
                xdec=(xdt * dec).astype(BF16),
                e_last=jnp.where(row_lo, jnp.exp(last[0]), jnp.exp(last[1])))
    states = [h_ref[0, j] for j in range(n_pair)]
    for c in range(nc):
        r = slice(c * CHUNK, (c + 1) * CHUNK)
        ys = [pre[c, j]["y"] + _mm_nt(c_bf[c, j // pairs_per_group], states[j]) * pre[c, j]["e"]
              for j in range(n_pair)]
        states = [states[j] * pre[c, j]["e_last"]
                  + _mm_tn(pre[c, j]["xdec"], b_bf[c, j // pairs_per_group]) for j in range(n_pair)]
        for grp in range(SSM_GROUPS):
            outs = [ys[j] * _silu(sz_ref[0, r, j * LANES:(j + 1) * LANES])
                    for j in range(grp * pairs_per_group, (grp + 1) * pairs_per_group)]
            inv = lax.rsqrt(sum(jnp.sum(o * o, axis=-1, keepdims=True) for o in outs) / gw + EPS)
            for jj, o in enumerate(outs):
                lo = grp * gw + jj * LANES
                o_ref[0, r, lo:lo + LANES] = (o * inv * sn[:, lo:lo + LANES]).astype(o_ref.dtype)
    for j in range(n_pair):
        h_ref[0, j] = states[j]


def _ssd_prompt(xbc, sz, small, conv_w, conv_b, ssm_par, ssm_norm, nc):
    b, t, _ = xbc.shape
    tb = nc * CHUNK
    blk = lambda w: pl.BlockSpec((1, tb, w), lambda i, j: (i, j, 0))
    n_pair = SSM_HEADS // 2
    return pl.pallas_call(
        functools.partial(_ssd_prompt_body, nc),
        grid=(b, t // tb),
        in_specs=[blk(SSM_CONV_CH), blk(SSM_INNER), blk(GATE_W), _const_spec(conv_w.shape),
                  _const_spec((1, SSM_CONV_CH)), _const_spec(ssm_par.shape),
                  _const_spec((1, SSM_INNER))],
        out_specs=[blk(SSM_INNER),
                   pl.BlockSpec((1, n_pair, 2 * SSM_P, SSM_N), lambda i, j: (i, 0, 0, 0))],
        out_shape=[jax.ShapeDtypeStruct((b, t, SSM_INNER), BF16),
                   jax.ShapeDtypeStruct((b, n_pair, 2 * SSM_P, SSM_N), F32)],
        scratch_shapes=[pltpu.VMEM((tb + 8, SSM_CONV_CH), F32)],
        compiler_params=_cparams("parallel", "arbitrary"),
        name="ssd_prompt",
    )(xbc, sz, small, conv_w, conv_b.reshape(1, -1), ssm_par, ssm_norm.reshape(1, -1))


def _col_bcast(row):
    return jnp.transpose(jnp.broadcast_to(row, (LANES, LANES)))


def _decode_conv(new_ref, cv_ref, cw):
    y = cw[CONV_W - 1:CONV_W] * new_ref[...]
    for i in range(CONV_W - 1):
        y = y + cw[i:i + 1] * cv_ref[:, i, :]
    return y


def _gdn_decode_body(bb, qkv_ref, gz_ref, sm_ref, cv_ref, s0_ref, cw_ref, gp_ref, gn_ref, o_ref,
                     s_ref, raw_ref):
    y = _silu(_decode_conv(qkv_ref, cv_ref, cw_ref[...]))
    sm, gp = sm_ref[...], gp_ref[...]
    a_all = jnp.exp(-jnp.exp(gp[0:1]) * _softplus(sm + gp[1:2]))
    beta_all = _sigmoid(sm)
    qs, ks, vs = [], [], []
    for h in range(GDN_HEADS):
        lo = h * GDN_DK
        qs.append(_l2n(y[:, lo:lo + GDN_DK]) * GDN_DK ** -0.5)
        ks.append(_l2n(y[:, GDN_QK + lo:GDN_QK + lo + GDN_DK]))
        vs.append(y[:, 2 * GDN_QK + lo:2 * GDN_QK + lo + GDN_DV])
    pairs = [(b, h) for b in range(bb) for h in range(GDN_HEADS)]
    k_col = [_col_bcast(ks[h][b:b + 1]) for b, h in pairs]
    q_col = [_col_bcast(qs[h][b:b + 1]) for b, h in pairs]
    decayed = [s0_ref[b, h] * a_all[b:b + 1, GATE_A + h:GATE_A + h + 1] for b, h in pairs]
    v_new = [beta_all[b:b + 1, GATE_B + h:GATE_B + h + 1]
             * (vs[h][b:b + 1] - jnp.sum(k_col[i] * decayed[i], axis=0, keepdims=True))
             for i, (b, h) in enumerate(pairs)]
    for i, (b, h) in enumerate(pairs):
        state = decayed[i] + k_col[i] * v_new[i]
        s_ref[b, h] = state
        raw_ref[b:b + 1, h * GDN_DV:(h + 1) * GDN_DV] = jnp.sum(q_col[i] * state, axis=0,
                                                                keepdims=True)
    gn = gn_ref[...]
    for h in range(GDN_HEADS):
        hs = slice(h * GDN_DV, (h + 1) * GDN_DV)
        o_ref[:, hs] = (_rms(raw_ref[:, hs], gn) * _silu(gz_ref[:, hs])).astype(o_ref.dtype)


DECODE_BLOCK = 8


def _gdn_decode(qkv, gz, small, conv0, s0, conv_w, gate_par, gdn_norm):
    b = qkv.shape[0]
    bb = DECODE_BLOCK
    row = lambda w: pl.BlockSpec((bb, w), lambda i: (i, 0))
    st = pl.BlockSpec((bb, GDN_HEADS, GDN_DK, GDN_DV), lambda i: (i, 0, 0, 0))
    return pl.pallas_call(
        functools.partial(_gdn_decode_body, bb),
        grid=(b // bb,),
        in_specs=[row(GDN_CONV_CH), row(GDN_VW), row(GATE_W),
                  pl.BlockSpec((bb, CONV_W - 1, GDN_CONV_CH), lambda i: (i, 0, 0)), st,
                  _const_spec(conv_w.shape), _const_spec(gate_par.shape), _const_spec((1, GDN_DV))],
        out_specs=[row(GDN_VW), st],
        out_shape=[jax.ShapeDtypeStruct((b, GDN_VW), F32), jax.ShapeDtypeStruct(s0.shape, F32)],
        scratch_shapes=[pltpu.VMEM((bb, GDN_VW), F32)],
        compiler_params=_cparams("parallel"),
        name="gdn_decode",
    )(qkv, gz, small, conv0, s0, conv_w, gate_par, gdn_norm.reshape(1, GDN_DV))


def _ssd_decode_body(bb, xbc_ref, sz_ref, sm_ref, cv_ref, h0_ref, cw_ref, cb_ref, sp_ref, sn_ref,
                     o_ref, h_ref, raw_ref):
    y = _silu(_decode_conv(xbc_ref, cv_ref, cw_ref[...]) + cb_ref[...])
    sm, sp = sm_ref[...], sp_ref[...]
    dt_all = _softplus(sm + sp[1:2])
    da_all = jnp.exp(dt_all * -jnp.exp(sp[0:1]))
    par_lo = _iota((1, LANES), 1) < SSM_P
    row_lo = _iota((2 * SSM_P, SSM_N), 0) < SSM_P
    n_pair = SSM_HEADS // 2
    pairs_per_group = n_pair // SSM_GROUPS
    pairs = [(b, j) for b in range(bb) for j in range(n_pair)]
    x_col, states = [], []
    for b, j in pairs:
        la, lb = GATE_DT + 2 * j, GATE_DT + 2 * j + 1
        row = slice(b, b + 1)
        xdt = y[row, j * LANES:(j + 1) * LANES] * jnp.where(par_lo, dt_all[row, la:la + 1],
                                                           dt_all[row, lb:lb + 1])
        x_col.append(_col_bcast(xdt))
    for i, (b, j) in enumerate(pairs):
        la, lb = GATE_DT + 2 * j, GATE_DT + 2 * j + 1
        row = slice(b, b + 1)
        grp = j // pairs_per_group
        b_g = y[row, SSM_INNER + grp * SSM_N:SSM_INNER + (grp + 1) * SSM_N]
        state = (h0_ref[b, j] * jnp.where(row_lo, da_all[row, la:la + 1], da_all[row, lb:lb + 1])
                 + x_col[i] * b_g)
        h_ref[b, j] = state
        states.append(state)
    for i, (b, j) in enumerate(pairs):
        grp = j // pairs_per_group
        lo = SSM_INNER + SSM_BC + grp * SSM_N
        y_col = jnp.sum(states[i] * y[b:b + 1, lo:lo + SSM_N], axis=1, keepdims=True)
        raw_ref[b:b + 1, j * LANES:(j + 1) * LANES] = jnp.transpose(
            jnp.broadcast_to(y_col, (LANES, LANES)))[0:1]
    sn = sn_ref[...]
    gw = SSM_INNER // SSM_GROUPS
    for grp in range(SSM_GROUPS):
        outs = []
        for jj in range(pairs_per_group):
            j = grp * pairs_per_group + jj
            la, lb = GATE_DT + 2 * j, GATE_DT + 2 * j + 1
            js = slice(j * LANES, (j + 1) * LANES)
            d_pair = jnp.where(par_lo, sp[2:3, la:la + 1], sp[2:3, lb:lb + 1])
            outs.append((raw_ref[:, js] + d_pair * y[:, js]) * _silu(sz_ref[:, js]))
        inv = lax.rsqrt(sum(jnp.sum(o * o, axis=-1, keepdims=True) for o in outs) / gw + EPS)
        for jj, o in enumerate(outs):
            lo = grp * gw + jj * LANES
            o_ref[:, lo:lo + LANES] = (o * inv * sn[:, lo:lo + LANES]).astype(o_ref.dtype)


def _ssd_decode(xbc, sz, small, conv0, h0, conv_w, conv_b, ssm_par, ssm_norm):
    b = xbc.shape[0]
    bb = DECODE_BLOCK
    n_pair = SSM_HEADS // 2
    row = lambda w: pl.BlockSpec((bb, w), lambda i: (i, 0))
    st = pl.BlockSpec((bb, n_pair, 2 * SSM_P, SSM_N), lambda i: (i, 0, 0, 0))
    return pl.pallas_call(
        functools.partial(_ssd_decode_body, bb),
        grid=(b // bb,),
        in_specs=[row(SSM_CONV_CH), row(SSM_INNER), row(GATE_W),
                  pl.BlockSpec((bb, CONV_W - 1, SSM_CONV_CH), lambda i: (i, 0, 0)), st,
                  _const_spec(conv_w.shape), _const_spec((1, SSM_CONV_CH)),
                  _const_spec(ssm_par.shape), _const_spec((1, SSM_INNER))],
        out_specs=[row(SSM_INNER), st],
        out_shape=[jax.ShapeDtypeStruct((b, SSM_INNER), F32), jax.ShapeDtypeStruct(h0.shape, F32)],
        scratch_shapes=[pltpu.VMEM((bb, SSM_INNER), F32)],
        compiler_params=_cparams("parallel"),
        name="ssd_decode",
    )(xbc, sz, small, conv0, h0, conv_w, conv_b.reshape(1, -1), ssm_par, ssm_norm.reshape(1, -1))


def _res_ffn_body(n_a, n_ff, x_ref, *refs):
    a_refs, w_refs = refs[:n_a], refs[n_a:2 * n_a]
    g_ref, wgu_ref, wd_ref, o_ref, xn_ref = refs[2 * n_a:]
    ff_tile = wd_ref.shape[1]
    d_ff = n_ff * ff_tile
    x1 = x_ref[...]
    for a_ref, w_ref in zip(a_refs, w_refs):
        x1 = x1 + jnp.dot(a_ref[...].astype(BF16), w_ref[...], preferred_element_type=F32)
    xn_ref[...] = _rms(x1, g_ref[...]).astype(BF16)
    o_ref[...] = x1

    def step(c, carry):
        xn = xn_ref[...]
        lo = pl.multiple_of(c * ff_tile, ff_tile)
        gate = jnp.dot(xn, wgu_ref[:, pl.ds(lo, ff_tile)], preferred_element_type=F32)
        up = jnp.dot(xn, wgu_ref[:, pl.ds(d_ff + lo, ff_tile)], preferred_element_type=F32)
        hid = (_silu(gate) * up).astype(BF16)
        o_ref[...] += jnp.dot(hid, wd_ref[c], preferred_element_type=F32)
        return carry

    lax.fori_loop(0, n_ff, step, 0)


def _res_ffn(x, acts, projs, gain, layer, wgu, wd, tm):
    rows, d = x.shape
    n_a, n_ff = len(acts), wd.shape[1]

    def layer_spec(w):
        tail = (0,) * (w.ndim - 1)
        return pl.BlockSpec((None,) + w.shape[1:], lambda i: (layer,) + tail,
                            pipeline_mode=pl.Buffered(1))

    return pl.pallas_call(
        functools.partial(_res_ffn_body, n_a, n_ff),
        grid=(rows // tm,),
        in_specs=[pl.BlockSpec((tm, d), lambda i: (i, 0))]
        + [pl.BlockSpec((tm, a.shape[1]), lambda i: (i, 0)) for a in acts]
        + [_const_spec(p.shape) for p in projs]
        + [_const_spec((1, d)), layer_spec(wgu), layer_spec(wd)],
        out_specs=pl.BlockSpec((tm, d), lambda i: (i, 0)),
        out_shape=jax.ShapeDtypeStruct((rows, d), F32),
        scratch_shapes=[pltpu.VMEM((tm, d), BF16)],
        compiler_params=_cparams("parallel"),
        name="res_ffn",
    )(x, *acts, *projs, gain.reshape(1, d), wgu, wd)


def _mla_proj_body(tm, q_scale, x_ref, cos_ref, sin_ref, g_ref, wcq_ref, wckv_ref, wkr_ref,
                   qan_ref, kvn_ref, wuq_ref, wuk_ref, wuv_ref, vone_ref, gq_ref, gk_ref, gkr_ref,
                   segq_ref, segk_ref, q_ref, k_ref, v_ref, rows_ref):
    xn = _rms(x_ref[...], g_ref[...]).astype(BF16)
    lane = _iota((tm, LANES), 1)
    cos_t, sin_t = cos_ref[...], sin_ref[...]

    kr = jnp.dot(xn, wkr_ref[...], preferred_element_type=F32)
    ssq_kr = jnp.sum(jnp.where(lane < QK_ROPE, kr * kr, 0.0), axis=-1, keepdims=True)
    krg = kr * gkr_ref[...]
    kr_rot = krg * cos_t + pltpu.roll(krg, LANES - QK_ROPE, 1) * sin_t

    c = _rms(jnp.dot(xn, wckv_ref[...], preferred_element_type=F32), kvn_ref[...])
    cb = c.astype(BF16)
    kx = jnp.dot(cb, wuk_ref[...], preferred_element_type=F32)
    ssq_k = jnp.dot((kx * kx).astype(BF16), segk_ref[...],
                    preferred_element_type=F32)
    inv_r = lax.rsqrt((ssq_k + ssq_kr) / QK_HEAD + EPS)
    tail = jnp.where(lane < QK_ROPE, kr_rot, inv_r)
    rows_ref[:, 0:KV_LORA] = c
    rows_ref[:, KV_LORA:MLA_ROW] = tail[:, 0:MLA_ROW - KV_LORA]
    kr_put = jnp.where((lane >= QK_NOPE) & (lane < QK_HEAD), pltpu.roll(kr_rot, QK_NOPE, 1), 0.0)
    gk = gk_ref[...]
    for h in range(MLA_HEADS):
        hs = slice(h * LANES, (h + 1) * LANES)
        inv_h = inv_r[:, QK_ROPE + h:QK_ROPE + h + 1]
        k_ref[:, hs] = ((kx[:, hs] * gk + kr_put) * inv_h).astype(k_ref.dtype)
    v_ref[...] = (jnp.dot(cb, wuv_ref[...], preferred_element_type=F32)
                  + vone_ref[...]).astype(v_ref.dtype)

    cq = _rms(jnp.dot(xn, wcq_ref[...], preferred_element_type=F32), qan_ref[...]).astype(BF16)
    qx = jnp.dot(cq, wuq_ref[...], preferred_element_type=F32)
    ssq_q = jnp.dot((qx * qx).astype(BF16), segq_ref[...], preferred_element_type=F32)
    inv_q = lax.rsqrt(ssq_q / QK_HEAD + EPS)
    gq = gq_ref[...]
    scale = q_scale
    in_rope = (lane >= QK_NOPE) & (lane < QK_HEAD)
    keep = jnp.where(lane < QK_NOPE, scale, jnp.where(in_rope, cos_t * scale, 0.0))
    swap = jnp.where(in_rope, sin_t * scale, 0.0)
    for h in range(MLA_HEADS):
        hs = slice(h * LANES, (h + 1) * LANES)
        t = qx[:, hs] * inv_q[:, h:h + 1] * gq
        q_ref[:, hs] = (t * keep + pltpu.roll(t, LANES - QK_ROPE, 1) * swap).astype(q_ref.dtype)


def _mla_proj(x, seq, pos0, w, tm, q_dtype, q_scale):
    rows, d = x.shape
    half = QK_ROPE // 2
    inv_freq = ROPE_THETA ** (-(jnp.arange(LANES) % half).astype(F32) / half)
    ang = (pos0 + jnp.arange(seq, dtype=F32))[:, None] * inv_freq[None, :]
    tr = tm if seq > 1 else 1
    nblk = seq // tr
    rope_spec = pl.BlockSpec((tr, LANES), lambda i: (i % nblk, 0))
    names = ("norm", "wcq", "wckv", "wkr", "qan", "kvn", "wuq", "wuk", "wuv", "vone", "gq", "gk",
             "gkr", "segq", "segk")
    consts = [w[n] for n in names]
    out_w = (HEAD_W, HEAD_W, HEAD_W, MLA_ROW)
    out_dt = (q_dtype, BF16, BF16, F32)
    return pl.pallas_call(
        functools.partial(_mla_proj_body, tm, q_scale),
        grid=(rows // tm,),
        in_specs=[pl.BlockSpec((tm, d), lambda i: (i, 0)), rope_spec, rope_spec]
        + [_const_spec(c.shape) for c in consts],
        out_specs=[pl.BlockSpec((tm, ow), lambda i: (i, 0)) for ow in out_w],
        out_shape=[jax.ShapeDtypeStruct((rows, ow), dt) for ow, dt in zip(out_w, out_dt)],
        compiler_params=_cparams("parallel"),
        name="mla_proj",
    )(x, jnp.cos(ang), jnp.sin(ang), *consts)


def _flash_body(tq, nh, q_ref, k_ref, v_ref, o_ref):
    qi = pl.program_id(2)
    half = tq // 2
    heads = [slice(j * LANES, (j + 1) * LANES) for j in range(nh)]
    q = [q_ref[0, :, hs] for hs in heads]

    def attend(carry, qs, kv_rows, mask):
        s = [lax.dot_general(qs[j], k_ref[0, kv_rows, hs], (((1,), (1,)), ((), ())),
                             preferred_element_type=F32) for j, hs in enumerate(heads)]
        if mask is not None:
            s = [jnp.where(mask, sj, -jnp.inf) for sj in s]
        m_new = [jnp.maximum(carry[j][0], jnp.max(s[j], axis=-1, keepdims=True)) for j in range(nh)]
        p = [jnp.exp2((s[j] - m_new[j]).astype(BF16)) for j in range(nh)]
        pv = [jnp.dot(p[j], v_ref[0, kv_rows, hs], preferred_element_type=F32)
              for j, hs in enumerate(heads)]
        return tuple((m_new[j], jnp.exp2(carry[j][0] - m_new[j]) * carry[j][1] + pv[j])
                     for j in range(nh))

    init = tuple((jnp.full((tq, 1), -jnp.inf, F32), jnp.zeros((tq, LANES), F32)) for _ in range(nh))
    carry = lax.fori_loop(
        0, 2 * qi, lambda kb, cr: attend(cr, q, pl.ds(pl.multiple_of(kb * half, half), half), None),
        init)
    base = pl.multiple_of(qi * tq, tq)
    carry = attend(carry, q, pl.ds(base, half),
                   _iota((tq, half), 1) <= _iota((tq, half), 0))
    lower = attend(tuple((m[half:], acc[half:]) for m, acc in carry), [qj[half:] for qj in q],
                   pl.ds(base + half, half), _iota((half, half), 1) <= _iota((half, half), 0))
    low = _iota((half, LANES), 1) < V_HEAD
    for part, rows in ((tuple((m[:half], acc[:half]) for m, acc in carry), slice(0, half)),
                       (lower, slice(half, tq))):
        outs = [acc / acc[:, V_HEAD:V_HEAD + 1] for _, acc in part]
        for j in range(nh // 2):
            o_ref[0, rows, j * LANES:(j + 1) * LANES] = jnp.where(
                low, outs[2 * j], pltpu.roll(outs[2 * j + 1], V_HEAD, 1)).astype(o_ref.dtype)


def _flash_attention(q, k, v, tq, nh):
    b, t, _ = q.shape
    kv_spec = pl.BlockSpec((1, t, nh * LANES), lambda i, j, n: (i, 0, j))
    return pl.pallas_call(
        functools.partial(_flash_body, tq, nh),
        grid=(b, MLA_HEADS // nh, t // tq),
        in_specs=[pl.BlockSpec((1, tq, nh * LANES), lambda i, j, n: (i, n, j)), kv_spec, kv_spec],
        out_specs=pl.BlockSpec((1, tq, nh * V_HEAD), lambda i, j, n: (i, n, j)),
        out_shape=jax.ShapeDtypeStruct((b, t, MLA_HEADS * V_HEAD), BF16),
        compiler_params=_cparams("parallel", "parallel", "arbitrary"),
        name="flash_attention",
    )(q, k, v)


def _absorb_body(q_ref, g_ref, w_ref, o_ref):
    t = (q_ref[...] * g_ref[...]).astype(BF16)
    o_ref[...] = jnp.dot(t, w_ref[...], preferred_element_type=F32).astype(o_ref.dtype)


def _absorb(q, gain, w_abs):
    b = q.shape[0]
    return pl.pallas_call(
        _absorb_body,
        grid=(MLA_HEADS,),
        in_specs=[pl.BlockSpec((b, LANES), lambda h: (0, h)), _const_spec((1, LANES)),
                  pl.BlockSpec((None, LANES, Q_EXT), lambda h: (h, 0, 0))],
        out_specs=pl.BlockSpec((None, b, Q_EXT), lambda h: (h, 0, 0)),
        out_shape=jax.ShapeDtypeStruct((MLA_HEADS, b, Q_EXT), BF16),
        compiler_params=_cparams("parallel"),
        name="mla_absorb",
    )(q, gain, w_abs)


def _paged_attn_body(n_pages, pt_ref, q_ref, rn_ref, cache_ref, o_ref, buf_ref, sem_ref, s_ref,
                     pb_ref):
    i = pl.program_id(0)
    slot = lax.rem(i, 2)

    def page_copy(seq, p, sl):
        return pltpu.make_async_copy(cache_ref.at[pt_ref[seq, p]], buf_ref.at[sl, p],
                                     sem_ref.at[sl])

    @pl.when(i == 0)
    def _():
        for p in range(n_pages):
            page_copy(0, p, 0).start()

    @pl.when(i + 1 < pl.num_programs(0))
    def _():
        for p in range(n_pages):
            page_copy(i + 1, p, 1 - slot).start()

    for p in range(n_pages):
        page_copy(i, p, slot).wait()

    q = q_ref[0]
    inv_lo = MLA_ROW - MLA_HEADS
    for p in range(n_pages):
        page = buf_ref[slot, p]
        cols = slice(p * PAGE_SIZE, (p + 1) * PAGE_SIZE)
        pb_ref[:, cols] = page.astype(BF16)
        s_ref[:, cols] = page[inv_lo:MLA_ROW]
    s_all = jnp.dot(q[:, :MLA_ROW], pb_ref[...], preferred_element_type=F32) * s_ref[...]
    rn = rn_ref[0]
    qf = q[:, :MLA_ROW].astype(F32)
    pick = _iota((MLA_HEADS, MLA_ROW), 1) == _iota((MLA_HEADS, MLA_ROW), 0) + inv_lo
    inv_new = jnp.sum(jnp.where(pick, rn, 0.0), axis=-1, keepdims=True)
    s_new = jnp.sum(qf * rn, axis=-1, keepdims=True) * inv_new
    m = jnp.maximum(jnp.max(s_all, axis=-1, keepdims=True), s_new)
    e_new = jnp.exp(s_new - m)
    e = jnp.exp(s_all - m)
    den = e_new + jnp.sum(e, axis=-1, keepdims=True)
    ctx = e_new * rn[:, 0:KV_LORA] + lax.dot_general(
        e.astype(BF16), pb_ref[0:KV_LORA, :], (((1,), (1,)), ((), ())), preferred_element_type=F32)
    o_ref[0] = ctx / den


def _paged_attention(q_ext, rows_new, cache_t, page_table):
    b, n_pages = page_table.shape
    grid_spec = pltpu.PrefetchScalarGridSpec(
        num_scalar_prefetch=1,
        grid=(b,),
        in_specs=[pl.BlockSpec((1, MLA_HEADS, Q_EXT), lambda i, pt: (i, 0, 0)),
                  pl.BlockSpec((1, 1, MLA_ROW), lambda i, pt: (i, 0, 0)),
                  pl.BlockSpec(memory_space=pl.ANY)],
        out_specs=pl.BlockSpec((1, MLA_HEADS, KV_LORA), lambda i, pt: (i, 0, 0)),
        scratch_shapes=[pltpu.VMEM((2, n_pages, MLA_ROW, PAGE_SIZE), F32),
                        pltpu.SemaphoreType.DMA((2,)),
                        pltpu.VMEM((MLA_HEADS, n_pages * PAGE_SIZE), F32),
                        pltpu.VMEM((MLA_ROW, n_pages * PAGE_SIZE), BF16)],
    )
    return pl.pallas_call(
        functools.partial(_paged_attn_body, n_pages),
        grid_spec=grid_spec,
        out_shape=jax.ShapeDtypeStruct((b, MLA_HEADS, KV_LORA), F32),
        compiler_params=_cparams("arbitrary"),
        name="paged_attention",
    )(page_table, q_ext, rows_new, cache_t)


def _ctx_out_body(c_ref, w_ref, o_ref):
    o_ref[...] = (jnp.dot(c_ref[0].astype(BF16), w_ref[0], preferred_element_type=F32)
                  + jnp.dot(c_ref[1].astype(BF16), w_ref[1], preferred_element_type=F32)
                  ).astype(o_ref.dtype)


def _ctx_out(ctx_t, w_pairs):
    b = ctx_t.shape[1]
    return pl.pallas_call(
        _ctx_out_body,
        grid=(MLA_HEADS // 2,),
        in_specs=[pl.BlockSpec((2, b, KV_LORA), lambda j: (j, 0, 0)),
                  pl.BlockSpec((None, 2, KV_LORA, LANES), lambda j: (j, 0, 0, 0))],
        out_specs=pl.BlockSpec((b, LANES), lambda j: (0, j)),
        out_shape=jax.ShapeDtypeStruct((b, MLA_HEADS * V_HEAD), BF16),
        compiler_params=_cparams("parallel"),
        name="mla_ctx_out",
    )(ctx_t, w_pairs)


def _lane_pad(x, lo, width):
    pad = [(0, 0)] * (x.ndim - 1) + [(lo, width - lo - x.shape[-1])]
    return jnp.pad(x.astype(F32), pad)


def _gate_tile(rows):
    tile = jnp.stack([_lane_pad(vec, off, GATE_W) for off, vec in rows])
    return jnp.pad(tile, ((0, 8 - len(rows)), (0, 0)))


def _prep_hybrid(j, w_in_a, conv_gdn_w, gdn_A_log, gdn_dt_bias, conv_ssm_w, ssm_A_log,
                 ssm_dt_bias, ssm_D):
    widths = (GDN_CONV_CH, GDN_VW, GDN_HEADS, GDN_HEADS, SSM_INNER, SSM_CONV_CH, SSM_HEADS)
    offs = np.concatenate([[0], np.cumsum(widths)])
    w = w_in_a[j].astype(BF16)
    part = lambda i: w[:, offs[i]:offs[i + 1]]
    assert (GATE_A, GATE_B, GATE_DT) == (0, GDN_HEADS, 2 * GDN_HEADS)
    small = _lane_pad(jnp.concatenate([part(2), part(3), part(6)], axis=1), 0, GATE_W)
    return dict(
        w_in=[part(0), part(1), part(4), part(5), small.astype(BF16)],
        conv_gdn=conv_gdn_w[j], conv_ssm=conv_ssm_w[j],
        gate_gdn=_gate_tile([(GATE_A, gdn_A_log[j]), (GATE_A, gdn_dt_bias[j])]),
        gate_ssm=_gate_tile([(GATE_DT, ssm_A_log[j]), (GATE_DT, ssm_dt_bias[j]),
                             (GATE_DT, ssm_D[j])]),
    )


def _prep_mla(j, norm_mix_c, w_in_c, q_a_norm, kv_a_norm, w_uq, w_uk, w_uv, q_norm, k_norm):
    half = QK_ROPE // 2
    w_in = w_in_c[j]

    def swapped(cols):
        return jnp.concatenate([-cols[..., half:], cols[..., :half]], axis=-1)

    def swapped_gain(g):
        return jnp.concatenate([g[half:], g[:half]])

    w_kr = w_in[:, Q_LORA + KV_LORA:]
    wkr = _lane_pad(jnp.concatenate([w_kr, swapped(w_kr)], axis=1), 0, LANES)
    qn, kn = q_norm[j], k_norm[j]
    gkr = _lane_pad(jnp.concatenate([kn[QK_NOPE:], swapped_gain(kn[QK_NOPE:])]), 0, LANES)
    gk = _lane_pad(kn[:QK_NOPE], 0, LANES)
    gq = jnp.concatenate([qn, swapped_gain(qn[QK_NOPE:])])

    uq = w_uq[j].reshape(Q_LORA, MLA_HEADS, QK_HEAD)
    wuq = jnp.concatenate([uq, swapped(uq[..., QK_NOPE:])], axis=-1).reshape(Q_LORA, HEAD_W)
    wuk = _lane_pad(w_uk[j], 0, LANES).reshape(KV_LORA, HEAD_W)
    wuv = _lane_pad(w_uv[j], 0, LANES).reshape(KV_LORA, HEAD_W)

    col = np.arange(HEAD_W)
    head, within = col // LANES, col % LANES
    lane = np.arange(LANES)
    inv_lane = QK_ROPE + head
    segq = ((within < QK_HEAD)[:, None] & (lane[None, :] == head[:, None]))
    segk = ((within < QK_NOPE)[:, None] & (lane[None, :] == inv_lane[:, None]))
    vone = (within == V_HEAD).astype(np.float32).reshape(1, HEAD_W)
    sel = lambda m: jnp.asarray(m, BF16)

    rope_pass = np.zeros((LANES, Q_EXT), np.float32)
    rope_pass[QK_NOPE:QK_HEAD, KV_LORA:KV_LORA + QK_ROPE] = np.eye(QK_ROPE)
    w_abs = jnp.pad(jnp.transpose(w_uk[j], (1, 2, 0)),
                    ((0, 0), (0, LANES - QK_NOPE), (0, Q_EXT - KV_LORA))) + rope_pass
    g_abs = jnp.concatenate([kn[:QK_NOPE], jnp.ones((QK_ROPE,), F32),
                             jnp.zeros((LANES - QK_HEAD,), F32)])
    uv = jnp.transpose(w_uv[j], (1, 0, 2)).reshape(MLA_HEADS // 2, 2, KV_LORA, V_HEAD)
    w_pairs = jnp.stack([_lane_pad(uv[:, 0], 0, LANES), _lane_pad(uv[:, 1], V_HEAD, LANES)], axis=1)
    row = lambda v: v.reshape(1, -1)
    return dict(
        norm=row(norm_mix_c[j]), wcq=w_in[:, :Q_LORA].astype(BF16),
        wckv=w_in[:, Q_LORA:Q_LORA + KV_LORA].astype(BF16), wkr=wkr.astype(BF16),
        qan=row(q_a_norm[j]), kvn=row(kv_a_norm[j]), wuq=wuq.astype(BF16), wuk=wuk.astype(BF16),
        wuv=wuv.astype(BF16), vone=jnp.asarray(vone), gq=row(gq), gk=row(gk), gkr=row(gkr),
        segq=sel(segq), segk=sel(segk),
        w_abs=w_abs.astype(BF16), g_abs=row(g_abs), w_pairs=w_pairs.astype(BF16),
    )


def _prep_ffn(w_gate_up, w_down, ff_tile):
    layers, d_ff, d = w_down.shape
    return (w_gate_up.astype(BF16),
            w_down.reshape(layers, d_ff // ff_tile, ff_tile, d).astype(BF16))


FF_TILE = 256
ROW_TILE = 512
FFN_ROW_TILE = 1024
MLA_ROW_TILE = 256
ATTN_TILE = 1024
ATTN_HEADS = 4
GDN_CHUNKS = 4
SSD_CHUNKS = 4


def _row_tile(rows, want):
    return want if rows % want == 0 else rows


def _trunk(x, pos0, states, cache_mla, page_table, hyb, mla, ffn, norm_mix_a, gdn_norm, conv_ssm_b,
           ssm_norm, w_out_a, w_out_c, norm_ffn):
    b, t, d = x.shape
    rows = b * t
    tm = _row_tile(rows, ROW_TILE)
    tm_f = _row_tile(rows, FFN_ROW_TILE)
    xf = x.reshape(rows, d)
    decode = states is not None

    qkv, gz, sz, xbc, small = _norm_proj(xf, norm_mix_a[0], hyb["w_in"], tm)
    if decode:
        s0, gconv0, h0, sconv0 = states
        r3 = lambda a: a.reshape(b, 1, a.shape[-1])
        n_pair = SSM_HEADS // 2
        o_gdn, s_new = _gdn_decode(qkv, gz, small, gconv0[0], s0[0], hyb["conv_gdn"],
                                   hyb["gate_gdn"], gdn_norm[0])
        o_ssd, h_new = _ssd_decode(xbc, sz, small, sconv0[0],
                                   h0[0].reshape(b, n_pair, 2 * SSM_P, SSM_N), hyb["conv_ssm"],
                                   conv_ssm_b[0], hyb["gate_ssm"], ssm_norm[0])
        gconv = jnp.concatenate([gconv0[0][:, 1:], r3(qkv)], axis=1)
        sconv = jnp.concatenate([sconv0[0][:, 1:], r3(xbc)], axis=1)
    else:
        r3 = lambda a: a.reshape(b, t, a.shape[-1])
        o_gdn, s_new = _gdn_prompt(r3(qkv), r3(gz), r3(small), hyb["conv_gdn"], hyb["gate_gdn"],
                                   gdn_norm[0], GDN_CHUNKS)
        o_ssd, h_new = _ssd_prompt(r3(xbc), r3(sz), r3(small), hyb["conv_ssm"], conv_ssm_b[0],
                                   hyb["gate_ssm"], ssm_norm[0], SSD_CHUNKS)
        gconv = r3(qkv)[:, t - (CONV_W - 1):]
        sconv = r3(xbc)[:, t - (CONV_W - 1):]
    h_new = h_new.reshape(b, SSM_HEADS, SSM_P, SSM_N)
    wo = w_out_a[0].astype(BF16)
    x1 = _res_ffn(xf, [o_gdn.reshape(rows, GDN_VW), o_ssd.reshape(rows, SSM_INNER)],
                  [wo[:GDN_VW], wo[GDN_VW:]], norm_ffn[0], 0, *ffn, tm_f)

    tm_c = _row_tile(rows, MLA_ROW_TILE)
    q_scale = QK_HEAD ** -0.5 * (1.0 if decode else math.log2(math.e))
    q, k, v, mla_rows = _mla_proj(x1, t, pos0, mla, tm_c, F32 if decode else BF16, q_scale)
    if decode:
        q_ext = jnp.transpose(_absorb(q, mla["g_abs"], mla["w_abs"]), (1, 0, 2))
        ctx = _paged_attention(q_ext, mla_rows.reshape(b, 1, MLA_ROW),
                               jnp.swapaxes(cache_mla[0], 1, 2), page_table)
        attn = _ctx_out(jnp.transpose(ctx, (1, 0, 2)), mla["w_pairs"])
    else:
        r3 = lambda a: a.reshape(b, t, a.shape[-1])
        attn = _flash_attention(r3(q), r3(k), r3(v), _row_tile(t, ATTN_TILE),
                                ATTN_HEADS).reshape(rows, -1)
    x2 = _res_ffn(x1, [attn], [w_out_c[0].astype(BF16)], norm_ffn[1], 1, *ffn, tm_f)
    return (x2.reshape(b, t, d), s_new[None], gconv[None], h_new[None], sconv[None],
            mla_rows.reshape(1, b, t, MLA_ROW))


def kernel(x_prompt, x_sample, state_gdn, state_gdn_conv, state_ssm, state_ssm_conv, cache_mla, page_table, norm_mix_a, w_in_a, conv_gdn_w, gdn_A_log, gdn_dt_bias, gdn_norm, conv_ssm_w, conv_ssm_b, ssm_A_log, ssm_dt_bias, ssm_D, ssm_norm, w_out_a, norm_mix_c, w_in_c, q_a_norm, kv_a_norm, w_uq, w_uk, w_uv, q_norm, k_norm, w_out_c, norm_ffn, w_gate_up, w_down):
    assert w_in_a.shape[0] == 1 and w_in_c.shape[0] == 1 and norm_ffn.shape[0] == 2
    hyb = _prep_hybrid(0, w_in_a, conv_gdn_w, gdn_A_log, gdn_dt_bias, conv_ssm_w, ssm_A_log,
                       ssm_dt_bias, ssm_D)
    mla = _prep_mla(0, norm_mix_c, w_in_c, q_a_norm, kv_a_norm, w_uq, w_uk, w_uv, q_norm, k_norm)
    ffn = _prep_ffn(w_gate_up, w_down, FF_TILE)
    shared = (hyb, mla, ffn, norm_mix_a, gdn_norm, conv_ssm_b, ssm_norm, w_out_a, w_out_c, norm_ffn)
    prompt = _trunk(x_prompt, 0, None, None, None, *shared)
    past_len = page_table.shape[1] * PAGE_SIZE
    sample = _trunk(x_sample, past_len, (state_gdn, state_gdn_conv, state_ssm, state_ssm_conv),
                    cache_mla, page_table, *shared)
    return (prompt[0], sample[0]) + prompt[1:] + sample[1:]
```

```python
import functools
import math

import jax
import jax.numpy as jnp
import numpy as np
from jax import lax
from jax.experimental import pallas as pl
from jax.experimental.pallas import tpu as pltpu

F32 = jnp.float32
BF16 = jnp.bfloat16

EPS = 1e-6
CONV_W = 4
CHUNK = 64
PAGE_SIZE = 128
GDN_HEADS = 4
GDN_DK = 128
GDN_DV = 128
GDN_QK = GDN_HEADS * GDN_DK
GDN_VW = GDN_HEADS * GDN_DV
GDN_CONV_CH = 2 * GDN_QK + GDN_VW
SSM_HEADS = 8
SSM_P = 64
SSM_N = 128
SSM_GROUPS = 2
SSM_INNER = SSM_HEADS * SSM_P
SSM_BC = SSM_GROUPS * SSM_N
SSM_CONV_CH = SSM_INNER + 2 * SSM_BC
MLA_HEADS = 16
Q_LORA = 512
KV_LORA = 256
QK_NOPE = 64
QK_ROPE = 32
QK_HEAD = QK_NOPE + QK_ROPE
V_HEAD = 64
MLA_ROW = KV_LORA + QK_ROPE + MLA_HEADS
ROPE_THETA = 10000.0

LANES = 128
HEAD_W = MLA_HEADS * LANES
GATE_W = LANES
GATE_A, GATE_B, GATE_DT = 0, GDN_HEADS, 2 * GDN_HEADS
Q_EXT = 3 * LANES
VMEM_LIMIT = 48 * 1024 * 1024

_HI = lax.Precision.HIGHEST


def _cparams(*sem):
    return pltpu.CompilerParams(dimension_semantics=sem, vmem_limit_bytes=VMEM_LIMIT)


def _const_spec(shape):
    zeros = (0,) * len(shape)
    return pl.BlockSpec(shape, lambda *_: zeros, pipeline_mode=pl.Buffered(1))


def _mm(a, b):
    return jnp.dot(a.astype(BF16), b.astype(BF16), preferred_element_type=F32)


def _mm_nt(a, b):
    return lax.dot_general(a.astype(BF16), b.astype(BF16), (((1,), (1,)), ((), ())),
                           preferred_element_type=F32)


def _mm_tn(a, b):
    return lax.dot_general(a.astype(BF16), b.astype(BF16), (((0,), (0,)), ((), ())),
                           preferred_element_type=F32)


def _sigmoid(x):
    return 1.0 / (1.0 + jnp.exp(-x))


def _silu(x):
    return x * _sigmoid(x)


def _softplus(x):
    return jnp.maximum(x, 0.0) + jnp.log(1.0 + jnp.exp(-jnp.abs(x)))


def _rms(x, gain):
    return x * lax.rsqrt(jnp.mean(x * x, axis=-1, keepdims=True) + EPS) * gain


def _l2n(x):
    return x * lax.rsqrt(jnp.sum(x * x, axis=-1, keepdims=True) + EPS)


def _iota(shape, dim):
    return lax.broadcasted_iota(jnp.int32, shape, dim)


def _chunk_cumsum(g, tb):
    row, col = _iota((tb, tb), 0), _iota((tb, tb), 1)
    shift = int(math.log2(CHUNK))
    same_chunk = lax.shift_right_logical(row, shift) == lax.shift_right_logical(col, shift)
    tril = jnp.where((col <= row) & same_chunk, 1.0, 0.0).astype(F32)
    return jnp.dot(tril, g, precision=_HI, preferred_element_type=F32)


def _rows_of(cols, lane0):
    sel = jnp.where(_iota((8, LANES), 1) == _iota((8, LANES), 0) + lane0, 1.0, 0.0).astype(F32)
    return lax.dot_general(sel, cols, (((1,), (1,)), ((), ())), precision=_HI,
                           preferred_element_type=F32)


def _causal_conv_block(x_ref, xp_ref, cw, tb, first):
    @pl.when(first)
    def _():
        xp_ref[0:8, :] = jnp.zeros((8, xp_ref.shape[1]), F32)

    xp_ref[8:8 + tb, :] = x_ref[...]
    off = 8 - (CONV_W - 1)
    y = cw[0:1] * xp_ref[off:off + tb, :]
    for i in range(1, CONV_W):
        y = y + cw[i:i + 1] * xp_ref[off + i:off + i + tb, :]
    xp_ref[0:8, :] = xp_ref[tb:tb + 8, :]
    return y


def _norm_proj_body(n_out, x_ref, g_ref, *refs):
    xn = _rms(x_ref[...], g_ref[...]).astype(BF16)
    for w_ref, o_ref in zip(refs[:n_out], refs[n_out:]):
        o_ref[...] = jnp.dot(xn, w_ref[...], preferred_element_type=F32)


def _norm_proj(x, gain, weights, tm):
    rows, d = x.shape
    n = len(weights)
    return pl.pallas_call(
        functools.partial(_norm_proj_body, n),
        grid=(rows // tm,),
        in_specs=[pl.BlockSpec((tm, d), lambda i: (i, 0)), _const_spec((1, d))]
        + [_const_spec(w.shape) for w in weights],
        out_specs=[pl.BlockSpec((tm, w.shape[1]), lambda i: (i, 0)) for w in weights],
        out_shape=[jax.ShapeDtypeStruct((rows, w.shape[1]), F32) for w in weights],
        compiler_params=_cparams("parallel"),
        name="norm_proj",
    )(x, gain.reshape(1, d), *weights)


def _gdn_prompt_body(nc, qkv_ref, gz_ref, sm_ref, cw_ref, gp_ref, gn_ref, o_ref, s_ref, xp_ref):
    tb = nc * CHUNK
    first = pl.program_id(1) == 0

    @pl.when(first)
    def _():
        s_ref[...] = jnp.zeros(s_ref.shape, F32)

    y = _silu(_causal_conv_block(qkv_ref.at[0], xp_ref, cw_ref[...], tb, first))
    sm, gp = sm_ref[0], gp_ref[...]
    g_all = -jnp.exp(gp[0:1]) * _softplus(sm + gp[1:2])
    beta_all = _sigmoid(sm)
    cum_all = _chunk_cumsum(g_all, tb)
    cum_rows = _rows_of(cum_all, GATE_A)
    incl = _iota((CHUNK, CHUNK), 1) <= _iota((CHUNK, CHUNK), 0)
    strict = _iota((CHUNK, CHUNK), 1) < _iota((CHUNK, CHUNK), 0)
    gn = gn_ref[...]
    pairs = [(c, h) for c in range(nc) for h in range(GDN_HEADS)]
    qs, ks, vs = [], [], []
    for h in range(GDN_HEADS):
        lo = h * GDN_DK
        qs.append(_l2n(y[:, lo:lo + GDN_DK]) * GDN_DK ** -0.5)
        ks.append(_l2n(y[:, GDN_QK + lo:GDN_QK + lo + GDN_DK]))
        vs.append(y[:, 2 * GDN_QK + lo:2 * GDN_QK + lo + GDN_DV])
    pre = []
    for c, h in pairs:
        r = slice(c * CHUNK, (c + 1) * CHUNK)
        q, k, v = qs[h][r], ks[h][r], vs[h][r]
        cum_c = cum_all[r, GATE_A + h:GATE_A + h + 1]
        cum_r = cum_rows[h:h + 1, c * CHUNK:(c + 1) * CHUNK]
        beta = beta_all[r, GATE_B + h:GATE_B + h + 1]
        decay = jnp.where(incl, jnp.exp(jnp.minimum(cum_c - cum_r, 0.0)), 0.0)
        e_c = jnp.exp(cum_c)
        kb = k * beta
        last = cum_c[CHUNK - 1:CHUNK]
        pre.append(dict(q=q.astype(BF16), k=k.astype(BF16), kb=kb, decay=decay,
                        rhs=jnp.concatenate([v * beta, kb * e_c], axis=1).astype(BF16),
                        qe=q * e_c, kd=(k * jnp.exp(last - cum_c)).astype(BF16),
                        e_last=jnp.exp(last)))
    lows = [jnp.where(strict, _mm_nt(p["kb"], p["k"]) * p["decay"], 0.0) for p in pre]
    attns = [(_mm_nt(p["q"], p["k"]) * p["decay"]).astype(BF16) for p in pre]
    row, col = _iota((CHUNK, CHUNK), 0), _iota((CHUNK, CHUNK), 1)
    invs = None
    for level in range(int(math.log2(CHUNK))):
        brow = lax.shift_right_logical(row, level)
        joins = ((brow & 1) == 1) & (lax.shift_right_logical(col, level) == brow - 1)
        subs = [jnp.where(joins, low, 0.0) for low in lows]
        if invs is None:
            eye = jnp.where(row == col, 1.0, 0.0).astype(F32)
            invs = [eye - sub for sub in subs]
        else:
            invs = [inv - _mm(_mm(inv, sub), inv) for inv, sub in zip(invs, subs)]
    sols = [_mm(inv, p["rhs"]).astype(BF16) for inv, p in zip(invs, pre)]
    k_sol = [_mm_tn(p["kd"], sol) for p, sol in zip(pre, sols)]
    a_sol = [_mm(attn, sol) for attn, sol in zip(attns, sols)]
    q_eff = [(p["qe"] - a[:, GDN_DV:]).astype(BF16) for p, a in zip(pre, a_sol)]
    m_eff = [ks[:, GDN_DV:].astype(BF16) for ks in k_sol]
    states = [s_ref[0, h] for h in range(GDN_HEADS)]
    for c in range(nc):
        r = slice(c * CHUNK, (c + 1) * CHUNK)
        idx = [c * GDN_HEADS + h for h in range(GDN_HEADS)]
        s_bf = [st.astype(BF16) for st in states]
        outs = [_mm(q_eff[i], s_bf[h]) + a_sol[i][:, :GDN_DV] for h, i in enumerate(idx)]
        states = [states[h] * pre[i]["e_last"] - _mm(m_eff[i], s_bf[h]) + k_sol[i][:, :GDN_DV]
                  for h, i in enumerate(idx)]
        for h, o in enumerate(outs):
            lo = h * GDN_DV
            gate = _silu(gz_ref[0, r, lo:lo + GDN_DV])
            o_ref[0, r, lo:lo + GDN_DV] = (_rms(o, gn) * gate).astype(o_ref.dtype)
    for h in range(GDN_HEADS):
        s_ref[0, h] = states[h]


def _gdn_prompt(qkv, gz, small, conv_w, gate_par, gdn_norm, nc):
    b, t, _ = qkv.shape
    tb = nc * CHUNK
    blk = lambda w: pl.BlockSpec((1, tb, w), lambda i, j: (i, j, 0))
    return pl.pallas_call(
        functools.partial(_gdn_prompt_body, nc),
        grid=(b, t // tb),
        in_specs=[blk(GDN_CONV_CH), blk(GDN_VW), blk(GATE_W), _const_spec(conv_w.shape),
                  _const_spec(gate_par.shape), _const_spec((1, GDN_DV))],
        out_specs=[blk(GDN_VW),
                   pl.BlockSpec((1, GDN_HEADS, GDN_DK, GDN_DV), lambda i, j: (i, 0, 0, 0))],
        out_shape=[jax.ShapeDtypeStruct((b, t, GDN_VW), BF16),
                   jax.ShapeDtypeStruct((b, GDN_HEADS, GDN_DK, GDN_DV), F32)],
        scratch_shapes=[pltpu.VMEM((tb + 8, GDN_CONV_CH), F32)],
        compiler_params=_cparams("parallel", "arbitrary"),
        name="gdn_prompt",
    )(qkv, gz, small, conv_w, gate_par, gdn_norm.reshape(1, GDN_DV))


def _ssd_prompt_body(nc, xbc_ref, sz_ref, sm_ref, cw_ref, cb_ref, sp_ref, sn_ref, o_ref, h_ref,
                     xp_ref):
    tb = nc * CHUNK
    first = pl.program_id(1) == 0

    @pl.when(first)
    def _():
        h_ref[...] = jnp.zeros(h_ref.shape, F32)

    y = _silu(_causal_conv_block(xbc_ref.at[0], xp_ref, cw_ref[...], tb, first) + cb_ref[...])
    sm, sp = sm_ref[0], sp_ref[...]
    dt_all = _softplus(sm + sp[1:2])
    cum_all = _chunk_cumsum(dt_all * -jnp.exp(sp[0:1]), tb)
    cum_rows = _rows_of(cum_all, GATE_DT)
    incl = _iota((CHUNK, CHUNK), 1) <= _iota((CHUNK, CHUNK), 0)
    lane_lo = _iota((CHUNK, LANES), 1) < SSM_P
    row_lo = _iota((2 * SSM_P, SSM_N), 0) < SSM_P
    par_lo = _iota((1, LANES), 1) < SSM_P
    sn = sn_ref[...]
    n_pair = SSM_HEADS // 2
    pairs_per_group = n_pair // SSM_GROUPS
    gw = SSM_INNER // SSM_GROUPS
    b_bf, c_bf, scores = {}, {}, {}
    for c in range(nc):
        r = slice(c * CHUNK, (c + 1) * CHUNK)
        for grp in range(SSM_GROUPS):
            lo = SSM_INNER + grp * SSM_N
            b_bf[c, grp] = y[r, lo:lo + SSM_N].astype(BF16)
            c_bf[c, grp] = y[r, lo + SSM_BC:lo + SSM_BC + SSM_N].astype(BF16)
            scores[c, grp] = _mm_nt(c_bf[c, grp], b_bf[c, grp])
    pre = {}
    for c in range(nc):
        r = slice(c * CHUNK, (c + 1) * CHUNK)
        for j in range(n_pair):
            grp = j // pairs_per_group
            cums, segs, dts = [], [], []
            for head in (2 * j, 2 * j + 1):
                cum_c = cum_all[r, GATE_DT + head:GATE_DT + head + 1]
                cum_r = cum_rows[head:head + 1, c * CHUNK:(c + 1) * CHUNK]
                cums.append(cum_c)
                segs.append(jnp.where(incl, jnp.exp(jnp.minimum(cum_c - cum_r, 0.0)), 0.0))
                dts.append(dt_all[r, GATE_DT + head:GATE_DT + head + 1])
            x_pair = y[r, j * LANES:(j + 1) * LANES]
            xdt = x_pair * jnp.where(lane_lo, dts[0], dts[1])
            last = [cm[CHUNK - 1:CHUNK] for cm in cums]
            dec = jnp.where(lane_lo, jnp.exp(last[0] - cums[0]), jnp.exp(last[1] - cums[1]))
            d_pair = jnp.where(par_lo, sp[2:3, GATE_DT + 2 * j:GATE_DT + 2 * j + 1],
                               sp[2:3, GATE_DT + 2 * j + 1:GATE_DT + 2 * j + 2])
            pre[c, j] = dict(
                y=(_mm(scores[c, grp] * segs[0], jnp.where(lane_lo, xdt, 0.0))
                   + _mm(scores[c, grp] * segs[1], jnp.where(lane_lo, 0.0, xdt))
                   + d_pair * x_pair),
                e=jnp.where(lane_lo, jnp.exp(cums[0]), jnp.exp(cums[1])),
                xdec=(xdt * dec).astype(BF16),
                e_last=jnp.where(row_lo, jnp.exp(last[0]), jnp.exp(last[1])))
    states = [h_ref[0, j] for j in range(n_pair)]
    for c in range(nc):
        r = slice(c * CHUNK, (c + 1) * CHUNK)
        ys = [pre[c, j]["y"] + _mm_nt(c_bf[c, j // pairs_per_group], states[j]) * pre[c, j]["e"]
              for j in range(n_pair)]
        states = [states[j] * pre[c, j]["e_last"]
                  + _mm_tn(pre[c, j]["xdec"], b_bf[c, j // pairs_per_group]) for j in range(n_pair)]
        for grp in range(SSM_GROUPS):
            outs = [ys[j] * _silu(sz_ref[0, r, j * LANES:(j + 1) * LANES])
                    for j in range(grp * pairs_per_group, (grp + 1) * pairs_per_group)]
            inv = lax.rsqrt(sum(jnp.sum(o * o, axis=-1, keepdims=True) for o in outs) / gw + EPS)
            for jj, o in enumerate(outs):
                lo = grp * gw + jj * LANES
                o_ref[0, r, lo:lo + LANES] = (o * inv * sn[:, lo:lo + LANES]).astype(o_ref.dtype)
    for j in range(n_pair):
        h_ref[0, j] = states[j]


def _ssd_prompt(xbc, sz, small, conv_w, conv_b, ssm_par, ssm_norm, nc):
    b, t, _ = xbc.shape
    tb = nc * CHUNK
    blk = lambda w: pl.BlockSpec((1, tb, w), lambda i, j: (i, j, 0))
    n_pair = SSM_HEADS // 2
    return pl.pallas_call(
        functools.partial(_ssd_prompt_body, nc),
        grid=(b, t // tb),
        in_specs=[blk(SSM_CONV_CH), blk(SSM_INNER), blk(GATE_W), _const_spec(conv_w.shape),
                  _const_spec((1, SSM_CONV_CH)), _const_spec(ssm_par.shape),
                  _const_spec((1, SSM_INNER))],
        out_specs=[blk(SSM_INNER),
                   pl.BlockSpec((1, n_pair, 2 * SSM_P, SSM_N), lambda i, j: (i, 0, 0, 0))],
        out_shape=[jax.ShapeDtypeStruct((b, t, SSM_INNER), BF16),
                   jax.ShapeDtypeStruct((b, n_pair, 2 * SSM_P, SSM_N), F32)],
        scratch_shapes=[pltpu.VMEM((tb + 8, SSM_CONV_CH), F32)],
        compiler_params=_cparams("parallel", "arbitrary"),
        name="ssd_prompt",
    )(xbc, sz, small, conv_w, conv_b.reshape(1, -1), ssm_par, ssm_norm.reshape(1, -1))


def _col_bcast(row):
    return jnp.transpose(jnp.broadcast_to(row, (LANES, LANES)))


def _decode_conv(new_ref, cv_ref, cw):
    y = cw[CONV_W - 1:CONV_W] * new_ref[...]
    for i in range(CONV_W - 1):
        y = y + cw[i:i + 1] * cv_ref[:, i, :]
    return y


def _gdn_decode_body(bb, qkv_ref, gz_ref, sm_ref, cv_ref, s0_ref, cw_ref, gp_ref, gn_ref, o_ref,
                     s_ref, raw_ref):
    y = _silu(_decode_conv(qkv_ref, cv_ref, cw_ref[...]))
    sm, gp = sm_ref[...], gp_ref[...]
    a_all = jnp.exp(-jnp.exp(gp[0:1]) * _softplus(sm + gp[1:2]))
    beta_all = _sigmoid(sm)
    qs, ks, vs = [], [], []
    for h in range(GDN_HEADS):
        lo = h * GDN_DK
        qs.append(_l2n(y[:, lo:lo + GDN_DK]) * GDN_DK ** -0.5)
        ks.append(_l2n(y[:, GDN_QK + lo:GDN_QK + lo + GDN_DK]))
        vs.append(y[:, 2 * GDN_QK + lo:2 * GDN_QK + lo + GDN_DV])
    pairs = [(b, h) for b in range(bb) for h in range(GDN_HEADS)]
    k_col = [_col_bcast(ks[h][b:b + 1]) for b, h in pairs]
    q_col = [_col_bcast(qs[h][b:b + 1]) for b, h in pairs]
    decayed = [s0_ref[b, h] * a_all[b:b + 1, GATE_A + h:GATE_A + h + 1] for b, h in pairs]
    v_new = [beta_all[b:b + 1, GATE_B + h:GATE_B + h + 1]
             * (vs[h][b:b + 1] - jnp.sum(k_col[i] * decayed[i], axis=0, keepdims=True))
             for i, (b, h) in enumerate(pairs)]
    for i, (b, h) in enumerate(pairs):
        state = decayed[i] + k_col[i] * v_new[i]
        s_ref[b, h] = state
        raw_ref[b:b + 1, h * GDN_DV:(h + 1) * GDN_DV] = jnp.sum(q_col[i] * state, axis=0,
                                                                keepdims=True)
    gn = gn_ref[...]
    for h in range(GDN_HEADS):
        hs = slice(h * GDN_DV, (h + 1) * GDN_DV)
        o_ref[:, hs] = (_rms(raw_ref[:, hs], gn) * _silu(gz_ref[:, hs])).astype(o_ref.dtype)


DECODE_BLOCK = 8


def _gdn_decode(qkv, gz, small, conv0, s0, conv_w, gate_par, gdn_norm):
    b = qkv.shape[0]
    bb = DECODE_BLOCK
    row = lambda w: pl.BlockSpec((bb, w), lambda i: (i, 0))
    st = pl.BlockSpec((bb, GDN_HEADS, GDN_DK, GDN_DV), lambda i: (i, 0, 0, 0))
    return pl.pallas_call(
        functools.partial(_gdn_decode_body, bb),
        grid=(b // bb,),
        in_specs=[row(GDN_CONV_CH), row(GDN_VW), row(GATE_W),
                  pl.BlockSpec((bb, CONV_W - 1, GDN_CONV_CH), lambda i: (i, 0, 0)), st,
                  _const_spec(conv_w.shape), _const_spec(gate_par.shape), _const_spec((1, GDN_DV))],
        out_specs=[row(GDN_VW), st],
        out_shape=[jax.ShapeDtypeStruct((b, GDN_VW), F32), jax.ShapeDtypeStruct(s0.shape, F32)],
        scratch_shapes=[pltpu.VMEM((bb, GDN_VW), F32)],
        compiler_params=_cparams("parallel"),
        name="gdn_decode",
    )(qkv, gz, small, conv0, s0, conv_w, gate_par, gdn_norm.reshape(1, GDN_DV))


def _ssd_decode_body(bb, xbc_ref, sz_ref, sm_ref, cv_ref, h0_ref, cw_ref, cb_ref, sp_ref, sn_ref,
                     o_ref, h_ref, raw_ref):
    y = _silu(_decode_conv(xbc_ref, cv_ref, cw_ref[...]) + cb_ref[...])
    sm, sp = sm_ref[...], sp_ref[...]
    dt_all = _softplus(sm + sp[1:2])
    da_all = jnp.exp(dt_all * -jnp.exp(sp[0:1]))
    par_lo = _iota((1, LANES), 1) < SSM_P
    row_lo = _iota((2 * SSM_P, SSM_N), 0) < SSM_P
    n_pair = SSM_HEADS // 2
    pairs_per_group = n_pair // SSM_GROUPS
    pairs = [(b, j) for b in range(bb) for j in range(n_pair)]
    x_col, states = [], []
    for b, j in pairs:
        la, lb = GATE_DT + 2 * j, GATE_DT + 2 * j + 1
        row = slice(b, b + 1)
        xdt = y[row, j * LANES:(j + 1) * LANES] * jnp.where(par_lo, dt_all[row, la:la + 1],
                                                           dt_all[row, lb:lb + 1])
        x_col.append(_col_bcast(xdt))
    for i, (b, j) in enumerate(pairs):
        la, lb = GATE_DT + 2 * j, GATE_DT + 2 * j + 1
        row = slice(b, b + 1)
        grp = j // pairs_per_group
        b_g = y[row, SSM_INNER + grp * SSM_N:SSM_INNER + (grp + 1) * SSM_N]
        state = (h0_ref[b, j] * jnp.where(row_lo, da_all[row, la:la + 1], da_all[row, lb:lb + 1])
                 + x_col[i] * b_g)
        h_ref[b, j] = state
        states.append(state)
    for i, (b, j) in enumerate(pairs):
        grp = j // pairs_per_group
        lo = SSM_INNER + SSM_BC + grp * SSM_N
        y_col = jnp.sum(states[i] * y[b:b + 1, lo:lo + SSM_N], axis=1, keepdims=True)
        raw_ref[b:b + 1, j * LANES:(j + 1) * LANES] = jnp.transpose(
            jnp.broadcast_to(y_col, (LANES, LANES)))[0:1]
    sn = sn_ref[...]
    gw = SSM_INNER // SSM_GROUPS
    for grp in range(SSM_GROUPS):
        outs = []
        for jj in range(pairs_per_group):
            j = grp * pairs_per_group + jj
            la, lb = GATE_DT + 2 * j, GATE_DT + 2 * j + 1
            js = slice(j * LANES, (j + 1) * LANES)
            d_pair = jnp.where(par_lo, sp[2:3, la:la + 1], sp[2:3, lb:lb + 1])
            outs.append((raw_ref[:, js] + d_pair * y[:, js]) * _silu(sz_ref[:, js]))
        inv = lax.rsqrt(sum(jnp.sum(o * o, axis=-1, keepdims=True) for o in outs) / gw + EPS)
        for jj, o in enumerate(outs):
            lo = grp * gw + jj * LANES
            o_ref[:, lo:lo + LANES] = (o * inv * sn[:, lo:lo + LANES]).astype(o_ref.dtype)


def _ssd_decode(xbc, sz, small, conv0, h0, conv_w, conv_b, ssm_par, ssm_norm):
    b = xbc.shape[0]
    bb = DECODE_BLOCK
    n_pair = SSM_HEADS // 2
    row = lambda w: pl.BlockSpec((bb, w), lambda i: (i, 0))
    st = pl.BlockSpec((bb, n_pair, 2 * SSM_P, SSM_N), lambda i: (i, 0, 0, 0))
    return pl.pallas_call(
        functools.partial(_ssd_decode_body, bb),
        grid=(b // bb,),
        in_specs=[row(SSM_CONV_CH), row(SSM_INNER), row(GATE_W),
                  pl.BlockSpec((bb, CONV_W - 1, SSM_CONV_CH), lambda i: (i, 0, 0)), st,
                  _const_spec(conv_w.shape), _const_spec((1, SSM_CONV_CH)),
                  _const_spec(ssm_par.shape), _const_spec((1, SSM_INNER))],
        out_specs=[row(SSM_INNER), st],
        out_shape=[jax.ShapeDtypeStruct((b, SSM_INNER), F32), jax.ShapeDtypeStruct(h0.shape, F32)],
        scratch_shapes=[pltpu.VMEM((bb, SSM_INNER), F32)],
        compiler_params=_cparams("parallel"),
        name="ssd_decode",
    )(xbc, sz, small, conv0, h0, conv_w, conv_b.reshape(1, -1), ssm_par, ssm_norm.reshape(1, -1))


def _res_ffn_body(n_a, n_ff, x_ref, *refs):
    a_refs, w_refs = refs[:n_a], refs[n_a:2 * n_a]
    g_ref, wgu_ref, wd_ref, o_ref, xn_ref = refs[2 * n_a:]
    ff_tile = wd_ref.shape[1]
    d_ff = n_ff * ff_tile
    x1 = x_ref[...]
    for a_ref, w_ref in zip(a_refs, w_refs):
        x1 = x1 + jnp.dot(a_ref[...].astype(BF16), w_ref[...], preferred_element_type=F32)
    xn_ref[...] = _rms(x1, g_ref[...]).astype(BF16)
    o_ref[...] = x1

    def step(c, carry):
        xn = xn_ref[...]
        lo = pl.multiple_of(c * ff_tile, ff_tile)
        gate = jnp.dot(xn, wgu_ref[:, pl.ds(lo, ff_tile)], preferred_element_type=F32)
        up = jnp.dot(xn, wgu_ref[:, pl.ds(d_ff + lo, ff_tile)], preferred_element_type=F32)
        hid = (_silu(gate) * up).astype(BF16)
        o_ref[...] += jnp.dot(hid, wd_ref[c], preferred_element_type=F32)
        return carry

    lax.fori_loop(0, n_ff, step, 0)


def _res_ffn(x, acts, projs, gain, layer, wgu, wd, tm):
    rows, d = x.shape
    n_a, n_ff = len(acts), wd.shape[1]

    def layer_spec(w):
        tail = (0,) * (w.ndim - 1)
        return pl.BlockSpec((None,) + w.shape[1:], lambda i: (layer,) + tail,
                            pipeline_mode=pl.Buffered(1))

    return pl.pallas_call(
        functools.partial(_res_ffn_body, n_a, n_ff),
        grid=(rows // tm,),
        in_specs=[pl.BlockSpec((tm, d), lambda i: (i, 0))]
        + [pl.BlockSpec((tm, a.shape[1]), lambda i: (i, 0)) for a in acts]
        + [_const_spec(p.shape) for p in projs]
        + [_const_spec((1, d)), layer_spec(wgu), layer_spec(wd)],
        out_specs=pl.BlockSpec((tm, d), lambda i: (i, 0)),
        out_shape=jax.ShapeDtypeStruct((rows, d), F32),
        scratch_shapes=[pltpu.VMEM((tm, d), BF16)],
        compiler_params=_cparams("parallel"),
        name="res_ffn",
    )(x, *acts, *projs, gain.reshape(1, d), wgu, wd)


def _mla_proj_body(tm, q_scale, x_ref, cos_ref, sin_ref, g_ref, wcq_ref, wckv_ref, wkr_ref,
                   qan_ref, kvn_ref, wuq_ref, wuk_ref, wuv_ref, vone_ref, gq_ref, gk_ref, gkr_ref,
                   segq_ref, segk_ref, q_ref, k_ref, v_ref, rows_ref):
    xn = _rms(x_ref[...], g_ref[...]).astype(BF16)
    lane = _iota((tm, LANES), 1)
    cos_t, sin_t = cos_ref[...], sin_ref[...]

    kr = jnp.dot(xn, wkr_ref[...], preferred_element_type=F32)
    ssq_kr = jnp.sum(jnp.where(lane < QK_ROPE, kr * kr, 0.0), axis=-1, keepdims=True)
    krg = kr * gkr_ref[...]
    kr_rot = krg * cos_t + pltpu.roll(krg, LANES - QK_ROPE, 1) * sin_t

    c = _rms(jnp.dot(xn, wckv_ref[...], preferred_element_type=F32), kvn_ref[...])
    cb = c.astype(BF16)
    kx = jnp.dot(cb, wuk_ref[...], preferred_element_type=F32)
    ssq_k = jnp.dot((kx * kx).astype(BF16), segk_ref[...],
                    preferred_element_type=F32)
    inv_r = lax.rsqrt((ssq_k + ssq_kr) / QK_HEAD + EPS)
    tail = jnp.where(lane < QK_ROPE, kr_rot, inv_r)
    rows_ref[:, 0:KV_LORA] = c
    rows_ref[:, KV_LORA:MLA_ROW] = tail[:, 0:MLA_ROW - KV_LORA]
    kr_put = jnp.where((lane >= QK_NOPE) & (lane < QK_HEAD), pltpu.roll(kr_rot, QK_NOPE, 1), 0.0)
    gk = gk_ref[...]
    for h in range(MLA_HEADS):
        hs = slice(h * LANES, (h + 1) * LANES)
        inv_h = inv_r[:, QK_ROPE + h:QK_ROPE + h + 1]
        k_ref[:, hs] = ((kx[:, hs] * gk + kr_put) * inv_h).astype(k_ref.dtype)
    v_ref[...] = (jnp.dot(cb, wuv_ref[...], preferred_element_type=F32)
                  + vone_ref[...]).astype(v_ref.dtype)

    cq = _rms(jnp.dot(xn, wcq_ref[...], preferred_element_type=F32), qan_ref[...]).astype(BF16)
    qx = jnp.dot(cq, wuq_ref[...], preferred_element_type=F32)
    ssq_q = jnp.dot((qx * qx).astype(BF16), segq_ref[...], preferred_element_type=F32)
    inv_q = lax.rsqrt(ssq_q / QK_HEAD + EPS)
    gq = gq_ref[...]
    scale = q_scale
    in_rope = (lane >= QK_NOPE) & (lane < QK_HEAD)
    keep = jnp.where(lane < QK_NOPE, scale, jnp.where(in_rope, cos_t * scale, 0.0))
    swap = jnp.where(in_rope, sin_t * scale, 0.0)
    for h in range(MLA_HEADS):
        hs = slice(h * LANES, (h + 1) * LANES)
        t = qx[:, hs] * inv_q[:, h:h + 1] * gq
        q_ref[:, hs] = (t * keep + pltpu.roll(t, LANES - QK_ROPE, 1) * swap).astype(q_ref.dtype)


def _mla_proj(x, seq, pos0, w, tm, q_dtype, q_scale):
    rows, d = x.shape
    half = QK_ROPE // 2
    inv_freq = ROPE_THETA ** (-(jnp.arange(LANES) % half).astype(F32) / half)
    ang = (pos0 + jnp.arange(seq, dtype=F32))[:, None] * inv_freq[None, :]
    tr = tm if seq > 1 else 1
    nblk = seq // tr
    rope_spec = pl.BlockSpec((tr, LANES), lambda i: (i % nblk, 0))
    names = ("norm", "wcq", "wckv", "wkr", "qan", "kvn", "wuq", "wuk", "wuv", "vone", "gq", "gk",
             "gkr", "segq", "segk")
    consts = [w[n] for n in names]
    out_w = (HEAD_W, HEAD_W, HEAD_W, MLA_ROW)
    out_dt = (q_dtype, BF16, BF16, F32)
    return pl.pallas_call(
        functools.partial(_mla_proj_body, tm, q_scale),
        grid=(rows // tm,),
        in_specs=[pl.BlockSpec((tm, d), lambda i: (i, 0)), rope_spec, rope_spec]
        + [_const_spec(c.shape) for c in consts],
        out_specs=[pl.BlockSpec((tm, ow), lambda i: (i, 0)) for ow in out_w],
        out_shape=[jax.ShapeDtypeStruct((rows, ow), dt) for ow, dt in zip(out_w, out_dt)],
        compiler_params=_cparams("parallel"),
        name="mla_proj",
    )(x, jnp.cos(ang), jnp.sin(ang), *consts)


def _flash_body(tq, nh, q_ref, k_ref, v_ref, o_ref):
    qi = pl.program_id(2)
    half = tq // 2
    heads = [slice(j * LANES, (j + 1) * LANES) for j in range(nh)]
    q = [q_ref[0, :, hs] for hs in heads]

    def attend(carry, qs, kv_rows, mask):
        s = [lax.dot_general(qs[j], k_ref[0, kv_rows, hs], (((1,), (1,)), ((), ())),
                             preferred_element_type=F32) for j, hs in enumerate(heads)]
        if mask is not None:
            s = [jnp.where(mask, sj, -jnp.inf) for sj in s]
        m_new = [jnp.maximum(carry[j][0], jnp.max(s[j], axis=-1, keepdims=True)) for j in range(nh)]
        p = [jnp.exp2((s[j] - m_new[j]).astype(BF16)) for j in range(nh)]
        pv = [jnp.dot(p[j], v_ref[0, kv_rows, hs], preferred_element_type=F32)
              for j, hs in enumerate(heads)]
        return tuple((m_new[j], jnp.exp2(carry[j][0] - m_new[j]) * carry[j][1] + pv[j])
                     for j in range(nh))

    init = tuple((jnp.full((tq, 1), -jnp.inf, F32), jnp.zeros((tq, LANES), F32)) for _ in range(nh))
    carry = lax.fori_loop(
        0, 2 * qi, lambda kb, cr: attend(cr, q, pl.ds(pl.multiple_of(kb * half, half), half), None),
        init)
    base = pl.multiple_of(qi * tq, tq)
    carry = attend(carry, q, pl.ds(base, half),
                   _iota((tq, half), 1) <= _iota((tq, half), 0))
    lower = attend(tuple((m[half:], acc[half:]) for m, acc in carry), [qj[half:] for qj in q],
                   pl.ds(base + half, half), _iota((half, half), 1) <= _iota((half, half), 0))
    low = _iota((half, LANES), 1) < V_HEAD
    for part, rows in ((tuple((m[:half], acc[:half]) for m, acc in carry), slice(0, half)),
                       (lower, slice(half, tq))):
        outs = [acc / acc[:, V_HEAD:V_HEAD + 1] for _, acc in part]
        for j in range(nh // 2):
            o_ref[0, rows, j * LANES:(j + 1) * LANES] = jnp.where(
                low, outs[2 * j], pltpu.roll(outs[2 * j + 1], V_HEAD, 1)).astype(o_ref.dtype)


def _flash_attention(q, k, v, tq, nh):
    b, t, _ = q.shape
    kv_spec = pl.BlockSpec((1, t, nh * LANES), lambda i, j, n: (i, 0, j))
    return pl.pallas_call(
        functools.partial(_flash_body, tq, nh),
        grid=(b, MLA_HEADS // nh, t // tq),
        in_specs=[pl.BlockSpec((1, tq, nh * LANES), lambda i, j, n: (i, n, j)), kv_spec, kv_spec],
        out_specs=pl.BlockSpec((1, tq, nh * V_HEAD), lambda i, j, n: (i, n, j)),
        out_shape=jax.ShapeDtypeStruct((b, t, MLA_HEADS * V_HEAD), BF16),
        compiler_params=_cparams("parallel", "parallel", "arbitrary"),
        name="flash_attention",
    )(q, k, v)


def _absorb_body(q_ref, g_ref, w_ref, o_ref):
    t = (q_ref[...] * g_ref[...]).astype(BF16)
    o_ref[...] = jnp.dot(t, w_ref[...], preferred_element_type=F32).astype(o_ref.dtype)


def _absorb(q, gain, w_abs):
    b = q.shape[0]
    return pl.pallas_call(
        _absorb_body,
        grid=(MLA_HEADS,),
        in_specs=[pl.BlockSpec((b, LANES), lambda h: (0, h)), _const_spec((1, LANES)),
                  pl.BlockSpec((None, LANES, Q_EXT), lambda h: (h, 0, 0))],
        out_specs=pl.BlockSpec((None, b, Q_EXT), lambda h: (h, 0, 0)),
        out_shape=jax.ShapeDtypeStruct((MLA_HEADS, b, Q_EXT), BF16),
        compiler_params=_cparams("parallel"),
        name="mla_absorb",
    )(q, gain, w_abs)


def _paged_attn_body(n_pages, pt_ref, q_ref, rn_ref, cache_ref, o_ref, buf_ref, sem_ref, s_ref,
                     pb_ref):
    i = pl.program_id(0)
    slot = lax.rem(i, 2)

    def page_copy(seq, p, sl):
        return pltpu.make_async_copy(cache_ref.at[pt_ref[seq, p]], buf_ref.at[sl, p],
                                     sem_ref.at[sl])

    @pl.when(i == 0)
    def _():
        for p in range(n_pages):
            page_copy(0, p, 0).start()

    @pl.when(i + 1 < pl.num_programs(0))
    def _():
        for p in range(n_pages):
            page_copy(i + 1, p, 1 - slot).start()

    for p in range(n_pages):
        page_copy(i, p, slot).wait()

    q = q_ref[0]
    inv_lo = MLA_ROW - MLA_HEADS
    for p in range(n_pages):
        page = buf_ref[slot, p]
        cols = slice(p * PAGE_SIZE, (p + 1) * PAGE_SIZE)
        pb_ref[:, cols] = page.astype(BF16)
        s_ref[:, cols] = page[inv_lo:MLA_ROW]
    s_all = jnp.dot(q[:, :MLA_ROW], pb_ref[...], preferred_element_type=F32) * s_ref[...]
    rn = rn_ref[0]
    qf = q[:, :MLA_ROW].astype(F32)
    pick = _iota((MLA_HEADS, MLA_ROW), 1) == _iota((MLA_HEADS, MLA_ROW), 0) + inv_lo
    inv_new = jnp.sum(jnp.where(pick, rn, 0.0), axis=-1, keepdims=True)
    s_new = jnp.sum(qf * rn, axis=-1, keepdims=True) * inv_new
    m = jnp.maximum(jnp.max(s_all, axis=-1, keepdims=True), s_new)
    e_new = jnp.exp(s_new - m)
    e = jnp.exp(s_all - m)
    den = e_new + jnp.sum(e, axis=-1, keepdims=True)
    ctx = e_new * rn[:, 0:KV_LORA] + lax.dot_general(
        e.astype(BF16), pb_ref[0:KV_LORA, :], (((1,), (1,)), ((), ())), preferred_element_type=F32)
    o_ref[0] = ctx / den


def _paged_attention(q_ext, rows_new, cache_t, page_table):
    b, n_pages = page_table.shape
    grid_spec = pltpu.PrefetchScalarGridSpec(
        num_scalar_prefetch=1,
        grid=(b,),
        in_specs=[pl.BlockSpec((1, MLA_HEADS, Q_EXT), lambda i, pt: (i, 0, 0)),
                  pl.BlockSpec((1, 1, MLA_ROW), lambda i, pt: (i, 0, 0)),
                  pl.BlockSpec(memory_space=pl.ANY)],
        out_specs=pl.BlockSpec((1, MLA_HEADS, KV_LORA), lambda i, pt: (i, 0, 0)),
        scratch_shapes=[pltpu.VMEM((2, n_pages, MLA_ROW, PAGE_SIZE), F32),
                        pltpu.SemaphoreType.DMA((2,)),
                        pltpu.VMEM((MLA_HEADS, n_pages * PAGE_SIZE), F32),
                        pltpu.VMEM((MLA_ROW, n_pages * PAGE_SIZE), BF16)],
    )
    return pl.pallas_call(
        functools.partial(_paged_attn_body, n_pages),
        grid_spec=grid_spec,
        out_shape=jax.ShapeDtypeStruct((b, MLA_HEADS, KV_LORA), F32),
        compiler_params=_cparams("arbitrary"),
        name="paged_attention",
    )(page_table, q_ext, rows_new, cache_t)


def _ctx_out_body(c_ref, w_ref, o_ref):
    o_ref[...] = (jnp.dot(c_ref[0].astype(BF16), w_ref[0], preferred_element_type=F32)
                  + jnp.dot(c_ref[1].astype(BF16), w_ref[1], preferred_element_type=F32)
                  ).astype(o_ref.dtype)


def _ctx_out(ctx_t, w_pairs):
    b = ctx_t.shape[1]
    return pl.pallas_call(
        _ctx_out_body,
        grid=(MLA_HEADS // 2,),
        in_specs=[pl.BlockSpec((2, b, KV_LORA), lambda j: (j, 0, 0)),
                  pl.BlockSpec((None, 2, KV_LORA, LANES), lambda j: (j, 0, 0, 0))],
        out_specs=pl.BlockSpec((b, LANES), lambda j: (0, j)),
        out_shape=jax.ShapeDtypeStruct((b, MLA_HEADS * V_HEAD), BF16),
        compiler_params=_cparams("parallel"),
        name="mla_ctx_out",
    )(ctx_t, w_pairs)


def _lane_pad(x, lo, width):
    pad = [(0, 0)] * (x.ndim - 1) + [(lo, width - lo - x.shape[-1])]
    return jnp.pad(x.astype(F32), pad)


def _gate_tile(rows):
    tile = jnp.stack([_lane_pad(vec, off, GATE_W) for off, vec in rows])
    return jnp.pad(tile, ((0, 8 - len(rows)), (0, 0)))


def _prep_hybrid(j, w_in_a, conv_gdn_w, gdn_A_log, gdn_dt_bias, conv_ssm_w, ssm_A_log,
                 ssm_dt_bias, ssm_D):
    widths = (GDN_CONV_CH, GDN_VW, GDN_HEADS, GDN_HEADS, SSM_INNER, SSM_CONV_CH, SSM_HEADS)
    offs = np.concatenate([[0], np.cumsum(widths)])
    w = w_in_a[j].astype(BF16)
    part = lambda i: w[:, offs[i]:offs[i + 1]]
    assert (GATE_A, GATE_B, GATE_DT) == (0, GDN_HEADS, 2 * GDN_HEADS)
    small = _lane_pad(jnp.concatenate([part(2), part(3), part(6)], axis=1), 0, GATE_W)
    return dict(
        w_in=[part(0), part(1), part(4), part(5), small.astype(BF16)],
        conv_gdn=conv_gdn_w[j], conv_ssm=conv_ssm_w[j],
        gate_gdn=_gate_tile([(GATE_A, gdn_A_log[j]), (GATE_A, gdn_dt_bias[j])]),
        gate_ssm=_gate_tile([(GATE_DT, ssm_A_log[j]), (GATE_DT, ssm_dt_bias[j]),
                             (GATE_DT, ssm_D[j])]),
    )


def _prep_mla(j, norm_mix_c, w_in_c, q_a_norm, kv_a_norm, w_uq, w_uk, w_uv, q_norm, k_norm):
    half = QK_ROPE // 2
    w_in = w_in_c[j]

    def swapped(cols):
        return jnp.concatenate([-cols[..., half:], cols[..., :half]], axis=-1)

    def swapped_gain(g):
        return jnp.concatenate([g[half:], g[:half]])

    w_kr = w_in[:, Q_LORA + KV_LORA:]
    wkr = _lane_pad(jnp.concatenate([w_kr, swapped(w_kr)], axis=1), 0, LANES)
    qn, kn = q_norm[j], k_norm[j]
    gkr = _lane_pad(jnp.concatenate([kn[QK_NOPE:], swapped_gain(kn[QK_NOPE:])]), 0, LANES)
    gk = _lane_pad(kn[:QK_NOPE], 0, LANES)
    gq = jnp.concatenate([qn, swapped_gain(qn[QK_NOPE:])])

    uq = w_uq[j].reshape(Q_LORA, MLA_HEADS, QK_HEAD)
    wuq = jnp.concatenate([uq, swapped(uq[..., QK_NOPE:])], axis=-1).reshape(Q_LORA, HEAD_W)
    wuk = _lane_pad(w_uk[j], 0, LANES).reshape(KV_LORA, HEAD_W)
    wuv = _lane_pad(w_uv[j], 0, LANES).reshape(KV_LORA, HEAD_W)

    col = np.arange(HEAD_W)
    head, within = col // LANES, col % LANES
    lane = np.arange(LANES)
    inv_lane = QK_ROPE + head
    segq = ((within < QK_HEAD)[:, None] & (lane[None, :] == head[:, None]))
    segk = ((within < QK_NOPE)[:, None] & (lane[None, :] == inv_lane[:, None]))
    vone = (within == V_HEAD).astype(np.float32).reshape(1, HEAD_W)
    sel = lambda m: jnp.asarray(m, BF16)

    rope_pass = np.zeros((LANES, Q_EXT), np.float32)
    rope_pass[QK_NOPE:QK_HEAD, KV_LORA:KV_LORA + QK_ROPE] = np.eye(QK_ROPE)
    w_abs = jnp.pad(jnp.transpose(w_uk[j], (1, 2, 0)),
                    ((0, 0), (0, LANES - QK_NOPE), (0, Q_EXT - KV_LORA))) + rope_pass
    g_abs = jnp.concatenate([kn[:QK_NOPE], jnp.ones((QK_ROPE,), F32),
                             jnp.zeros((LANES - QK_HEAD,), F32)])
    uv = jnp.transpose(w_uv[j], (1, 0, 2)).reshape(MLA_HEADS // 2, 2, KV_LORA, V_HEAD)
    w_pairs = jnp.stack([_lane_pad(uv[:, 0], 0, LANES), _lane_pad(uv[:, 1], V_HEAD, LANES)], axis=1)
    row = lambda v: v.reshape(1, -1)
    return dict(
        norm=row(norm_mix_c[j]), wcq=w_in[:, :Q_LORA].astype(BF16),
        wckv=w_in[:, Q_LORA:Q_LORA + KV_LORA].astype(BF16), wkr=wkr.astype(BF16),
        qan=row(q_a_norm[j]), kvn=row(kv_a_norm[j]), wuq=wuq.astype(BF16), wuk=wuk.astype(BF16),
        wuv=wuv.astype(BF16), vone=jnp.asarray(vone), gq=row(gq), gk=row(gk), gkr=row(gkr),
        segq=sel(segq), segk=sel(segk),
        w_abs=w_abs.astype(BF16), g_abs=row(g_abs), w_pairs=w_pairs.astype(BF16),
    )


def _prep_ffn(w_gate_up, w_down, ff_tile):
    layers, d_ff, d = w_down.shape
    return (w_gate_up.astype(BF16),
            w_down.reshape(layers, d_ff // ff_tile, ff_tile, d).astype(BF16))


FF_TILE = 256
ROW_TILE = 512
FFN_ROW_TILE = 1024
MLA_ROW_TILE = 256
ATTN_TILE = 1024
ATTN_HEADS = 4
GDN_CHUNKS = 4
SSD_CHUNKS = 4


def _row_tile(rows, want):
    return want if rows % want == 0 else rows


def _trunk(x, pos0, states, cache_mla, page_table, hyb, mla, ffn, norm_mix_a, gdn_norm, conv_ssm_b,
           ssm_norm, w_out_a, w_out_c, norm_ffn):
    b, t, d = x.shape
    rows = b * t
    tm = _row_tile(rows, ROW_TILE)
    tm_f = _row_tile(rows, FFN_ROW_TILE)
    xf = x.reshape(rows, d)
    decode = states is not None

    qkv, gz, sz, xbc, small = _norm_proj(xf, norm_mix_a[0], hyb["w_in"], tm)
    if decode:
        s0, gconv0, h0, sconv0 = states
        r3 = lambda a: a.reshape(b, 1, a.shape[-1])
        n_pair = SSM_HEADS // 2
        o_gdn, s_new = _gdn_decode(qkv, gz, small, gconv0[0], s0[0], hyb["conv_gdn"],
                                   hyb["gate_gdn"], gdn_norm[0])
        o_ssd, h_new = _ssd_decode(xbc, sz, small, sconv0[0],
                                   h0[0].reshape(b, n_pair, 2 * SSM_P, SSM_N), hyb["conv_ssm"],
                                   conv_ssm_b[0], hyb["gate_ssm"], ssm_norm[0])
        gconv = jnp.concatenate([gconv0[0][:, 1:], r3(qkv)], axis=1)
        sconv = jnp.concatenate([sconv0[0][:, 1:], r3(xbc)], axis=1)
    else:
        r3 = lambda a: a.reshape(b, t, a.shape[-1])
        o_gdn, s_new = _gdn_prompt(r3(qkv), r3(gz), r3(small), hyb["conv_gdn"], hyb["gate_gdn"],
                                   gdn_norm[0], GDN_CHUNKS)
        o_ssd, h_new = _ssd_prompt(r3(xbc), r3(sz), r3(small), hyb["conv_ssm"], conv_ssm_b[0],
                                   hyb["gate_ssm"], ssm_norm[0], SSD_CHUNKS)
        gconv = r3(qkv)[:, t - (CONV_W - 1):]
        sconv = r3(xbc)[:, t - (CONV_W - 1):]
    h_new = h_new.reshape(b, SSM_HEADS, SSM_P, SSM_N)
    wo = w_out_a[0].astype(BF16)
    x1 = _res_ffn(xf, [o_gdn.reshape(rows, GDN_VW), o_ssd.reshape(rows, SSM_INNER)],
                  [wo[:GDN_VW], wo[GDN_VW:]], norm_ffn[0], 0, *ffn, tm_f)

    tm_c = _row_tile(rows, MLA_ROW_TILE)
    q_scale = QK_HEAD ** -0.5 * (1.0 if decode else math.log2(math.e))
    q, k, v, mla_rows = _mla_proj(x1, t, pos0, mla, tm_c, F32 if decode else BF16, q_scale)
    if decode:
        q_ext = jnp.transpose(_absorb(q, mla["g_abs"], mla["w_abs"]), (1, 0, 2))
        ctx = _paged_attention(q_ext, mla_rows.reshape(b, 1, MLA_ROW),
                               jnp.swapaxes(cache_mla[0], 1, 2), page_table)
        attn = _ctx_out(jnp.transpose(ctx, (1, 0, 2)), mla["w_pairs"])
    else:
        r3 = lambda a: a.reshape(b, t, a.shape[-1])
        attn = _flash_attention(r3(q), r3(k), r3(v), _row_tile(t, ATTN_TILE),
                                ATTN_HEADS).reshape(rows, -1)
    x2 = _res_ffn(x1, [attn], [w_out_c[0].astype(BF16)], norm_ffn[1], 1, *ffn, tm_f)
    return (x2.reshape(b, t, d), s_new[None], gconv[None], h_new[None], sconv[None],
            mla_rows.reshape(1, b, t, MLA_ROW))


def kernel(x_prompt, x_sample, state_gdn, state_gdn_conv, state_ssm, state_ssm_conv, cache_mla, page_table, norm_mix_a, w_in_a, conv_gdn_w, gdn_A_log, gdn_dt_bias, gdn_norm, conv_ssm_w, conv_ssm_b, ssm_A_log, ssm_dt_bias, ssm_D, ssm_norm, w_out_a, norm_mix_c, w_in_c, q_a_norm, kv_a_norm, w_uq, w_uk, w_uv, q_norm, k_norm, w_out_c, norm_ffn, w_gate_up, w_down):
    assert w_in_a.shape[0] == 1 and w_in_c.shape[0] == 1 and norm_ffn.shape[0] == 2
    hyb = _prep_hybrid(0, w_in_a, conv_gdn_w, gdn_A_log, gdn_dt_bias, conv_ssm_w, ssm_A_log,
                       ssm_dt_bias, ssm_D)
    mla = _prep_mla(0, norm_mix_c, w_in_c, q_a_norm, kv_a_norm, w_uq, w_uk, w_uv, q_norm, k_norm)
    ffn = _prep_ffn(w_gate_up, w_down, FF_TILE)
    shared = (hyb, mla, ffn, norm_mix_a, gdn_norm, conv_ssm_b, ssm_norm, w_out_a, w_out_c, norm_ffn)
    prompt = _trunk(x_prompt, 0, None, None, None, *shared)
    past_len = page_table.shape[1] * PAGE_SIZE
    sample = _trunk(x_sample, past_len, (state_gdn, state_gdn_conv, state_ssm, state_ssm_conv),
                    cache_mla, page_table, *shared)
    return (prompt[0], sample[0]) + prompt[1:] + sample[1:]
```

```python
import functools
import math

import jax
import jax.numpy as jnp
import numpy as np
from jax import lax
from jax.experimental import pallas as pl
from jax.experimental.pallas import tpu as pltpu

F32 = jnp.float32
BF16 = jnp.bfloat16

EPS = 1e-6
CONV_W = 4
CHUNK = 64
PAGE_SIZE = 128
GDN_HEADS = 4
GDN_DK = 128
GDN_DV = 128
GDN_QK = GDN_HEADS * GDN_DK
GDN_VW = GDN_HEADS * GDN_DV
GDN_CONV_CH = 2 * GDN_QK + GDN_VW
SSM_HEADS = 8
SSM_P = 64
SSM_N = 128
SSM_GROUPS = 2
SSM_INNER = SSM_HEADS * SSM_P
SSM_BC = SSM_GROUPS * SSM_N
SSM_CONV_CH = SSM_INNER + 2 * SSM_BC
MLA_HEADS = 16
Q_LORA = 512
KV_LORA = 256
QK_NOPE = 64
QK_ROPE = 32
QK_HEAD = QK_NOPE + QK_ROPE
V_HEAD = 64
MLA_ROW = KV_LORA + QK_ROPE + MLA_HEADS
ROPE_THETA = 10000.0

LANES = 128
SUBLANES = 8
HEAD_W = MLA_HEADS * LANES
GATE_W = LANES
GATE_A, GATE_B, GATE_DT = 0, GDN_HEADS, 2 * GDN_HEADS
Q_EXT = 3 * LANES
VMEM_LIMIT = 48 * 1024 * 1024

_HI = lax.Precision.HIGHEST


def _cparams(*sem):
    return pltpu.CompilerParams(dimension_semantics=sem, vmem_limit_bytes=VMEM_LIMIT)


def _const_spec(shape):
    zeros = (0,) * len(shape)
    return pl.BlockSpec(shape, lambda *_: zeros, pipeline_mode=pl.Buffered(1))


def _mm(a, b):
    return jnp.dot(a.astype(BF16), b.astype(BF16), preferred_element_type=F32)


def _mm_nt(a, b):
    return lax.dot_general(a.astype(BF16), b.astype(BF16), (((1,), (1,)), ((), ())),
                           preferred_element_type=F32)


def _mm_tn(a, b):
    return lax.dot_general(a.astype(BF16), b.astype(BF16), (((0,), (0,)), ((), ())),
                           preferred_element_type=F32)


def _sigmoid(x):
    return 1.0 / (1.0 + jnp.exp(-x))


def _silu(x):
    return x * _sigmoid(x)


def _softplus(x):
    return jnp.maximum(x, 0.0) + jnp.log(1.0 + jnp.exp(-jnp.abs(x)))


def _rms(x, gain):
    return x * lax.rsqrt(jnp.mean(x * x, axis=-1, keepdims=True) + EPS) * gain


def _l2n(x):
    return x * lax.rsqrt(jnp.sum(x * x, axis=-1, keepdims=True) + EPS)


def _iota(shape, dim):
    return lax.broadcasted_iota(jnp.int32, shape, dim)


def _chunk_cumsum(g, tb):
    row, col = _iota((tb, tb), 0), _iota((tb, tb), 1)
    shift = int(math.log2(CHUNK))
    same_chunk = lax.shift_right_logical(row, shift) == lax.shift_right_logical(col, shift)
    tril = jnp.where((col <= row) & same_chunk, 1.0, 0.0).astype(F32)
    return jnp.dot(tril, g, precision=_HI, preferred_element_type=F32)


def _rows_of(cols, lane0):
    tile = (SUBLANES, LANES)
    sel = jnp.where(_iota(tile, 1) == _iota(tile, 0) + lane0, 1.0, 0.0).astype(F32)
    return lax.dot_general(sel, cols, (((1,), (1,)), ((), ())), precision=_HI,
                           preferred_element_type=F32)


def _causal_conv_block(x_ref, xp_ref, cw, tb, first):
    @pl.when(first)
    def _():
        xp_ref[0:SUBLANES, :] = jnp.zeros((SUBLANES, xp_ref.shape[1]), F32)

    xp_ref[SUBLANES:SUBLANES + tb, :] = x_ref[...]
    off = SUBLANES - (CONV_W - 1)
    y = cw[0:1] * xp_ref[off:off + tb, :]
    for i in range(1, CONV_W):
        y = y + cw[i:i + 1] * xp_ref[off + i:off + i + tb, :]
    xp_ref[0:SUBLANES, :] = xp_ref[tb:tb + SUBLANES, :]
    return y


def _norm_proj_body(n_out, x_ref, g_ref, *refs):
    xn = _rms(x_ref[...], g_ref[...]).astype(BF16)
    for w_ref, o_ref in zip(refs[:n_out], refs[n_out:]):
        o_ref[...] = jnp.dot(xn, w_ref[...], preferred_element_type=F32)


def _norm_proj(x, gain, weights, tm):
    rows, d = x.shape
    n = len(weights)
    return pl.pallas_call(
        functools.partial(_norm_proj_body, n),
        grid=(rows // tm,),
        in_specs=[pl.BlockSpec((tm, d), lambda i: (i, 0)), _const_spec((1, d))]
        + [_const_spec(w.shape) for w in weights],
        out_specs=[pl.BlockSpec((tm, w.shape[1]), lambda i: (i, 0)) for w in weights],
        out_shape=[jax.ShapeDtypeStruct((rows, w.shape[1]), F32) for w in weights],
        compiler_params=_cparams("parallel"),
        name="norm_proj",
    )(x, gain.reshape(1, d), *weights)


def _gdn_prompt_body(nc, qkv_ref, gz_ref, sm_ref, cw_ref, gp_ref, gn_ref, o_ref, s_ref, xp_ref):
    tb = nc * CHUNK
    first = pl.program_id(1) == 0

    @pl.when(first)
    def _():
        s_ref[...] = jnp.zeros(s_ref.shape, F32)

    y = _silu(_causal_conv_block(qkv_ref.at[0], xp_ref, cw_ref[...], tb, first))
    sm, gp = sm_ref[0], gp_ref[...]
    g_all = -jnp.exp(gp[0:1]) * _softplus(sm + gp[1:2])
    beta_all = _sigmoid(sm)
    cum_all = _chunk_cumsum(g_all, tb)
    cum_rows = _rows_of(cum_all, GATE_A)
    incl = _iota((CHUNK, CHUNK), 1) <= _iota((CHUNK, CHUNK), 0)
    strict = _iota((CHUNK, CHUNK), 1) < _iota((CHUNK, CHUNK), 0)
    gn = gn_ref[...]
    pairs = [(c, h) for c in range(nc) for h in range(GDN_HEADS)]
    qs, ks, vs = [], [], []
    for h in range(GDN_HEADS):
        lo = h * GDN_DK
        qs.append(_l2n(y[:, lo:lo + GDN_DK]) * GDN_DK ** -0.5)
        ks.append(_l2n(y[:, GDN_QK + lo:GDN_QK + lo + GDN_DK]))
        vs.append(y[:, 2 * GDN_QK + lo:2 * GDN_QK + lo + GDN_DV])
    pre = []
    for c, h in pairs:
        r = slice(c * CHUNK, (c + 1) * CHUNK)
        q, k, v = qs[h][r], ks[h][r], vs[h][r]
        cum_c = cum_all[r, GATE_A + h:GATE_A + h + 1]
        cum_r = cum_rows[h:h + 1, c * CHUNK:(c + 1) * CHUNK]
        beta = beta_all[r, GATE_B + h:GATE_B + h + 1]
        decay = jnp.where(incl, jnp.exp(jnp.minimum(cum_c - cum_r, 0.0)), 0.0)
        e_c = jnp.exp(cum_c)
        kb = k * beta
        last = cum_c[CHUNK - 1:CHUNK]
        pre.append(dict(q=q.astype(BF16), k=k.astype(BF16), kb=kb, decay=decay,
                        rhs=jnp.concatenate([v * beta, kb * e_c], axis=1).astype(BF16),
                        qe=q * e_c, kd=(k * jnp.exp(last - cum_c)).astype(BF16),
                        e_last=jnp.exp(last)))
    lows = [jnp.where(strict, _mm_nt(p["kb"], p["k"]) * p["decay"], 0.0) for p in pre]
    attns = [(_mm_nt(p["q"], p["k"]) * p["decay"]).astype(BF16) for p in pre]
    row, col = _iota((CHUNK, CHUNK), 0), _iota((CHUNK, CHUNK), 1)
    invs = None
    for level in range(int(math.log2(CHUNK))):
        brow = lax.shift_right_logical(row, level)
        joins = ((brow & 1) == 1) & (lax.shift_right_logical(col, level) == brow - 1)
        subs = [jnp.where(joins, low, 0.0) for low in lows]
        if invs is None:
            eye = jnp.where(row == col, 1.0, 0.0).astype(F32)
            invs = [eye - sub for sub in subs]
        else:
            invs = [inv - _mm(_mm(inv, sub), inv) for inv, sub in zip(invs, subs)]
    sols = [_mm(inv, p["rhs"]).astype(BF16) for inv, p in zip(invs, pre)]
    k_sol = [_mm_tn(p["kd"], sol) for p, sol in zip(pre, sols)]
    a_sol = [_mm(attn, sol) for attn, sol in zip(attns, sols)]
    q_eff = [(p["qe"] - a[:, GDN_DV:]).astype(BF16) for p, a in zip(pre, a_sol)]
    m_eff = [ks[:, GDN_DV:].astype(BF16) for ks in k_sol]
    states = [s_ref[0, h] for h in range(GDN_HEADS)]
    for c in range(nc):
        r = slice(c * CHUNK, (c + 1) * CHUNK)
        idx = [c * GDN_HEADS + h for h in range(GDN_HEADS)]
        s_bf = [st.astype(BF16) for st in states]
        outs = [_mm(q_eff[i], s_bf[h]) + a_sol[i][:, :GDN_DV] for h, i in enumerate(idx)]
        states = [states[h] * pre[i]["e_last"] - _mm(m_eff[i], s_bf[h]) + k_sol[i][:, :GDN_DV]
                  for h, i in enumerate(idx)]
        for h, o in enumerate(outs):
            lo = h * GDN_DV
            gate = _silu(gz_ref[0, r, lo:lo + GDN_DV])
            o_ref[0, r, lo:lo + GDN_DV] = (_rms(o, gn) * gate).astype(o_ref.dtype)
    for h in range(GDN_HEADS):
        s_ref[0, h] = states[h]


def _gdn_prompt(qkv, gz, small, conv_w, gate_par, gdn_norm, nc):
    b, t, _ = qkv.shape
    tb = nc * CHUNK
    blk = lambda w: pl.BlockSpec((1, tb, w), lambda i, j: (i, j, 0))
    return pl.pallas_call(
        functools.partial(_gdn_prompt_body, nc),
        grid=(b, t // tb),
        in_specs=[blk(GDN_CONV_CH), blk(GDN_VW), blk(GATE_W), _const_spec(conv_w.shape),
                  _const_spec(gate_par.shape), _const_spec((1, GDN_DV))],
        out_specs=[blk(GDN_VW),
                   pl.BlockSpec((1, GDN_HEADS, GDN_DK, GDN_DV), lambda i, j: (i, 0, 0, 0))],
        out_shape=[jax.ShapeDtypeStruct((b, t, GDN_VW), BF16),
                   jax.ShapeDtypeStruct((b, GDN_HEADS, GDN_DK, GDN_DV), F32)],
        scratch_shapes=[pltpu.VMEM((tb + SUBLANES, GDN_CONV_CH), F32)],
        compiler_params=_cparams("parallel", "arbitrary"),
        name="gdn_prompt",
    )(qkv, gz, small, conv_w, gate_par, gdn_norm.reshape(1, GDN_DV))


def _ssd_prompt_body(nc, xbc_ref, sz_ref, sm_ref, cw_ref, cb_ref, sp_ref, sn_ref, o_ref, h_ref,
                     xp_ref):
    tb = nc * CHUNK
    first = pl.program_id(1) == 0

    @pl.when(first)
    def _():
        h_ref[...] = jnp.zeros(h_ref.shape, F32)

    y = _silu(_causal_conv_block(xbc_ref.at[0], xp_ref, cw_ref[...], tb, first) + cb_ref[...])
    sm, sp = sm_ref[0], sp_ref[...]
    dt_all = _softplus(sm + sp[1:2])
    cum_all = _chunk_cumsum(dt_all * -jnp.exp(sp[0:1]), tb)
    cum_rows = _rows_of(cum_all, GATE_DT)
    incl = _iota((CHUNK, CHUNK), 1) <= _iota((CHUNK, CHUNK), 0)
    lane_lo = _iota((CHUNK, LANES), 1) < SSM_P
    row_lo = _iota((2 * SSM_P, SSM_N), 0) < SSM_P
    par_lo = _iota((1, LANES), 1) < SSM_P
    sn = sn_ref[...]
    n_pair = SSM_HEADS // 2
    pairs_per_group = n_pair // SSM_GROUPS
    gw = SSM_INNER // SSM_GROUPS
    b_bf, c_bf, scores = {}, {}, {}
    for c in range(nc):
        r = slice(c * CHUNK, (c + 1) * CHUNK)
        for grp in range(SSM_GROUPS):
            lo = SSM_INNER + grp * SSM_N
            b_bf[c, grp] = y[r, lo:lo + SSM_N].astype(BF16)
            c_bf[c, grp] = y[r, lo + SSM_BC:lo + SSM_BC + SSM_N].astype(BF16)
            scores[c, grp] = _mm_nt(c_bf[c, grp], b_bf[c, grp])
    pre = {}
    for c in range(nc):
        r = slice(c * CHUNK, (c + 1) * CHUNK)
        for j in range(n_pair):
            grp = j // pairs_per_group
            cums, segs, dts = [], [], []
            for head in (2 * j, 2 * j + 1):
                cum_c = cum_all[r, GATE_DT + head:GATE_DT + head + 1]
                cum_r = cum_rows[head:head + 1, c * CHUNK:(c + 1) * CHUNK]
                cums.append(cum_c)
                segs.append(jnp.where(incl, jnp.exp(jnp.minimum(cum_c - cum_r, 0.0)), 0.0))
                dts.append(dt_all[r, GATE_DT + head:GATE_DT + head + 1])
            x_pair = y[r, j * LANES:(j + 1) * LANES]
            xdt = x_pair * jnp.where(lane_lo, dts[0], dts[1])
            last = [cm[CHUNK - 1:CHUNK] for cm in cums]
            dec = jnp.where(lane_lo, jnp.exp(last[0] - cums[0]), jnp.exp(last[1] - cums[1]))
            d_pair = jnp.where(par_lo, sp[2:3, GATE_DT + 2 * j:GATE_DT + 2 * j + 1],
                               sp[2:3, GATE_DT + 2 * j + 1:GATE_DT + 2 * j + 2])
            pre[c, j] = dict(
                y=(_mm(scores[c, grp] * segs[0], jnp.where(lane_lo, xdt, 0.0))
                   + _mm(scores[c, grp] * segs[1], jnp.where(lane_lo, 0.0, xdt))
                   + d_pair * x_pair),
                e=jnp.where(lane_lo, jnp.exp(cums[0]), jnp.exp(cums[1])),
                xdec=(xdt * dec).astype(BF16),
                e_last=jnp.where(row_lo, jnp.exp(last[0]), jnp.exp(last[1])))
    states = [h_ref[0, j] for j in range(n_pair)]
    for c in range(nc):
        r = slice(c * CHUNK, (c + 1) * CHUNK)
        ys = [pre[c, j]["y"] + _mm_nt(c_bf[c, j // pairs_per_group], states[j]) * pre[c, j]["e"]
              for j in range(n_pair)]
        states = [states[j] * pre[c, j]["e_last"]
                  + _mm_tn(pre[c, j]["xdec"], b_bf[c, j // pairs_per_group]) for j in range(n_pair)]
        for grp in range(SSM_GROUPS):
            outs = [ys[j] * _silu(sz_ref[0, r, j * LANES:(j + 1) * LANES])
                    for j in range(grp * pairs_per_group, (grp + 1) * pairs_per_group)]
            inv = lax.rsqrt(sum(jnp.sum(o * o, axis=-1, keepdims=True) for o in outs) / gw + EPS)
            for jj, o in enumerate(outs):
                lo = grp * gw + jj * LANES
                o_ref[0, r, lo:lo + LANES] = (o * inv * sn[:, lo:lo + LANES]).astype(o_ref.dtype)
    for j in range(n_pair):
        h_ref[0, j] = states[j]


def _ssd_prompt(xbc, sz, small, conv_w, conv_b, ssm_par, ssm_norm, nc):
    b, t, _ = xbc.shape
    tb = nc * CHUNK
    blk = lambda w: pl.BlockSpec((1, tb, w), lambda i, j: (i, j, 0))
    n_pair = SSM_HEADS // 2
    return pl.pallas_call(
        functools.partial(_ssd_prompt_body, nc),
        grid=(b, t // tb),
        in_specs=[blk(SSM_CONV_CH), blk(SSM_INNER), blk(GATE_W), _const_spec(conv_w.shape),
                  _const_spec((1, SSM_CONV_CH)), _const_spec(ssm_par.shape),
                  _const_spec((1, SSM_INNER))],
        out_specs=[blk(SSM_INNER),
                   pl.BlockSpec((1, n_pair, 2 * SSM_P, SSM_N), lambda i, j: (i, 0, 0, 0))],
        out_shape=[jax.ShapeDtypeStruct((b, t, SSM_INNER), BF16),
                   jax.ShapeDtypeStruct((b, n_pair, 2 * SSM_P, SSM_N), F32)],
        scratch_shapes=[pltpu.VMEM((tb + SUBLANES, SSM_CONV_CH), F32)],
        compiler_params=_cparams("parallel", "arbitrary"),
        name="ssd_prompt",
    )(xbc, sz, small, conv_w, conv_b.reshape(1, -1), ssm_par, ssm_norm.reshape(1, -1))


def _col_bcast(row):
    return jnp.transpose(jnp.broadcast_to(row, (LANES, LANES)))


def _decode_conv(new_ref, cv_ref, cw):
    y = cw[CONV_W - 1:CONV_W] * new_ref[...]
    for i in range(CONV_W - 1):
        y = y + cw[i:i + 1] * cv_ref[:, i, :]
    return y


def _gdn_decode_body(bb, qkv_ref, gz_ref, sm_ref, cv_ref, s0_ref, cw_ref, gp_ref, gn_ref, o_ref,
                     s_ref, raw_ref):
    y = _silu(_decode_conv(qkv_ref, cv_ref, cw_ref[...]))
    sm, gp = sm_ref[...], gp_ref[...]
    a_all = jnp.exp(-jnp.exp(gp[0:1]) * _softplus(sm + gp[1:2]))
    beta_all = _sigmoid(sm)
    qs, ks, vs = [], [], []
    for h in range(GDN_HEADS):
        lo = h * GDN_DK
        qs.append(_l2n(y[:, lo:lo + GDN_DK]) * GDN_DK ** -0.5)
        ks.append(_l2n(y[:, GDN_QK + lo:GDN_QK + lo + GDN_DK]))
        vs.append(y[:, 2 * GDN_QK + lo:2 * GDN_QK + lo + GDN_DV])
    pairs = [(b, h) for b in range(bb) for h in range(GDN_HEADS)]
    k_col = [_col_bcast(ks[h][b:b + 1]) for b, h in pairs]
    q_col = [_col_bcast(qs[h][b:b + 1]) for b, h in pairs]
    decayed = [s0_ref[b, h] * a_all[b:b + 1, GATE_A + h:GATE_A + h + 1] for b, h in pairs]
    v_new = [beta_all[b:b + 1, GATE_B + h:GATE_B + h + 1]
             * (vs[h][b:b + 1] - jnp.sum(k_col[i] * decayed[i], axis=0, keepdims=True))
             for i, (b, h) in enumerate(pairs)]
    for i, (b, h) in enumerate(pairs):
        state = decayed[i] + k_col[i] * v_new[i]
        s_ref[b, h] = state
        raw_ref[b:b + 1, h * GDN_DV:(h + 1) * GDN_DV] = jnp.sum(q_col[i] * state, axis=0,
                                                                keepdims=True)
    gn = gn_ref[...]
    for h in range(GDN_HEADS):
        hs = slice(h * GDN_DV, (h + 1) * GDN_DV)
        o_ref[:, hs] = (_rms(raw_ref[:, hs], gn) * _silu(gz_ref[:, hs])).astype(o_ref.dtype)


DECODE_BLOCK = SUBLANES


def _gdn_decode(qkv, gz, small, conv0, s0, conv_w, gate_par, gdn_norm):
    b = qkv.shape[0]
    bb = DECODE_BLOCK
    row = lambda w: pl.BlockSpec((bb, w), lambda i: (i, 0))
    st = pl.BlockSpec((bb, GDN_HEADS, GDN_DK, GDN_DV), lambda i: (i, 0, 0, 0))
    return pl.pallas_call(
        functools.partial(_gdn_decode_body, bb),
        grid=(b // bb,),
        in_specs=[row(GDN_CONV_CH), row(GDN_VW), row(GATE_W),
                  pl.BlockSpec((bb, CONV_W - 1, GDN_CONV_CH), lambda i: (i, 0, 0)), st,
                  _const_spec(conv_w.shape), _const_spec(gate_par.shape), _const_spec((1, GDN_DV))],
        out_specs=[row(GDN_VW), st],
        out_shape=[jax.ShapeDtypeStruct((b, GDN_VW), F32), jax.ShapeDtypeStruct(s0.shape, F32)],
        scratch_shapes=[pltpu.VMEM((bb, GDN_VW), F32)],
        compiler_params=_cparams("parallel"),
        name="gdn_decode",
    )(qkv, gz, small, conv0, s0, conv_w, gate_par, gdn_norm.reshape(1, GDN_DV))


def _ssd_decode_body(bb, xbc_ref, sz_ref, sm_ref, cv_ref, h0_ref, cw_ref, cb_ref, sp_ref, sn_ref,
                     o_ref, h_ref, raw_ref):
    y = _silu(_decode_conv(xbc_ref, cv_ref, cw_ref[...]) + cb_ref[...])
    sm, sp = sm_ref[...], sp_ref[...]
    dt_all = _softplus(sm + sp[1:2])
    da_all = jnp.exp(dt_all * -jnp.exp(sp[0:1]))
    par_lo = _iota((1, LANES), 1) < SSM_P
    row_lo = _iota((2 * SSM_P, SSM_N), 0) < SSM_P
    n_pair = SSM_HEADS // 2
    pairs_per_group = n_pair // SSM_GROUPS
    pairs = [(b, j) for b in range(bb) for j in range(n_pair)]
    x_col, states = [], []
    for b, j in pairs:
        la, lb = GATE_DT + 2 * j, GATE_DT + 2 * j + 1
        row = slice(b, b + 1)
        xdt = y[row, j * LANES:(j + 1) * LANES] * jnp.where(par_lo, dt_all[row, la:la + 1],
                                                           dt_all[row, lb:lb + 1])
        x_col.append(_col_bcast(xdt))
    for i, (b, j) in enumerate(pairs):
        la, lb = GATE_DT + 2 * j, GATE_DT + 2 * j + 1
        row = slice(b, b + 1)
        grp = j // pairs_per_group
        b_g = y[row, SSM_INNER + grp * SSM_N:SSM_INNER + (grp + 1) * SSM_N]
        state = (h0_ref[b, j] * jnp.where(row_lo, da_all[row, la:la + 1], da_all[row, lb:lb + 1])
                 + x_col[i] * b_g)
        h_ref[b, j] = state
        states.append(state)
    for i, (b, j) in enumerate(pairs):
        grp = j // pairs_per_group
        lo = SSM_INNER + SSM_BC + grp * SSM_N
        y_col = jnp.sum(states[i] * y[b:b + 1, lo:lo + SSM_N], axis=1, keepdims=True)
        raw_ref[b:b + 1, j * LANES:(j + 1) * LANES] = jnp.transpose(
            jnp.broadcast_to(y_col, (LANES, LANES)))[0:1]
    sn = sn_ref[...]
    gw = SSM_INNER // SSM_GROUPS
    for grp in range(SSM_GROUPS):
        outs = []
        for jj in range(pairs_per_group):
            j = grp * pairs_per_group + jj
            la, lb = GATE_DT + 2 * j, GATE_DT + 2 * j + 1
            js = slice(j * LANES, (j + 1) * LANES)
            d_pair = jnp.where(par_lo, sp[2:3, la:la + 1], sp[2:3, lb:lb + 1])
            outs.append((raw_ref[:, js] + d_pair * y[:, js]) * _silu(sz_ref[:, js]))
        inv = lax.rsqrt(sum(jnp.sum(o * o, axis=-1, keepdims=True) for o in outs) / gw + EPS)
        for jj, o in enumerate(outs):
            lo = grp * gw + jj * LANES
            o_ref[:, lo:lo + LANES] = (o * inv * sn[:, lo:lo + LANES]).astype(o_ref.dtype)


def _ssd_decode(xbc, sz, small, conv0, h0, conv_w, conv_b, ssm_par, ssm_norm):
    b = xbc.shape[0]
    bb = DECODE_BLOCK
    n_pair = SSM_HEADS // 2
    row = lambda w: pl.BlockSpec((bb, w), lambda i: (i, 0))
    st = pl.BlockSpec((bb, n_pair, 2 * SSM_P, SSM_N), lambda i: (i, 0, 0, 0))
    return pl.pallas_call(
        functools.partial(_ssd_decode_body, bb),
        grid=(b // bb,),
        in_specs=[row(SSM_CONV_CH), row(SSM_INNER), row(GATE_W),
                  pl.BlockSpec((bb, CONV_W - 1, SSM_CONV_CH), lambda i: (i, 0, 0)), st,
                  _const_spec(conv_w.shape), _const_spec((1, SSM_CONV_CH)),
                  _const_spec(ssm_par.shape), _const_spec((1, SSM_INNER))],
        out_specs=[row(SSM_INNER), st],
        out_shape=[jax.ShapeDtypeStruct((b, SSM_INNER), F32), jax.ShapeDtypeStruct(h0.shape, F32)],
        scratch_shapes=[pltpu.VMEM((bb, SSM_INNER), F32)],
        compiler_params=_cparams("parallel"),
        name="ssd_decode",
    )(xbc, sz, small, conv0, h0, conv_w, conv_b.reshape(1, -1), ssm_par, ssm_norm.reshape(1, -1))


def _res_ffn_body(n_a, n_ff, x_ref, *refs):
    a_refs, w_refs = refs[:n_a], refs[n_a:2 * n_a]
    g_ref, wgu_ref, wd_ref, o_ref, xn_ref = refs[2 * n_a:]
    ff_tile = wd_ref.shape[1]
    d_ff = n_ff * ff_tile
    x1 = x_ref[...]
    for a_ref, w_ref in zip(a_refs, w_refs):
        x1 = x1 + jnp.dot(a_ref[...].astype(BF16), w_ref[...], preferred_element_type=F32)
    xn_ref[...] = _rms(x1, g_ref[...]).astype(BF16)
    o_ref[...] = x1

    def step(c, carry):
        xn = xn_ref[...]
        lo = pl.multiple_of(c * ff_tile, ff_tile)
        gate = jnp.dot(xn, wgu_ref[:, pl.ds(lo, ff_tile)], preferred_element_type=F32)
        up = jnp.dot(xn, wgu_ref[:, pl.ds(d_ff + lo, ff_tile)], preferred_element_type=F32)
        hid = (_silu(gate) * up).astype(BF16)
        o_ref[...] += jnp.dot(hid, wd_ref[c], preferred_element_type=F32)
        return carry

    lax.fori_loop(0, n_ff, step, 0)


def _res_ffn(x, acts, projs, gain, layer, wgu, wd, tm):
    rows, d = x.shape
    n_a, n_ff = len(acts), wd.shape[1]

    def layer_spec(w):
        tail = (0,) * (w.ndim - 1)
        return pl.BlockSpec((None,) + w.shape[1:], lambda i: (layer,) + tail,
                            pipeline_mode=pl.Buffered(1))

    return pl.pallas_call(
        functools.partial(_res_ffn_body, n_a, n_ff),
        grid=(rows // tm,),
        in_specs=[pl.BlockSpec((tm, d), lambda i: (i, 0))]
        + [pl.BlockSpec((tm, a.shape[1]), lambda i: (i, 0)) for a in acts]
        + [_const_spec(p.shape) for p in projs]
        + [_const_spec((1, d)), layer_spec(wgu), layer_spec(wd)],
        out_specs=pl.BlockSpec((tm, d), lambda i: (i, 0)),
        out_shape=jax.ShapeDtypeStruct((rows, d), F32),
        scratch_shapes=[pltpu.VMEM((tm, d), BF16)],
        compiler_params=_cparams("parallel"),
        name="res_ffn",
    )(x, *acts, *projs, gain.reshape(1, d), wgu, wd)


def _mla_proj_body(tm, q_scale, x_ref, cos_ref, sin_ref, g_ref, wcq_ref, wckv_ref, wkr_ref,
                   qan_ref, kvn_ref, wuq_ref, wuk_ref, wuv_ref, vone_ref, gq_ref, gk_ref, gkr_ref,
                   segq_ref, segk_ref, q_ref, k_ref, v_ref, rows_ref):
    xn = _rms(x_ref[...], g_ref[...]).astype(BF16)
    lane = _iota((tm, LANES), 1)
    cos_t, sin_t = cos_ref[...], sin_ref[...]

    kr = jnp.dot(xn, wkr_ref[...], preferred_element_type=F32)
    ssq_kr = jnp.sum(jnp.where(lane < QK_ROPE, kr * kr, 0.0), axis=-1, keepdims=True)
    krg = kr * gkr_ref[...]
    kr_rot = krg * cos_t + pltpu.roll(krg, LANES - QK_ROPE, 1) * sin_t

    c = _rms(jnp.dot(xn, wckv_ref[...], preferred_element_type=F32), kvn_ref[...])
    cb = c.astype(BF16)
    kx = jnp.dot(cb, wuk_ref[...], preferred_element_type=F32)
    ssq_k = jnp.dot((kx * kx).astype(BF16), segk_ref[...],
                    preferred_element_type=F32)
    inv_r = lax.rsqrt((ssq_k + ssq_kr) / QK_HEAD + EPS)
    tail = jnp.where(lane < QK_ROPE, kr_rot, inv_r)
    rows_ref[:, 0:KV_LORA] = c
    rows_ref[:, KV_LORA:MLA_ROW] = tail[:, 0:MLA_ROW - KV_LORA]
    kr_put = jnp.where((lane >= QK_NOPE) & (lane < QK_HEAD), pltpu.roll(kr_rot, QK_NOPE, 1), 0.0)
    gk = gk_ref[...]
    for h in range(MLA_HEADS):
        hs = slice(h * LANES, (h + 1) * LANES)
        inv_h = inv_r[:, QK_ROPE + h:QK_ROPE + h + 1]
        k_ref[:, hs] = ((kx[:, hs] * gk + kr_put) * inv_h).astype(k_ref.dtype)
    v_ref[...] = (jnp.dot(cb, wuv_ref[...], preferred_element_type=F32)
                  + vone_ref[...]).astype(v_ref.dtype)

    cq = _rms(jnp.dot(xn, wcq_ref[...], preferred_element_type=F32), qan_ref[...]).astype(BF16)
    qx = jnp.dot(cq, wuq_ref[...], preferred_element_type=F32)
    ssq_q = jnp.dot((qx * qx).astype(BF16), segq_ref[...], preferred_element_type=F32)
    inv_q = lax.rsqrt(ssq_q / QK_HEAD + EPS)
    gq = gq_ref[...]
    scale = q_scale
    in_rope = (lane >= QK_NOPE) & (lane < QK_HEAD)
    keep = jnp.where(lane < QK_NOPE, scale, jnp.where(in_rope, cos_t * scale, 0.0))
    swap = jnp.where(in_rope, sin_t * scale, 0.0)
    for h in range(MLA_HEADS):
        hs = slice(h * LANES, (h + 1) * LANES)
        t = qx[:, hs] * inv_q[:, h:h + 1] * gq
        q_ref[:, hs] = (t * keep + pltpu.roll(t, LANES - QK_ROPE, 1) * swap).astype(q_ref.dtype)


def _mla_proj(x, seq, pos0, w, tm, q_dtype, q_scale):
    rows, d = x.shape
    half = QK_ROPE // 2
    inv_freq = ROPE_THETA ** (-(jnp.arange(LANES) % half).astype(F32) / half)
    ang = (pos0 + jnp.arange(seq, dtype=F32))[:, None] * inv_freq[None, :]
    tr = tm if seq > 1 else 1
    nblk = seq // tr
    rope_spec = pl.BlockSpec((tr, LANES), lambda i: (i % nblk, 0))
    names = ("norm", "wcq", "wckv", "wkr", "qan", "kvn", "wuq", "wuk", "wuv", "vone", "gq", "gk",
             "gkr", "segq", "segk")
    consts = [w[n] for n in names]
    out_w = (HEAD_W, HEAD_W, HEAD_W, MLA_ROW)
    out_dt = (q_dtype, BF16, BF16, F32)
    return pl.pallas_call(
        functools.partial(_mla_proj_body, tm, q_scale),
        grid=(rows // tm,),
        in_specs=[pl.BlockSpec((tm, d), lambda i: (i, 0)), rope_spec, rope_spec]
        + [_const_spec(c.shape) for c in consts],
        out_specs=[pl.BlockSpec((tm, ow), lambda i: (i, 0)) for ow in out_w],
        out_shape=[jax.ShapeDtypeStruct((rows, ow), dt) for ow, dt in zip(out_w, out_dt)],
        compiler_params=_cparams("parallel"),
        name="mla_proj",
    )(x, jnp.cos(ang), jnp.sin(ang), *consts)


def _flash_body(tq, nh, q_ref, k_ref, v_ref, o_ref):
    qi = pl.program_id(2)
    half = tq // 2
    heads = [slice(j * LANES, (j + 1) * LANES) for j in range(nh)]
    q = [q_ref[0, :, hs] for hs in heads]

    def attend(carry, qs, kv_rows, mask):
        s = [lax.dot_general(qs[j], k_ref[0, kv_rows, hs], (((1,), (1,)), ((), ())),
                             preferred_element_type=F32) for j, hs in enumerate(heads)]
        if mask is not None:
            s = [jnp.where(mask, sj, -jnp.inf) for sj in s]
        m_new = [jnp.maximum(carry[j][0], jnp.max(s[j], axis=-1, keepdims=True)) for j in range(nh)]
        p = [jnp.exp2((s[j] - m_new[j]).astype(BF16)) for j in range(nh)]
        pv = [jnp.dot(p[j], v_ref[0, kv_rows, hs], preferred_element_type=F32)
              for j, hs in enumerate(heads)]
        return tuple((m_new[j], jnp.exp2(carry[j][0] - m_new[j]) * carry[j][1] + pv[j])
                     for j in range(nh))

    init = tuple((jnp.full((tq, 1), -jnp.inf, F32), jnp.zeros((tq, LANES), F32)) for _ in range(nh))
    carry = lax.fori_loop(
        0, 2 * qi, lambda kb, cr: attend(cr, q, pl.ds(pl.multiple_of(kb * half, half), half), None),
        init)
    base = pl.multiple_of(qi * tq, tq)
    carry = attend(carry, q, pl.ds(base, half),
                   _iota((tq, half), 1) <= _iota((tq, half), 0))
    lower = attend(tuple((m[half:], acc[half:]) for m, acc in carry), [qj[half:] for qj in q],
                   pl.ds(base + half, half), _iota((half, half), 1) <= _iota((half, half), 0))
    low = _iota((half, LANES), 1) < V_HEAD
    for part, rows in ((tuple((m[:half], acc[:half]) for m, acc in carry), slice(0, half)),
                       (lower, slice(half, tq))):
        outs = [acc / acc[:, V_HEAD:V_HEAD + 1] for _, acc in part]
        for j in range(nh // 2):
            o_ref[0, rows, j * LANES:(j + 1) * LANES] = jnp.where(
                low, outs[2 * j], pltpu.roll(outs[2 * j + 1], V_HEAD, 1)).astype(o_ref.dtype)


def _flash_attention(q, k, v, tq, nh):
    b, t, _ = q.shape
    kv_spec = pl.BlockSpec((1, t, nh * LANES), lambda i, j, n: (i, 0, j))
    return pl.pallas_call(
        functools.partial(_flash_body, tq, nh),
        grid=(b, MLA_HEADS // nh, t // tq),
        in_specs=[pl.BlockSpec((1, tq, nh * LANES), lambda i, j, n: (i, n, j)), kv_spec, kv_spec],
        out_specs=pl.BlockSpec((1, tq, nh * V_HEAD), lambda i, j, n: (i, n, j)),
        out_shape=jax.ShapeDtypeStruct((b, t, MLA_HEADS * V_HEAD), BF16),
        compiler_params=_cparams("parallel", "parallel", "arbitrary"),
        name="flash_attention",
    )(q, k, v)


def _absorb_body(q_ref, g_ref, w_ref, o_ref):
    t = (q_ref[...] * g_ref[...]).astype(BF16)
    o_ref[...] = jnp.dot(t, w_ref[...], preferred_element_type=F32).astype(o_ref.dtype)


def _absorb(q, gain, w_abs):
    b = q.shape[0]
    return pl.pallas_call(
        _absorb_body,
        grid=(MLA_HEADS,),
        in_specs=[pl.BlockSpec((b, LANES), lambda h: (0, h)), _const_spec((1, LANES)),
                  pl.BlockSpec((None, LANES, Q_EXT), lambda h: (h, 0, 0))],
        out_specs=pl.BlockSpec((None, b, Q_EXT), lambda h: (h, 0, 0)),
        out_shape=jax.ShapeDtypeStruct((MLA_HEADS, b, Q_EXT), BF16),
        compiler_params=_cparams("parallel"),
        name="mla_absorb",
    )(q, gain, w_abs)


PAGE_SLOTS = 3


def _paged_attn_body(n_pages, pt_ref, q_ref, rn_ref, cache_ref, o_ref, buf_ref, sem_ref, s_ref,
                     pb_ref):
    i = pl.program_id(0)
    n_seq = pl.num_programs(0)
    slot = lax.rem(i, PAGE_SLOTS)

    def page_copy(seq, p, sl):
        return pltpu.make_async_copy(cache_ref.at[pt_ref[seq, p]], buf_ref.at[sl, p],
                                     sem_ref.at[sl])

    def fetch(seq):
        @pl.when(seq < n_seq)
        def _():
            for p in range(n_pages):
                page_copy(seq, p, lax.rem(seq, PAGE_SLOTS)).start()

    @pl.when(i == 0)
    def _():
        for ahead in range(PAGE_SLOTS - 1):
            fetch(ahead)

    fetch(i + PAGE_SLOTS - 1)

    for p in range(n_pages):
        page_copy(i, p, slot).wait()

    q = q_ref[0]
    inv_lo = MLA_ROW - MLA_HEADS
    for p in range(n_pages):
        page = buf_ref[slot, p]
        cols = slice(p * PAGE_SIZE, (p + 1) * PAGE_SIZE)
        pb_ref[:, cols] = page.astype(BF16)
        s_ref[:, cols] = page[inv_lo:MLA_ROW]
    s_all = jnp.dot(q[:, :MLA_ROW], pb_ref[...], preferred_element_type=F32) * s_ref[...]
    rn = rn_ref[0]
    qf = q[:, :MLA_ROW].astype(F32)
    pick = _iota((MLA_HEADS, MLA_ROW), 1) == _iota((MLA_HEADS, MLA_ROW), 0) + inv_lo
    inv_new = jnp.sum(jnp.where(pick, rn, 0.0), axis=-1, keepdims=True)
    s_new = jnp.sum(qf * rn, axis=-1, keepdims=True) * inv_new
    m = jnp.maximum(jnp.max(s_all, axis=-1, keepdims=True), s_new)
    e_new = jnp.exp(s_new - m)
    e = jnp.exp(s_all - m)
    den = e_new + jnp.sum(e, axis=-1, keepdims=True)
    ctx = e_new * rn[:, 0:KV_LORA] + lax.dot_general(
        e.astype(BF16), pb_ref[0:KV_LORA, :], (((1,), (1,)), ((), ())), preferred_element_type=F32)
    o_ref[0] = ctx / den


def _paged_attention(q_ext, rows_new, cache_t, page_table):
    b, n_pages = page_table.shape
    grid_spec = pltpu.PrefetchScalarGridSpec(
        num_scalar_prefetch=1,
        grid=(b,),
        in_specs=[pl.BlockSpec((1, MLA_HEADS, Q_EXT), lambda i, pt: (i, 0, 0)),
                  pl.BlockSpec((1, 1, MLA_ROW), lambda i, pt: (i, 0, 0)),
                  pl.BlockSpec(memory_space=pl.ANY)],
        out_specs=pl.BlockSpec((1, MLA_HEADS, KV_LORA), lambda i, pt: (i, 0, 0)),
        scratch_shapes=[pltpu.VMEM((PAGE_SLOTS, n_pages, MLA_ROW, PAGE_SIZE), F32),
                        pltpu.SemaphoreType.DMA((PAGE_SLOTS,)),
                        pltpu.VMEM((MLA_HEADS, n_pages * PAGE_SIZE), F32),
                        pltpu.VMEM((MLA_ROW, n_pages * PAGE_SIZE), BF16)],
    )
    return pl.pallas_call(
        functools.partial(_paged_attn_body, n_pages),
        grid_spec=grid_spec,
        out_shape=jax.ShapeDtypeStruct((b, MLA_HEADS, KV_LORA), F32),
        compiler_params=_cparams("arbitrary"),
        name="paged_attention",
    )(page_table, q_ext, rows_new, cache_t)


def _ctx_out_body(c_ref, w_ref, o_ref):
    o_ref[...] = (jnp.dot(c_ref[0].astype(BF16), w_ref[0], preferred_element_type=F32)
                  + jnp.dot(c_ref[1].astype(BF16), w_ref[1], preferred_element_type=F32)
                  ).astype(o_ref.dtype)


def _ctx_out(ctx_t, w_pairs):
    b = ctx_t.shape[1]
    return pl.pallas_call(
        _ctx_out_body,
        grid=(MLA_HEADS // 2,),
        in_specs=[pl.BlockSpec((2, b, KV_LORA), lambda j: (j, 0, 0)),
                  pl.BlockSpec((None, 2, KV_LORA, LANES), lambda j: (j, 0, 0, 0))],
        out_specs=pl.BlockSpec((b, LANES), lambda j: (0, j)),
        out_shape=jax.ShapeDtypeStruct((b, MLA_HEADS * V_HEAD), BF16),
        compiler_params=_cparams("parallel"),
        name="mla_ctx_out",
    )(ctx_t, w_pairs)


def _lane_pad(x, lo, width):
    pad = [(0, 0)] * (x.ndim - 1) + [(lo, width - lo - x.shape[-1])]
    return jnp.pad(x.astype(F32), pad)


def _gate_tile(rows):
    tile = jnp.stack([_lane_pad(vec, off, GATE_W) for off, vec in rows])
    return jnp.pad(tile, ((0, SUBLANES - len(rows)), (0, 0)))


def _prep_hybrid(j, w_in_a, conv_gdn_w, gdn_A_log, gdn_dt_bias, conv_ssm_w, ssm_A_log,
                 ssm_dt_bias, ssm_D):
    widths = (GDN_CONV_CH, GDN_VW, GDN_HEADS, GDN_HEADS, SSM_INNER, SSM_CONV_CH, SSM_HEADS)
    offs = np.concatenate([[0], np.cumsum(widths)])
    w = w_in_a[j].astype(BF16)
    part = lambda i: w[:, offs[i]:offs[i + 1]]
    assert (GATE_A, GATE_B, GATE_DT) == (0, GDN_HEADS, 2 * GDN_HEADS)
    small = _lane_pad(jnp.concatenate([part(2), part(3), part(6)], axis=1), 0, GATE_W)
    return dict(
        w_in=[part(0), part(1), part(4), part(5), small.astype(BF16)],
        conv_gdn=conv_gdn_w[j], conv_ssm=conv_ssm_w[j],
        gate_gdn=_gate_tile([(GATE_A, gdn_A_log[j]), (GATE_A, gdn_dt_bias[j])]),
        gate_ssm=_gate_tile([(GATE_DT, ssm_A_log[j]), (GATE_DT, ssm_dt_bias[j]),
                             (GATE_DT, ssm_D[j])]),
    )


def _prep_mla(j, norm_mix_c, w_in_c, q_a_norm, kv_a_norm, w_uq, w_uk, w_uv, q_norm, k_norm):
    half = QK_ROPE // 2
    w_in = w_in_c[j]

    def swapped(cols):
        return jnp.concatenate([-cols[..., half:], cols[..., :half]], axis=-1)

    def swapped_gain(g):
        return jnp.concatenate([g[half:], g[:half]])

    w_kr = w_in[:, Q_LORA + KV_LORA:]
    wkr = _lane_pad(jnp.concatenate([w_kr, swapped(w_kr)], axis=1), 0, LANES)
    qn, kn = q_norm[j], k_norm[j]
    gkr = _lane_pad(jnp.concatenate([kn[QK_NOPE:], swapped_gain(kn[QK_NOPE:])]), 0, LANES)
    gk = _lane_pad(kn[:QK_NOPE], 0, LANES)
    gq = jnp.concatenate([qn, swapped_gain(qn[QK_NOPE:])])

    uq = w_uq[j].reshape(Q_LORA, MLA_HEADS, QK_HEAD)
    wuq = jnp.concatenate([uq, swapped(uq[..., QK_NOPE:])], axis=-1).reshape(Q_LORA, HEAD_W)
    wuk = _lane_pad(w_uk[j], 0, LANES).reshape(KV_LORA, HEAD_W)
    wuv = _lane_pad(w_uv[j], 0, LANES).reshape(KV_LORA, HEAD_W)

    col = np.arange(HEAD_W)
    head, within = col // LANES, col % LANES
    lane = np.arange(LANES)
    inv_lane = QK_ROPE + head
    segq = ((within < QK_HEAD)[:, None] & (lane[None, :] == head[:, None]))
    segk = ((within < QK_NOPE)[:, None] & (lane[None, :] == inv_lane[:, None]))
    vone = (within == V_HEAD).astype(np.float32).reshape(1, HEAD_W)
    sel = lambda m: jnp.asarray(m, BF16)

    rope_pass = np.zeros((LANES, Q_EXT), np.float32)
    rope_pass[QK_NOPE:QK_HEAD, KV_LORA:KV_LORA + QK_ROPE] = np.eye(QK_ROPE)
    w_abs = jnp.pad(jnp.transpose(w_uk[j], (1, 2, 0)),
                    ((0, 0), (0, LANES - QK_NOPE), (0, Q_EXT - KV_LORA))) + rope_pass
    g_abs = jnp.concatenate([kn[:QK_NOPE], jnp.ones((QK_ROPE,), F32),
                             jnp.zeros((LANES - QK_HEAD,), F32)])
    uv = jnp.transpose(w_uv[j], (1, 0, 2)).reshape(MLA_HEADS // 2, 2, KV_LORA, V_HEAD)
    w_pairs = jnp.stack([_lane_pad(uv[:, 0], 0, LANES), _lane_pad(uv[:, 1], V_HEAD, LANES)], axis=1)
    row = lambda v: v.reshape(1, -1)
    return dict(
        norm=row(norm_mix_c[j]), wcq=w_in[:, :Q_LORA].astype(BF16),
        wckv=w_in[:, Q_LORA:Q_LORA + KV_LORA].astype(BF16), wkr=wkr.astype(BF16),
        qan=row(q_a_norm[j]), kvn=row(kv_a_norm[j]), wuq=wuq.astype(BF16), wuk=wuk.astype(BF16),
        wuv=wuv.astype(BF16), vone=jnp.asarray(vone), gq=row(gq), gk=row(gk), gkr=row(gkr),
        segq=sel(segq), segk=sel(segk),
        w_abs=w_abs.astype(BF16), g_abs=row(g_abs), w_pairs=w_pairs.astype(BF16),
    )


def _prep_ffn(w_gate_up, w_down, ff_tile):
    layers, d_ff, d = w_down.shape
    return (w_gate_up.astype(BF16),
            w_down.reshape(layers, d_ff // ff_tile, ff_tile, d).astype(BF16))


FF_TILE = 256
ROW_TILE = 512
FFN_ROW_TILE = 1024
MLA_ROW_TILE = 256
ATTN_TILE = 1024
ATTN_HEADS = 4
GDN_CHUNKS = 4
SSD_CHUNKS = 4


def _row_tile(rows, want):
    return want if rows % want == 0 else rows


def _trunk(x, pos0, states, cache_mla, page_table, hyb, mla, ffn, norm_mix_a, gdn_norm, conv_ssm_b,
           ssm_norm, w_out_a, w_out_c, norm_ffn):
    b, t, d = x.shape
    rows = b * t
    tm = _row_tile(rows, ROW_TILE)
    tm_f = _row_tile(rows, FFN_ROW_TILE)
    xf = x.reshape(rows, d)
    decode = states is not None

    qkv, gz, sz, xbc, small = _norm_proj(xf, norm_mix_a[0], hyb["w_in"], tm)
    if decode:
        s0, gconv0, h0, sconv0 = states
        r3 = lambda a: a.reshape(b, 1, a.shape[-1])
        n_pair = SSM_HEADS // 2
        o_gdn, s_new = _gdn_decode(qkv, gz, small, gconv0[0], s0[0], hyb["conv_gdn"],
                                   hyb["gate_gdn"], gdn_norm[0])
        o_ssd, h_new = _ssd_decode(xbc, sz, small, sconv0[0],
                                   h0[0].reshape(b, n_pair, 2 * SSM_P, SSM_N), hyb["conv_ssm"],
                                   conv_ssm_b[0], hyb["gate_ssm"], ssm_norm[0])
        gconv = jnp.concatenate([gconv0[0][:, 1:], r3(qkv)], axis=1)
        sconv = jnp.concatenate([sconv0[0][:, 1:], r3(xbc)], axis=1)
    else:
        r3 = lambda a: a.reshape(b, t, a.shape[-1])
        o_gdn, s_new = _gdn_prompt(r3(qkv), r3(gz), r3(small), hyb["conv_gdn"], hyb["gate_gdn"],
                                   gdn_norm[0], GDN_CHUNKS)
        o_ssd, h_new = _ssd_prompt(r3(xbc), r3(sz), r3(small), hyb["conv_ssm"], conv_ssm_b[0],
                                   hyb["gate_ssm"], ssm_norm[0], SSD_CHUNKS)
        gconv = r3(qkv)[:, t - (CONV_W - 1):]
        sconv = r3(xbc)[:, t - (CONV_W - 1):]
    h_new = h_new.reshape(b, SSM_HEADS, SSM_P, SSM_N)
    wo = w_out_a[0].astype(BF16)
    x1 = _res_ffn(xf, [o_gdn.reshape(rows, GDN_VW), o_ssd.reshape(rows, SSM_INNER)],
                  [wo[:GDN_VW], wo[GDN_VW:]], norm_ffn[0], 0, *ffn, tm_f)

    tm_c = _row_tile(rows, MLA_ROW_TILE)
    q_scale = QK_HEAD ** -0.5 * (1.0 if decode else math.log2(math.e))
    q, k, v, mla_rows = _mla_proj(x1, t, pos0, mla, tm_c, F32 if decode else BF16, q_scale)
    if decode:
        q_ext = jnp.transpose(_absorb(q, mla["g_abs"], mla["w_abs"]), (1, 0, 2))
        ctx = _paged_attention(q_ext, mla_rows.reshape(b, 1, MLA_ROW),
                               jnp.swapaxes(cache_mla[0], 1, 2), page_table)
        attn = _ctx_out(jnp.transpose(ctx, (1, 0, 2)), mla["w_pairs"])
    else:
        r3 = lambda a: a.reshape(b, t, a.shape[-1])
        attn = _flash_attention(r3(q), r3(k), r3(v), _row_tile(t, ATTN_TILE),
                                ATTN_HEADS).reshape(rows, -1)
    x2 = _res_ffn(x1, [attn], [w_out_c[0].astype(BF16)], norm_ffn[1], 1, *ffn, tm_f)
    return (x2.reshape(b, t, d), s_new[None], gconv[None], h_new[None], sconv[None],
            mla_rows.reshape(1, b, t, MLA_ROW))


def kernel(x_prompt, x_sample, state_gdn, state_gdn_conv, state_ssm, state_ssm_conv, cache_mla, page_table, norm_mix_a, w_in_a, conv_gdn_w, gdn_A_log, gdn_dt_bias, gdn_norm, conv_ssm_w, conv_ssm_b, ssm_A_log, ssm_dt_bias, ssm_D, ssm_norm, w_out_a, norm_mix_c, w_in_c, q_a_norm, kv_a_norm, w_uq, w_uk, w_uv, q_norm, k_norm, w_out_c, norm_ffn, w_gate_up, w_down):
    assert w_in_a.shape[0] == 1 and w_in_c.shape[0] == 1 and norm_ffn.shape[0] == 2
    hyb = _prep_hybrid(0, w_in_a, conv_gdn_w, gdn_A_log, gdn_dt_bias, conv_ssm_w, ssm_A_log,
                       ssm_dt_bias, ssm_D)
    mla = _prep_mla(0, norm_mix_c, w_in_c, q_a_norm, kv_a_norm, w_uq, w_uk, w_uv, q_norm, k_norm)
    ffn = _prep_ffn(w_gate_up, w_down, FF_TILE)
    shared = (hyb, mla, ffn, norm_mix_a, gdn_norm, conv_ssm_b, ssm_norm, w_out_a, w_out_c, norm_ffn)
    prompt = _trunk(x_prompt, 0, None, None, None, *shared)
    past_len = page_table.shape[1] * PAGE_SIZE
    sample = _trunk(x_sample, past_len, (state_gdn, state_gdn_conv, state_ssm, state_ssm_conv),
                    cache_mla, page_table, *shared)
    return (prompt[0], sample[0]) + prompt[1:] + sample[1:]
```

```python
import functools
import math

import jax
import jax.numpy as jnp
import numpy as np
from jax import lax
from jax.experimental import pallas as pl
from jax.experimental.pallas import tpu as pltpu

F32 = jnp.float32
BF16 = jnp.bfloat16

EPS = 1e-6
CONV_W = 4
CHUNK = 64
PAGE_SIZE = 128
GDN_HEADS = 4
GDN_DK = 128
GDN_DV = 128
GDN_QK = GDN_HEADS * GDN_DK
GDN_VW = GDN_HEADS * GDN_DV
GDN_CONV_CH = 2 * GDN_QK + GDN_VW
SSM_HEADS = 8
SSM_P = 64
SSM_N = 128
SSM_GROUPS = 2
SSM_INNER = SSM_HEADS * SSM_P
SSM_BC = SSM_GROUPS * SSM_N
SSM_CONV_CH = SSM_INNER + 2 * SSM_BC
MLA_HEADS = 16
Q_LORA = 512
KV_LORA = 256
QK_NOPE = 64
QK_ROPE = 32
QK_HEAD = QK_NOPE + QK_ROPE
V_HEAD = 64
MLA_ROW = KV_LORA + QK_ROPE + MLA_HEADS
ROPE_THETA = 10000.0

LANES = 128
SUBLANES = 8
HEAD_W = MLA_HEADS * LANES
GATE_W = LANES
GATE_A, GATE_B, GATE_DT = 0, GDN_HEADS, 2 * GDN_HEADS
Q_EXT = 3 * LANES
VMEM_LIMIT = 48 * 1024 * 1024

_HI = lax.Precision.HIGHEST


def _cparams(*sem):
    return pltpu.CompilerParams(dimension_semantics=sem, vmem_limit_bytes=VMEM_LIMIT)


def _const_spec(shape):
    zeros = (0,) * len(shape)
    return pl.BlockSpec(shape, lambda *_: zeros, pipeline_mode=pl.Buffered(1))


def _mm(a, b):
    return jnp.dot(a.astype(BF16), b.astype(BF16), preferred_element_type=F32)


def _mm_nt(a, b):
    return lax.dot_general(a.astype(BF16), b.astype(BF16), (((1,), (1,)), ((), ())),
                           preferred_element_type=F32)


def _mm_tn(a, b):
    return lax.dot_general(a.astype(BF16), b.astype(BF16), (((0,), (0,)), ((), ())),
                           preferred_element_type=F32)


def _sigmoid(x):
    return 1.0 / (1.0 + jnp.exp(-x))


def _silu(x):
    return x * _sigmoid(x)


def _softplus(x):
    return jnp.maximum(x, 0.0) + jnp.log(1.0 + jnp.exp(-jnp.abs(x)))


def _rms(x, gain):
    return x * lax.rsqrt(jnp.mean(x * x, axis=-1, keepdims=True) + EPS) * gain


def _l2n(x):
    return x * lax.rsqrt(jnp.sum(x * x, axis=-1, keepdims=True) + EPS)


def _iota(shape, dim):
    return lax.broadcasted_iota(jnp.int32, shape, dim)


def _chunk_cumsum(g, tb):
    row, col = _iota((tb, tb), 0), _iota((tb, tb), 1)
    shift = int(math.log2(CHUNK))
    same_chunk = lax.shift_right_logical(row, shift) == lax.shift_right_logical(col, shift)
    tril = jnp.where((col <= row) & same_chunk, 1.0, 0.0).astype(F32)
    return jnp.dot(tril, g, precision=_HI, preferred_element_type=F32)


def _rows_of(cols, lane0):
    tile = (SUBLANES, LANES)
    sel = jnp.where(_iota(tile, 1) == _iota(tile, 0) + lane0, 1.0, 0.0).astype(F32)
    return lax.dot_general(sel, cols, (((1,), (1,)), ((), ())), precision=_HI,
                           preferred_element_type=F32)


def _causal_conv_block(x_ref, xp_ref, cw, tb, first):
    @pl.when(first)
    def _():
        xp_ref[0:SUBLANES, :] = jnp.zeros((SUBLANES, xp_ref.shape[1]), F32)

    xp_ref[SUBLANES:SUBLANES + tb, :] = x_ref[...]
    off = SUBLANES - (CONV_W - 1)
    y = cw[0:1] * xp_ref[off:off + tb, :]
    for i in range(1, CONV_W):
        y = y + cw[i:i + 1] * xp_ref[off + i:off + i + tb, :]
    xp_ref[0:SUBLANES, :] = xp_ref[tb:tb + SUBLANES, :]
    return y


def _norm_proj_body(n_out, x_ref, g_ref, *refs):
    xn = _rms(x_ref[...], g_ref[...]).astype(BF16)
    for w_ref, o_ref in zip(refs[:n_out], refs[n_out:]):
        o_ref[...] = jnp.dot(xn, w_ref[...], preferred_element_type=F32)


def _norm_proj(x, gain, weights, tm):
    rows, d = x.shape
    n = len(weights)
    return pl.pallas_call(
        functools.partial(_norm_proj_body, n),
        grid=(rows // tm,),
        in_specs=[pl.BlockSpec((tm, d), lambda i: (i, 0)), _const_spec((1, d))]
        + [_const_spec(w.shape) for w in weights],
        out_specs=[pl.BlockSpec((tm, w.shape[1]), lambda i: (i, 0)) for w in weights],
        out_shape=[jax.ShapeDtypeStruct((rows, w.shape[1]), F32) for w in weights],
        compiler_params=_cparams("parallel"),
        name="norm_proj",
    )(x, gain.reshape(1, d), *weights)


def _gdn_prompt_body(nc, qkv_ref, gz_ref, sm_ref, cw_ref, gp_ref, gn_ref, o_ref, s_ref, xp_ref):
    tb = nc * CHUNK
    first = pl.program_id(1) == 0

    @pl.when(first)
    def _():
        s_ref[...] = jnp.zeros(s_ref.shape, F32)

    y = _silu(_causal_conv_block(qkv_ref.at[0], xp_ref, cw_ref[...], tb, first))
    sm, gp = sm_ref[0], gp_ref[...]
    g_all = -jnp.exp(gp[0:1]) * _softplus(sm + gp[1:2])
    beta_all = _sigmoid(sm)
    cum_all = _chunk_cumsum(g_all, tb)
    cum_rows = _rows_of(cum_all, GATE_A)
    incl = _iota((CHUNK, CHUNK), 1) <= _iota((CHUNK, CHUNK), 0)
    strict = _iota((CHUNK, CHUNK), 1) < _iota((CHUNK, CHUNK), 0)
    gn = gn_ref[...]
    pairs = [(c, h) for c in range(nc) for h in range(GDN_HEADS)]
    qs, ks, vs = [], [], []
    for h in range(GDN_HEADS):
        lo = h * GDN_DK
        qs.append(_l2n(y[:, lo:lo + GDN_DK]) * GDN_DK ** -0.5)
        ks.append(_l2n(y[:, GDN_QK + lo:GDN_QK + lo + GDN_DK]))
        vs.append(y[:, 2 * GDN_QK + lo:2 * GDN_QK + lo + GDN_DV])
    pre = []
    for c, h in pairs:
        r = slice(c * CHUNK, (c + 1) * CHUNK)
        q, k, v = qs[h][r], ks[h][r], vs[h][r]
        cum_c = cum_all[r, GATE_A + h:GATE_A + h + 1]
        cum_r = cum_rows[h:h + 1, c * CHUNK:(c + 1) * CHUNK]
        beta = beta_all[r, GATE_B + h:GATE_B + h + 1]
        decay = jnp.where(incl, jnp.exp(jnp.minimum(cum_c - cum_r, 0.0)), 0.0)
        e_c = jnp.exp(cum_c)
        kb = k * beta
        last = cum_c[CHUNK - 1:CHUNK]
        pre.append(dict(q=q.astype(BF16), k=k.astype(BF16), kb=kb, decay=decay,
                        rhs=jnp.concatenate([v * beta, kb * e_c], axis=1).astype(BF16),
                        qe=q * e_c, kd=(k * jnp.exp(last - cum_c)).astype(BF16),
                        e_last=jnp.exp(last)))
    lows = [jnp.where(strict, _mm_nt(p["kb"], p["k"]) * p["decay"], 0.0) for p in pre]
    attns = [(_mm_nt(p["q"], p["k"]) * p["decay"]).astype(BF16) for p in pre]
    row, col = _iota((CHUNK, CHUNK), 0), _iota((CHUNK, CHUNK), 1)
    invs = None
    for level in range(int(math.log2(CHUNK))):
        brow = lax.shift_right_logical(row, level)
        joins = ((brow & 1) == 1) & (lax.shift_right_logical(col, level) == brow - 1)
        subs = [jnp.where(joins, low, 0.0) for low in lows]
        if invs is None:
            eye = jnp.where(row == col, 1.0, 0.0).astype(F32)
            invs = [eye - sub for sub in subs]
        else:
            invs = [inv - _mm(_mm(inv, sub), inv) for inv, sub in zip(invs, subs)]
    sols = [_mm(inv, p["rhs"]).astype(BF16) for inv, p in zip(invs, pre)]
    k_sol = [_mm_tn(p["kd"], sol) for p, sol in zip(pre, sols)]
    a_sol = [_mm(attn, sol) for attn, sol in zip(attns, sols)]
    q_eff = [(p["qe"] - a[:, GDN_DV:]).astype(BF16) for p, a in zip(pre, a_sol)]
    m_eff = [ks[:, GDN_DV:].astype(BF16) for ks in k_sol]
    states = [s_ref[0, h] for h in range(GDN_HEADS)]
    for c in range(nc):
        r = slice(c * CHUNK, (c + 1) * CHUNK)
        idx = [c * GDN_HEADS + h for h in range(GDN_HEADS)]
        s_bf = [st.astype(BF16) for st in states]
        outs = [_mm(q_eff[i], s_bf[h]) + a_sol[i][:, :GDN_DV] for h, i in enumerate(idx)]
        states = [states[h] * pre[i]["e_last"] - _mm(m_eff[i], s_bf[h]) + k_sol[i][:, :GDN_DV]
                  for h, i in enumerate(idx)]
        for h, o in enumerate(outs):
            lo = h * GDN_DV
            gate = _silu(gz_ref[0, r, lo:lo + GDN_DV])
            o_ref[0, r, lo:lo + GDN_DV] = (_rms(o, gn) * gate).astype(o_ref.dtype)
    for h in range(GDN_HEADS):
        s_ref[0, h] = states[h]


def _gdn_prompt(qkv, gz, small, conv_w, gate_par, gdn_norm, nc):
    b, t, _ = qkv.shape
    tb = nc * CHUNK
    blk = lambda w: pl.BlockSpec((1, tb, w), lambda i, j: (i, j, 0))
    return pl.pallas_call(
        functools.partial(_gdn_prompt_body, nc),
        grid=(b, t // tb),
        in_specs=[blk(GDN_CONV_CH), blk(GDN_VW), blk(GATE_W), _const_spec(conv_w.shape),
                  _const_spec(gate_par.shape), _const_spec((1, GDN_DV))],
        out_specs=[blk(GDN_VW),
                   pl.BlockSpec((1, GDN_HEADS, GDN_DK, GDN_DV), lambda i, j: (i, 0, 0, 0))],
        out_shape=[jax.ShapeDtypeStruct((b, t, GDN_VW), BF16),
                   jax.ShapeDtypeStruct((b, GDN_HEADS, GDN_DK, GDN_DV), F32)],
        scratch_shapes=[pltpu.VMEM((tb + SUBLANES, GDN_CONV_CH), F32)],
        compiler_params=_cparams("parallel", "arbitrary"),
        name="gdn_prompt",
    )(qkv, gz, small, conv_w, gate_par, gdn_norm.reshape(1, GDN_DV))


def _ssd_prompt_body(nc, xbc_ref, sz_ref, sm_ref, cw_ref, cb_ref, sp_ref, sn_ref, o_ref, h_ref,
                     xp_ref):
    tb = nc * CHUNK
    first = pl.program_id(1) == 0

    @pl.when(first)
    def _():
        h_ref[...] = jnp.zeros(h_ref.shape, F32)

    y = _silu(_causal_conv_block(xbc_ref.at[0], xp_ref, cw_ref[...], tb, first) + cb_ref[...])
    sm, sp = sm_ref[0], sp_ref[...]
    dt_all = _softplus(sm + sp[1:2])
    cum_all = _chunk_cumsum(dt_all * -jnp.exp(sp[0:1]), tb)
    cum_rows = _rows_of(cum_all, GATE_DT)
    incl = _iota((CHUNK, CHUNK), 1) <= _iota((CHUNK, CHUNK), 0)
    lane_lo = _iota((CHUNK, LANES), 1) < SSM_P
    row_lo = _iota((2 * SSM_P, SSM_N), 0) < SSM_P
    par_lo = _iota((1, LANES), 1) < SSM_P
    sn = sn_ref[...]
    n_pair = SSM_HEADS // 2
    pairs_per_group = n_pair // SSM_GROUPS
    gw = SSM_INNER // SSM_GROUPS
    b_bf, c_bf, scores = {}, {}, {}
    for c in range(nc):
        r = slice(c * CHUNK, (c + 1) * CHUNK)
        for grp in range(SSM_GROUPS):
            lo = SSM_INNER + grp * SSM_N
            b_bf[c, grp] = y[r, lo:lo + SSM_N].astype(BF16)
            c_bf[c, grp] = y[r, lo + SSM_BC:lo + SSM_BC + SSM_N].astype(BF16)
            scores[c, grp] = _mm_nt(c_bf[c, grp], b_bf[c, grp])
    pre = {}
    for c in range(nc):
        r = slice(c * CHUNK, (c + 1) * CHUNK)
        for j in range(n_pair):
            grp = j // pairs_per_group
            cums, segs, dts = [], [], []
            for head in (2 * j, 2 * j + 1):
                cum_c = cum_all[r, GATE_DT + head:GATE_DT + head + 1]
                cum_r = cum_rows[head:head + 1, c * CHUNK:(c + 1) * CHUNK]
                cums.append(cum_c)
                segs.append(jnp.where(incl, jnp.exp(jnp.minimum(cum_c - cum_r, 0.0)), 0.0))
                dts.append(dt_all[r, GATE_DT + head:GATE_DT + head + 1])
            x_pair = y[r, j * LANES:(j + 1) * LANES]
            xdt = x_pair * jnp.where(lane_lo, dts[0], dts[1])
            last = [cm[CHUNK - 1:CHUNK] for cm in cums]
            dec = jnp.where(lane_lo, jnp.exp(last[0] - cums[0]), jnp.exp(last[1] - cums[1]))
            d_pair = jnp.where(par_lo, sp[2:3, GATE_DT + 2 * j:GATE_DT + 2 * j + 1],
                               sp[2:3, GATE_DT + 2 * j + 1:GATE_DT + 2 * j + 2])
            pre[c, j] = dict(
                y=(_mm(scores[c, grp] * segs[0], jnp.where(lane_lo, xdt, 0.0))
                   + _mm(scores[c, grp] * segs[1], jnp.where(lane_lo, 0.0, xdt))
                   + d_pair * x_pair),
                e=jnp.where(lane_lo, jnp.exp(cums[0]), jnp.exp(cums[1])),
                xdec=(xdt * dec).astype(BF16),
                e_last=jnp.where(row_lo, jnp.exp(last[0]), jnp.exp(last[1])))
    states = [h_ref[0, j] for j in range(n_pair)]
    for c in range(nc):
        r = slice(c * CHUNK, (c + 1) * CHUNK)
        ys = [pre[c, j]["y"] + _mm_nt(c_bf[c, j // pairs_per_group], states[j]) * pre[c, j]["e"]
              for j in range(n_pair)]
        states = [states[j] * pre[c, j]["e_last"]
                  + _mm_tn(pre[c, j]["xdec"], b_bf[c, j // pairs_per_group]) for j in range(n_pair)]
        for grp in range(SSM_GROUPS):
            outs = [ys[j] * _silu(sz_ref[0, r, j * LANES:(j + 1) * LANES])
                    for j in range(grp * pairs_per_group, (grp + 1) * pairs_per_group)]
            inv = lax.rsqrt(sum(jnp.sum(o * o, axis=-1, keepdims=True) for o in outs) / gw + EPS)
            for jj, o in enumerate(outs):
                lo = grp * gw + jj * LANES
                o_ref[0, r, lo:lo + LANES] = (o * inv * sn[:, lo:lo + LANES]).astype(o_ref.dtype)
    for j in range(n_pair):
        h_ref[0, j] = states[j]


def _ssd_prompt(xbc, sz, small, conv_w, conv_b, ssm_par, ssm_norm, nc):
    b, t, _ = xbc.shape
    tb = nc * CHUNK
    blk = lambda w: pl.BlockSpec((1, tb, w), lambda i, j: (i, j, 0))
    n_pair = SSM_HEADS // 2
    return pl.pallas_call(
        functools.partial(_ssd_prompt_body, nc),
        grid=(b, t // tb),
        in_specs=[blk(SSM_CONV_CH), blk(SSM_INNER), blk(GATE_W), _const_spec(conv_w.shape),
                  _const_spec((1, SSM_CONV_CH)), _const_spec(ssm_par.shape),
                  _const_spec((1, SSM_INNER))],
        out_specs=[blk(SSM_INNER),
                   pl.BlockSpec((1, n_pair, 2 * SSM_P, SSM_N), lambda i, j: (i, 0, 0, 0))],
        out_shape=[jax.ShapeDtypeStruct((b, t, SSM_INNER), BF16),
                   jax.ShapeDtypeStruct((b, n_pair, 2 * SSM_P, SSM_N), F32)],
        scratch_shapes=[pltpu.VMEM((tb + SUBLANES, SSM_CONV_CH), F32)],
        compiler_params=_cparams("parallel", "arbitrary"),
        name="ssd_prompt",
    )(xbc, sz, small, conv_w, conv_b.reshape(1, -1), ssm_par, ssm_norm.reshape(1, -1))


def _col_bcast(row):
    return jnp.transpose(jnp.broadcast_to(row, (LANES, LANES)))


def _decode_conv(new_ref, cv_ref, cw):
    y = cw[CONV_W - 1:CONV_W] * new_ref[...]
    for i in range(CONV_W - 1):
        y = y + cw[i:i + 1] * cv_ref[:, i, :]
    return y


def _gdn_decode_body(bb, qkv_ref, gz_ref, sm_ref, cv_ref, s0_ref, cw_ref, gp_ref, gn_ref, o_ref,
                     s_ref, raw_ref):
    y = _silu(_decode_conv(qkv_ref, cv_ref, cw_ref[...]))
    sm, gp = sm_ref[...], gp_ref[...]
    a_all = jnp.exp(-jnp.exp(gp[0:1]) * _softplus(sm + gp[1:2]))
    beta_all = _sigmoid(sm)
    qs, ks, vs = [], [], []
    for h in range(GDN_HEADS):
        lo = h * GDN_DK
        qs.append(_l2n(y[:, lo:lo + GDN_DK]) * GDN_DK ** -0.5)
        ks.append(_l2n(y[:, GDN_QK + lo:GDN_QK + lo + GDN_DK]))
        vs.append(y[:, 2 * GDN_QK + lo:2 * GDN_QK + lo + GDN_DV])
    pairs = [(b, h) for b in range(bb) for h in range(GDN_HEADS)]
    k_col = [_col_bcast(ks[h][b:b + 1]) for b, h in pairs]
    q_col = [_col_bcast(qs[h][b:b + 1]) for b, h in pairs]
    decayed = [s0_ref[b, h] * a_all[b:b + 1, GATE_A + h:GATE_A + h + 1] for b, h in pairs]
    v_new = [beta_all[b:b + 1, GATE_B + h:GATE_B + h + 1]
             * (vs[h][b:b + 1] - jnp.sum(k_col[i] * decayed[i], axis=0, keepdims=True))
             for i, (b, h) in enumerate(pairs)]
    for i, (b, h) in enumerate(pairs):
        state = decayed[i] + k_col[i] * v_new[i]
        s_ref[b, h] = state
        raw_ref[b:b + 1, h * GDN_DV:(h + 1) * GDN_DV] = jnp.sum(q_col[i] * state, axis=0,
                                                                keepdims=True)
    gn = gn_ref[...]
    for h in range(GDN_HEADS):
        hs = slice(h * GDN_DV, (h + 1) * GDN_DV)
        o_ref[:, hs] = (_rms(raw_ref[:, hs], gn) * _silu(gz_ref[:, hs])).astype(o_ref.dtype)


DECODE_BLOCK = SUBLANES


def _gdn_decode(qkv, gz, small, conv0, s0, conv_w, gate_par, gdn_norm):
    b = qkv.shape[0]
    bb = DECODE_BLOCK
    row = lambda w: pl.BlockSpec((bb, w), lambda i: (i, 0))
    st = pl.BlockSpec((bb, GDN_HEADS, GDN_DK, GDN_DV), lambda i: (i, 0, 0, 0))
    return pl.pallas_call(
        functools.partial(_gdn_decode_body, bb),
        grid=(b // bb,),
        in_specs=[row(GDN_CONV_CH), row(GDN_VW), row(GATE_W),
                  pl.BlockSpec((bb, CONV_W - 1, GDN_CONV_CH), lambda i: (i, 0, 0)), st,
                  _const_spec(conv_w.shape), _const_spec(gate_par.shape), _const_spec((1, GDN_DV))],
        out_specs=[row(GDN_VW), st],
        out_shape=[jax.ShapeDtypeStruct((b, GDN_VW), F32), jax.ShapeDtypeStruct(s0.shape, F32)],
        scratch_shapes=[pltpu.VMEM((bb, GDN_VW), F32)],
        compiler_params=_cparams("parallel"),
        name="gdn_decode",
    )(qkv, gz, small, conv0, s0, conv_w, gate_par, gdn_norm.reshape(1, GDN_DV))


def _ssd_decode_body(bb, xbc_ref, sz_ref, sm_ref, cv_ref, h0_ref, cw_ref, cb_ref, sp_ref, sn_ref,
                     o_ref, h_ref, raw_ref):
    y = _silu(_decode_conv(xbc_ref, cv_ref, cw_ref[...]) + cb_ref[...])
    sm, sp = sm_ref[...], sp_ref[...]
    dt_all = _softplus(sm + sp[1:2])
    da_all = jnp.exp(dt_all * -jnp.exp(sp[0:1]))
    par_lo = _iota((1, LANES), 1) < SSM_P
    row_lo = _iota((2 * SSM_P, SSM_N), 0) < SSM_P
    n_pair = SSM_HEADS // 2
    pairs_per_group = n_pair // SSM_GROUPS
    pairs = [(b, j) for b in range(bb) for j in range(n_pair)]
    x_col, states = [], []
    for b, j in pairs:
        la, lb = GATE_DT + 2 * j, GATE_DT + 2 * j + 1
        row = slice(b, b + 1)
        xdt = y[row, j * LANES:(j + 1) * LANES] * jnp.where(par_lo, dt_all[row, la:la + 1],
                                                           dt_all[row, lb:lb + 1])
        x_col.append(_col_bcast(xdt))
    for i, (b, j) in enumerate(pairs):
        la, lb = GATE_DT + 2 * j, GATE_DT + 2 * j + 1
        row = slice(b, b + 1)
        grp = j // pairs_per_group
        b_g = y[row, SSM_INNER + grp * SSM_N:SSM_INNER + (grp + 1) * SSM_N]
        state = (h0_ref[b, j] * jnp.where(row_lo, da_all[row, la:la + 1], da_all[row, lb:lb + 1])
                 + x_col[i] * b_g)
        h_ref[b, j] = state
        states.append(state)
    lane_id = _iota((LANES, LANES), 1)
    for j in range(n_pair):
        lo = SSM_INNER + SSM_BC + (j // pairs_per_group) * SSM_N
        cols = jnp.zeros((LANES, LANES), F32)
        for b in range(bb):
            y_col = jnp.sum(states[b * n_pair + j] * y[b:b + 1, lo:lo + SSM_N], axis=1,
                            keepdims=True)
            cols = jnp.where(lane_id == b, y_col, cols)
        raw_ref[:, j * LANES:(j + 1) * LANES] = jnp.transpose(cols)[0:bb]
    sn = sn_ref[...]
    gw = SSM_INNER // SSM_GROUPS
    for grp in range(SSM_GROUPS):
        outs = []
        for jj in range(pairs_per_group):
            j = grp * pairs_per_group + jj
            la, lb = GATE_DT + 2 * j, GATE_DT + 2 * j + 1
            js = slice(j * LANES, (j + 1) * LANES)
            d_pair = jnp.where(par_lo, sp[2:3, la:la + 1], sp[2:3, lb:lb + 1])
            outs.append((raw_ref[:, js] + d_pair * y[:, js]) * _silu(sz_ref[:, js]))
        inv = lax.rsqrt(sum(jnp.sum(o * o, axis=-1, keepdims=True) for o in outs) / gw + EPS)
        for jj, o in enumerate(outs):
            lo = grp * gw + jj * LANES
            o_ref[:, lo:lo + LANES] = (o * inv * sn[:, lo:lo + LANES]).astype(o_ref.dtype)


def _ssd_decode(xbc, sz, small, conv0, h0, conv_w, conv_b, ssm_par, ssm_norm):
    b = xbc.shape[0]
    bb = DECODE_BLOCK
    n_pair = SSM_HEADS // 2
    row = lambda w: pl.BlockSpec((bb, w), lambda i: (i, 0))
    st = pl.BlockSpec((bb, n_pair, 2 * SSM_P, SSM_N), lambda i: (i, 0, 0, 0))
    return pl.pallas_call(
        functools.partial(_ssd_decode_body, bb),
        grid=(b // bb,),
        in_specs=[row(SSM_CONV_CH), row(SSM_INNER), row(GATE_W),
                  pl.BlockSpec((bb, CONV_W - 1, SSM_CONV_CH), lambda i: (i, 0, 0)), st,
                  _const_spec(conv_w.shape), _const_spec((1, SSM_CONV_CH)),
                  _const_spec(ssm_par.shape), _const_spec((1, SSM_INNER))],
        out_specs=[row(SSM_INNER), st],
        out_shape=[jax.ShapeDtypeStruct((b, SSM_INNER), F32), jax.ShapeDtypeStruct(h0.shape, F32)],
        scratch_shapes=[pltpu.VMEM((bb, SSM_INNER), F32)],
        compiler_params=_cparams("parallel"),
        name="ssd_decode",
    )(xbc, sz, small, conv0, h0, conv_w, conv_b.reshape(1, -1), ssm_par, ssm_norm.reshape(1, -1))


def _res_ffn_body(n_a, n_ff, x_ref, *refs):
    a_refs, w_refs = refs[:n_a], refs[n_a:2 * n_a]
    g_ref, wgu_ref, wd_ref, o_ref, xn_ref = refs[2 * n_a:]
    ff_tile = wd_ref.shape[1]
    d_ff = n_ff * ff_tile
    x1 = x_ref[...]
    for a_ref, w_ref in zip(a_refs, w_refs):
        x1 = x1 + jnp.dot(a_ref[...].astype(BF16), w_ref[...], preferred_element_type=F32)
    xn_ref[...] = _rms(x1, g_ref[...]).astype(BF16)
    o_ref[...] = x1

    def step(c, carry):
        xn = xn_ref[...]
        lo = pl.multiple_of(c * ff_tile, ff_tile)
        gate = jnp.dot(xn, wgu_ref[:, pl.ds(lo, ff_tile)], preferred_element_type=F32)
        up = jnp.dot(xn, wgu_ref[:, pl.ds(d_ff + lo, ff_tile)], preferred_element_type=F32)
        hid = (_silu(gate) * up).astype(BF16)
        o_ref[...] += jnp.dot(hid, wd_ref[c], preferred_element_type=F32)
        return carry

    lax.fori_loop(0, n_ff, step, 0)


def _res_ffn(x, acts, projs, gain, layer, wgu, wd, tm):
    rows, d = x.shape
    n_a, n_ff = len(acts), wd.shape[1]

    def layer_spec(w):
        tail = (0,) * (w.ndim - 1)
        return pl.BlockSpec((None,) + w.shape[1:], lambda i: (layer,) + tail,
                            pipeline_mode=pl.Buffered(1))

    return pl.pallas_call(
        functools.partial(_res_ffn_body, n_a, n_ff),
        grid=(rows // tm,),
        in_specs=[pl.BlockSpec((tm, d), lambda i: (i, 0))]
        + [pl.BlockSpec((tm, a.shape[1]), lambda i: (i, 0)) for a in acts]
        + [_const_spec(p.shape) for p in projs]
        + [_const_spec((1, d)), layer_spec(wgu), layer_spec(wd)],
        out_specs=pl.BlockSpec((tm, d), lambda i: (i, 0)),
        out_shape=jax.ShapeDtypeStruct((rows, d), F32),
        scratch_shapes=[pltpu.VMEM((tm, d), BF16)],
        compiler_params=_cparams("parallel"),
        name="res_ffn",
    )(x, *acts, *projs, gain.reshape(1, d), wgu, wd)


def _mla_proj_body(tm, q_scale, x_ref, cos_ref, sin_ref, g_ref, wcq_ref, wckv_ref, wkr_ref,
                   qan_ref, kvn_ref, wuq_ref, wuk_ref, wuv_ref, vone_ref, gq_ref, gk_ref, gkr_ref,
                   segq_ref, segk_ref, q_ref, k_ref, v_ref, rows_ref):
    xn = _rms(x_ref[...], g_ref[...]).astype(BF16)
    lane = _iota((tm, LANES), 1)
    cos_t, sin_t = cos_ref[...], sin_ref[...]

    kr = jnp.dot(xn, wkr_ref[...], preferred_element_type=F32)
    ssq_kr = jnp.sum(jnp.where(lane < QK_ROPE, kr * kr, 0.0), axis=-1, keepdims=True)
    krg = kr * gkr_ref[...]
    kr_rot = krg * cos_t + pltpu.roll(krg, LANES - QK_ROPE, 1) * sin_t

    c = _rms(jnp.dot(xn, wckv_ref[...], preferred_element_type=F32), kvn_ref[...])
    cb = c.astype(BF16)
    kx = jnp.dot(cb, wuk_ref[...], preferred_element_type=F32)
    ssq_k = jnp.dot((kx * kx).astype(BF16), segk_ref[...],
                    preferred_element_type=F32)
    inv_r = lax.rsqrt((ssq_k + ssq_kr) / QK_HEAD + EPS)
    tail = jnp.where(lane < QK_ROPE, kr_rot, inv_r)
    rows_ref[:, 0:KV_LORA] = c
    rows_ref[:, KV_LORA:MLA_ROW] = tail[:, 0:MLA_ROW - KV_LORA]
    kr_put = jnp.where((lane >= QK_NOPE) & (lane < QK_HEAD), pltpu.roll(kr_rot, QK_NOPE, 1), 0.0)
    gk = gk_ref[...]
    for h in range(MLA_HEADS):
        hs = slice(h * LANES, (h + 1) * LANES)
        inv_h = inv_r[:, QK_ROPE + h:QK_ROPE + h + 1]
        k_ref[:, hs] = ((kx[:, hs] * gk + kr_put) * inv_h).astype(k_ref.dtype)
    v_ref[...] = (jnp.dot(cb, wuv_ref[...], preferred_element_type=F32)
                  + vone_ref[...]).astype(v_ref.dtype)

    cq = _rms(jnp.dot(xn, wcq_ref[...], preferred_element_type=F32), qan_ref[...]).astype(BF16)
    qx = jnp.dot(cq, wuq_ref[...], preferred_element_type=F32)
    ssq_q = jnp.dot((qx * qx).astype(BF16), segq_ref[...], preferred_element_type=F32)
    inv_q = lax.rsqrt(ssq_q / QK_HEAD + EPS)
    gq = gq_ref[...]
    scale = q_scale
    in_rope = (lane >= QK_NOPE) & (lane < QK_HEAD)
    keep = jnp.where(lane < QK_NOPE, scale, jnp.where(in_rope, cos_t * scale, 0.0))
    swap = jnp.where(in_rope, sin_t * scale, 0.0)
    for h in range(MLA_HEADS):
        hs = slice(h * LANES, (h + 1) * LANES)
        t = qx[:, hs] * inv_q[:, h:h + 1] * gq
        q_ref[:, hs] = (t * keep + pltpu.roll(t, LANES - QK_ROPE, 1) * swap).astype(q_ref.dtype)


def _mla_proj(x, seq, pos0, w, tm, q_dtype, q_scale):
    rows, d = x.shape
    half = QK_ROPE // 2
    inv_freq = ROPE_THETA ** (-(jnp.arange(LANES) % half).astype(F32) / half)
    ang = (pos0 + jnp.arange(seq, dtype=F32))[:, None] * inv_freq[None, :]
    tr = tm if seq > 1 else 1
    nblk = seq // tr
    rope_spec = pl.BlockSpec((tr, LANES), lambda i: (i % nblk, 0))
    names = ("norm", "wcq", "wckv", "wkr", "qan", "kvn", "wuq", "wuk", "wuv", "vone", "gq", "gk",
             "gkr", "segq", "segk")
    consts = [w[n] for n in names]
    out_w = (HEAD_W, HEAD_W, HEAD_W, MLA_ROW)
    out_dt = (q_dtype, BF16, BF16, F32)
    return pl.pallas_call(
        functools.partial(_mla_proj_body, tm, q_scale),
        grid=(rows // tm,),
        in_specs=[pl.BlockSpec((tm, d), lambda i: (i, 0)), rope_spec, rope_spec]
        + [_const_spec(c.shape) for c in consts],
        out_specs=[pl.BlockSpec((tm, ow), lambda i: (i, 0)) for ow in out_w],
        out_shape=[jax.ShapeDtypeStruct((rows, ow), dt) for ow, dt in zip(out_w, out_dt)],
        compiler_params=_cparams("parallel"),
        name="mla_proj",
    )(x, jnp.cos(ang), jnp.sin(ang), *consts)


def _flash_body(tq, nh, q_ref, k_ref, v_ref, o_ref):
    qi = pl.program_id(2)
    half = tq // 2
    heads = [slice(j * LANES, (j + 1) * LANES) for j in range(nh)]
    q = [q_ref[0, :, hs] for hs in heads]

    def attend(carry, qs, kv_rows, mask):
        s = [lax.dot_general(qs[j], k_ref[0, kv_rows, hs], (((1,), (1,)), ((), ())),
                             preferred_element_type=F32) for j, hs in enumerate(heads)]
        if mask is not None:
            s = [jnp.where(mask, sj, -jnp.inf) for sj in s]
        m_new = [jnp.maximum(carry[j][0], jnp.max(s[j], axis=-1, keepdims=True)) for j in range(nh)]
        p = [jnp.exp2((s[j] - m_new[j]).astype(BF16)) for j in range(nh)]
        pv = [jnp.dot(p[j], v_ref[0, kv_rows, hs], preferred_element_type=F32)
              for j, hs in enumerate(heads)]
        return tuple((m_new[j], jnp.exp2(carry[j][0] - m_new[j]) * carry[j][1] + pv[j])
                     for j in range(nh))

    init = tuple((jnp.full((tq, 1), -jnp.inf, F32), jnp.zeros((tq, LANES), F32)) for _ in range(nh))
    carry = lax.fori_loop(
        0, 2 * qi, lambda kb, cr: attend(cr, q, pl.ds(pl.multiple_of(kb * half, half), half), None),
        init)
    base = pl.multiple_of(qi * tq, tq)
    carry = attend(carry, q, pl.ds(base, half),
                   _iota((tq, half), 1) <= _iota((tq, half), 0))
    lower = attend(tuple((m[half:], acc[half:]) for m, acc in carry), [qj[half:] for qj in q],
                   pl.ds(base + half, half), _iota((half, half), 1) <= _iota((half, half), 0))
    low = _iota((half, LANES), 1) < V_HEAD
    for part, rows in ((tuple((m[:half], acc[:half]) for m, acc in carry), slice(0, half)),
                       (lower, slice(half, tq))):
        outs = [acc / acc[:, V_HEAD:V_HEAD + 1] for _, acc in part]
        for j in range(nh // 2):
            o_ref[0, rows, j * LANES:(j + 1) * LANES] = jnp.where(
                low, outs[2 * j], pltpu.roll(outs[2 * j + 1], V_HEAD, 1)).astype(o_ref.dtype)


def _flash_attention(q, k, v, tq, nh):
    b, t, _ = q.shape
    kv_spec = pl.BlockSpec((1, t, nh * LANES), lambda i, j, n: (i, 0, j))
    return pl.pallas_call(
        functools.partial(_flash_body, tq, nh),
        grid=(b, MLA_HEADS // nh, t // tq),
        in_specs=[pl.BlockSpec((1, tq, nh * LANES), lambda i, j, n: (i, n, j)), kv_spec, kv_spec],
        out_specs=pl.BlockSpec((1, tq, nh * V_HEAD), lambda i, j, n: (i, n, j)),
        out_shape=jax.ShapeDtypeStruct((b, t, MLA_HEADS * V_HEAD), BF16),
        compiler_params=_cparams("parallel", "parallel", "arbitrary"),
        name="flash_attention",
    )(q, k, v)


def _absorb_body(q_ref, g_ref, w_ref, o_ref):
    t = (q_ref[...] * g_ref[...]).astype(BF16)
    o_ref[...] = jnp.dot(t, w_ref[...], preferred_element_type=F32).astype(o_ref.dtype)


def _absorb(q, gain, w_abs):
    b = q.shape[0]
    return pl.pallas_call(
        _absorb_body,
        grid=(MLA_HEADS,),
        in_specs=[pl.BlockSpec((b, LANES), lambda h: (0, h)), _const_spec((1, LANES)),
                  pl.BlockSpec((None, LANES, Q_EXT), lambda h: (h, 0, 0))],
        out_specs=pl.BlockSpec((None, b, Q_EXT), lambda h: (h, 0, 0)),
        out_shape=jax.ShapeDtypeStruct((MLA_HEADS, b, Q_EXT), BF16),
        compiler_params=_cparams("parallel"),
        name="mla_absorb",
    )(q, gain, w_abs)


PAGE_SLOTS = 4


def _paged_attn_body(n_pages, pt_ref, q_ref, rn_ref, cache_ref, o_ref, buf_ref, sem_ref, s_ref,
                     pb_ref):
    i = pl.program_id(0)
    n_seq = pl.num_programs(0)
    slot = lax.rem(i, PAGE_SLOTS)

    def page_copy(seq, p, sl):
        return pltpu.make_async_copy(cache_ref.at[pt_ref[seq, p]], buf_ref.at[sl, p],
                                     sem_ref.at[sl])

    def fetch(seq):
        @pl.when(seq < n_seq)
        def _():
            for p in range(n_pages):
                page_copy(seq, p, lax.rem(seq, PAGE_SLOTS)).start()

    @pl.when(i == 0)
    def _():
        for ahead in range(PAGE_SLOTS - 1):
            fetch(ahead)

    fetch(i + PAGE_SLOTS - 1)

    for p in range(n_pages):
        page_copy(i, p, slot).wait()

    q = q_ref[0]
    inv_lo = MLA_ROW - MLA_HEADS
    for p in range(n_pages):
        page = buf_ref[slot, p]
        cols = slice(p * PAGE_SIZE, (p + 1) * PAGE_SIZE)
        pb_ref[:, cols] = page.astype(BF16)
        s_ref[:, cols] = page[inv_lo:MLA_ROW]
    s_all = jnp.dot(q[:, :MLA_ROW], pb_ref[...], preferred_element_type=F32) * s_ref[...]
    rn = rn_ref[0]
    qf = q[:, :MLA_ROW].astype(F32)
    pick = _iota((MLA_HEADS, MLA_ROW), 1) == _iota((MLA_HEADS, MLA_ROW), 0) + inv_lo
    inv_new = jnp.sum(jnp.where(pick, rn, 0.0), axis=-1, keepdims=True)
    s_new = jnp.sum(qf * rn, axis=-1, keepdims=True) * inv_new
    m = jnp.maximum(jnp.max(s_all, axis=-1, keepdims=True), s_new)
    e_new = jnp.exp(s_new - m)
    e = jnp.exp(s_all - m)
    den = e_new + jnp.sum(e, axis=-1, keepdims=True)
    ctx = e_new * rn[:, 0:KV_LORA] + lax.dot_general(
        e.astype(BF16), pb_ref[0:KV_LORA, :], (((1,), (1,)), ((), ())), preferred_element_type=F32)
    o_ref[0] = ctx / den


def _paged_attention(q_ext, rows_new, cache_t, page_table):
    b, n_pages = page_table.shape
    grid_spec = pltpu.PrefetchScalarGridSpec(
        num_scalar_prefetch=1,
        grid=(b,),
        in_specs=[pl.BlockSpec((1, MLA_HEADS, Q_EXT), lambda i, pt: (i, 0, 0)),
                  pl.BlockSpec((1, 1, MLA_ROW), lambda i, pt: (i, 0, 0)),
                  pl.BlockSpec(memory_space=pl.ANY)],
        out_specs=pl.BlockSpec((1, MLA_HEADS, KV_LORA), lambda i, pt: (i, 0, 0)),
        scratch_shapes=[pltpu.VMEM((PAGE_SLOTS, n_pages, MLA_ROW, PAGE_SIZE), F32),
                        pltpu.SemaphoreType.DMA((PAGE_SLOTS,)),
                        pltpu.VMEM((MLA_HEADS, n_pages * PAGE_SIZE), F32),
                        pltpu.VMEM((MLA_ROW, n_pages * PAGE_SIZE), BF16)],
    )
    return pl.pallas_call(
        functools.partial(_paged_attn_body, n_pages),
        grid_spec=grid_spec,
        out_shape=jax.ShapeDtypeStruct((b, MLA_HEADS, KV_LORA), F32),
        compiler_params=_cparams("arbitrary"),
        name="paged_attention",
    )(page_table, q_ext, rows_new, cache_t)


def _ctx_out_body(c_ref, w_ref, o_ref):
    o_ref[...] = (jnp.dot(c_ref[0].astype(BF16), w_ref[0], preferred_element_type=F32)
                  + jnp.dot(c_ref[1].astype(BF16), w_ref[1], preferred_element_type=F32)
                  ).astype(o_ref.dtype)


def _ctx_out(ctx_t, w_pairs):
    b = ctx_t.shape[1]
    return pl.pallas_call(
        _ctx_out_body,
        grid=(MLA_HEADS // 2,),
        in_specs=[pl.BlockSpec((2, b, KV_LORA), lambda j: (j, 0, 0)),
                  pl.BlockSpec((None, 2, KV_LORA, LANES), lambda j: (j, 0, 0, 0))],
        out_specs=pl.BlockSpec((b, LANES), lambda j: (0, j)),
        out_shape=jax.ShapeDtypeStruct((b, MLA_HEADS * V_HEAD), BF16),
        compiler_params=_cparams("parallel"),
        name="mla_ctx_out",
    )(ctx_t, w_pairs)


def _lane_pad(x, lo, width):
    pad = [(0, 0)] * (x.ndim - 1) + [(lo, width - lo - x.shape[-1])]
    return jnp.pad(x.astype(F32), pad)


def _gate_tile(rows):
    tile = jnp.stack([_lane_pad(vec, off, GATE_W) for off, vec in rows])
    return jnp.pad(tile, ((0, SUBLANES - len(rows)), (0, 0)))


def _prep_hybrid(j, w_in_a, conv_gdn_w, gdn_A_log, gdn_dt_bias, conv_ssm_w, ssm_A_log,
                 ssm_dt_bias, ssm_D):
    widths = (GDN_CONV_CH, GDN_VW, GDN_HEADS, GDN_HEADS, SSM_INNER, SSM_CONV_CH, SSM_HEADS)
    offs = np.concatenate([[0], np.cumsum(widths)])
    w = w_in_a[j].astype(BF16)
    part = lambda i: w[:, offs[i]:offs[i + 1]]
    assert (GATE_A, GATE_B, GATE_DT) == (0, GDN_HEADS, 2 * GDN_HEADS)
    small = _lane_pad(jnp.concatenate([part(2), part(3), part(6)], axis=1), 0, GATE_W)
    return dict(
        w_in=[part(0), part(1), part(4), part(5), small.astype(BF16)],
        conv_gdn=conv_gdn_w[j], conv_ssm=conv_ssm_w[j],
        gate_gdn=_gate_tile([(GATE_A, gdn_A_log[j]), (GATE_A, gdn_dt_bias[j])]),
        gate_ssm=_gate_tile([(GATE_DT, ssm_A_log[j]), (GATE_DT, ssm_dt_bias[j]),
                             (GATE_DT, ssm_D[j])]),
    )


def _prep_mla(j, norm_mix_c, w_in_c, q_a_norm, kv_a_norm, w_uq, w_uk, w_uv, q_norm, k_norm):
    half = QK_ROPE // 2
    w_in = w_in_c[j]

    def swapped(cols):
        return jnp.concatenate([-cols[..., half:], cols[..., :half]], axis=-1)

    def swapped_gain(g):
        return jnp.concatenate([g[half:], g[:half]])

    w_kr = w_in[:, Q_LORA + KV_LORA:]
    wkr = _lane_pad(jnp.concatenate([w_kr, swapped(w_kr)], axis=1), 0, LANES)
    qn, kn = q_norm[j], k_norm[j]
    gkr = _lane_pad(jnp.concatenate([kn[QK_NOPE:], swapped_gain(kn[QK_NOPE:])]), 0, LANES)
    gk = _lane_pad(kn[:QK_NOPE], 0, LANES)
    gq = jnp.concatenate([qn, swapped_gain(qn[QK_NOPE:])])

    uq = w_uq[j].reshape(Q_LORA, MLA_HEADS, QK_HEAD)
    wuq = jnp.concatenate([uq, swapped(uq[..., QK_NOPE:])], axis=-1).reshape(Q_LORA, HEAD_W)
    wuk = _lane_pad(w_uk[j], 0, LANES).reshape(KV_LORA, HEAD_W)
    wuv = _lane_pad(w_uv[j], 0, LANES).reshape(KV_LORA, HEAD_W)

    col = np.arange(HEAD_W)
    head, within = col // LANES, col % LANES
    lane = np.arange(LANES)
    inv_lane = QK_ROPE + head
    segq = ((within < QK_HEAD)[:, None] & (lane[None, :] == head[:, None]))
    segk = ((within < QK_NOPE)[:, None] & (lane[None, :] == inv_lane[:, None]))
    vone = (within == V_HEAD).astype(np.float32).reshape(1, HEAD_W)
    sel = lambda m: jnp.asarray(m, BF16)

    rope_pass = np.zeros((LANES, Q_EXT), np.float32)
    rope_pass[QK_NOPE:QK_HEAD, KV_LORA:KV_LORA + QK_ROPE] = np.eye(QK_ROPE)
    w_abs = jnp.pad(jnp.transpose(w_uk[j], (1, 2, 0)),
                    ((0, 0), (0, LANES - QK_NOPE), (0, Q_EXT - KV_LORA))) + rope_pass
    g_abs = jnp.concatenate([kn[:QK_NOPE], jnp.ones((QK_ROPE,), F32),
                             jnp.zeros((LANES - QK_HEAD,), F32)])
    uv = jnp.transpose(w_uv[j], (1, 0, 2)).reshape(MLA_HEADS // 2, 2, KV_LORA, V_HEAD)
    w_pairs = jnp.stack([_lane_pad(uv[:, 0], 0, LANES), _lane_pad(uv[:, 1], V_HEAD, LANES)], axis=1)
    row = lambda v: v.reshape(1, -1)
    return dict(
        norm=row(norm_mix_c[j]), wcq=w_in[:, :Q_LORA].astype(BF16),
        wckv=w_in[:, Q_LORA:Q_LORA + KV_LORA].astype(BF16), wkr=wkr.astype(BF16),
        qan=row(q_a_norm[j]), kvn=row(kv_a_norm[j]), wuq=wuq.astype(BF16), wuk=wuk.astype(BF16),
        wuv=wuv.astype(BF16), vone=jnp.asarray(vone), gq=row(gq), gk=row(gk), gkr=row(gkr),
        segq=sel(segq), segk=sel(segk),
        w_abs=w_abs.astype(BF16), g_abs=row(g_abs), w_pairs=w_pairs.astype(BF16),
    )


def _prep_ffn(w_gate_up, w_down, ff_tile):
    layers, d_ff, d = w_down.shape
    return (w_gate_up.astype(BF16),
            w_down.reshape(layers, d_ff // ff_tile, ff_tile, d).astype(BF16))


FF_TILE = 256
ROW_TILE = 512
FFN_ROW_TILE = 1024
MLA_ROW_TILE = 256
ATTN_TILE = 1024
ATTN_HEADS = 4
GDN_CHUNKS = 4
SSD_CHUNKS = 4


def _row_tile(rows, want):
    return want if rows % want == 0 else rows


def _trunk(x, pos0, states, cache_mla, page_table, hyb, mla, ffn, norm_mix_a, gdn_norm, conv_ssm_b,
           ssm_norm, w_out_a, w_out_c, norm_ffn):
    b, t, d = x.shape
    rows = b * t
    tm = _row_tile(rows, ROW_TILE)
    tm_f = _row_tile(rows, FFN_ROW_TILE)
    xf = x.reshape(rows, d)
    decode = states is not None

    qkv, gz, sz, xbc, small = _norm_proj(xf, norm_mix_a[0], hyb["w_in"], tm)
    if decode:
        s0, gconv0, h0, sconv0 = states
        r3 = lambda a: a.reshape(b, 1, a.shape[-1])
        n_pair = SSM_HEADS // 2
        o_gdn, s_new = _gdn_decode(qkv, gz, small, gconv0[0], s0[0], hyb["conv_gdn"],
                                   hyb["gate_gdn"], gdn_norm[0])
        o_ssd, h_new = _ssd_decode(xbc, sz, small, sconv0[0],
                                   h0[0].reshape(b, n_pair, 2 * SSM_P, SSM_N), hyb["conv_ssm"],
                                   conv_ssm_b[0], hyb["gate_ssm"], ssm_norm[0])
        gconv = jnp.concatenate([gconv0[0][:, 1:], r3(qkv)], axis=1)
        sconv = jnp.concatenate([sconv0[0][:, 1:], r3(xbc)], axis=1)
    else:
        r3 = lambda a: a.reshape(b, t, a.shape[-1])
        o_gdn, s_new = _gdn_prompt(r3(qkv), r3(gz), r3(small), hyb["conv_gdn"], hyb["gate_gdn"],
                                   gdn_norm[0], GDN_CHUNKS)
        o_ssd, h_new = _ssd_prompt(r3(xbc), r3(sz), r3(small), hyb["conv_ssm"], conv_ssm_b[0],
                                   hyb["gate_ssm"], ssm_norm[0], SSD_CHUNKS)
        gconv = r3(qkv)[:, t - (CONV_W - 1):]
        sconv = r3(xbc)[:, t - (CONV_W - 1):]
    h_new = h_new.reshape(b, SSM_HEADS, SSM_P, SSM_N)
    wo = w_out_a[0].astype(BF16)
    x1 = _res_ffn(xf, [o_gdn.reshape(rows, GDN_VW), o_ssd.reshape(rows, SSM_INNER)],
                  [wo[:GDN_VW], wo[GDN_VW:]], norm_ffn[0], 0, *ffn, tm_f)

    tm_c = _row_tile(rows, MLA_ROW_TILE)
    q_scale = QK_HEAD ** -0.5 * (1.0 if decode else math.log2(math.e))
    q, k, v, mla_rows = _mla_proj(x1, t, pos0, mla, tm_c, F32 if decode else BF16, q_scale)
    if decode:
        q_ext = jnp.transpose(_absorb(q, mla["g_abs"], mla["w_abs"]), (1, 0, 2))
        ctx = _paged_attention(q_ext, mla_rows.reshape(b, 1, MLA_ROW),
                               jnp.swapaxes(cache_mla[0], 1, 2), page_table)
        attn = _ctx_out(jnp.transpose(ctx, (1, 0, 2)), mla["w_pairs"])
    else:
        r3 = lambda a: a.reshape(b, t, a.shape[-1])
        attn = _flash_attention(r3(q), r3(k), r3(v), _row_tile(t, ATTN_TILE),
                                ATTN_HEADS).reshape(rows, -1)
    x2 = _res_ffn(x1, [attn], [w_out_c[0].astype(BF16)], norm_ffn[1], 1, *ffn, tm_f)
    return (x2.reshape(b, t, d), s_new[None], gconv[None], h_new[None], sconv[None],
            mla_rows.reshape(1, b, t, MLA_ROW))


def kernel(x_prompt, x_sample, state_gdn, state_gdn_conv, state_ssm, state_ssm_conv, cache_mla, page_table, norm_mix_a, w_in_a, conv_gdn_w, gdn_A_log, gdn_dt_bias, gdn_norm, conv_ssm_w, conv_ssm_b, ssm_A_log, ssm_dt_bias, ssm_D, ssm_norm, w_out_a, norm_mix_c, w_in_c, q_a_norm, kv_a_norm, w_uq, w_uk, w_uv, q_norm, k_norm, w_out_c, norm_ffn, w_gate_up, w_down):
    assert w_in_a.shape[0] == 1 and w_in_c.shape[0] == 1 and norm_ffn.shape[0] == 2
    hyb = _prep_hybrid(0, w_in_a, conv_gdn_w, gdn_A_log, gdn_dt_bias, conv_ssm_w, ssm_A_log,
                       ssm_dt_bias, ssm_D)
    mla = _prep_mla(0, norm_mix_c, w_in_c, q_a_norm, kv_a_norm, w_uq, w_uk, w_uv, q_norm, k_norm)
    ffn = _prep_ffn(w_gate_up, w_down, FF_TILE)
    shared = (hyb, mla, ffn, norm_mix_a, gdn_norm, conv_ssm_b, ssm_norm, w_out_a, w_out_c, norm_ffn)
    prompt = _trunk(x_prompt, 0, None, None, None, *shared)
    past_len = page_table.shape[1] * PAGE_SIZE
    sample = _trunk(x_sample, past_len, (state_gdn, state_gdn_conv, state_ssm, state_ssm_conv),
                    cache_mla, page_table, *shared)
    return (prompt[0], sample[0]) + prompt[1:] + sample[1:]
```

```python
import functools
import math

import jax
import jax.numpy as jnp
import numpy as np
from jax import lax
from jax.experimental import pallas as pl
from jax.experimental.pallas import tpu as pltpu

F32 = jnp.float32
BF16 = jnp.bfloat16

EPS = 1e-6
CONV_W = 4
CHUNK = 64
PAGE_SIZE = 128
GDN_HEADS = 4
GDN_DK = 128
GDN_DV = 128
GDN_QK = GDN_HEADS * GDN_DK
GDN_VW = GDN_HEADS * GDN_DV
GDN_CONV_CH = 2 * GDN_QK + GDN_VW
SSM_HEADS = 8
SSM_P = 64
SSM_N = 128
SSM_GROUPS = 2
SSM_INNER = SSM_HEADS * SSM_P
SSM_BC = SSM_GROUPS * SSM_N
SSM_CONV_CH = SSM_INNER + 2 * SSM_BC
MLA_HEADS = 16
Q_LORA = 512
KV_LORA = 256
QK_NOPE = 64
QK_ROPE = 32
QK_HEAD = QK_NOPE + QK_ROPE
V_HEAD = 64
MLA_ROW = KV_LORA + QK_ROPE + MLA_HEADS
ROPE_THETA = 10000.0

LANES = 128
SUBLANES = 8
HEAD_W = MLA_HEADS * LANES
GATE_W = LANES
GATE_A, GATE_B, GATE_DT = 0, GDN_HEADS, 2 * GDN_HEADS
Q_EXT = 3 * LANES
VMEM_LIMIT = 48 * 1024 * 1024

_HI = lax.Precision.HIGHEST


def _cparams(*sem):
    return pltpu.CompilerParams(dimension_semantics=sem, vmem_limit_bytes=VMEM_LIMIT)


def _const_spec(shape):
    zeros = (0,) * len(shape)
    return pl.BlockSpec(shape, lambda *_: zeros, pipeline_mode=pl.Buffered(1))


def _mm(a, b):
    return jnp.dot(a.astype(BF16), b.astype(BF16), preferred_element_type=F32)


def _mm_nt(a, b):
    return lax.dot_general(a.astype(BF16), b.astype(BF16), (((1,), (1,)), ((), ())),
                           preferred_element_type=F32)


def _mm_tn(a, b):
    return lax.dot_general(a.astype(BF16), b.astype(BF16), (((0,), (0,)), ((), ())),
                           preferred_element_type=F32)


def _sigmoid(x):
    return 1.0 / (1.0 + jnp.exp(-x))


def _silu(x):
    return x * _sigmoid(x)


def _softplus(x):
    return jnp.maximum(x, 0.0) + jnp.log(1.0 + jnp.exp(-jnp.abs(x)))


def _rms(x, gain):
    return x * lax.rsqrt(jnp.mean(x * x, axis=-1, keepdims=True) + EPS) * gain


def _l2n(x):
    return x * lax.rsqrt(jnp.sum(x * x, axis=-1, keepdims=True) + EPS)


def _iota(shape, dim):
    return lax.broadcasted_iota(jnp.int32, shape, dim)


def _chunk_cumsum(g, tb):
    row, col = _iota((tb, tb), 0), _iota((tb, tb), 1)
    shift = int(math.log2(CHUNK))
    same_chunk = lax.shift_right_logical(row, shift) == lax.shift_right_logical(col, shift)
    tril = jnp.where((col <= row) & same_chunk, 1.0, 0.0).astype(F32)
    return jnp.dot(tril, g, precision=_HI, preferred_element_type=F32)


def _rows_of(cols, lane0):
    tile = (SUBLANES, LANES)
    sel = jnp.where(_iota(tile, 1) == _iota(tile, 0) + lane0, 1.0, 0.0).astype(F32)
    return lax.dot_general(sel, cols, (((1,), (1,)), ((), ())), precision=_HI,
                           preferred_element_type=F32)


def _causal_conv_block(x_ref, xp_ref, cw, tb, first):
    @pl.when(first)
    def _():
        xp_ref[0:SUBLANES, :] = jnp.zeros((SUBLANES, xp_ref.shape[1]), F32)

    xp_ref[SUBLANES:SUBLANES + tb, :] = x_ref[...]
    off = SUBLANES - (CONV_W - 1)
    y = cw[0:1] * xp_ref[off:off + tb, :]
    for i in range(1, CONV_W):
        y = y + cw[i:i + 1] * xp_ref[off + i:off + i + tb, :]
    xp_ref[0:SUBLANES, :] = xp_ref[tb:tb + SUBLANES, :]
    return y


def _norm_proj_body(n_out, x_ref, g_ref, *refs):
    xn = _rms(x_ref[...], g_ref[...]).astype(BF16)
    for w_ref, o_ref in zip(refs[:n_out], refs[n_out:]):
        o_ref[...] = jnp.dot(xn, w_ref[...], preferred_element_type=F32)


def _norm_proj(x, gain, weights, tm):
    rows, d = x.shape
    n = len(weights)
    return pl.pallas_call(
        functools.partial(_norm_proj_body, n),
        grid=(rows // tm,),
        in_specs=[pl.BlockSpec((tm, d), lambda i: (i, 0)), _const_spec((1, d))]
        + [_const_spec(w.shape) for w in weights],
        out_specs=[pl.BlockSpec((tm, w.shape[1]), lambda i: (i, 0)) for w in weights],
        out_shape=[jax.ShapeDtypeStruct((rows, w.shape[1]), F32) for w in weights],
        compiler_params=_cparams("parallel"),
        name="norm_proj",
    )(x, gain.reshape(1, d), *weights)


def _gdn_prompt_body(nc, qkv_ref, gz_ref, sm_ref, cw_ref, gp_ref, gn_ref, o_ref, s_ref, xp_ref):
    tb = nc * CHUNK
    first = pl.program_id(1) == 0

    @pl.when(first)
    def _():
        s_ref[...] = jnp.zeros(s_ref.shape, F32)

    y = _silu(_causal_conv_block(qkv_ref.at[0], xp_ref, cw_ref[...], tb, first))
    sm, gp = sm_ref[0], gp_ref[...]
    g_all = -jnp.exp(gp[0:1]) * _softplus(sm + gp[1:2])
    beta_all = _sigmoid(sm)
    cum_all = _chunk_cumsum(g_all, tb)
    cum_rows = _rows_of(cum_all, GATE_A)
    incl = _iota((CHUNK, CHUNK), 1) <= _iota((CHUNK, CHUNK), 0)
    strict = _iota((CHUNK, CHUNK), 1) < _iota((CHUNK, CHUNK), 0)
    gn = gn_ref[...]
    pairs = [(c, h) for c in range(nc) for h in range(GDN_HEADS)]
    qs, ks, vs = [], [], []
    for h in range(GDN_HEADS):
        lo = h * GDN_DK
        qs.append(_l2n(y[:, lo:lo + GDN_DK]) * GDN_DK ** -0.5)
        ks.append(_l2n(y[:, GDN_QK + lo:GDN_QK + lo + GDN_DK]))
        vs.append(y[:, 2 * GDN_QK + lo:2 * GDN_QK + lo + GDN_DV])
    pre = []
    for c, h in pairs:
        r = slice(c * CHUNK, (c + 1) * CHUNK)
        q, k, v = qs[h][r], ks[h][r], vs[h][r]
        cum_c = cum_all[r, GATE_A + h:GATE_A + h + 1]
        cum_r = cum_rows[h:h + 1, c * CHUNK:(c + 1) * CHUNK]
        beta = beta_all[r, GATE_B + h:GATE_B + h + 1]
        decay = jnp.where(incl, jnp.exp(jnp.minimum(cum_c - cum_r, 0.0)), 0.0)
        e_c = jnp.exp(cum_c)
        kb = k * beta
        last = cum_c[CHUNK - 1:CHUNK]
        pre.append(dict(q=q.astype(BF16), k=k.astype(BF16), kb=kb, decay=decay,
                        rhs=jnp.concatenate([v * beta, kb * e_c], axis=1).astype(BF16),
                        qe=q * e_c, kd=(k * jnp.exp(last - cum_c)).astype(BF16),
                        e_last=jnp.exp(last)))
    lows = [jnp.where(strict, _mm_nt(p["kb"], p["k"]) * p["decay"], 0.0) for p in pre]
    attns = [(_mm_nt(p["q"], p["k"]) * p["decay"]).astype(BF16) for p in pre]
    row, col = _iota((CHUNK, CHUNK), 0), _iota((CHUNK, CHUNK), 1)
    invs = None
    for level in range(int(math.log2(CHUNK))):
        brow = lax.shift_right_logical(row, level)
        joins = ((brow & 1) == 1) & (lax.shift_right_logical(col, level) == brow - 1)
        subs = [jnp.where(joins, low, 0.0) for low in lows]
        if invs is None:
            eye = jnp.where(row == col, 1.0, 0.0).astype(F32)
            invs = [eye - sub for sub in subs]
        else:
            invs = [inv - _mm(_mm(inv, sub), inv) for inv, sub in zip(invs, subs)]
    sols = [_mm(inv, p["rhs"]).astype(BF16) for inv, p in zip(invs, pre)]
    k_sol = [_mm_tn(p["kd"], sol) for p, sol in zip(pre, sols)]
    a_sol = [_mm(attn, sol) for attn, sol in zip(attns, sols)]
    q_eff = [(p["qe"] - a[:, GDN_DV:]).astype(BF16) for p, a in zip(pre, a_sol)]
    m_eff = [ks[:, GDN_DV:].astype(BF16) for ks in k_sol]
    states = [s_ref[0, h] for h in range(GDN_HEADS)]
    for c in range(nc):
        r = slice(c * CHUNK, (c + 1) * CHUNK)
        idx = [c * GDN_HEADS + h for h in range(GDN_HEADS)]
        s_bf = [st.astype(BF16) for st in states]
        outs = [_mm(q_eff[i], s_bf[h]) + a_sol[i][:, :GDN_DV] for h, i in enumerate(idx)]
        states = [states[h] * pre[i]["e_last"] - _mm(m_eff[i], s_bf[h]) + k_sol[i][:, :GDN_DV]
                  for h, i in enumerate(idx)]
        for h, o in enumerate(outs):
            lo = h * GDN_DV
            gate = _silu(gz_ref[0, r, lo:lo + GDN_DV])
            o_ref[0, r, lo:lo + GDN_DV] = (_rms(o, gn) * gate).astype(o_ref.dtype)
    for h in range(GDN_HEADS):
        s_ref[0, h] = states[h]


def _gdn_prompt(qkv, gz, small, conv_w, gate_par, gdn_norm, nc):
    b, t, _ = qkv.shape
    tb = nc * CHUNK
    blk = lambda w: pl.BlockSpec((1, tb, w), lambda i, j: (i, j, 0))
    return pl.pallas_call(
        functools.partial(_gdn_prompt_body, nc),
        grid=(b, t // tb),
        in_specs=[blk(GDN_CONV_CH), blk(GDN_VW), blk(GATE_W), _const_spec(conv_w.shape),
                  _const_spec(gate_par.shape), _const_spec((1, GDN_DV))],
        out_specs=[blk(GDN_VW),
                   pl.BlockSpec((1, GDN_HEADS, GDN_DK, GDN_DV), lambda i, j: (i, 0, 0, 0))],
        out_shape=[jax.ShapeDtypeStruct((b, t, GDN_VW), BF16),
                   jax.ShapeDtypeStruct((b, GDN_HEADS, GDN_DK, GDN_DV), F32)],
        scratch_shapes=[pltpu.VMEM((tb + SUBLANES, GDN_CONV_CH), F32)],
        compiler_params=_cparams("parallel", "arbitrary"),
        name="gdn_prompt",
    )(qkv, gz, small, conv_w, gate_par, gdn_norm.reshape(1, GDN_DV))


def _ssd_prompt_body(nc, xbc_ref, sz_ref, sm_ref, cw_ref, cb_ref, sp_ref, sn_ref, o_ref, h_ref,
                     xp_ref):
    tb = nc * CHUNK
    first = pl.program_id(1) == 0

    @pl.when(first)
    def _():
        h_ref[...] = jnp.zeros(h_ref.shape, F32)

    y = _silu(_causal_conv_block(xbc_ref.at[0], xp_ref, cw_ref[...], tb, first) + cb_ref[...])
    sm, sp = sm_ref[0], sp_ref[...]
    dt_all = _softplus(sm + sp[1:2])
    cum_all = _chunk_cumsum(dt_all * -jnp.exp(sp[0:1]), tb)
    cum_rows = _rows_of(cum_all, GATE_DT)
    incl = _iota((CHUNK, CHUNK), 1) <= _iota((CHUNK, CHUNK), 0)
    lane_lo = _iota((CHUNK, LANES), 1) < SSM_P
    row_lo = _iota((2 * SSM_P, SSM_N), 0) < SSM_P
    par_lo = _iota((1, LANES), 1) < SSM_P
    sn = sn_ref[...]
    n_pair = SSM_HEADS // 2
    pairs_per_group = n_pair // SSM_GROUPS
    gw = SSM_INNER // SSM_GROUPS
    b_bf, c_bf, scores = {}, {}, {}
    for c in range(nc):
        r = slice(c * CHUNK, (c + 1) * CHUNK)
        for grp in range(SSM_GROUPS):
            lo = SSM_INNER + grp * SSM_N
            b_bf[c, grp] = y[r, lo:lo + SSM_N].astype(BF16)
            c_bf[c, grp] = y[r, lo + SSM_BC:lo + SSM_BC + SSM_N].astype(BF16)
            scores[c, grp] = _mm_nt(c_bf[c, grp], b_bf[c, grp])
    pre = {}
    for c in range(nc):
        r = slice(c * CHUNK, (c + 1) * CHUNK)
        for j in range(n_pair):
            grp = j // pairs_per_group
            cums, segs, dts = [], [], []
            for head in (2 * j, 2 * j + 1):
                cum_c = cum_all[r, GATE_DT + head:GATE_DT + head + 1]
                cum_r = cum_rows[head:head + 1, c * CHUNK:(c + 1) * CHUNK]
                cums.append(cum_c)
                segs.append(jnp.where(incl, jnp.exp(jnp.minimum(cum_c - cum_r, 0.0)), 0.0))
                dts.append(dt_all[r, GATE_DT + head:GATE_DT + head + 1])
            x_pair = y[r, j * LANES:(j + 1) * LANES]
            xdt = x_pair * jnp.where(lane_lo, dts[0], dts[1])
            last = [cm[CHUNK - 1:CHUNK] for cm in cums]
            dec = jnp.where(lane_lo, jnp.exp(last[0] - cums[0]), jnp.exp(last[1] - cums[1]))
            d_pair = jnp.where(par_lo, sp[2:3, GATE_DT + 2 * j:GATE_DT + 2 * j + 1],
                               sp[2:3, GATE_DT + 2 * j + 1:GATE_DT + 2 * j + 2])
            pre[c, j] = dict(
                y=(_mm(scores[c, grp] * segs[0], jnp.where(lane_lo, xdt, 0.0))
                   + _mm(scores[c, grp] * segs[1], jnp.where(lane_lo, 0.0, xdt))
                   + d_pair * x_pair),
                e=jnp.where(lane_lo, jnp.exp(cums[0]), jnp.exp(cums[1])),
                xdec=(xdt * dec).astype(BF16),
                e_last=jnp.where(row_lo, jnp.exp(last[0]), jnp.exp(last[1])))
    states = [h_ref[0, j] for j in range(n_pair)]
    for c in range(nc):
        r = slice(c * CHUNK, (c + 1) * CHUNK)
        ys = [pre[c, j]["y"] + _mm_nt(c_bf[c, j // pairs_per_group], states[j]) * pre[c, j]["e"]
              for j in range(n_pair)]
        states = [states[j] * pre[c, j]["e_last"]
                  + _mm_tn(pre[c, j]["xdec"], b_bf[c, j // pairs_per_group]) for j in range(n_pair)]
        for grp in range(SSM_GROUPS):
            outs = [ys[j] * _silu(sz_ref[0, r, j * LANES:(j + 1) * LANES])
                    for j in range(grp * pairs_per_group, (grp + 1) * pairs_per_group)]
            inv = lax.rsqrt(sum(jnp.sum(o * o, axis=-1, keepdims=True) for o in outs) / gw + EPS)
            for jj, o in enumerate(outs):
                lo = grp * gw + jj * LANES
                o_ref[0, r, lo:lo + LANES] = (o * inv * sn[:, lo:lo + LANES]).astype(o_ref.dtype)
    for j in range(n_pair):
        h_ref[0, j] = states[j]


def _ssd_prompt(xbc, sz, small, conv_w, conv_b, ssm_par, ssm_norm, nc):
    b, t, _ = xbc.shape
    tb = nc * CHUNK
    blk = lambda w: pl.BlockSpec((1, tb, w), lambda i, j: (i, j, 0))
    n_pair = SSM_HEADS // 2
    return pl.pallas_call(
        functools.partial(_ssd_prompt_body, nc),
        grid=(b, t // tb),
        in_specs=[blk(SSM_CONV_CH), blk(SSM_INNER), blk(GATE_W), _const_spec(conv_w.shape),
                  _const_spec((1, SSM_CONV_CH)), _const_spec(ssm_par.shape),
                  _const_spec((1, SSM_INNER))],
        out_specs=[blk(SSM_INNER),
                   pl.BlockSpec((1, n_pair, 2 * SSM_P, SSM_N), lambda i, j: (i, 0, 0, 0))],
        out_shape=[jax.ShapeDtypeStruct((b, t, SSM_INNER), BF16),
                   jax.ShapeDtypeStruct((b, n_pair, 2 * SSM_P, SSM_N), F32)],
        scratch_shapes=[pltpu.VMEM((tb + SUBLANES, SSM_CONV_CH), F32)],
        compiler_params=_cparams("parallel", "arbitrary"),
        name="ssd_prompt",
    )(xbc, sz, small, conv_w, conv_b.reshape(1, -1), ssm_par, ssm_norm.reshape(1, -1))


def _col_bcast(row):
    return jnp.transpose(jnp.broadcast_to(row, (LANES, LANES)))


def _decode_conv(new_ref, cv_ref, cw):
    y = cw[CONV_W - 1:CONV_W] * new_ref[...]
    for i in range(CONV_W - 1):
        y = y + cw[i:i + 1] * cv_ref[:, i, :]
    return y


def _gdn_decode_body(bb, qkv_ref, gz_ref, sm_ref, cv_ref, s0_ref, cw_ref, gp_ref, gn_ref, o_ref,
                     s_ref, raw_ref):
    y = _silu(_decode_conv(qkv_ref, cv_ref, cw_ref[...]))
    sm, gp = sm_ref[...], gp_ref[...]
    a_all = jnp.exp(-jnp.exp(gp[0:1]) * _softplus(sm + gp[1:2]))
    beta_all = _sigmoid(sm)
    qs, ks, vs = [], [], []
    for h in range(GDN_HEADS):
        lo = h * GDN_DK
        qs.append(_l2n(y[:, lo:lo + GDN_DK]) * GDN_DK ** -0.5)
        ks.append(_l2n(y[:, GDN_QK + lo:GDN_QK + lo + GDN_DK]))
        vs.append(y[:, 2 * GDN_QK + lo:2 * GDN_QK + lo + GDN_DV])
    pairs = [(b, h) for b in range(bb) for h in range(GDN_HEADS)]
    k_col = [_col_bcast(ks[h][b:b + 1]) for b, h in pairs]
    q_col = [_col_bcast(qs[h][b:b + 1]) for b, h in pairs]
    decayed = [s0_ref[b, h] * a_all[b:b + 1, GATE_A + h:GATE_A + h + 1] for b, h in pairs]
    v_new = [beta_all[b:b + 1, GATE_B + h:GATE_B + h + 1]
             * (vs[h][b:b + 1] - jnp.sum(k_col[i] * decayed[i], axis=0, keepdims=True))
             for i, (b, h) in enumerate(pairs)]
    for i, (b, h) in enumerate(pairs):
        state = decayed[i] + k_col[i] * v_new[i]
        s_ref[b, h] = state
        raw_ref[b:b + 1, h * GDN_DV:(h + 1) * GDN_DV] = jnp.sum(q_col[i] * state, axis=0,
                                                                keepdims=True)
    gn = gn_ref[...]
    for h in range(GDN_HEADS):
        hs = slice(h * GDN_DV, (h + 1) * GDN_DV)
        o_ref[:, hs] = (_rms(raw_ref[:, hs], gn) * _silu(gz_ref[:, hs])).astype(o_ref.dtype)


DECODE_BLOCK = SUBLANES


def _gdn_decode(qkv, gz, small, conv0, s0, conv_w, gate_par, gdn_norm):
    b = qkv.shape[0]
    bb = DECODE_BLOCK
    row = lambda w: pl.BlockSpec((bb, w), lambda i: (i, 0))
    st = pl.BlockSpec((bb, GDN_HEADS, GDN_DK, GDN_DV), lambda i: (i, 0, 0, 0))
    return pl.pallas_call(
        functools.partial(_gdn_decode_body, bb),
        grid=(b // bb,),
        in_specs=[row(GDN_CONV_CH), row(GDN_VW), row(GATE_W),
                  pl.BlockSpec((bb, CONV_W - 1, GDN_CONV_CH), lambda i: (i, 0, 0)), st,
                  _const_spec(conv_w.shape), _const_spec(gate_par.shape), _const_spec((1, GDN_DV))],
        out_specs=[row(GDN_VW), st],
        out_shape=[jax.ShapeDtypeStruct((b, GDN_VW), F32), jax.ShapeDtypeStruct(s0.shape, F32)],
        scratch_shapes=[pltpu.VMEM((bb, GDN_VW), F32)],
        compiler_params=_cparams("parallel"),
        name="gdn_decode",
    )(qkv, gz, small, conv0, s0, conv_w, gate_par, gdn_norm.reshape(1, GDN_DV))


def _ssd_decode_body(bb, xbc_ref, sz_ref, sm_ref, cv_ref, h0_ref, cw_ref, cb_ref, sp_ref, sn_ref,
                     o_ref, h_ref, raw_ref):
    y = _silu(_decode_conv(xbc_ref, cv_ref, cw_ref[...]) + cb_ref[...])
    sm, sp = sm_ref[...], sp_ref[...]
    dt_all = _softplus(sm + sp[1:2])
    da_all = jnp.exp(dt_all * -jnp.exp(sp[0:1]))
    par_lo = _iota((1, LANES), 1) < SSM_P
    row_lo = _iota((2 * SSM_P, SSM_N), 0) < SSM_P
    n_pair = SSM_HEADS // 2
    pairs_per_group = n_pair // SSM_GROUPS
    pairs = [(b, j) for b in range(bb) for j in range(n_pair)]
    x_col, states = [], []
    for b, j in pairs:
        la, lb = GATE_DT + 2 * j, GATE_DT + 2 * j + 1
        row = slice(b, b + 1)
        xdt = y[row, j * LANES:(j + 1) * LANES] * jnp.where(par_lo, dt_all[row, la:la + 1],
                                                           dt_all[row, lb:lb + 1])
        x_col.append(_col_bcast(xdt))
    for i, (b, j) in enumerate(pairs):
        la, lb = GATE_DT + 2 * j, GATE_DT + 2 * j + 1
        row = slice(b, b + 1)
        grp = j // pairs_per_group
        b_g = y[row, SSM_INNER + grp * SSM_N:SSM_INNER + (grp + 1) * SSM_N]
        state = (h0_ref[b, j] * jnp.where(row_lo, da_all[row, la:la + 1], da_all[row, lb:lb + 1])
                 + x_col[i] * b_g)
        h_ref[b, j] = state
        states.append(state)
    lane_id = _iota((LANES, LANES), 1)
    for j in range(n_pair):
        lo = SSM_INNER + SSM_BC + (j // pairs_per_group) * SSM_N
        cols = jnp.zeros((LANES, LANES), F32)
        for b in range(bb):
            y_col = jnp.sum(states[b * n_pair + j] * y[b:b + 1, lo:lo + SSM_N], axis=1,
                            keepdims=True)
            cols = jnp.where(lane_id == b, y_col, cols)
        raw_ref[:, j * LANES:(j + 1) * LANES] = jnp.transpose(cols)[0:bb]
    sn = sn_ref[...]
    gw = SSM_INNER // SSM_GROUPS
    for grp in range(SSM_GROUPS):
        outs = []
        for jj in range(pairs_per_group):
            j = grp * pairs_per_group + jj
            la, lb = GATE_DT + 2 * j, GATE_DT + 2 * j + 1
            js = slice(j * LANES, (j + 1) * LANES)
            d_pair = jnp.where(par_lo, sp[2:3, la:la + 1], sp[2:3, lb:lb + 1])
            outs.append((raw_ref[:, js] + d_pair * y[:, js]) * _silu(sz_ref[:, js]))
        inv = lax.rsqrt(sum(jnp.sum(o * o, axis=-1, keepdims=True) for o in outs) / gw + EPS)
        for jj, o in enumerate(outs):
            lo = grp * gw + jj * LANES
            o_ref[:, lo:lo + LANES] = (o * inv * sn[:, lo:lo + LANES]).astype(o_ref.dtype)


def _ssd_decode(xbc, sz, small, conv0, h0, conv_w, conv_b, ssm_par, ssm_norm):
    b = xbc.shape[0]
    bb = DECODE_BLOCK
    n_pair = SSM_HEADS // 2
    row = lambda w: pl.BlockSpec((bb, w), lambda i: (i, 0))
    st = pl.BlockSpec((bb, n_pair, 2 * SSM_P, SSM_N), lambda i: (i, 0, 0, 0))
    return pl.pallas_call(
        functools.partial(_ssd_decode_body, bb),
        grid=(b // bb,),
        in_specs=[row(SSM_CONV_CH), row(SSM_INNER), row(GATE_W),
                  pl.BlockSpec((bb, CONV_W - 1, SSM_CONV_CH), lambda i: (i, 0, 0)), st,
                  _const_spec(conv_w.shape), _const_spec((1, SSM_CONV_CH)),
                  _const_spec(ssm_par.shape), _const_spec((1, SSM_INNER))],
        out_specs=[row(SSM_INNER), st],
        out_shape=[jax.ShapeDtypeStruct((b, SSM_INNER), F32), jax.ShapeDtypeStruct(h0.shape, F32)],
        scratch_shapes=[pltpu.VMEM((bb, SSM_INNER), F32)],
        compiler_params=_cparams("parallel"),
        name="ssd_decode",
    )(xbc, sz, small, conv0, h0, conv_w, conv_b.reshape(1, -1), ssm_par, ssm_norm.reshape(1, -1))


def _res_ffn_body(n_a, n_ff, x_ref, *refs):
    a_refs, w_refs = refs[:n_a], refs[n_a:2 * n_a]
    g_ref, wgu_ref, wd_ref, o_ref, xn_ref = refs[2 * n_a:]
    ff_tile = wd_ref.shape[1]
    d_ff = n_ff * ff_tile
    x1 = x_ref[...]
    for a_ref, w_ref in zip(a_refs, w_refs):
        x1 = x1 + jnp.dot(a_ref[...].astype(BF16), w_ref[...], preferred_element_type=F32)
    xn_ref[...] = _rms(x1, g_ref[...]).astype(BF16)
    o_ref[...] = x1

    def step(c, carry):
        xn = xn_ref[...]
        lo = pl.multiple_of(c * ff_tile, ff_tile)
        gate = jnp.dot(xn, wgu_ref[:, pl.ds(lo, ff_tile)], preferred_element_type=F32)
        up = jnp.dot(xn, wgu_ref[:, pl.ds(d_ff + lo, ff_tile)], preferred_element_type=F32)
        hid = (_silu(gate) * up).astype(BF16)
        o_ref[...] += jnp.dot(hid, wd_ref[c], preferred_element_type=F32)
        return carry

    lax.fori_loop(0, n_ff, step, 0, unroll=True)


def _res_ffn(x, acts, projs, gain, layer, wgu, wd, tm):
    rows, d = x.shape
    n_a, n_ff = len(acts), wd.shape[1]

    def layer_spec(w):
        tail = (0,) * (w.ndim - 1)
        return pl.BlockSpec((None,) + w.shape[1:], lambda i: (layer,) + tail,
                            pipeline_mode=pl.Buffered(1))

    return pl.pallas_call(
        functools.partial(_res_ffn_body, n_a, n_ff),
        grid=(rows // tm,),
        in_specs=[pl.BlockSpec((tm, d), lambda i: (i, 0))]
        + [pl.BlockSpec((tm, a.shape[1]), lambda i: (i, 0)) for a in acts]
        + [_const_spec(p.shape) for p in projs]
        + [_const_spec((1, d)), layer_spec(wgu), layer_spec(wd)],
        out_specs=pl.BlockSpec((tm, d), lambda i: (i, 0)),
        out_shape=jax.ShapeDtypeStruct((rows, d), F32),
        scratch_shapes=[pltpu.VMEM((tm, d), BF16)],
        compiler_params=_cparams("parallel"),
        name="res_ffn",
    )(x, *acts, *projs, gain.reshape(1, d), wgu, wd)


def _mla_proj_body(tm, q_scale, x_ref, cos_ref, sin_ref, g_ref, wcq_ref, wckv_ref, wkr_ref,
                   qan_ref, kvn_ref, wuq_ref, wuk_ref, wuv_ref, vone_ref, gq_ref, gk_ref, gkr_ref,
                   segq_ref, segk_ref, q_ref, k_ref, v_ref, rows_ref):
    xn = _rms(x_ref[...], g_ref[...]).astype(BF16)
    lane = _iota((tm, LANES), 1)
    cos_t, sin_t = cos_ref[...], sin_ref[...]

    kr = jnp.dot(xn, wkr_ref[...], preferred_element_type=F32)
    ssq_kr = jnp.sum(jnp.where(lane < QK_ROPE, kr * kr, 0.0), axis=-1, keepdims=True)
    krg = kr * gkr_ref[...]
    kr_rot = krg * cos_t + pltpu.roll(krg, LANES - QK_ROPE, 1) * sin_t

    c = _rms(jnp.dot(xn, wckv_ref[...], preferred_element_type=F32), kvn_ref[...])
    cb = c.astype(BF16)
    kx = jnp.dot(cb, wuk_ref[...], preferred_element_type=F32)
    ssq_k = jnp.dot((kx * kx).astype(BF16), segk_ref[...],
                    preferred_element_type=F32)
    inv_r = lax.rsqrt((ssq_k + ssq_kr) / QK_HEAD + EPS)
    tail = jnp.where(lane < QK_ROPE, kr_rot, inv_r)
    rows_ref[:, 0:KV_LORA] = c
    rows_ref[:, KV_LORA:MLA_ROW] = tail[:, 0:MLA_ROW - KV_LORA]
    kr_put = jnp.where((lane >= QK_NOPE) & (lane < QK_HEAD), pltpu.roll(kr_rot, QK_NOPE, 1), 0.0)
    gk = gk_ref[...]
    for h in range(MLA_HEADS):
        hs = slice(h * LANES, (h + 1) * LANES)
        inv_h = inv_r[:, QK_ROPE + h:QK_ROPE + h + 1]
        k_ref[:, hs] = ((kx[:, hs] * gk + kr_put) * inv_h).astype(k_ref.dtype)
    v_ref[...] = (jnp.dot(cb, wuv_ref[...], preferred_element_type=F32)
                  + vone_ref[...]).astype(v_ref.dtype)

    cq = _rms(jnp.dot(xn, wcq_ref[...], preferred_element_type=F32), qan_ref[...]).astype(BF16)
    qx = jnp.dot(cq, wuq_ref[...], preferred_element_type=F32)
    ssq_q = jnp.dot((qx * qx).astype(BF16), segq_ref[...], preferred_element_type=F32)
    inv_q = lax.rsqrt(ssq_q / QK_HEAD + EPS)
    gq = gq_ref[...]
    scale = q_scale
    in_rope = (lane >= QK_NOPE) & (lane < QK_HEAD)
    keep = jnp.where(lane < QK_NOPE, scale, jnp.where(in_rope, cos_t * scale, 0.0))
    swap = jnp.where(in_rope, sin_t * scale, 0.0)
    for h in range(MLA_HEADS):
        hs = slice(h * LANES, (h + 1) * LANES)
        t = qx[:, hs] * inv_q[:, h:h + 1] * gq
        q_ref[:, hs] = (t * keep + pltpu.roll(t, LANES - QK_ROPE, 1) * swap).astype(q_ref.dtype)


def _mla_proj(x, seq, pos0, w, tm, q_dtype, q_scale):
    rows, d = x.shape
    half = QK_ROPE // 2
    inv_freq = ROPE_THETA ** (-(jnp.arange(LANES) % half).astype(F32) / half)
    ang = (pos0 + jnp.arange(seq, dtype=F32))[:, None] * inv_freq[None, :]
    tr = tm if seq > 1 else 1
    nblk = seq // tr
    rope_spec = pl.BlockSpec((tr, LANES), lambda i: (i % nblk, 0))
    names = ("norm", "wcq", "wckv", "wkr", "qan", "kvn", "wuq", "wuk", "wuv", "vone", "gq", "gk",
             "gkr", "segq", "segk")
    consts = [w[n] for n in names]
    out_w = (HEAD_W, HEAD_W, HEAD_W, MLA_ROW)
    out_dt = (q_dtype, BF16, BF16, F32)
    return pl.pallas_call(
        functools.partial(_mla_proj_body, tm, q_scale),
        grid=(rows // tm,),
        in_specs=[pl.BlockSpec((tm, d), lambda i: (i, 0)), rope_spec, rope_spec]
        + [_const_spec(c.shape) for c in consts],
        out_specs=[pl.BlockSpec((tm, ow), lambda i: (i, 0)) for ow in out_w],
        out_shape=[jax.ShapeDtypeStruct((rows, ow), dt) for ow, dt in zip(out_w, out_dt)],
        compiler_params=_cparams("parallel"),
        name="mla_proj",
    )(x, jnp.cos(ang), jnp.sin(ang), *consts)


def _flash_body(tq, nh, q_ref, k_ref, v_ref, o_ref):
    qi = pl.program_id(2)
    half = tq // 2
    heads = [slice(j * LANES, (j + 1) * LANES) for j in range(nh)]
    q = [q_ref[0, :, hs] for hs in heads]

    def attend(carry, qs, kv_rows, mask):
        s = [lax.dot_general(qs[j], k_ref[0, kv_rows, hs], (((1,), (1,)), ((), ())),
                             preferred_element_type=F32) for j, hs in enumerate(heads)]
        if mask is not None:
            s = [jnp.where(mask, sj, -jnp.inf) for sj in s]
        m_new = [jnp.maximum(carry[j][0], jnp.max(s[j], axis=-1, keepdims=True)) for j in range(nh)]
        p = [jnp.exp2((s[j] - m_new[j]).astype(BF16)) for j in range(nh)]
        pv = [jnp.dot(p[j], v_ref[0, kv_rows, hs], preferred_element_type=F32)
              for j, hs in enumerate(heads)]
        return tuple((m_new[j], jnp.exp2(carry[j][0] - m_new[j]) * carry[j][1] + pv[j])
                     for j in range(nh))

    init = tuple((jnp.full((tq, 1), -jnp.inf, F32), jnp.zeros((tq, LANES), F32)) for _ in range(nh))
    carry = lax.fori_loop(
        0, 2 * qi, lambda kb, cr: attend(cr, q, pl.ds(pl.multiple_of(kb * half, half), half), None),
        init)
    base = pl.multiple_of(qi * tq, tq)
    carry = attend(carry, q, pl.ds(base, half),
                   _iota((tq, half), 1) <= _iota((tq, half), 0))
    lower = attend(tuple((m[half:], acc[half:]) for m, acc in carry), [qj[half:] for qj in q],
                   pl.ds(base + half, half), _iota((half, half), 1) <= _iota((half, half), 0))
    low = _iota((half, LANES), 1) < V_HEAD
    for part, rows in ((tuple((m[:half], acc[:half]) for m, acc in carry), slice(0, half)),
                       (lower, slice(half, tq))):
        outs = [acc / acc[:, V_HEAD:V_HEAD + 1] for _, acc in part]
        for j in range(nh // 2):
            o_ref[0, rows, j * LANES:(j + 1) * LANES] = jnp.where(
                low, outs[2 * j], pltpu.roll(outs[2 * j + 1], V_HEAD, 1)).astype(o_ref.dtype)


def _flash_attention(q, k, v, tq, nh):
    b, t, _ = q.shape
    kv_spec = pl.BlockSpec((1, t, nh * LANES), lambda i, j, n: (i, 0, j))
    return pl.pallas_call(
        functools.partial(_flash_body, tq, nh),
        grid=(b, MLA_HEADS // nh, t // tq),
        in_specs=[pl.BlockSpec((1, tq, nh * LANES), lambda i, j, n: (i, n, j)), kv_spec, kv_spec],
        out_specs=pl.BlockSpec((1, tq, nh * V_HEAD), lambda i, j, n: (i, n, j)),
        out_shape=jax.ShapeDtypeStruct((b, t, MLA_HEADS * V_HEAD), BF16),
        compiler_params=_cparams("parallel", "parallel", "arbitrary"),
        name="flash_attention",
    )(q, k, v)


def _absorb_body(q_ref, g_ref, w_ref, o_ref):
    t = (q_ref[...] * g_ref[...]).astype(BF16)
    o_ref[...] = jnp.dot(t, w_ref[...], preferred_element_type=F32).astype(o_ref.dtype)


def _absorb(q, gain, w_abs):
    b = q.shape[0]
    return pl.pallas_call(
        _absorb_body,
        grid=(MLA_HEADS,),
        in_specs=[pl.BlockSpec((b, LANES), lambda h: (0, h)), _const_spec((1, LANES)),
                  pl.BlockSpec((None, LANES, Q_EXT), lambda h: (h, 0, 0))],
        out_specs=pl.BlockSpec((None, b, Q_EXT), lambda h: (h, 0, 0)),
        out_shape=jax.ShapeDtypeStruct((MLA_HEADS, b, Q_EXT), BF16),
        compiler_params=_cparams("parallel"),
        name="mla_absorb",
    )(q, gain, w_abs)


PAGE_SLOTS = 4


def _paged_attn_body(n_pages, pt_ref, q_ref, rn_ref, cache_ref, o_ref, buf_ref, sem_ref, s_ref,
                     pb_ref):
    i = pl.program_id(0)
    n_seq = pl.num_programs(0)
    slot = lax.rem(i, PAGE_SLOTS)

    def page_copy(seq, p, sl):
        return pltpu.make_async_copy(cache_ref.at[pt_ref[seq, p]], buf_ref.at[sl, p],
                                     sem_ref.at[sl])

    def fetch(seq):
        @pl.when(seq < n_seq)
        def _():
            for p in range(n_pages):
                page_copy(seq, p, lax.rem(seq, PAGE_SLOTS)).start()

    @pl.when(i == 0)
    def _():
        for ahead in range(PAGE_SLOTS - 1):
            fetch(ahead)

    fetch(i + PAGE_SLOTS - 1)

    for p in range(n_pages):
        page_copy(i, p, slot).wait()

    q = q_ref[0]
    inv_lo = MLA_ROW - MLA_HEADS
    for p in range(n_pages):
        page = buf_ref[slot, p]
        cols = slice(p * PAGE_SIZE, (p + 1) * PAGE_SIZE)
        pb_ref[:, cols] = page.astype(BF16)
        s_ref[:, cols] = page[inv_lo:MLA_ROW]
    s_all = jnp.dot(q[:, :MLA_ROW], pb_ref[...], preferred_element_type=F32) * s_ref[...]
    rn = rn_ref[0]
    qf = q[:, :MLA_ROW].astype(F32)
    pick = _iota((MLA_HEADS, MLA_ROW), 1) == _iota((MLA_HEADS, MLA_ROW), 0) + inv_lo
    inv_new = jnp.sum(jnp.where(pick, rn, 0.0), axis=-1, keepdims=True)
    s_new = jnp.sum(qf * rn, axis=-1, keepdims=True) * inv_new
    m = jnp.maximum(jnp.max(s_all, axis=-1, keepdims=True), s_new)
    e_new = jnp.exp(s_new - m)
    e = jnp.exp(s_all - m)
    den = e_new + jnp.sum(e, axis=-1, keepdims=True)
    ctx = e_new * rn[:, 0:KV_LORA] + lax.dot_general(
        e.astype(BF16), pb_ref[0:KV_LORA, :], (((1,), (1,)), ((), ())), preferred_element_type=F32)
    o_ref[0] = ctx / den


def _paged_attention(q_ext, rows_new, cache_t, page_table):
    b, n_pages = page_table.shape
    grid_spec = pltpu.PrefetchScalarGridSpec(
        num_scalar_prefetch=1,
        grid=(b,),
        in_specs=[pl.BlockSpec((1, MLA_HEADS, Q_EXT), lambda i, pt: (i, 0, 0)),
                  pl.BlockSpec((1, 1, MLA_ROW), lambda i, pt: (i, 0, 0)),
                  pl.BlockSpec(memory_space=pl.ANY)],
        out_specs=pl.BlockSpec((1, MLA_HEADS, KV_LORA), lambda i, pt: (i, 0, 0)),
        scratch_shapes=[pltpu.VMEM((PAGE_SLOTS, n_pages, MLA_ROW, PAGE_SIZE), F32),
                        pltpu.SemaphoreType.DMA((PAGE_SLOTS,)),
                        pltpu.VMEM((MLA_HEADS, n_pages * PAGE_SIZE), F32),
                        pltpu.VMEM((MLA_ROW, n_pages * PAGE_SIZE), BF16)],
    )
    return pl.pallas_call(
        functools.partial(_paged_attn_body, n_pages),
        grid_spec=grid_spec,
        out_shape=jax.ShapeDtypeStruct((b, MLA_HEADS, KV_LORA), F32),
        compiler_params=_cparams("arbitrary"),
        name="paged_attention",
    )(page_table, q_ext, rows_new, cache_t)


def _ctx_out_body(c_ref, w_ref, o_ref):
    o_ref[...] = (jnp.dot(c_ref[0].astype(BF16), w_ref[0], preferred_element_type=F32)
                  + jnp.dot(c_ref[1].astype(BF16), w_ref[1], preferred_element_type=F32)
                  ).astype(o_ref.dtype)


def _ctx_out(ctx_t, w_pairs):
    b = ctx_t.shape[1]
    return pl.pallas_call(
        _ctx_out_body,
        grid=(MLA_HEADS // 2,),
        in_specs=[pl.BlockSpec((2, b, KV_LORA), lambda j: (j, 0, 0)),
                  pl.BlockSpec((None, 2, KV_LORA, LANES), lambda j: (j, 0, 0, 0))],
        out_specs=pl.BlockSpec((b, LANES), lambda j: (0, j)),
        out_shape=jax.ShapeDtypeStruct((b, MLA_HEADS * V_HEAD), BF16),
        compiler_params=_cparams("parallel"),
        name="mla_ctx_out",
    )(ctx_t, w_pairs)


def _lane_pad(x, lo, width):
    pad = [(0, 0)] * (x.ndim - 1) + [(lo, width - lo - x.shape[-1])]
    return jnp.pad(x.astype(F32), pad)


def _gate_tile(rows):
    tile = jnp.stack([_lane_pad(vec, off, GATE_W) for off, vec in rows])
    return jnp.pad(tile, ((0, SUBLANES - len(rows)), (0, 0)))


def _prep_hybrid(j, w_in_a, conv_gdn_w, gdn_A_log, gdn_dt_bias, conv_ssm_w, ssm_A_log,
                 ssm_dt_bias, ssm_D):
    widths = (GDN_CONV_CH, GDN_VW, GDN_HEADS, GDN_HEADS, SSM_INNER, SSM_CONV_CH, SSM_HEADS)
    offs = np.concatenate([[0], np.cumsum(widths)])
    w = w_in_a[j].astype(BF16)
    part = lambda i: w[:, offs[i]:offs[i + 1]]
    assert (GATE_A, GATE_B, GATE_DT) == (0, GDN_HEADS, 2 * GDN_HEADS)
    small = _lane_pad(jnp.concatenate([part(2), part(3), part(6)], axis=1), 0, GATE_W)
    return dict(
        w_in=[part(0), part(1), part(4), part(5), small.astype(BF16)],
        conv_gdn=conv_gdn_w[j], conv_ssm=conv_ssm_w[j],
        gate_gdn=_gate_tile([(GATE_A, gdn_A_log[j]), (GATE_A, gdn_dt_bias[j])]),
        gate_ssm=_gate_tile([(GATE_DT, ssm_A_log[j]), (GATE_DT, ssm_dt_bias[j]),
                             (GATE_DT, ssm_D[j])]),
    )


def _prep_mla(j, norm_mix_c, w_in_c, q_a_norm, kv_a_norm, w_uq, w_uk, w_uv, q_norm, k_norm):
    half = QK_ROPE // 2
    w_in = w_in_c[j]

    def swapped(cols):
        return jnp.concatenate([-cols[..., half:], cols[..., :half]], axis=-1)

    def swapped_gain(g):
        return jnp.concatenate([g[half:], g[:half]])

    w_kr = w_in[:, Q_LORA + KV_LORA:]
    wkr = _lane_pad(jnp.concatenate([w_kr, swapped(w_kr)], axis=1), 0, LANES)
    qn, kn = q_norm[j], k_norm[j]
    gkr = _lane_pad(jnp.concatenate([kn[QK_NOPE:], swapped_gain(kn[QK_NOPE:])]), 0, LANES)
    gk = _lane_pad(kn[:QK_NOPE], 0, LANES)
    gq = jnp.concatenate([qn, swapped_gain(qn[QK_NOPE:])])

    uq = w_uq[j].reshape(Q_LORA, MLA_HEADS, QK_HEAD)
    wuq = jnp.concatenate([uq, swapped(uq[..., QK_NOPE:])], axis=-1).reshape(Q_LORA, HEAD_W)
    wuk = _lane_pad(w_uk[j], 0, LANES).reshape(KV_LORA, HEAD_W)
    wuv = _lane_pad(w_uv[j], 0, LANES).reshape(KV_LORA, HEAD_W)

    col = np.arange(HEAD_W)
    head, within = col // LANES, col % LANES
    lane = np.arange(LANES)
    inv_lane = QK_ROPE + head
    segq = ((within < QK_HEAD)[:, None] & (lane[None, :] == head[:, None]))
    segk = ((within < QK_NOPE)[:, None] & (lane[None, :] == inv_lane[:, None]))
    vone = (within == V_HEAD).astype(np.float32).reshape(1, HEAD_W)
    sel = lambda m: jnp.asarray(m, BF16)

    rope_pass = np.zeros((LANES, Q_EXT), np.float32)
    rope_pass[QK_NOPE:QK_HEAD, KV_LORA:KV_LORA + QK_ROPE] = np.eye(QK_ROPE)
    w_abs = jnp.pad(jnp.transpose(w_uk[j], (1, 2, 0)),
                    ((0, 0), (0, LANES - QK_NOPE), (0, Q_EXT - KV_LORA))) + rope_pass
    g_abs = jnp.concatenate([kn[:QK_NOPE], jnp.ones((QK_ROPE,), F32),
                             jnp.zeros((LANES - QK_HEAD,), F32)])
    uv = jnp.transpose(w_uv[j], (1, 0, 2)).reshape(MLA_HEADS // 2, 2, KV_LORA, V_HEAD)
    w_pairs = jnp.stack([_lane_pad(uv[:, 0], 0, LANES), _lane_pad(uv[:, 1], V_HEAD, LANES)], axis=1)
    row = lambda v: v.reshape(1, -1)
    return dict(
        norm=row(norm_mix_c[j]), wcq=w_in[:, :Q_LORA].astype(BF16),
        wckv=w_in[:, Q_LORA:Q_LORA + KV_LORA].astype(BF16), wkr=wkr.astype(BF16),
        qan=row(q_a_norm[j]), kvn=row(kv_a_norm[j]), wuq=wuq.astype(BF16), wuk=wuk.astype(BF16),
        wuv=wuv.astype(BF16), vone=jnp.asarray(vone), gq=row(gq), gk=row(gk), gkr=row(gkr),
        segq=sel(segq), segk=sel(segk),
        w_abs=w_abs.astype(BF16), g_abs=row(g_abs), w_pairs=w_pairs.astype(BF16),
    )


def _prep_ffn(w_gate_up, w_down, ff_tile):
    layers, d_ff, d = w_down.shape
    return (w_gate_up.astype(BF16),
            w_down.reshape(layers, d_ff // ff_tile, ff_tile, d).astype(BF16))


FF_TILE = 256
ROW_TILE = 512
FFN_ROW_TILE = 1024
MLA_ROW_TILE = 256
ATTN_TILE = 1024
ATTN_HEADS = 4
GDN_CHUNKS = 4
SSD_CHUNKS = 4


def _row_tile(rows, want):
    return want if rows % want == 0 else rows


def _trunk(x, pos0, states, cache_mla, page_table, hyb, mla, ffn, norm_mix_a, gdn_norm, conv_ssm_b,
           ssm_norm, w_out_a, w_out_c, norm_ffn):
    b, t, d = x.shape
    rows = b * t
    tm = _row_tile(rows, ROW_TILE)
    tm_f = _row_tile(rows, FFN_ROW_TILE)
    xf = x.reshape(rows, d)
    decode = states is not None

    qkv, gz, sz, xbc, small = _norm_proj(xf, norm_mix_a[0], hyb["w_in"], tm)
    if decode:
        s0, gconv0, h0, sconv0 = states
        r3 = lambda a: a.reshape(b, 1, a.shape[-1])
        n_pair = SSM_HEADS // 2
        o_gdn, s_new = _gdn_decode(qkv, gz, small, gconv0[0], s0[0], hyb["conv_gdn"],
                                   hyb["gate_gdn"], gdn_norm[0])
        o_ssd, h_new = _ssd_decode(xbc, sz, small, sconv0[0],
                                   h0[0].reshape(b, n_pair, 2 * SSM_P, SSM_N), hyb["conv_ssm"],
                                   conv_ssm_b[0], hyb["gate_ssm"], ssm_norm[0])
        gconv = jnp.concatenate([gconv0[0][:, 1:], r3(qkv)], axis=1)
        sconv = jnp.concatenate([sconv0[0][:, 1:], r3(xbc)], axis=1)
    else:
        r3 = lambda a: a.reshape(b, t, a.shape[-1])
        o_gdn, s_new = _gdn_prompt(r3(qkv), r3(gz), r3(small), hyb["conv_gdn"], hyb["gate_gdn"],
                                   gdn_norm[0], GDN_CHUNKS)
        o_ssd, h_new = _ssd_prompt(r3(xbc), r3(sz), r3(small), hyb["conv_ssm"], conv_ssm_b[0],
                                   hyb["gate_ssm"], ssm_norm[0], SSD_CHUNKS)
        gconv = r3(qkv)[:, t - (CONV_W - 1):]
        sconv = r3(xbc)[:, t - (CONV_W - 1):]
    h_new = h_new.reshape(b, SSM_HEADS, SSM_P, SSM_N)
    wo = w_out_a[0].astype(BF16)
    x1 = _res_ffn(xf, [o_gdn.reshape(rows, GDN_VW), o_ssd.reshape(rows, SSM_INNER)],
                  [wo[:GDN_VW], wo[GDN_VW:]], norm_ffn[0], 0, *ffn, tm_f)

    tm_c = _row_tile(rows, MLA_ROW_TILE)
    q_scale = QK_HEAD ** -0.5 * (1.0 if decode else math.log2(math.e))
    q, k, v, mla_rows = _mla_proj(x1, t, pos0, mla, tm_c, F32 if decode else BF16, q_scale)
    if decode:
        q_ext = jnp.transpose(_absorb(q, mla["g_abs"], mla["w_abs"]), (1, 0, 2))
        ctx = _paged_attention(q_ext, mla_rows.reshape(b, 1, MLA_ROW),
                               jnp.swapaxes(cache_mla[0], 1, 2), page_table)
        attn = _ctx_out(jnp.transpose(ctx, (1, 0, 2)), mla["w_pairs"])
    else:
        r3 = lambda a: a.reshape(b, t, a.shape[-1])
        attn = _flash_attention(r3(q), r3(k), r3(v), _row_tile(t, ATTN_TILE),
                                ATTN_HEADS).reshape(rows, -1)
    x2 = _res_ffn(x1, [attn], [w_out_c[0].astype(BF16)], norm_ffn[1], 1, *ffn, tm_f)
    return (x2.reshape(b, t, d), s_new[None], gconv[None], h_new[None], sconv[None],
            mla_rows.reshape(1, b, t, MLA_ROW))


def kernel(x_prompt, x_sample, state_gdn, state_gdn_conv, state_ssm, state_ssm_conv, cache_mla, page_table, norm_mix_a, w_in_a, conv_gdn_w, gdn_A_log, gdn_dt_bias, gdn_norm, conv_ssm_w, conv_ssm_b, ssm_A_log, ssm_dt_bias, ssm_D, ssm_norm, w_out_a, norm_mix_c, w_in_c, q_a_norm, kv_a_norm, w_uq, w_uk, w_uv, q_norm, k_norm, w_out_c, norm_ffn, w_gate_up, w_down):
    assert w_in_a.shape[0] == 1 and w_in_c.shape[0] == 1 and norm_ffn.shape[0] == 2
    hyb = _prep_hybrid(0, w_in_a, conv_gdn_w, gdn_A_log, gdn_dt_bias, conv_ssm_w, ssm_A_log,
                       ssm_dt_bias, ssm_D)
    mla = _prep_mla(0, norm_mix_c, w_in_c, q_a_norm, kv_a_norm, w_uq, w_uk, w_uv, q_norm, k_norm)
    ffn = _prep_ffn(w_gate_up, w_down, FF_TILE)
    shared = (hyb, mla, ffn, norm_mix_a, gdn_norm, conv_ssm_b, ssm_norm, w_out_a, w_out_c, norm_ffn)
    prompt = _trunk(x_prompt, 0, None, None, None, *shared)
    past_len = page_table.shape[1] * PAGE_SIZE
    sample = _trunk(x_sample, past_len, (state_gdn, state_gdn_conv, state_ssm, state_ssm_conv),
                    cache_mla, page_table, *shared)
    return (prompt[0], sample[0]) + prompt[1:] + sample[1:]
```

```python
import functools
import math

import jax
import jax.numpy as jnp
import numpy as np
from jax import lax
from jax.experimental import pallas as pl
from jax.experimental.pallas import tpu as pltpu

F32 = jnp.float32
BF16 = jnp.bfloat16

EPS = 1e-6
CONV_W = 4
CHUNK = 64
PAGE_SIZE = 128
GDN_HEADS = 4
GDN_DK = 128
GDN_DV = 128
GDN_QK = GDN_HEADS * GDN_DK
GDN_VW = GDN_HEADS * GDN_DV
GDN_CONV_CH = 2 * GDN_QK + GDN_VW
SSM_HEADS = 8
SSM_P = 64
SSM_N = 128
SSM_GROUPS = 2
SSM_INNER = SSM_HEADS * SSM_P
SSM_BC = SSM_GROUPS * SSM_N
SSM_CONV_CH = SSM_INNER + 2 * SSM_BC
MLA_HEADS = 16
Q_LORA = 512
KV_LORA = 256
QK_NOPE = 64
QK_ROPE = 32
QK_HEAD = QK_NOPE + QK_ROPE
V_HEAD = 64
MLA_ROW = KV_LORA + QK_ROPE + MLA_HEADS
ROPE_THETA = 10000.0

LANES = 128
SUBLANES = 8
HEAD_W = MLA_HEADS * LANES
GATE_W = LANES
GATE_A, GATE_B, GATE_DT = 0, GDN_HEADS, 2 * GDN_HEADS
Q_EXT = 3 * LANES
VMEM_LIMIT = 48 * 1024 * 1024

_HI = lax.Precision.HIGHEST


def _cparams(*sem):
    return pltpu.CompilerParams(dimension_semantics=sem, vmem_limit_bytes=VMEM_LIMIT)


def _const_spec(shape):
    zeros = (0,) * len(shape)
    return pl.BlockSpec(shape, lambda *_: zeros, pipeline_mode=pl.Buffered(1))


def _mm(a, b):
    return jnp.dot(a.astype(BF16), b.astype(BF16), preferred_element_type=F32)


def _mm_nt(a, b):
    return lax.dot_general(a.astype(BF16), b.astype(BF16), (((1,), (1,)), ((), ())),
                           preferred_element_type=F32)


def _mm_tn(a, b):
    return lax.dot_general(a.astype(BF16), b.astype(BF16), (((0,), (0,)), ((), ())),
                           preferred_element_type=F32)


def _sigmoid(x):
    return 1.0 / (1.0 + jnp.exp(-x))


def _silu(x):
    return x * _sigmoid(x)


def _softplus(x):
    return jnp.maximum(x, 0.0) + jnp.log(1.0 + jnp.exp(-jnp.abs(x)))


def _rms(x, gain):
    return x * lax.rsqrt(jnp.mean(x * x, axis=-1, keepdims=True) + EPS) * gain


def _l2n(x):
    return x * lax.rsqrt(jnp.sum(x * x, axis=-1, keepdims=True) + EPS)


def _iota(shape, dim):
    return lax.broadcasted_iota(jnp.int32, shape, dim)


def _chunk_cumsum(g, tb):
    row, col = _iota((tb, tb), 0), _iota((tb, tb), 1)
    shift = int(math.log2(CHUNK))
    same_chunk = lax.shift_right_logical(row, shift) == lax.shift_right_logical(col, shift)
    tril = jnp.where((col <= row) & same_chunk, 1.0, 0.0).astype(F32)
    return jnp.dot(tril, g, precision=_HI, preferred_element_type=F32)


def _rows_of(cols, lane0):
    tile = (SUBLANES, LANES)
    sel = jnp.where(_iota(tile, 1) == _iota(tile, 0) + lane0, 1.0, 0.0).astype(F32)
    return lax.dot_general(sel, cols, (((1,), (1,)), ((), ())), precision=_HI,
                           preferred_element_type=F32)


def _causal_conv_block(x_ref, xp_ref, cw, tb, first):
    @pl.when(first)
    def _():
        xp_ref[0:SUBLANES, :] = jnp.zeros((SUBLANES, xp_ref.shape[1]), F32)

    xp_ref[SUBLANES:SUBLANES + tb, :] = x_ref[...]
    off = SUBLANES - (CONV_W - 1)
    y = cw[0:1] * xp_ref[off:off + tb, :]
    for i in range(1, CONV_W):
        y = y + cw[i:i + 1] * xp_ref[off + i:off + i + tb, :]
    xp_ref[0:SUBLANES, :] = xp_ref[tb:tb + SUBLANES, :]
    return y


def _norm_proj_body(n_out, x_ref, g_ref, *refs):
    xn = _rms(x_ref[...], g_ref[...]).astype(BF16)
    for w_ref, o_ref in zip(refs[:n_out], refs[n_out:]):
        o_ref[...] = jnp.dot(xn, w_ref[...], preferred_element_type=F32)


def _norm_proj(x, gain, weights, tm):
    rows, d = x.shape
    n = len(weights)
    return pl.pallas_call(
        functools.partial(_norm_proj_body, n),
        grid=(rows // tm,),
        in_specs=[pl.BlockSpec((tm, d), lambda i: (i, 0)), _const_spec((1, d))]
        + [_const_spec(w.shape) for w in weights],
        out_specs=[pl.BlockSpec((tm, w.shape[1]), lambda i: (i, 0)) for w in weights],
        out_shape=[jax.ShapeDtypeStruct((rows, w.shape[1]), F32) for w in weights],
        compiler_params=_cparams("parallel"),
        name="norm_proj",
    )(x, gain.reshape(1, d), *weights)


def _gdn_prompt_body(nc, qkv_ref, gz_ref, sm_ref, cw_ref, gp_ref, gn_ref, o_ref, s_ref, xp_ref):
    tb = nc * CHUNK
    first = pl.program_id(1) == 0

    @pl.when(first)
    def _():
        s_ref[...] = jnp.zeros(s_ref.shape, F32)

    y = _silu(_causal_conv_block(qkv_ref.at[0], xp_ref, cw_ref[...], tb, first))
    sm, gp = sm_ref[0], gp_ref[...]
    g_all = -jnp.exp(gp[0:1]) * _softplus(sm + gp[1:2])
    beta_all = _sigmoid(sm)
    cum_all = _chunk_cumsum(g_all, tb)
    cum_rows = _rows_of(cum_all, GATE_A)
    incl = _iota((CHUNK, CHUNK), 1) <= _iota((CHUNK, CHUNK), 0)
    strict = _iota((CHUNK, CHUNK), 1) < _iota((CHUNK, CHUNK), 0)
    gn = gn_ref[...]
    pairs = [(c, h) for c in range(nc) for h in range(GDN_HEADS)]
    qs, ks, vs = [], [], []
    for h in range(GDN_HEADS):
        lo = h * GDN_DK
        qs.append(_l2n(y[:, lo:lo + GDN_DK]) * GDN_DK ** -0.5)
        ks.append(_l2n(y[:, GDN_QK + lo:GDN_QK + lo + GDN_DK]))
        vs.append(y[:, 2 * GDN_QK + lo:2 * GDN_QK + lo + GDN_DV])
    pre = []
    for c, h in pairs:
        r = slice(c * CHUNK, (c + 1) * CHUNK)
        q, k, v = qs[h][r], ks[h][r], vs[h][r]
        cum_c = cum_all[r, GATE_A + h:GATE_A + h + 1]
        cum_r = cum_rows[h:h + 1, c * CHUNK:(c + 1) * CHUNK]
        beta = beta_all[r, GATE_B + h:GATE_B + h + 1]
        decay = jnp.where(incl, jnp.exp(jnp.minimum(cum_c - cum_r, 0.0)), 0.0)
        e_c = jnp.exp(cum_c)
        kb = k * beta
        last = cum_c[CHUNK - 1:CHUNK]
        pre.append(dict(q=q.astype(BF16), k=k.astype(BF16), kb=kb, decay=decay,
                        rhs=jnp.concatenate([v * beta, kb * e_c], axis=1).astype(BF16),
                        qe=q * e_c, kd=(k * jnp.exp(last - cum_c)).astype(BF16),
                        e_last=jnp.exp(last)))
    lows = [jnp.where(strict, _mm_nt(p["kb"], p["k"]) * p["decay"], 0.0) for p in pre]
    attns = [(_mm_nt(p["q"], p["k"]) * p["decay"]).astype(BF16) for p in pre]
    row, col = _iota((CHUNK, CHUNK), 0), _iota((CHUNK, CHUNK), 1)
    invs = None
    for level in range(int(math.log2(CHUNK))):
        brow = lax.shift_right_logical(row, level)
        joins = ((brow & 1) == 1) & (lax.shift_right_logical(col, level) == brow - 1)
        subs = [jnp.where(joins, low, 0.0) for low in lows]
        if invs is None:
            eye = jnp.where(row == col, 1.0, 0.0).astype(F32)
            invs = [eye - sub for sub in subs]
        else:
            invs = [inv - _mm(_mm(inv, sub), inv) for inv, sub in zip(invs, subs)]
    sols = [_mm(inv, p["rhs"]).astype(BF16) for inv, p in zip(invs, pre)]
    k_sol = [_mm_tn(p["kd"], sol) for p, sol in zip(pre, sols)]
    a_sol = [_mm(attn, sol) for attn, sol in zip(attns, sols)]
    q_eff = [(p["qe"] - a[:, GDN_DV:]).astype(BF16) for p, a in zip(pre, a_sol)]
    m_eff = [ks[:, GDN_DV:].astype(BF16) for ks in k_sol]
    states = [s_ref[0, h] for h in range(GDN_HEADS)]
    for c in range(nc):
        r = slice(c * CHUNK, (c + 1) * CHUNK)
        idx = [c * GDN_HEADS + h for h in range(GDN_HEADS)]
        s_bf = [st.astype(BF16) for st in states]
        outs = [_mm(q_eff[i], s_bf[h]) + a_sol[i][:, :GDN_DV] for h, i in enumerate(idx)]
        states = [states[h] * pre[i]["e_last"] - _mm(m_eff[i], s_bf[h]) + k_sol[i][:, :GDN_DV]
                  for h, i in enumerate(idx)]
        for h, o in enumerate(outs):
            lo = h * GDN_DV
            gate = _silu(gz_ref[0, r, lo:lo + GDN_DV])
            o_ref[0, r, lo:lo + GDN_DV] = (_rms(o, gn) * gate).astype(o_ref.dtype)
    for h in range(GDN_HEADS):
        s_ref[0, h] = states[h]


def _gdn_prompt(qkv, gz, small, conv_w, gate_par, gdn_norm, nc):
    b, t, _ = qkv.shape
    tb = nc * CHUNK
    blk = lambda w: pl.BlockSpec((1, tb, w), lambda i, j: (i, j, 0))
    return pl.pallas_call(
        functools.partial(_gdn_prompt_body, nc),
        grid=(b, t // tb),
        in_specs=[blk(GDN_CONV_CH), blk(GDN_VW), blk(GATE_W), _const_spec(conv_w.shape),
                  _const_spec(gate_par.shape), _const_spec((1, GDN_DV))],
        out_specs=[blk(GDN_VW),
                   pl.BlockSpec((1, GDN_HEADS, GDN_DK, GDN_DV), lambda i, j: (i, 0, 0, 0))],
        out_shape=[jax.ShapeDtypeStruct((b, t, GDN_VW), BF16),
                   jax.ShapeDtypeStruct((b, GDN_HEADS, GDN_DK, GDN_DV), F32)],
        scratch_shapes=[pltpu.VMEM((tb + SUBLANES, GDN_CONV_CH), F32)],
        compiler_params=_cparams("parallel", "arbitrary"),
        name="gdn_prompt",
    )(qkv, gz, small, conv_w, gate_par, gdn_norm.reshape(1, GDN_DV))


def _ssd_prompt_body(nc, xbc_ref, sz_ref, sm_ref, cw_ref, cb_ref, sp_ref, sn_ref, o_ref, h_ref,
                     xp_ref):
    tb = nc * CHUNK
    first = pl.program_id(1) == 0

    @pl.when(first)
    def _():
        h_ref[...] = jnp.zeros(h_ref.shape, F32)

    y = _silu(_causal_conv_block(xbc_ref.at[0], xp_ref, cw_ref[...], tb, first) + cb_ref[...])
    sm, sp = sm_ref[0], sp_ref[...]
    dt_all = _softplus(sm + sp[1:2])
    cum_all = _chunk_cumsum(dt_all * -jnp.exp(sp[0:1]), tb)
    cum_rows = _rows_of(cum_all, GATE_DT)
    incl = _iota((CHUNK, CHUNK), 1) <= _iota((CHUNK, CHUNK), 0)
    lane_lo = _iota((CHUNK, LANES), 1) < SSM_P
    row_lo = _iota((2 * SSM_P, SSM_N), 0) < SSM_P
    par_lo = _iota((1, LANES), 1) < SSM_P
    sn = sn_ref[...]
    n_pair = SSM_HEADS // 2
    pairs_per_group = n_pair // SSM_GROUPS
    gw = SSM_INNER // SSM_GROUPS
    b_bf, c_bf, scores = {}, {}, {}
    for c in range(nc):
        r = slice(c * CHUNK, (c + 1) * CHUNK)
        for grp in range(SSM_GROUPS):
            lo = SSM_INNER + grp * SSM_N
            b_bf[c, grp] = y[r, lo:lo + SSM_N].astype(BF16)
            c_bf[c, grp] = y[r, lo + SSM_BC:lo + SSM_BC + SSM_N].astype(BF16)
            scores[c, grp] = _mm_nt(c_bf[c, grp], b_bf[c, grp])
    pre = {}
    for c in range(nc):
        r = slice(c * CHUNK, (c + 1) * CHUNK)
        for j in range(n_pair):
            grp = j // pairs_per_group
            cums, segs, dts = [], [], []
            for head in (2 * j, 2 * j + 1):
                cum_c = cum_all[r, GATE_DT + head:GATE_DT + head + 1]
                cum_r = cum_rows[head:head + 1, c * CHUNK:(c + 1) * CHUNK]
                cums.append(cum_c)
                segs.append(jnp.where(incl, jnp.exp(jnp.minimum(cum_c - cum_r, 0.0)), 0.0))
                dts.append(dt_all[r, GATE_DT + head:GATE_DT + head + 1])
            x_pair = y[r, j * LANES:(j + 1) * LANES]
            xdt = x_pair * jnp.where(lane_lo, dts[0], dts[1])
            last = [cm[CHUNK - 1:CHUNK] for cm in cums]
            dec = jnp.where(lane_lo, jnp.exp(last[0] - cums[0]), jnp.exp(last[1] - cums[1]))
            d_pair = jnp.where(par_lo, sp[2:3, GATE_DT + 2 * j:GATE_DT + 2 * j + 1],
                               sp[2:3, GATE_DT + 2 * j + 1:GATE_DT + 2 * j + 2])
            pre[c, j] = dict(
                y=(_mm(scores[c, grp] * segs[0], jnp.where(lane_lo, xdt, 0.0))
                   + _mm(scores[c, grp] * segs[1], jnp.where(lane_lo, 0.0, xdt))
                   + d_pair * x_pair),
                e=jnp.where(lane_lo, jnp.exp(cums[0]), jnp.exp(cums[1])),
                xdec=(xdt * dec).astype(BF16),
                e_last=jnp.where(row_lo, jnp.exp(last[0]), jnp.exp(last[1])))
    states = [h_ref[0, j] for j in range(n_pair)]
    for c in range(nc):
        r = slice(c * CHUNK, (c + 1) * CHUNK)
        ys = [pre[c, j]["y"] + _mm_nt(c_bf[c, j // pairs_per_group], states[j]) * pre[c, j]["e"]
              for j in range(n_pair)]
        states = [states[j] * pre[c, j]["e_last"]
                  + _mm_tn(pre[c, j]["xdec"], b_bf[c, j // pairs_per_group]) for j in range(n_pair)]
        for grp in range(SSM_GROUPS):
            outs = [ys[j] * _silu(sz_ref[0, r, j * LANES:(j + 1) * LANES])
                    for j in range(grp * pairs_per_group, (grp + 1) * pairs_per_group)]
            inv = lax.rsqrt(sum(jnp.sum(o * o, axis=-1, keepdims=True) for o in outs) / gw + EPS)
            for jj, o in enumerate(outs):
                lo = grp * gw + jj * LANES
                o_ref[0, r, lo:lo + LANES] = (o * inv * sn[:, lo:lo + LANES]).astype(o_ref.dtype)
    for j in range(n_pair):
        h_ref[0, j] = states[j]


def _ssd_prompt(xbc, sz, small, conv_w, conv_b, ssm_par, ssm_norm, nc):
    b, t, _ = xbc.shape
    tb = nc * CHUNK
    blk = lambda w: pl.BlockSpec((1, tb, w), lambda i, j: (i, j, 0))
    n_pair = SSM_HEADS // 2
    return pl.pallas_call(
        functools.partial(_ssd_prompt_body, nc),
        grid=(b, t // tb),
        in_specs=[blk(SSM_CONV_CH), blk(SSM_INNER), blk(GATE_W), _const_spec(conv_w.shape),
                  _const_spec((1, SSM_CONV_CH)), _const_spec(ssm_par.shape),
                  _const_spec((1, SSM_INNER))],
        out_specs=[blk(SSM_INNER),
                   pl.BlockSpec((1, n_pair, 2 * SSM_P, SSM_N), lambda i, j: (i, 0, 0, 0))],
        out_shape=[jax.ShapeDtypeStruct((b, t, SSM_INNER), BF16),
                   jax.ShapeDtypeStruct((b, n_pair, 2 * SSM_P, SSM_N), F32)],
        scratch_shapes=[pltpu.VMEM((tb + SUBLANES, SSM_CONV_CH), F32)],
        compiler_params=_cparams("parallel", "arbitrary"),
        name="ssd_prompt",
    )(xbc, sz, small, conv_w, conv_b.reshape(1, -1), ssm_par, ssm_norm.reshape(1, -1))


def _col_bcast(row):
    return jnp.transpose(jnp.broadcast_to(row, (LANES, LANES)))


def _decode_conv(new_ref, cv_ref, cw):
    y = cw[CONV_W - 1:CONV_W] * new_ref[...]
    for i in range(CONV_W - 1):
        y = y + cw[i:i + 1] * cv_ref[:, i, :]
    return y


def _gdn_decode_body(bb, qkv_ref, gz_ref, sm_ref, cv_ref, s0_ref, cw_ref, gp_ref, gn_ref, o_ref,
                     s_ref, raw_ref):
    y = _silu(_decode_conv(qkv_ref, cv_ref, cw_ref[...]))
    sm, gp = sm_ref[...], gp_ref[...]
    a_all = jnp.exp(-jnp.exp(gp[0:1]) * _softplus(sm + gp[1:2]))
    beta_all = _sigmoid(sm)
    qs, ks, vs = [], [], []
    for h in range(GDN_HEADS):
        lo = h * GDN_DK
        qs.append(_l2n(y[:, lo:lo + GDN_DK]) * GDN_DK ** -0.5)
        ks.append(_l2n(y[:, GDN_QK + lo:GDN_QK + lo + GDN_DK]))
        vs.append(y[:, 2 * GDN_QK + lo:2 * GDN_QK + lo + GDN_DV])
    pairs = [(b, h) for b in range(bb) for h in range(GDN_HEADS)]
    k_col = [_col_bcast(ks[h][b:b + 1]) for b, h in pairs]
    q_col = [_col_bcast(qs[h][b:b + 1]) for b, h in pairs]
    decayed = [s0_ref[b, h] * a_all[b:b + 1, GATE_A + h:GATE_A + h + 1] for b, h in pairs]
    v_new = [beta_all[b:b + 1, GATE_B + h:GATE_B + h + 1]
             * (vs[h][b:b + 1] - jnp.sum(k_col[i] * decayed[i], axis=0, keepdims=True))
             for i, (b, h) in enumerate(pairs)]
    for i, (b, h) in enumerate(pairs):
        state = decayed[i] + k_col[i] * v_new[i]
        s_ref[b, h] = state
        raw_ref[b:b + 1, h * GDN_DV:(h + 1) * GDN_DV] = jnp.sum(q_col[i] * state, axis=0,
                                                                keepdims=True)
    gn = gn_ref[...]
    for h in range(GDN_HEADS):
        hs = slice(h * GDN_DV, (h + 1) * GDN_DV)
        o_ref[:, hs] = (_rms(raw_ref[:, hs], gn) * _silu(gz_ref[:, hs])).astype(o_ref.dtype)


DECODE_BLOCK = SUBLANES


def _gdn_decode(qkv, gz, small, conv0, s0, conv_w, gate_par, gdn_norm):
    b = qkv.shape[0]
    bb = DECODE_BLOCK
    row = lambda w: pl.BlockSpec((bb, w), lambda i: (i, 0))
    st = pl.BlockSpec((bb, GDN_HEADS, GDN_DK, GDN_DV), lambda i: (i, 0, 0, 0))
    return pl.pallas_call(
        functools.partial(_gdn_decode_body, bb),
        grid=(b // bb,),
        in_specs=[row(GDN_CONV_CH), row(GDN_VW), row(GATE_W),
                  pl.BlockSpec((bb, CONV_W - 1, GDN_CONV_CH), lambda i: (i, 0, 0)), st,
                  _const_spec(conv_w.shape), _const_spec(gate_par.shape), _const_spec((1, GDN_DV))],
        out_specs=[row(GDN_VW), st],
        out_shape=[jax.ShapeDtypeStruct((b, GDN_VW), F32), jax.ShapeDtypeStruct(s0.shape, F32)],
        scratch_shapes=[pltpu.VMEM((bb, GDN_VW), F32)],
        compiler_params=_cparams("parallel"),
        name="gdn_decode",
    )(qkv, gz, small, conv0, s0, conv_w, gate_par, gdn_norm.reshape(1, GDN_DV))


def _ssd_decode_body(bb, xbc_ref, sz_ref, sm_ref, cv_ref, h0_ref, cw_ref, cb_ref, sp_ref, sn_ref,
                     o_ref, h_ref, raw_ref):
    y = _silu(_decode_conv(xbc_ref, cv_ref, cw_ref[...]) + cb_ref[...])
    sm, sp = sm_ref[...], sp_ref[...]
    dt_all = _softplus(sm + sp[1:2])
    da_all = jnp.exp(dt_all * -jnp.exp(sp[0:1]))
    par_lo = _iota((1, LANES), 1) < SSM_P
    row_lo = _iota((2 * SSM_P, SSM_N), 0) < SSM_P
    n_pair = SSM_HEADS // 2
    pairs_per_group = n_pair // SSM_GROUPS
    pairs = [(b, j) for b in range(bb) for j in range(n_pair)]
    x_col, states = [], []
    for b, j in pairs:
        la, lb = GATE_DT + 2 * j, GATE_DT + 2 * j + 1
        row = slice(b, b + 1)
        xdt = y[row, j * LANES:(j + 1) * LANES] * jnp.where(par_lo, dt_all[row, la:la + 1],
                                                           dt_all[row, lb:lb + 1])
        x_col.append(_col_bcast(xdt))
    for i, (b, j) in enumerate(pairs):
        la, lb = GATE_DT + 2 * j, GATE_DT + 2 * j + 1
        row = slice(b, b + 1)
        grp = j // pairs_per_group
        b_g = y[row, SSM_INNER + grp * SSM_N:SSM_INNER + (grp + 1) * SSM_N]
        state = (h0_ref[b, j] * jnp.where(row_lo, da_all[row, la:la + 1], da_all[row, lb:lb + 1])
                 + x_col[i] * b_g)
        h_ref[b, j] = state
        states.append(state)
    lane_id = _iota((LANES, LANES), 1)
    for j in range(n_pair):
        lo = SSM_INNER + SSM_BC + (j // pairs_per_group) * SSM_N
        cols = jnp.zeros((LANES, LANES), F32)
        for b in range(bb):
            y_col = jnp.sum(states[b * n_pair + j] * y[b:b + 1, lo:lo + SSM_N], axis=1,
                            keepdims=True)
            cols = jnp.where(lane_id == b, y_col, cols)
        raw_ref[:, j * LANES:(j + 1) * LANES] = jnp.transpose(cols)[0:bb]
    sn = sn_ref[...]
    gw = SSM_INNER // SSM_GROUPS
    for grp in range(SSM_GROUPS):
        outs = []
        for jj in range(pairs_per_group):
            j = grp * pairs_per_group + jj
            la, lb = GATE_DT + 2 * j, GATE_DT + 2 * j + 1
            js = slice(j * LANES, (j + 1) * LANES)
            d_pair = jnp.where(par_lo, sp[2:3, la:la + 1], sp[2:3, lb:lb + 1])
            outs.append((raw_ref[:, js] + d_pair * y[:, js]) * _silu(sz_ref[:, js]))
        inv = lax.rsqrt(sum(jnp.sum(o * o, axis=-1, keepdims=True) for o in outs) / gw + EPS)
        for jj, o in enumerate(outs):
            lo = grp * gw + jj * LANES
            o_ref[:, lo:lo + LANES] = (o * inv * sn[:, lo:lo + LANES]).astype(o_ref.dtype)


def _ssd_decode(xbc, sz, small, conv0, h0, conv_w, conv_b, ssm_par, ssm_norm):
    b = xbc.shape[0]
    bb = DECODE_BLOCK
    n_pair = SSM_HEADS // 2
    row = lambda w: pl.BlockSpec((bb, w), lambda i: (i, 0))
    st = pl.BlockSpec((bb, n_pair, 2 * SSM_P, SSM_N), lambda i: (i, 0, 0, 0))
    return pl.pallas_call(
        functools.partial(_ssd_decode_body, bb),
        grid=(b // bb,),
        in_specs=[row(SSM_CONV_CH), row(SSM_INNER), row(GATE_W),
                  pl.BlockSpec((bb, CONV_W - 1, SSM_CONV_CH), lambda i: (i, 0, 0)), st,
                  _const_spec(conv_w.shape), _const_spec((1, SSM_CONV_CH)),
                  _const_spec(ssm_par.shape), _const_spec((1, SSM_INNER))],
        out_specs=[row(SSM_INNER), st],
        out_shape=[jax.ShapeDtypeStruct((b, SSM_INNER), F32), jax.ShapeDtypeStruct(h0.shape, F32)],
        scratch_shapes=[pltpu.VMEM((bb, SSM_INNER), F32)],
        compiler_params=_cparams("parallel"),
        name="ssd_decode",
    )(xbc, sz, small, conv0, h0, conv_w, conv_b.reshape(1, -1), ssm_par, ssm_norm.reshape(1, -1))


def _res_ffn_body(n_a, n_ff, x_ref, *refs):
    a_refs, w_refs = refs[:n_a], refs[n_a:2 * n_a]
    g_ref, wgu_ref, wd_ref, o_ref, xn_ref = refs[2 * n_a:]
    ff_tile = wd_ref.shape[1]
    d_ff = n_ff * ff_tile
    x1 = x_ref[...]
    for a_ref, w_ref in zip(a_refs, w_refs):
        x1 = x1 + jnp.dot(a_ref[...].astype(BF16), w_ref[...], preferred_element_type=F32)
    xn_ref[...] = _rms(x1, g_ref[...]).astype(BF16)
    o_ref[...] = x1

    def step(c, carry):
        xn = xn_ref[...]
        lo = pl.multiple_of(c * ff_tile, ff_tile)
        gate = jnp.dot(xn, wgu_ref[:, pl.ds(lo, ff_tile)], preferred_element_type=F32)
        up = jnp.dot(xn, wgu_ref[:, pl.ds(d_ff + lo, ff_tile)], preferred_element_type=F32)
        hid = (_silu(gate) * up).astype(BF16)
        o_ref[...] += jnp.dot(hid, wd_ref[c], preferred_element_type=F32)
        return carry

    lax.fori_loop(0, n_ff, step, 0, unroll=True)


def _res_ffn(x, acts, projs, gain, layer, wgu, wd, tm):
    rows, d = x.shape
    n_a, n_ff = len(acts), wd.shape[1]

    def layer_spec(w):
        tail = (0,) * (w.ndim - 1)
        return pl.BlockSpec((None,) + w.shape[1:], lambda i: (layer,) + tail,
                            pipeline_mode=pl.Buffered(1))

    return pl.pallas_call(
        functools.partial(_res_ffn_body, n_a, n_ff),
        grid=(rows // tm,),
        in_specs=[pl.BlockSpec((tm, d), lambda i: (i, 0))]
        + [pl.BlockSpec((tm, a.shape[1]), lambda i: (i, 0)) for a in acts]
        + [_const_spec(p.shape) for p in projs]
        + [_const_spec((1, d)), layer_spec(wgu), layer_spec(wd)],
        out_specs=pl.BlockSpec((tm, d), lambda i: (i, 0)),
        out_shape=jax.ShapeDtypeStruct((rows, d), F32),
        scratch_shapes=[pltpu.VMEM((tm, d), BF16)],
        compiler_params=_cparams("parallel"),
        name="res_ffn",
    )(x, *acts, *projs, gain.reshape(1, d), wgu, wd)


def _mla_proj_body(tm, q_scale, x_ref, cos_ref, sin_ref, g_ref, wcq_ref, wckv_ref, wkr_ref,
                   qan_ref, kvn_ref, wuq_ref, wuk_ref, wuv_ref, vone_ref, gq_ref, gk_ref, gkr_ref,
                   segq_ref, segk_ref, q_ref, k_ref, v_ref, rows_ref):
    xn = _rms(x_ref[...], g_ref[...]).astype(BF16)
    lane = _iota((tm, LANES), 1)
    cos_t, sin_t = cos_ref[...], sin_ref[...]

    kr = jnp.dot(xn, wkr_ref[...], preferred_element_type=F32)
    ssq_kr = jnp.sum(jnp.where(lane < QK_ROPE, kr * kr, 0.0), axis=-1, keepdims=True)
    krg = kr * gkr_ref[...]
    kr_rot = krg * cos_t + pltpu.roll(krg, LANES - QK_ROPE, 1) * sin_t

    c = _rms(jnp.dot(xn, wckv_ref[...], preferred_element_type=F32), kvn_ref[...])
    cb = c.astype(BF16)
    kx = jnp.dot(cb, wuk_ref[...], preferred_element_type=F32)
    ssq_k = jnp.dot((kx * kx).astype(BF16), segk_ref[...],
                    preferred_element_type=F32)
    inv_r = lax.rsqrt((ssq_k + ssq_kr) / QK_HEAD + EPS)
    tail = jnp.where(lane < QK_ROPE, kr_rot, inv_r)
    rows_ref[:, 0:KV_LORA] = c
    rows_ref[:, KV_LORA:MLA_ROW] = tail[:, 0:MLA_ROW - KV_LORA]
    kr_put = jnp.where((lane >= QK_NOPE) & (lane < QK_HEAD), pltpu.roll(kr_rot, QK_NOPE, 1), 0.0)
    gk = gk_ref[...]
    for h in range(MLA_HEADS):
        hs = slice(h * LANES, (h + 1) * LANES)
        inv_h = inv_r[:, QK_ROPE + h:QK_ROPE + h + 1]
        k_ref[:, hs] = ((kx[:, hs] * gk + kr_put) * inv_h).astype(k_ref.dtype)
    v_ref[...] = (jnp.dot(cb, wuv_ref[...], preferred_element_type=F32)
                  + vone_ref[...]).astype(v_ref.dtype)

    cq = _rms(jnp.dot(xn, wcq_ref[...], preferred_element_type=F32), qan_ref[...]).astype(BF16)
    qx = jnp.dot(cq, wuq_ref[...], preferred_element_type=F32)
    ssq_q = jnp.dot((qx * qx).astype(BF16), segq_ref[...], preferred_element_type=F32)
    inv_q = lax.rsqrt(ssq_q / QK_HEAD + EPS)
    gq = gq_ref[...]
    scale = q_scale
    in_rope = (lane >= QK_NOPE) & (lane < QK_HEAD)
    keep = jnp.where(lane < QK_NOPE, scale, jnp.where(in_rope, cos_t * scale, 0.0))
    swap = jnp.where(in_rope, sin_t * scale, 0.0)
    for h in range(MLA_HEADS):
        hs = slice(h * LANES, (h + 1) * LANES)
        t = qx[:, hs] * inv_q[:, h:h + 1] * gq
        q_ref[:, hs] = (t * keep + pltpu.roll(t, LANES - QK_ROPE, 1) * swap).astype(q_ref.dtype)


def _mla_proj(x, seq, pos0, w, tm, q_dtype, q_scale):
    rows, d = x.shape
    half = QK_ROPE // 2
    inv_freq = ROPE_THETA ** (-(jnp.arange(LANES) % half).astype(F32) / half)
    ang = (pos0 + jnp.arange(seq, dtype=F32))[:, None] * inv_freq[None, :]
    tr = tm if seq > 1 else 1
    nblk = seq // tr
    rope_spec = pl.BlockSpec((tr, LANES), lambda i: (i % nblk, 0))
    names = ("norm", "wcq", "wckv", "wkr", "qan", "kvn", "wuq", "wuk", "wuv", "vone", "gq", "gk",
             "gkr", "segq", "segk")
    consts = [w[n] for n in names]
    out_w = (HEAD_W, HEAD_W, HEAD_W, MLA_ROW)
    out_dt = (q_dtype, BF16, BF16, F32)
    return pl.pallas_call(
        functools.partial(_mla_proj_body, tm, q_scale),
        grid=(rows // tm,),
        in_specs=[pl.BlockSpec((tm, d), lambda i: (i, 0)), rope_spec, rope_spec]
        + [_const_spec(c.shape) for c in consts],
        out_specs=[pl.BlockSpec((tm, ow), lambda i: (i, 0)) for ow in out_w],
        out_shape=[jax.ShapeDtypeStruct((rows, ow), dt) for ow, dt in zip(out_w, out_dt)],
        compiler_params=_cparams("parallel"),
        name="mla_proj",
    )(x, jnp.cos(ang), jnp.sin(ang), *consts)


def _flash_body(tq, nh, q_ref, k_ref, v_ref, o_ref):
    qi = pl.program_id(2)
    half = tq // 2
    heads = [slice(j * LANES, (j + 1) * LANES) for j in range(nh)]
    q = [q_ref[0, :, hs] for hs in heads]

    def attend(carry, qs, kv_rows, mask):
        s = [lax.dot_general(qs[j], k_ref[0, kv_rows, hs], (((1,), (1,)), ((), ())),
                             preferred_element_type=F32) for j, hs in enumerate(heads)]
        if mask is not None:
            s = [jnp.where(mask, sj, -jnp.inf) for sj in s]
        m_new = [jnp.maximum(carry[j][0], jnp.max(s[j], axis=-1, keepdims=True)) for j in range(nh)]
        p = [jnp.exp2((s[j] - m_new[j]).astype(BF16)) for j in range(nh)]
        pv = [jnp.dot(p[j], v_ref[0, kv_rows, hs], preferred_element_type=F32)
              for j, hs in enumerate(heads)]
        return tuple((m_new[j], jnp.exp2(carry[j][0] - m_new[j]) * carry[j][1] + pv[j])
                     for j in range(nh))

    init = tuple((jnp.full((tq, 1), -jnp.inf, F32), jnp.zeros((tq, LANES), F32)) for _ in range(nh))
    def two_blocks(kb, cr):
        for part in range(2):
            start = pl.multiple_of(kb * tq + part * half, half)
            cr = attend(cr, q, pl.ds(start, half), None)
        return cr

    carry = lax.fori_loop(0, qi, two_blocks, init)
    base = pl.multiple_of(qi * tq, tq)
    carry = attend(carry, q, pl.ds(base, half),
                   _iota((tq, half), 1) <= _iota((tq, half), 0))
    lower = attend(tuple((m[half:], acc[half:]) for m, acc in carry), [qj[half:] for qj in q],
                   pl.ds(base + half, half), _iota((half, half), 1) <= _iota((half, half), 0))
    low = _iota((half, LANES), 1) < V_HEAD
    for part, rows in ((tuple((m[:half], acc[:half]) for m, acc in carry), slice(0, half)),
                       (lower, slice(half, tq))):
        outs = [acc / acc[:, V_HEAD:V_HEAD + 1] for _, acc in part]
        for j in range(nh // 2):
            o_ref[0, rows, j * LANES:(j + 1) * LANES] = jnp.where(
                low, outs[2 * j], pltpu.roll(outs[2 * j + 1], V_HEAD, 1)).astype(o_ref.dtype)


def _flash_attention(q, k, v, tq, nh):
    b, t, _ = q.shape
    kv_spec = pl.BlockSpec((1, t, nh * LANES), lambda i, j, n: (i, 0, j))
    return pl.pallas_call(
        functools.partial(_flash_body, tq, nh),
        grid=(b, MLA_HEADS // nh, t // tq),
        in_specs=[pl.BlockSpec((1, tq, nh * LANES), lambda i, j, n: (i, n, j)), kv_spec, kv_spec],
        out_specs=pl.BlockSpec((1, tq, nh * V_HEAD), lambda i, j, n: (i, n, j)),
        out_shape=jax.ShapeDtypeStruct((b, t, MLA_HEADS * V_HEAD), BF16),
        compiler_params=_cparams("parallel", "parallel", "arbitrary"),
        name="flash_attention",
    )(q, k, v)


def _absorb_body(q_ref, g_ref, w_ref, o_ref):
    t = (q_ref[...] * g_ref[...]).astype(BF16)
    o_ref[...] = jnp.dot(t, w_ref[...], preferred_element_type=F32).astype(o_ref.dtype)


def _absorb(q, gain, w_abs):
    b = q.shape[0]
    return pl.pallas_call(
        _absorb_body,
        grid=(MLA_HEADS,),
        in_specs=[pl.BlockSpec((b, LANES), lambda h: (0, h)), _const_spec((1, LANES)),
                  pl.BlockSpec((None, LANES, Q_EXT), lambda h: (h, 0, 0))],
        out_specs=pl.BlockSpec((None, b, Q_EXT), lambda h: (h, 0, 0)),
        out_shape=jax.ShapeDtypeStruct((MLA_HEADS, b, Q_EXT), BF16),
        compiler_params=_cparams("parallel"),
        name="mla_absorb",
    )(q, gain, w_abs)


PAGE_SLOTS = 4


def _paged_attn_body(n_pages, pt_ref, q_ref, rn_ref, cache_ref, o_ref, buf_ref, sem_ref, s_ref,
                     pb_ref):
    i = pl.program_id(0)
    n_seq = pl.num_programs(0)
    slot = lax.rem(i, PAGE_SLOTS)

    def page_copy(seq, p, sl):
        return pltpu.make_async_copy(cache_ref.at[pt_ref[seq, p]], buf_ref.at[sl, p],
                                     sem_ref.at[sl])

    def fetch(seq):
        @pl.when(seq < n_seq)
        def _():
            for p in range(n_pages):
                page_copy(seq, p, lax.rem(seq, PAGE_SLOTS)).start()

    @pl.when(i == 0)
    def _():
        for ahead in range(PAGE_SLOTS - 1):
            fetch(ahead)

    fetch(i + PAGE_SLOTS - 1)

    for p in range(n_pages):
        page_copy(i, p, slot).wait()

    q = q_ref[0]
    inv_lo = MLA_ROW - MLA_HEADS
    for p in range(n_pages):
        page = buf_ref[slot, p]
        cols = slice(p * PAGE_SIZE, (p + 1) * PAGE_SIZE)
        pb_ref[:, cols] = page.astype(BF16)
        s_ref[:, cols] = page[inv_lo:MLA_ROW]
    s_all = jnp.dot(q[:, :MLA_ROW], pb_ref[...], preferred_element_type=F32) * s_ref[...]
    rn = rn_ref[0]
    qf = q[:, :MLA_ROW].astype(F32)
    pick = _iota((MLA_HEADS, MLA_ROW), 1) == _iota((MLA_HEADS, MLA_ROW), 0) + inv_lo
    inv_new = jnp.sum(jnp.where(pick, rn, 0.0), axis=-1, keepdims=True)
    s_new = jnp.sum(qf * rn, axis=-1, keepdims=True) * inv_new
    m = jnp.maximum(jnp.max(s_all, axis=-1, keepdims=True), s_new)
    e_new = jnp.exp(s_new - m)
    e = jnp.exp(s_all - m)
    den = e_new + jnp.sum(e, axis=-1, keepdims=True)
    ctx = e_new * rn[:, 0:KV_LORA] + lax.dot_general(
        e.astype(BF16), pb_ref[0:KV_LORA, :], (((1,), (1,)), ((), ())), preferred_element_type=F32)
    o_ref[0] = ctx / den


def _paged_attention(q_ext, rows_new, cache_t, page_table):
    b, n_pages = page_table.shape
    grid_spec = pltpu.PrefetchScalarGridSpec(
        num_scalar_prefetch=1,
        grid=(b,),
        in_specs=[pl.BlockSpec((1, MLA_HEADS, Q_EXT), lambda i, pt: (i, 0, 0)),
                  pl.BlockSpec((1, 1, MLA_ROW), lambda i, pt: (i, 0, 0)),
                  pl.BlockSpec(memory_space=pl.ANY)],
        out_specs=pl.BlockSpec((1, MLA_HEADS, KV_LORA), lambda i, pt: (i, 0, 0)),
        scratch_shapes=[pltpu.VMEM((PAGE_SLOTS, n_pages, MLA_ROW, PAGE_SIZE), F32),
                        pltpu.SemaphoreType.DMA((PAGE_SLOTS,)),
                        pltpu.VMEM((MLA_HEADS, n_pages * PAGE_SIZE), F32),
                        pltpu.VMEM((MLA_ROW, n_pages * PAGE_SIZE), BF16)],
    )
    return pl.pallas_call(
        functools.partial(_paged_attn_body, n_pages),
        grid_spec=grid_spec,
        out_shape=jax.ShapeDtypeStruct((b, MLA_HEADS, KV_LORA), F32),
        compiler_params=_cparams("arbitrary"),
        name="paged_attention",
    )(page_table, q_ext, rows_new, cache_t)


def _ctx_out_body(c_ref, w_ref, o_ref):
    o_ref[...] = (jnp.dot(c_ref[0].astype(BF16), w_ref[0], preferred_element_type=F32)
                  + jnp.dot(c_ref[1].astype(BF16), w_ref[1], preferred_element_type=F32)
                  ).astype(o_ref.dtype)


def _ctx_out(ctx_t, w_pairs):
    b = ctx_t.shape[1]
    return pl.pallas_call(
        _ctx_out_body,
        grid=(MLA_HEADS // 2,),
        in_specs=[pl.BlockSpec((2, b, KV_LORA), lambda j: (j, 0, 0)),
                  pl.BlockSpec((None, 2, KV_LORA, LANES), lambda j: (j, 0, 0, 0))],
        out_specs=pl.BlockSpec((b, LANES), lambda j: (0, j)),
        out_shape=jax.ShapeDtypeStruct((b, MLA_HEADS * V_HEAD), BF16),
        compiler_params=_cparams("parallel"),
        name="mla_ctx_out",
    )(ctx_t, w_pairs)


def _lane_pad(x, lo, width):
    pad = [(0, 0)] * (x.ndim - 1) + [(lo, width - lo - x.shape[-1])]
    return jnp.pad(x.astype(F32), pad)


def _gate_tile(rows):
    tile = jnp.stack([_lane_pad(vec, off, GATE_W) for off, vec in rows])
    return jnp.pad(tile, ((0, SUBLANES - len(rows)), (0, 0)))


def _prep_hybrid(j, w_in_a, conv_gdn_w, gdn_A_log, gdn_dt_bias, conv_ssm_w, ssm_A_log,
                 ssm_dt_bias, ssm_D):
    widths = (GDN_CONV_CH, GDN_VW, GDN_HEADS, GDN_HEADS, SSM_INNER, SSM_CONV_CH, SSM_HEADS)
    offs = np.concatenate([[0], np.cumsum(widths)])
    w = w_in_a[j].astype(BF16)
    part = lambda i: w[:, offs[i]:offs[i + 1]]
    assert (GATE_A, GATE_B, GATE_DT) == (0, GDN_HEADS, 2 * GDN_HEADS)
    small = _lane_pad(jnp.concatenate([part(2), part(3), part(6)], axis=1), 0, GATE_W)
    return dict(
        w_in=[part(0), part(1), part(4), part(5), small.astype(BF16)],
        conv_gdn=conv_gdn_w[j], conv_ssm=conv_ssm_w[j],
        gate_gdn=_gate_tile([(GATE_A, gdn_A_log[j]), (GATE_A, gdn_dt_bias[j])]),
        gate_ssm=_gate_tile([(GATE_DT, ssm_A_log[j]), (GATE_DT, ssm_dt_bias[j]),
                             (GATE_DT, ssm_D[j])]),
    )


def _prep_mla(j, norm_mix_c, w_in_c, q_a_norm, kv_a_norm, w_uq, w_uk, w_uv, q_norm, k_norm):
    half = QK_ROPE // 2
    w_in = w_in_c[j]

    def swapped(cols):
        return jnp.concatenate([-cols[..., half:], cols[..., :half]], axis=-1)

    def swapped_gain(g):
        return jnp.concatenate([g[half:], g[:half]])

    w_kr = w_in[:, Q_LORA + KV_LORA:]
    wkr = _lane_pad(jnp.concatenate([w_kr, swapped(w_kr)], axis=1), 0, LANES)
    qn, kn = q_norm[j], k_norm[j]
    gkr = _lane_pad(jnp.concatenate([kn[QK_NOPE:], swapped_gain(kn[QK_NOPE:])]), 0, LANES)
    gk = _lane_pad(kn[:QK_NOPE], 0, LANES)
    gq = jnp.concatenate([qn, swapped_gain(qn[QK_NOPE:])])

    uq = w_uq[j].reshape(Q_LORA, MLA_HEADS, QK_HEAD)
    wuq = jnp.concatenate([uq, swapped(uq[..., QK_NOPE:])], axis=-1).reshape(Q_LORA, HEAD_W)
    wuk = _lane_pad(w_uk[j], 0, LANES).reshape(KV_LORA, HEAD_W)
    wuv = _lane_pad(w_uv[j], 0, LANES).reshape(KV_LORA, HEAD_W)

    col = np.arange(HEAD_W)
    head, within = col // LANES, col % LANES
    lane = np.arange(LANES)
    inv_lane = QK_ROPE + head
    segq = ((within < QK_HEAD)[:, None] & (lane[None, :] == head[:, None]))
    segk = ((within < QK_NOPE)[:, None] & (lane[None, :] == inv_lane[:, None]))
    vone = (within == V_HEAD).astype(np.float32).reshape(1, HEAD_W)
    sel = lambda m: jnp.asarray(m, BF16)

    rope_pass = np.zeros((LANES, Q_EXT), np.float32)
    rope_pass[QK_NOPE:QK_HEAD, KV_LORA:KV_LORA + QK_ROPE] = np.eye(QK_ROPE)
    w_abs = jnp.pad(jnp.transpose(w_uk[j], (1, 2, 0)),
                    ((0, 0), (0, LANES - QK_NOPE), (0, Q_EXT - KV_LORA))) + rope_pass
    g_abs = jnp.concatenate([kn[:QK_NOPE], jnp.ones((QK_ROPE,), F32),
                             jnp.zeros((LANES - QK_HEAD,), F32)])
    uv = jnp.transpose(w_uv[j], (1, 0, 2)).reshape(MLA_HEADS // 2, 2, KV_LORA, V_HEAD)
    w_pairs = jnp.stack([_lane_pad(uv[:, 0], 0, LANES), _lane_pad(uv[:, 1], V_HEAD, LANES)], axis=1)
    row = lambda v: v.reshape(1, -1)
    return dict(
        norm=row(norm_mix_c[j]), wcq=w_in[:, :Q_LORA].astype(BF16),
        wckv=w_in[:, Q_LORA:Q_LORA + KV_LORA].astype(BF16), wkr=wkr.astype(BF16),
        qan=row(q_a_norm[j]), kvn=row(kv_a_norm[j]), wuq=wuq.astype(BF16), wuk=wuk.astype(BF16),
        wuv=wuv.astype(BF16), vone=jnp.asarray(vone), gq=row(gq), gk=row(gk), gkr=row(gkr),
        segq=sel(segq), segk=sel(segk),
        w_abs=w_abs.astype(BF16), g_abs=row(g_abs), w_pairs=w_pairs.astype(BF16),
    )


def _prep_ffn(w_gate_up, w_down, ff_tile):
    layers, d_ff, d = w_down.shape
    return (w_gate_up.astype(BF16),
            w_down.reshape(layers, d_ff // ff_tile, ff_tile, d).astype(BF16))


FF_TILE = 256
ROW_TILE = 512
FFN_ROW_TILE = 1024
MLA_ROW_TILE = 256
ATTN_TILE = 1024
ATTN_HEADS = 4
GDN_CHUNKS = 4
SSD_CHUNKS = 4


def _row_tile(rows, want):
    return want if rows % want == 0 else rows


def _trunk(x, pos0, states, cache_mla, page_table, hyb, mla, ffn, norm_mix_a, gdn_norm, conv_ssm_b,
           ssm_norm, w_out_a, w_out_c, norm_ffn):
    b, t, d = x.shape
    rows = b * t
    tm = _row_tile(rows, ROW_TILE)
    tm_f = _row_tile(rows, FFN_ROW_TILE)
    xf = x.reshape(rows, d)
    decode = states is not None

    qkv, gz, sz, xbc, small = _norm_proj(xf, norm_mix_a[0], hyb["w_in"], tm)
    if decode:
        s0, gconv0, h0, sconv0 = states
        r3 = lambda a: a.reshape(b, 1, a.shape[-1])
        n_pair = SSM_HEADS // 2
        o_gdn, s_new = _gdn_decode(qkv, gz, small, gconv0[0], s0[0], hyb["conv_gdn"],
                                   hyb["gate_gdn"], gdn_norm[0])
        o_ssd, h_new = _ssd_decode(xbc, sz, small, sconv0[0],
                                   h0[0].reshape(b, n_pair, 2 * SSM_P, SSM_N), hyb["conv_ssm"],
                                   conv_ssm_b[0], hyb["gate_ssm"], ssm_norm[0])
        gconv = jnp.concatenate([gconv0[0][:, 1:], r3(qkv)], axis=1)
        sconv = jnp.concatenate([sconv0[0][:, 1:], r3(xbc)], axis=1)
    else:
        r3 = lambda a: a.reshape(b, t, a.shape[-1])
        o_gdn, s_new = _gdn_prompt(r3(qkv), r3(gz), r3(small), hyb["conv_gdn"], hyb["gate_gdn"],
                                   gdn_norm[0], GDN_CHUNKS)
        o_ssd, h_new = _ssd_prompt(r3(xbc), r3(sz), r3(small), hyb["conv_ssm"], conv_ssm_b[0],
                                   hyb["gate_ssm"], ssm_norm[0], SSD_CHUNKS)
        gconv = r3(qkv)[:, t - (CONV_W - 1):]
        sconv = r3(xbc)[:, t - (CONV_W - 1):]
    h_new = h_new.reshape(b, SSM_HEADS, SSM_P, SSM_N)
    wo = w_out_a[0].astype(BF16)
    x1 = _res_ffn(xf, [o_gdn.reshape(rows, GDN_VW), o_ssd.reshape(rows, SSM_INNER)],
                  [wo[:GDN_VW], wo[GDN_VW:]], norm_ffn[0], 0, *ffn, tm_f)

    tm_c = _row_tile(rows, MLA_ROW_TILE)
    q_scale = QK_HEAD ** -0.5 * (1.0 if decode else math.log2(math.e))
    q, k, v, mla_rows = _mla_proj(x1, t, pos0, mla, tm_c, F32 if decode else BF16, q_scale)
    if decode:
        q_ext = jnp.transpose(_absorb(q, mla["g_abs"], mla["w_abs"]), (1, 0, 2))
        ctx = _paged_attention(q_ext, mla_rows.reshape(b, 1, MLA_ROW),
                               jnp.swapaxes(cache_mla[0], 1, 2), page_table)
        attn = _ctx_out(jnp.transpose(ctx, (1, 0, 2)), mla["w_pairs"])
    else:
        r3 = lambda a: a.reshape(b, t, a.shape[-1])
        attn = _flash_attention(r3(q), r3(k), r3(v), _row_tile(t, ATTN_TILE),
                                ATTN_HEADS).reshape(rows, -1)
    x2 = _res_ffn(x1, [attn], [w_out_c[0].astype(BF16)], norm_ffn[1], 1, *ffn, tm_f)
    return (x2.reshape(b, t, d), s_new[None], gconv[None], h_new[None], sconv[None],
            mla_rows.reshape(1, b, t, MLA_ROW))


def kernel(x_prompt, x_sample, state_gdn, state_gdn_conv, state_ssm, state_ssm_conv, cache_mla, page_table, norm_mix_a, w_in_a, conv_gdn_w, gdn_A_log, gdn_dt_bias, gdn_norm, conv_ssm_w, conv_ssm_b, ssm_A_log, ssm_dt_bias, ssm_D, ssm_norm, w_out_a, norm_mix_c, w_in_c, q_a_norm, kv_a_norm, w_uq, w_uk, w_uv, q_norm, k_norm, w_out_c, norm_ffn, w_gate_up, w_down):
    assert w_in_a.shape[0] == 1 and w_in_c.shape[0] == 1 and norm_ffn.shape[0] == 2
    hyb = _prep_hybrid(0, w_in_a, conv_gdn_w, gdn_A_log, gdn_dt_bias, conv_ssm_w, ssm_A_log,
                       ssm_dt_bias, ssm_D)
    mla = _prep_mla(0, norm_mix_c, w_in_c, q_a_norm, kv_a_norm, w_uq, w_uk, w_uv, q_norm, k_norm)
    ffn = _prep_ffn(w_gate_up, w_down, FF_TILE)
    shared = (hyb, mla, ffn, norm_mix_a, gdn_norm, conv_ssm_b, ssm_norm, w_out_a, w_out_c, norm_ffn)
    prompt = _trunk(x_prompt, 0, None, None, None, *shared)
    past_len = page_table.shape[1] * PAGE_SIZE
    sample = _trunk(x_sample, past_len, (state_gdn, state_gdn_conv, state_ssm, state_ssm_conv),
                    cache_mla, page_table, *shared)
    return (prompt[0], sample[0]) + prompt[1:] + sample[1:]
```
